```python
import math
import jax, jax.numpy as jnp
from jax import lax
import numpy as np

D_MODEL = 4096
BATCH = 2
SEQ = 4096
DEPTH = 2

HEAD_DIM = 128
N_GROUPS = 4
MIX_WIDTH = D_MODEL
GROUP_HEADS = MIX_WIDTH // (N_GROUPS * HEAD_DIM)
GROUP_WIDTH = GROUP_HEADS * HEAD_DIM
DIFF_QK = HEAD_DIM // 2
MLA_Q_RANK = 1536
MLA_KV_RANK = 512
MLA_NOPE = 128
MLA_ROPE = 64
MLA_V = HEAD_DIM
ROPE_THETA = 10000.0
REL_BUCKETS = 32
REL_MAX_DIST = 128
MEM_TOKENS = 256
MEM_HEADS = 4
MEM_HEAD_DIM = 128
D_FF = 4 * D_MODEL
Q_BLOCK = 128
NORM_EPS = 1e-6
NEG_INF = -1e30

A_COLS = 3 * GROUP_WIDTH
B_COLS = MLA_Q_RANK + MLA_KV_RANK + MLA_ROPE
C_COLS = 3 * GROUP_WIDTH
D_COLS = 3 * GROUP_WIDTH + GROUP_HEADS
IN_COLS = A_COLS + B_COLS + C_COLS + D_COLS

kernel_name = "hybrid_parallel_groups_diff_mla_stickbreak_fox"


def _rmsnorm(x, g):
    xf = x.astype(jnp.float32)
    y = xf * lax.rsqrt(jnp.mean(xf * xf, axis=-1, keepdims=True) + NORM_EPS)
    return (y * g.astype(jnp.float32)).astype(x.dtype)


def _heads(t, n_heads, d):
    b, s, _ = t.shape
    return t.reshape(b, s, n_heads, d).transpose(0, 2, 1, 3)


def _rope(t, pos):
    d = t.shape[-1]
    half = d // 2
    inv = ROPE_THETA ** (-jnp.arange(half, dtype=jnp.float32) * 2.0 / d)
    ang = pos.astype(jnp.float32)[:, None, :, None] * inv
    cos, sin = jnp.cos(ang), jnp.sin(ang)
    t1 = t[..., :half].astype(jnp.float32)
    t2 = t[..., half:].astype(jnp.float32)
    return jnp.concatenate([t1 * cos - t2 * sin, t1 * sin + t2 * cos], axis=-1).astype(t.dtype)


def _rel_bucket(dist):
    max_exact = REL_BUCKETS // 2
    nf = jnp.maximum(dist, 1).astype(jnp.float32)
    large = max_exact + (jnp.log(nf / max_exact) / math.log(REL_MAX_DIST / max_exact)
                         * (REL_BUCKETS - max_exact)).astype(jnp.int32)
    large = jnp.minimum(large, REL_BUCKETS - 1)
    return jnp.where(dist < max_exact, dist, large)


def _t5_bias(pos, t0, table):
    pos_q = lax.dynamic_slice_in_dim(pos, t0, Q_BLOCK, axis=1)
    dist = jnp.maximum(pos_q[:, :, None] - pos[:, None, :], 0)
    bias = jnp.take(table, _rel_bucket(dist), axis=0)
    return bias.transpose(0, 3, 1, 2).astype(jnp.float32)


def _causal_mask(t0, s_len, strict=False):
    t_idx = t0 + jnp.arange(Q_BLOCK)
    s_idx = jnp.arange(s_len)
    if strict:
        return s_idx[None, :] < t_idx[:, None]
    return s_idx[None, :] <= t_idx[:, None]


def _sweep(block_fn, n_blocks):
    out = lax.map(block_fn, jnp.arange(n_blocks))
    nb, b, h, qb, dv = out.shape
    return out.transpose(1, 0, 3, 2, 4).reshape(b, nb * qb, h * dv)


def _causal_softmax_blocks(q, k, v, scale, bias_fn=None):
    s_len = q.shape[2]

    def block(i):
        t0 = i * Q_BLOCK
        qb = lax.dynamic_slice_in_dim(q, t0, Q_BLOCK, axis=2)
        logits = jnp.einsum('bhqd,bhkd->bhqk', qb, k).astype(jnp.float32) * scale
        if bias_fn is not None:
            logits = logits + bias_fn(t0)
        logits = jnp.where(_causal_mask(t0, s_len), logits, NEG_INF)
        p = jax.nn.softmax(logits, axis=-1)
        return jnp.einsum('bhqk,bhkd->bhqd', p.astype(v.dtype), v)

    return _sweep(block, s_len // Q_BLOCK)


def _diff_attention(a_in, pos, rel_table, lam, lam_init, sub_g):
    qa, ka, va = jnp.split(a_in, 3, axis=-1)
    q = _heads(qa, GROUP_HEADS, HEAD_DIM)
    k = _heads(ka, GROUP_HEADS, HEAD_DIM)
    v = _heads(va, GROUP_HEADS, HEAD_DIM)
    q1, q2 = q[..., :DIFF_QK], q[..., DIFF_QK:]
    k1, k2 = k[..., :DIFF_QK], k[..., DIFF_QK:]
    s_len = q.shape[2]
    scale = DIFF_QK ** -0.5

    def block(i):
        t0 = i * Q_BLOCK
        bias = _t5_bias(pos, t0, rel_table)
        mask = _causal_mask(t0, s_len)

        def probs(qh, kh):
            qb = lax.dynamic_slice_in_dim(qh, t0, Q_BLOCK, axis=2)
            logits = jnp.einsum('bhqd,bhkd->bhqk', qb, kh).astype(jnp.float32) * scale + bias
            return jax.nn.softmax(jnp.where(mask, logits, NEG_INF), axis=-1)

        p = probs(q1, k1) - lam * probs(q2, k2)
        o = jnp.einsum('bhqk,bhkd->bhqd', p.astype(v.dtype), v)
        return _rmsnorm(o, sub_g) * (1.0 - lam_init)

    return _sweep(block, s_len // Q_BLOCK)


def _mla(b_in, pos, q_norm_g, kv_norm_g, w_uq, w_ukv):
    c_q, c_kv, k_rope = jnp.split(b_in, [MLA_Q_RANK, MLA_Q_RANK + MLA_KV_RANK], axis=-1)
    q = _heads(_rmsnorm(c_q, q_norm_g) @ w_uq, GROUP_HEADS, MLA_NOPE + MLA_ROPE)
    kv = _heads(_rmsnorm(c_kv, kv_norm_g) @ w_ukv, GROUP_HEADS, MLA_NOPE + MLA_V)
    q_nope, q_pe = q[..., :MLA_NOPE], _rope(q[..., MLA_NOPE:], pos)
    k_nope, v = kv[..., :MLA_NOPE], kv[..., MLA_NOPE:]
    k_pe = jnp.broadcast_to(_rope(k_rope[:, None], pos), q_pe.shape)
    qf = jnp.concatenate([q_nope, q_pe], axis=-1)
    kf = jnp.concatenate([k_nope, k_pe], axis=-1)
    return _causal_softmax_blocks(qf, kf, v, (MLA_NOPE + MLA_ROPE) ** -0.5)


def _stick_breaking(c_in):
    qc, kc, vc = jnp.split(c_in, 3, axis=-1)
    q = _heads(qc, GROUP_HEADS, HEAD_DIM)
    k = _heads(kc, GROUP_HEADS, HEAD_DIM)
    v = _heads(vc, GROUP_HEADS, HEAD_DIM)
    s_len = q.shape[2]
    scale = HEAD_DIM ** -0.5

    def block(i):
        t0 = i * Q_BLOCK
        qb = lax.dynamic_slice_in_dim(q, t0, Q_BLOCK, axis=2)
        z = jnp.einsum('bhqd,bhkd->bhqk', qb, k).astype(jnp.float32) * scale
        strict = _causal_mask(t0, s_len, strict=True)
        log_1m = jnp.where(strict, jax.nn.log_sigmoid(-z), 0.0)
        suffix = lax.cumsum(log_1m, axis=3, reverse=True) - log_1m
        a = jnp.where(strict, jnp.exp(jax.nn.log_sigmoid(z) + suffix), 0.0)
        return jnp.einsum('bhqk,bhkd->bhqd', a.astype(v.dtype), v)

    return _sweep(block, s_len // Q_BLOCK)


def _forgetting_attention(d_in, b_f):
    qd, kd, vd, f_logit = jnp.split(d_in, [GROUP_WIDTH, 2 * GROUP_WIDTH, 3 * GROUP_WIDTH], axis=-1)
    q = _heads(qd, GROUP_HEADS, HEAD_DIM)
    k = _heads(kd, GROUP_HEADS, HEAD_DIM)
    v = _heads(vd, GROUP_HEADS, HEAD_DIM)
    log_f = jax.nn.log_sigmoid(f_logit.astype(jnp.float32) + b_f.astype(jnp.float32))
    cum = jnp.cumsum(log_f, axis=1).transpose(0, 2, 1)

    def decay_bias(t0):
        cq = lax.dynamic_slice_in_dim(cum, t0, Q_BLOCK, axis=2)
        return cq[..., None] - cum[:, :, None, :]

    return _causal_softmax_blocks(q, k, v, HEAD_DIM ** -0.5, decay_bias)


def _memory_attention(h, mem_n, w_q, w_k, w_v, w_o):
    b, s, _ = h.shape
    m = mem_n.shape[1]
    q = (h @ w_q).reshape(b, s, MEM_HEADS, MEM_HEAD_DIM)
    k = (mem_n @ w_k).reshape(b, m, MEM_HEADS, MEM_HEAD_DIM)
    v = (mem_n @ w_v).reshape(b, m, MEM_HEADS, MEM_HEAD_DIM)
    logits = jnp.einsum('bshd,bmhd->bhsm', q, k).astype(jnp.float32) * MEM_HEAD_DIM ** -0.5
    p = jax.nn.softmax(logits, axis=-1)
    o = jnp.einsum('bhsm,bmhd->bshd', p.astype(v.dtype), v).reshape(b, s, MEM_HEADS * MEM_HEAD_DIM)
    return o @ w_o


def setup_inputs(seed: int = 0) -> dict:
    key = jax.random.key(seed)
    ks = jax.random.split(key, 32)
    f32 = jnp.float32

    def nrm(k, shape, scale):
        return jax.random.normal(k, shape, f32) * scale

    def gain(k, shape):
        return 1.0 + 0.05 * jax.random.normal(k, shape, f32)

    L, D = DEPTH, D_MODEL
    start = jax.random.randint(ks[2], (BATCH, 1), 0, 1024, dtype=jnp.int32)
    positions = start + jnp.arange(SEQ, dtype=jnp.int32)[None, :]
    return {
        "x": nrm(ks[0], (BATCH, SEQ, D), 1.0),
        "mem": nrm(ks[1], (BATCH, MEM_TOKENS, D), 1.0),
        "positions": positions,
        "attn_norm_g": gain(ks[3], (L, D)),
        "w_in": nrm(ks[4], (L, D, IN_COLS), D ** -0.5),
        "w_out": nrm(ks[5], (L, MIX_WIDTH, D), MIX_WIDTH ** -0.5),
        "rel_table": nrm(ks[6], (REL_BUCKETS, GROUP_HEADS), 0.5),
        "diff_lam_q1": nrm(ks[7], (L, DIFF_QK), 0.1),
        "diff_lam_k1": nrm(ks[8], (L, DIFF_QK), 0.1),
        "diff_lam_q2": nrm(ks[9], (L, DIFF_QK), 0.1),
        "diff_lam_k2": nrm(ks[10], (L, DIFF_QK), 0.1),
        "diff_sub_g": gain(ks[11], (L, HEAD_DIM)),
        "mla_q_norm_g": gain(ks[12], (L, MLA_Q_RANK)),
        "mla_kv_norm_g": gain(ks[13], (L, MLA_KV_RANK)),
        "mla_w_uq": nrm(ks[14], (L, MLA_Q_RANK, GROUP_HEADS * (MLA_NOPE + MLA_ROPE)), MLA_Q_RANK ** -0.5),
        "mla_w_ukv": nrm(ks[15], (L, MLA_KV_RANK, GROUP_HEADS * (MLA_NOPE + MLA_V)), MLA_KV_RANK ** -0.5),
        "fox_b_f": 2.0 + 0.5 * jax.random.normal(ks[16], (L, GROUP_HEADS), f32),
        "mem_q_norm_g": gain(ks[17], (L, D)),
        "mem_kv_norm_g": gain(ks[18], (L, D)),
        "mem_w_q": nrm(ks[19], (L, D, MEM_HEADS * MEM_HEAD_DIM), D ** -0.5),
        "mem_w_k": nrm(ks[20], (L, D, MEM_HEADS * MEM_HEAD_DIM), D ** -0.5),
        "mem_w_v": nrm(ks[21], (L, D, MEM_HEADS * MEM_HEAD_DIM), D ** -0.5),
        "mem_w_o": nrm(ks[22], (L, MEM_HEADS * MEM_HEAD_DIM, D), (MEM_HEADS * MEM_HEAD_DIM) ** -0.5),
        "mlp_norm_g": gain(ks[23], (L, D)),
        "w_up": nrm(ks[24], (L, D, D_FF), D ** -0.5),
        "w_down": nrm(ks[25], (L, D_FF, D), D_FF ** -0.5),
        "final_norm_g": gain(ks[26], (D,)),
    }


def reference(x, mem, positions, attn_norm_g, w_in, w_out, rel_table,
              diff_lam_q1, diff_lam_k1, diff_lam_q2, diff_lam_k2, diff_sub_g,
              mla_q_norm_g, mla_kv_norm_g, mla_w_uq, mla_w_ukv, fox_b_f,
              mem_q_norm_g, mem_kv_norm_g, mem_w_q, mem_w_k, mem_w_v, mem_w_o,
              mlp_norm_g, w_up, w_down, final_norm_g):
    b_end = A_COLS + B_COLS
    c_end = b_end + C_COLS
    for l in range(DEPTH):
        h = _rmsnorm(x, attn_norm_g[l])
        proj = h @ w_in[l]
        a_in, b_in, c_in, d_in = jnp.split(proj, [A_COLS, b_end, c_end], axis=-1)

        lam_init = 0.8 - 0.6 * math.exp(-0.3 * l)
        lam = (jnp.exp(jnp.sum(diff_lam_q1[l].astype(jnp.float32) * diff_lam_k1[l].astype(jnp.float32)))
               - jnp.exp(jnp.sum(diff_lam_q2[l].astype(jnp.float32) * diff_lam_k2[l].astype(jnp.float32)))
               + lam_init)
        o_a = _diff_attention(a_in, positions, rel_table, lam, lam_init, diff_sub_g[l])
        o_b = _mla(b_in, positions, mla_q_norm_g[l], mla_kv_norm_g[l], mla_w_uq[l], mla_w_ukv[l])
        o_c = _stick_breaking(c_in)
        o_d = _forgetting_attention(d_in, fox_b_f[l])
        mix = jnp.concatenate([o_a, o_b, o_c, o_d], axis=-1)
        x = x + mix @ w_out[l]

        h = _rmsnorm(x, mem_q_norm_g[l])
        mem_n = _rmsnorm(mem, mem_kv_norm_g[l])
        x = x + _memory_attention(h, mem_n, mem_w_q[l], mem_w_k[l], mem_w_v[l], mem_w_o[l])

        h = _rmsnorm(x, mlp_norm_g[l])
        x = x + jnp.square(jax.nn.relu(h @ w_up[l])) @ w_down[l]
    return _rmsnorm(x, final_norm_g)
```

```python
import functools
import math

import jax
import jax.numpy as jnp
from jax import lax
from jax.experimental import pallas as pl
from jax.experimental.pallas import tpu as pltpu

F32 = jnp.float32
BF16 = jnp.bfloat16

D_MODEL = 4096
DEPTH = 2
HEAD_DIM = 128
GROUP_HEADS = 8
GROUP_WIDTH = GROUP_HEADS * HEAD_DIM
DIFF_QK = HEAD_DIM // 2
MLA_Q_RANK = 1536
MLA_KV_RANK = 512
MLA_NOPE = 128
MLA_ROPE = 64
ROPE_THETA = 10000.0
REL_BUCKETS = 32
REL_MAX_DIST = 128
MEM_HEADS = 4
MEM_HEAD_DIM = 128
NORM_EPS = 1e-6
NEG_INF = -1e30

A_COLS = 3 * GROUP_WIDTH
B_COLS = MLA_Q_RANK + MLA_KV_RANK + MLA_ROPE
C_COLS = 3 * GROUP_WIDTH

LANES = 128
VMEM_LIMIT = 48 * 1024 * 1024

ATT_T = 256


def _cparams(sem):
    return pltpu.CompilerParams(dimension_semantics=sem, vmem_limit_bytes=VMEM_LIMIT)


def _dot(a, b):
    return jnp.dot(a, b, preferred_element_type=F32)


def _dot_nt(a, b):
    return lax.dot_general(a, b, (((1,), (1,)), ((), ())), preferred_element_type=F32)


def _rms(x, g):
    return x * lax.rsqrt(jnp.mean(x * x, axis=-1, keepdims=True) + NORM_EPS) * g


def _rmsnorm_kernel(x_ref, g_ref, o_ref):
    o_ref[...] = _rms(x_ref[...].astype(F32), g_ref[...]).astype(o_ref.dtype)


def _rmsnorm(x, g, out_dtype, tm=256):
    m, d = x.shape
    return pl.pallas_call(
        _rmsnorm_kernel,
        grid=(m // tm,),
        in_specs=[pl.BlockSpec((tm, d), lambda i: (i, 0)),
                  pl.BlockSpec((1, d), lambda i: (0, 0))],
        out_specs=pl.BlockSpec((tm, d), lambda i: (i, 0)),
        out_shape=jax.ShapeDtypeStruct((m, d), out_dtype),
        compiler_params=_cparams(("parallel",)),
        name="rmsnorm",
    )(x, g.reshape(1, d).astype(F32))


def _mm_kernel(a_ref, w_ref, *rest, nk, relu2, has_res):
    if has_res:
        res_ref, o_ref, acc_ref = rest
    else:
        o_ref, acc_ref = rest
    k = pl.program_id(2)

    @pl.when(k == 0)
    def _():
        acc_ref[...] = jnp.zeros_like(acc_ref)

    acc_ref[...] += _dot(a_ref[...], w_ref[...])

    @pl.when(k == nk - 1)
    def _():
        r = acc_ref[...]
        if relu2:
            r = jnp.square(jnp.maximum(r, 0.0))
        if has_res:
            r = r + res_ref[...]
        o_ref[...] = r.astype(o_ref.dtype)


def _matmul(a, w, out_dtype, *, res=None, relu2=False, tm=1024, tn=1024, tk=512, name="matmul"):
    m, kd = a.shape
    n = w.shape[1]
    tm, tn, tk = min(tm, m), min(tn, n), min(tk, kd)
    assert m % tm == 0 and n % tn == 0 and kd % tk == 0
    nk = kd // tk
    in_specs = [pl.BlockSpec((tm, tk), lambda i, j, k: (i, k)),
                pl.BlockSpec((tk, tn), lambda i, j, k: (k, j))]
    args = [a, w]
    if res is not None:
        in_specs.append(pl.BlockSpec((tm, tn), lambda i, j, k: (i, j)))
        args.append(res)
    return pl.pallas_call(
        functools.partial(_mm_kernel, nk=nk, relu2=relu2, has_res=res is not None),
        grid=(m // tm, n // tn, nk),
        in_specs=in_specs,
        out_specs=pl.BlockSpec((tm, tn), lambda i, j, k: (i, j)),
        out_shape=jax.ShapeDtypeStruct((m, n), out_dtype),
        scratch_shapes=[pltpu.VMEM((tm, tn), F32)],
        compiler_params=_cparams(("parallel", "parallel", "arbitrary")),
        name=name,
    )(*args)


def _out_proj_kernel(a0, a1, a2, a3, w_ref, res_ref, o_ref):
    acc = res_ref[...]
    for g, a_ref in enumerate((a0, a1, a2, a3)):
        acc = acc + _dot(a_ref[...], w_ref[g * GROUP_WIDTH:(g + 1) * GROUP_WIDTH, :])
    o_ref[...] = acc


def _out_proj(mixes, w, res, tm=512, tn=512):
    m = res.shape[0]
    n = w.shape[1]
    a_spec = pl.BlockSpec((tm, GROUP_WIDTH), lambda i, j: (i, 0))
    return pl.pallas_call(
        _out_proj_kernel,
        grid=(m // tm, n // tn),
        in_specs=[a_spec, a_spec, a_spec, a_spec,
                  pl.BlockSpec((w.shape[0], tn), lambda i, j: (0, j)),
                  pl.BlockSpec((tm, tn), lambda i, j: (i, j))],
        out_specs=pl.BlockSpec((tm, tn), lambda i, j: (i, j)),
        out_shape=jax.ShapeDtypeStruct((m, n), F32),
        compiler_params=_cparams(("parallel", "parallel")),
        name="out_proj",
    )(*mixes, w, res)


def _rope_kernel(pos_ref, tab_ref, cos_ref, sin_ref):
    ang = pos_ref[...].astype(F32) * tab_ref[0:1, :]
    cos_ref[...] = jnp.cos(ang) * tab_ref[1:2, :]
    sin_ref[...] = jnp.sin(ang) * tab_ref[2:3, :]


def _rope_tables(pos_col, tm=1024):
    m = pos_col.shape[0]
    half = MLA_ROPE // 2
    inv = ROPE_THETA ** (-jnp.arange(half, dtype=F32) * 2.0 / MLA_ROPE)
    zeros = jnp.zeros((LANES - MLA_ROPE,), F32)
    ones = jnp.ones((half,), F32)
    tab = jnp.stack([jnp.concatenate([inv, inv, zeros]),
                     jnp.concatenate([ones, ones, zeros]),
                     jnp.concatenate([-ones, ones, zeros])]
                    + [jnp.zeros((LANES,), F32)] * 5)
    spec = pl.BlockSpec((tm, LANES), lambda i: (i, 0))
    return pl.pallas_call(
        _rope_kernel,
        grid=(m // tm,),
        in_specs=[pl.BlockSpec((tm, 1), lambda i: (i, 0)),
                  pl.BlockSpec((8, LANES), lambda i: (0, 0))],
        out_specs=[spec, spec],
        out_shape=[jax.ShapeDtypeStruct((m, LANES), F32)] * 2,
        compiler_params=_cparams(("parallel",)),
        name="rope_tables",
    )(pos_col, tab)


def _mla_q_kernel(cq_ref, g_ref, w_ref, cos_ref, sin_ref, o_ref):
    n = _rms(cq_ref[...].astype(F32), g_ref[...]).astype(BF16)
    t = _dot(n, w_ref[...])
    o_ref[:, :GROUP_WIDTH] = t[:, :GROUP_WIDTH].astype(o_ref.dtype)
    cos, sin = cos_ref[...], sin_ref[...]
    for h in range(GROUP_HEADS):
        lo = GROUP_WIDTH + h * LANES
        pe = t[:, lo:lo + LANES] * cos + t[:, lo + GROUP_WIDTH:lo + GROUP_WIDTH + LANES] * sin
        o_ref[:, lo:lo + LANES] = pe.astype(o_ref.dtype)


def _mla_q(proj, g, w, cos, sin, tm=512):
    m = proj.shape[0]
    cq_block = (2 * A_COLS + C_COLS) // MLA_Q_RANK
    return pl.pallas_call(
        _mla_q_kernel,
        grid=(m // tm,),
        in_specs=[pl.BlockSpec((tm, MLA_Q_RANK), lambda i: (i, cq_block)),
                  pl.BlockSpec((1, MLA_Q_RANK), lambda i: (0, 0)),
                  pl.BlockSpec(w.shape, lambda i: (0, 0)),
                  pl.BlockSpec((tm, LANES), lambda i: (i, 0)),
                  pl.BlockSpec((tm, LANES), lambda i: (i, 0))],
        out_specs=pl.BlockSpec((tm, 2 * GROUP_WIDTH), lambda i: (i, 0)),
        out_shape=jax.ShapeDtypeStruct((m, 2 * GROUP_WIDTH), BF16),
        compiler_params=_cparams(("parallel",)),
        name="mla_q",
    )(proj, g.reshape(1, -1).astype(F32), w, cos, sin)


def _mla_kv_kernel(ckv_ref, g_ref, w_ref, kr_ref, cos_ref, sin_ref, kv_ref, kpe_ref):
    n = _rms(ckv_ref[...].astype(F32), g_ref[...]).astype(BF16)
    kv_ref[...] = _dot(n, w_ref[...]).astype(kv_ref.dtype)
    kr = kr_ref[...]
    kpe_ref[...] = (kr[:, :LANES] * cos_ref[...] + kr[:, LANES:] * sin_ref[...]).astype(kpe_ref.dtype)


def _mla_kv(proj, g, w, kr, cos, sin, tm=512):
    m = proj.shape[0]
    ckv_block = (2 * A_COLS + C_COLS + MLA_Q_RANK) // MLA_KV_RANK
    return pl.pallas_call(
        _mla_kv_kernel,
        grid=(m // tm,),
        in_specs=[pl.BlockSpec((tm, MLA_KV_RANK), lambda i: (i, ckv_block)),
                  pl.BlockSpec((1, MLA_KV_RANK), lambda i: (0, 0)),
                  pl.BlockSpec(w.shape, lambda i: (0, 0)),
                  pl.BlockSpec((tm, 2 * LANES), lambda i: (i, 0)),
                  pl.BlockSpec((tm, LANES), lambda i: (i, 0)),
                  pl.BlockSpec((tm, LANES), lambda i: (i, 0))],
        out_specs=[pl.BlockSpec((tm, 2 * GROUP_WIDTH), lambda i: (i, 0)),
                   pl.BlockSpec((tm, LANES), lambda i: (i, 0))],
        out_shape=[jax.ShapeDtypeStruct((m, 2 * GROUP_WIDTH), BF16),
                   jax.ShapeDtypeStruct((m, LANES), BF16)],
        compiler_params=_cparams(("parallel",)),
        name="mla_kv",
    )(proj, g.reshape(1, -1).astype(F32), w, kr, cos, sin)


def _forget_logit_kernel(w_ref, h_ref, o_ref):
    o_ref[...] = _dot_nt(w_ref[...], h_ref[...])


def _forget_logits(w_t, h, tm=1024):
    m, d = h.shape
    rows = w_t.shape[0]
    return pl.pallas_call(
        _forget_logit_kernel,
        grid=(m // tm,),
        in_specs=[pl.BlockSpec((rows, d), lambda i: (0, 0)),
                  pl.BlockSpec((tm, d), lambda i: (i, 0))],
        out_specs=pl.BlockSpec((rows, tm), lambda i: (0, i)),
        out_shape=jax.ShapeDtypeStruct((rows, m), F32),
        compiler_params=_cparams(("parallel",)),
        name="forget_logits",
    )(w_t, h)


def _split3(x):
    x1 = x.astype(BF16)
    r1 = x - x1.astype(F32)
    x2 = r1.astype(BF16)
    x3 = (r1 - x2.astype(F32)).astype(BF16)
    return x1, x2, x3


def _forget_cum_kernel(f_ref, b_ref, o_ref, *, chunk):
    x = f_ref[...] + b_ref[...]
    logf = jnp.minimum(x, 0.0) - jnp.log1p(jnp.exp(-jnp.abs(x)))
    r = lax.broadcasted_iota(jnp.int32, (chunk, chunk), 0)
    c = lax.broadcasted_iota(jnp.int32, (chunk, chunk), 1)
    tri = jnp.where(r <= c, 1.0, 0.0).astype(BF16)
    carry = jnp.zeros((x.shape[0], 1), F32)
    for ci in range(x.shape[1] // chunk):
        parts = _split3(logf[:, ci * chunk:(ci + 1) * chunk])
        cs = carry + _dot(parts[0], tri) + _dot(parts[1], tri) + _dot(parts[2], tri)
        o_ref[:, ci * chunk:(ci + 1) * chunk] = cs
        carry = cs[:, chunk - 1:chunk]


def _forget_cum(f_t, b_f, batch, seq):
    rows = f_t.shape[0]
    return pl.pallas_call(
        functools.partial(_forget_cum_kernel, chunk=256),
        grid=(batch,),
        in_specs=[pl.BlockSpec((rows, seq), lambda b: (0, b)),
                  pl.BlockSpec((rows, 1), lambda b: (0, 0))],
        out_specs=pl.BlockSpec((None, rows, seq), lambda b: (b, 0, 0)),
        out_shape=jax.ShapeDtypeStruct((batch, rows, seq), F32),
        compiler_params=_cparams(("parallel",)),
        name="forget_cum",
    )(f_t, b_f)


def _tile_iota(t):
    return (lax.broadcasted_iota(jnp.int32, (t, t), 0),
            lax.broadcasted_iota(jnp.int32, (t, t), 1))


def _softmax_update(s, v, m, l, acc):
    m_new = jnp.maximum(m, jnp.max(s, axis=-1, keepdims=True))
    alpha = jnp.exp(m - m_new)
    p = jnp.exp(s - m_new)
    l = alpha * l + jnp.sum(p, axis=-1, keepdims=True)
    acc = alpha * acc + _dot(p.astype(BF16), v)
    return m_new, l, acc


def _softmax_init(t):
    return (jnp.full((t, 1), NEG_INF, F32), jnp.zeros((t, 1), F32), jnp.zeros((t, HEAD_DIM), F32))


def _kv_slice(ref, j, t):
    return ref[pl.ds(pl.multiple_of(j * t, t), t), :]


def _row_slice(ref, j, t):
    return ref[:, pl.ds(pl.multiple_of(j * t, t), t)]


def _fox_kernel(q_ref, k_ref, v_ref, cum_ref, o_ref, *, t):
    qi = pl.program_id(2)
    q = q_ref[...]
    row, col = _tile_iota(t)

    def step(j, carry, masked):
        s = _dot_nt(q, _kv_slice(k_ref, j, t)) - _row_slice(cum_ref, j, t)
        if masked:
            s = jnp.where(col <= row, s, NEG_INF)
        return _softmax_update(s, _kv_slice(v_ref, j, t), *carry)

    carry = lax.fori_loop(0, qi, lambda j, c: step(j, c, False), _softmax_init(t))
    _, l, acc = step(qi, carry, True)
    o_ref[...] = (acc / l).astype(o_ref.dtype)


def _mla_kernel(qn_ref, qp_ref, kn_ref, kp_ref, v_ref, o_ref, *, t):
    qi = pl.program_id(2)
    qn, qp = qn_ref[...], qp_ref[...]
    row, col = _tile_iota(t)

    def step(j, carry, masked):
        s = _dot_nt(qn, _kv_slice(kn_ref, j, t)) + _dot_nt(qp, _kv_slice(kp_ref, j, t))
        if masked:
            s = jnp.where(col <= row, s, NEG_INF)
        return _softmax_update(s, _kv_slice(v_ref, j, t), *carry)

    carry = lax.fori_loop(0, qi, lambda j, c: step(j, c, False), _softmax_init(t))
    _, l, acc = step(qi, carry, True)
    o_ref[...] = (acc / l).astype(o_ref.dtype)


def _diff_kernel(lq1_ref, lk1_ref, lq2_ref, lk2_ref, q_ref, k_ref, v_ref, posq_ref, posk_ref,
                 bias_ref, subg_ref, o_ref, *, t, lam_init):
    qi = pl.program_id(2)
    lam = (jnp.exp(jnp.sum(lq1_ref[...] * lk1_ref[...], axis=-1, keepdims=True))
           - jnp.exp(jnp.sum(lq2_ref[...] * lk2_ref[...], axis=-1, keepdims=True)) + lam_init)
    q = q_ref[...]
    lane = lax.broadcasted_iota(jnp.int32, q.shape, 1)
    zero = jnp.zeros_like(q)
    q1 = jnp.where(lane < DIFF_QK, q, zero)
    q2 = jnp.where(lane >= DIFF_QK, q, zero)
    posq = posq_ref[...]
    bias_by_dist = jnp.broadcast_to(bias_ref[...], (t, LANES))
    row, col = _tile_iota(t)

    def step(j, carry, masked):
        k = _kv_slice(k_ref, j, t)
        v = _kv_slice(v_ref, j, t)
        dist = jnp.clip(posq - _row_slice(posk_ref, j, t), 0, REL_MAX_DIST - 1)
        bias = jnp.concatenate(
            [jnp.take_along_axis(bias_by_dist, dist[:, c * LANES:(c + 1) * LANES], axis=1)
             for c in range(t // LANES)], axis=1)
        s1 = _dot_nt(q1, k) + bias
        s2 = _dot_nt(q2, k) + bias
        if masked:
            s1 = jnp.where(col <= row, s1, NEG_INF)
            s2 = jnp.where(col <= row, s2, NEG_INF)
        c1 = _softmax_update(s1, v, *carry[:3])
        c2 = _softmax_update(s2, v, *carry[3:])
        return c1 + c2

    init = _softmax_init(t) + _softmax_init(t)
    carry = lax.fori_loop(0, qi, lambda j, c: step(j, c, False), init)
    _, l1, a1, _, l2, a2 = step(qi, carry, True)
    o = a1 / l1 - lam * (a2 / l2)
    o_ref[...] = (_rms(o, subg_ref[...]) * (1.0 - lam_init)).astype(o_ref.dtype)


def _sb_kernel(q_ref, k_ref, v_ref, o_ref, *, t):
    qi = pl.program_id(2)
    q = q_ref[...]
    row, col = _tile_iota(t)
    later = jnp.where(row > col, 1.0, 0.0).astype(BF16)

    def step(j, carry, masked):
        tail, acc = carry
        z = _dot_nt(q, _kv_slice(k_ref, j, t))
        sp = jnp.log1p(jnp.exp(-jnp.abs(z)))
        log_beta = jnp.minimum(z, 0.0) - sp
        log_1m = -jnp.maximum(z, 0.0) - sp
        if masked:
            log_1m = jnp.where(col < row, log_1m, 0.0)
        hi = log_1m.astype(BF16)
        lo = (log_1m - hi.astype(F32)).astype(BF16)
        suffix = _dot(hi, later) + _dot(lo, later)
        a = jnp.exp(log_beta + suffix + tail)
        if masked:
            a = jnp.where(col < row, a, 0.0)
        acc = acc + _dot(a.astype(BF16), _kv_slice(v_ref, j, t))
        tail = tail + jnp.sum(log_1m, axis=-1, keepdims=True)
        return tail, acc

    carry = step(qi, (jnp.zeros((t, 1), F32), jnp.zeros((t, HEAD_DIM), F32)), True)
    _, acc = lax.fori_loop(0, qi, lambda jj, c: step(qi - 1 - jj, c, False), carry)
    o_ref[...] = acc.astype(o_ref.dtype)


def _attn_call(kernel, batch, seq, in_arrays, in_specs, name):
    nq = seq // ATT_T
    return pl.pallas_call(
        kernel,
        grid=(batch, GROUP_HEADS, nq),
        in_specs=in_specs,
        out_specs=pl.BlockSpec((ATT_T, HEAD_DIM), lambda b, h, i: (b * nq + i, h)),
        out_shape=jax.ShapeDtypeStruct((batch * seq, GROUP_WIDTH), BF16),
        compiler_params=_cparams(("parallel", "parallel", "arbitrary")),
        name=name,
    )(*in_arrays)


def _q_spec(seq, col0):
    nq = seq // ATT_T
    return pl.BlockSpec((ATT_T, HEAD_DIM), lambda b, h, i: (b * nq + i, col0 + h))


def _kv_spec(seq, col0):
    return pl.BlockSpec((seq, HEAD_DIM), lambda b, h, i: (b, col0 + h))


def _shared_spec(seq):
    return pl.BlockSpec((seq, HEAD_DIM), lambda b, h, i: (b, 0))


def _full_spec(shape):
    return pl.BlockSpec(shape, lambda b, h, i: (0,) * len(shape))


def _mem_kernel(x_ref, g_ref, wq_ref, k_ref, v_ref, wo_ref, o_ref):
    x = x_ref[...]
    h = _rms(x, g_ref[...]).astype(BF16)
    q = _dot(h, wq_ref[...]).astype(BF16)
    outs = []
    for hd in range(MEM_HEADS):
        sl = slice(hd * MEM_HEAD_DIM, (hd + 1) * MEM_HEAD_DIM)
        s = _dot_nt(q[:, sl], k_ref[:, sl])
        p = jnp.exp(s - jnp.max(s, axis=-1, keepdims=True))
        o = _dot(p.astype(BF16), v_ref[:, sl]) / jnp.sum(p, axis=-1, keepdims=True)
        outs.append(o.astype(BF16))
    o_ref[...] = x + _dot(jnp.concatenate(outs, axis=-1), wo_ref[...])


def _mem_block(x, g, wq, k, v, wo, batch, seq, tm=256):
    m, d = x.shape
    n_mem = k.shape[0] // batch
    tiles_per_batch = seq // tm
    width = MEM_HEADS * MEM_HEAD_DIM
    return pl.pallas_call(
        _mem_kernel,
        grid=(m // tm,),
        in_specs=[pl.BlockSpec((tm, d), lambda i: (i, 0)),
                  pl.BlockSpec((1, d), lambda i: (0, 0)),
                  pl.BlockSpec((d, width), lambda i: (0, 0)),
                  pl.BlockSpec((n_mem, width), lambda i: (i // tiles_per_batch, 0)),
                  pl.BlockSpec((n_mem, width), lambda i: (i // tiles_per_batch, 0)),
                  pl.BlockSpec((width, d), lambda i: (0, 0))],
        out_specs=pl.BlockSpec((tm, d), lambda i: (i, 0)),
        out_shape=jax.ShapeDtypeStruct((m, d), F32),
        compiler_params=_cparams(("parallel",)),
        name="mem_block",
    )(x, g.reshape(1, d).astype(F32), wq, k, v, wo)


def _rel_bias_by_distance(rel_table):
    d = jnp.arange(REL_MAX_DIST, dtype=jnp.int32)
    max_exact = REL_BUCKETS // 2
    nf = jnp.maximum(d, 1).astype(F32)
    large = max_exact + (jnp.log(nf / max_exact) / math.log(REL_MAX_DIST / max_exact)
                         * (REL_BUCKETS - max_exact)).astype(jnp.int32)
    large = jnp.minimum(large, REL_BUCKETS - 1)
    bucket = jnp.where(d < max_exact, d, large)
    return jnp.take(rel_table, bucket, axis=0).T.astype(F32)


def _prep_in_proj(w):
    gw = GROUP_WIDTH
    a = w[:, :A_COLS]
    b = w[:, A_COLS:A_COLS + B_COLS]
    c = w[:, A_COLS + B_COLS:A_COLS + B_COLS + C_COLS]
    d = w[:, A_COLS + B_COLS + C_COLS:A_COLS + B_COLS + C_COLS + 3 * gw]
    f = w[:, A_COLS + B_COLS + C_COLS + 3 * gw:]
    main = jnp.concatenate([
        a[:, :gw] * DIFF_QK ** -0.5, a[:, gw:],
        c[:, :gw] * HEAD_DIM ** -0.5, c[:, gw:],
        d[:, :gw] * HEAD_DIM ** -0.5, d[:, gw:],
        b[:, :MLA_Q_RANK + MLA_KV_RANK]], axis=1).astype(BF16)
    kr = b[:, MLA_Q_RANK + MLA_KV_RANK:]
    half = MLA_ROPE // 2
    pad = jnp.zeros((w.shape[0], LANES - MLA_ROPE), w.dtype)
    rope_k = jnp.concatenate([kr, pad, kr[:, half:], kr[:, :half], pad], axis=1).astype(BF16)
    f_t = jnp.concatenate([f.T, jnp.zeros((16 - GROUP_HEADS, w.shape[0]), w.dtype)], axis=0).astype(BF16)
    return main, rope_k, f_t


def _prep_uq(w):
    qk = MLA_NOPE + MLA_ROPE
    half = MLA_ROPE // 2
    w = (w * qk ** -0.5).reshape(w.shape[0], GROUP_HEADS, qk)
    pad = jnp.zeros((w.shape[0], GROUP_HEADS, LANES - MLA_ROPE), w.dtype)
    nope = w[:, :, :MLA_NOPE]
    r1 = w[:, :, MLA_NOPE:MLA_NOPE + half]
    r2 = w[:, :, MLA_NOPE + half:]
    rope = jnp.concatenate([r1, r2, pad], axis=-1)
    swapped = jnp.concatenate([r2, r1, pad], axis=-1)
    return jnp.concatenate([nope.reshape(w.shape[0], -1), rope.reshape(w.shape[0], -1),
                            swapped.reshape(w.shape[0], -1)], axis=1).astype(BF16)


def _prep_ukv(w):
    w = w.reshape(w.shape[0], GROUP_HEADS, MLA_NOPE + HEAD_DIM)
    return jnp.concatenate([w[:, :, :MLA_NOPE].reshape(w.shape[0], -1),
                            w[:, :, MLA_NOPE:].reshape(w.shape[0], -1)], axis=1).astype(BF16)


def kernel(x, mem, positions, attn_norm_g, w_in, w_out, rel_table, diff_lam_q1, diff_lam_k1, diff_lam_q2, diff_lam_k2, diff_sub_g, mla_q_norm_g, mla_kv_norm_g, mla_w_uq, mla_w_ukv, fox_b_f, mem_q_norm_g, mem_kv_norm_g, mem_w_q, mem_w_k, mem_w_v, mem_w_o, mlp_norm_g, w_up, w_down, final_norm_g):
    batch, seq, d = x.shape
    m = batch * seq
    gw = GROUP_WIDTH
    nh = GROUP_HEADS
    xf = x.reshape(m, d)
    mem2 = mem.reshape(batch * mem.shape[1], d)
    pos_col = positions.reshape(m, 1).astype(jnp.int32)
    pos_row = positions.reshape(batch, 1, seq).astype(jnp.int32)
    cos, sin = _rope_tables(pos_col)
    bias_by_dist = _rel_bias_by_distance(rel_table).reshape(nh, 1, LANES)
    a0, c0, d0 = 0, 3 * nh, 6 * nh

    for l in range(DEPTH):
        w_main, w_rope_k, w_f_t = _prep_in_proj(w_in[l])
        h = _rmsnorm(xf, attn_norm_g[l], BF16)
        proj = _matmul(h, w_main, BF16, name="in_proj")
        k_rope = _matmul(h, w_rope_k, F32, name="in_proj_rope_k")
        f_t = _forget_logits(w_f_t, h)
        b_f = jnp.concatenate([fox_b_f[l].astype(F32), jnp.zeros((16 - nh,), F32)]).reshape(16, 1)
        cum = _forget_cum(f_t, b_f, batch, seq).reshape(batch, 16, 1, seq)

        lam_init = 0.8 - 0.6 * math.exp(-0.3 * l)
        lam_vecs = [v[l].reshape(1, DIFF_QK).astype(F32)
                    for v in (diff_lam_q1, diff_lam_k1, diff_lam_q2, diff_lam_k2)]
        o_a = _attn_call(
            functools.partial(_diff_kernel, t=ATT_T, lam_init=lam_init), batch, seq,
            lam_vecs + [proj, proj, proj, pos_col, pos_row, bias_by_dist,
                        diff_sub_g[l].reshape(1, HEAD_DIM).astype(F32)],
            [_full_spec((1, DIFF_QK))] * 4
            + [_q_spec(seq, a0), _kv_spec(seq, a0 + nh), _kv_spec(seq, a0 + 2 * nh),
               pl.BlockSpec((ATT_T, 1), lambda b, h, i: (b * (seq // ATT_T) + i, 0)),
               pl.BlockSpec((None, 1, seq), lambda b, h, i: (b, 0, 0)),
               pl.BlockSpec((None, 1, LANES), lambda b, h, i: (h, 0, 0)),
               _full_spec((1, HEAD_DIM))],
            "diff_attn")

        q_b = _mla_q(proj, mla_q_norm_g[l], _prep_uq(mla_w_uq[l]), cos, sin)
        kv_b, kpe_b = _mla_kv(proj, mla_kv_norm_g[l], _prep_ukv(mla_w_ukv[l]), k_rope, cos, sin)
        o_b = _attn_call(
            functools.partial(_mla_kernel, t=ATT_T), batch, seq,
            [q_b, q_b, kv_b, kpe_b, kv_b],
            [_q_spec(seq, 0), _q_spec(seq, nh), _kv_spec(seq, 0), _shared_spec(seq), _kv_spec(seq, nh)],
            "mla_attn")

        o_c = _attn_call(
            functools.partial(_sb_kernel, t=ATT_T), batch, seq,
            [proj, proj, proj],
            [_q_spec(seq, c0), _kv_spec(seq, c0 + nh), _kv_spec(seq, c0 + 2 * nh)],
            "stickbreak_attn")

        o_d = _attn_call(
            functools.partial(_fox_kernel, t=ATT_T), batch, seq,
            [proj, proj, proj, cum],
            [_q_spec(seq, d0), _kv_spec(seq, d0 + nh), _kv_spec(seq, d0 + 2 * nh),
             pl.BlockSpec((None, None, 1, seq), lambda b, h, i: (b, h, 0, 0))],
            "fox_attn")

        xf = _out_proj([o_a, o_b, o_c, o_d], w_out[l].astype(BF16), xf)

        mem_n = _rmsnorm(mem2, mem_kv_norm_g[l], BF16)
        k_m = _matmul(mem_n, mem_w_k[l].astype(BF16), BF16, name="mem_k")
        v_m = _matmul(mem_n, mem_w_v[l].astype(BF16), BF16, name="mem_v")
        xf = _mem_block(xf, mem_q_norm_g[l], (mem_w_q[l] * MEM_HEAD_DIM ** -0.5).astype(BF16),
                        k_m, v_m, mem_w_o[l].astype(BF16), batch, seq)

        h = _rmsnorm(xf, mlp_norm_g[l], BF16)
        u = _matmul(h, w_up[l].astype(BF16), BF16, relu2=True, name="mlp_up")
        xf = _matmul(u, w_down[l].astype(BF16), F32, res=xf, name="mlp_down")

    return _rmsnorm(xf, final_norm_g, F32).reshape(batch, seq, d)
```

```python
import functools
import math

import jax
import jax.numpy as jnp
from jax import lax
from jax.experimental import pallas as pl
from jax.experimental.pallas import tpu as pltpu

F32 = jnp.float32
BF16 = jnp.bfloat16

D_MODEL = 4096
DEPTH = 2
HEAD_DIM = 128
GROUP_HEADS = 8
GROUP_WIDTH = GROUP_HEADS * HEAD_DIM
DIFF_QK = HEAD_DIM // 2
MLA_Q_RANK = 1536
MLA_KV_RANK = 512
MLA_NOPE = 128
MLA_ROPE = 64
ROPE_THETA = 10000.0
REL_BUCKETS = 32
REL_MAX_DIST = 128
MEM_HEADS = 4
MEM_HEAD_DIM = 128
NORM_EPS = 1e-6
NEG_INF = -1e30

A_COLS = 3 * GROUP_WIDTH
B_COLS = MLA_Q_RANK + MLA_KV_RANK + MLA_ROPE
C_COLS = 3 * GROUP_WIDTH

LANES = 128
VMEM_LIMIT = 48 * 1024 * 1024

LOG2E = math.log2(math.e)

ATT_T = 512
SB_T = 256
HEADS_PER_STEP = 2
SB_DEAD_LOG2 = -160.0


def _cparams(sem):
    return pltpu.CompilerParams(dimension_semantics=sem, vmem_limit_bytes=VMEM_LIMIT)


def _dot(a, b):
    return jnp.dot(a, b, preferred_element_type=F32)


def _dot_nt(a, b):
    return lax.dot_general(a, b, (((1,), (1,)), ((), ())), preferred_element_type=F32)


def _rms(x, g):
    return x * lax.rsqrt(jnp.mean(x * x, axis=-1, keepdims=True) + NORM_EPS) * g


def _rmsnorm_kernel(x_ref, g_ref, o_ref):
    o_ref[...] = _rms(x_ref[...].astype(F32), g_ref[...]).astype(o_ref.dtype)


def _rmsnorm(x, g, out_dtype, tm=256):
    m, d = x.shape
    return pl.pallas_call(
        _rmsnorm_kernel,
        grid=(m // tm,),
        in_specs=[pl.BlockSpec((tm, d), lambda i: (i, 0)),
                  pl.BlockSpec((1, d), lambda i: (0, 0))],
        out_specs=pl.BlockSpec((tm, d), lambda i: (i, 0)),
        out_shape=jax.ShapeDtypeStruct((m, d), out_dtype),
        compiler_params=_cparams(("parallel",)),
        name="rmsnorm",
    )(x, g.reshape(1, d).astype(F32))


def _mm_kernel(a_ref, w_ref, *rest, nk, relu2, has_res):
    if has_res:
        res_ref, o_ref, acc_ref = rest
    else:
        o_ref, acc_ref = rest
    k = pl.program_id(2)

    @pl.when(k == 0)
    def _():
        acc_ref[...] = jnp.zeros_like(acc_ref)

    acc_ref[...] += _dot(a_ref[...], w_ref[...])

    @pl.when(k == nk - 1)
    def _():
        r = acc_ref[...]
        if relu2:
            r = jnp.square(jnp.maximum(r, 0.0))
        if has_res:
            r = r + res_ref[...]
        o_ref[...] = r.astype(o_ref.dtype)


def _matmul(a, w, out_dtype, *, res=None, relu2=False, tm=1024, tn=1024, tk=512, name="matmul"):
    m, kd = a.shape
    n = w.shape[1]
    tm, tn, tk = min(tm, m), min(tn, n), min(tk, kd)
    assert m % tm == 0 and n % tn == 0 and kd % tk == 0
    nk = kd // tk
    in_specs = [pl.BlockSpec((tm, tk), lambda i, j, k: (i, k)),
                pl.BlockSpec((tk, tn), lambda i, j, k: (k, j))]
    args = [a, w]
    if res is not None:
        in_specs.append(pl.BlockSpec((tm, tn), lambda i, j, k: (i, j)))
        args.append(res)
    return pl.pallas_call(
        functools.partial(_mm_kernel, nk=nk, relu2=relu2, has_res=res is not None),
        grid=(m // tm, n // tn, nk),
        in_specs=in_specs,
        out_specs=pl.BlockSpec((tm, tn), lambda i, j, k: (i, j)),
        out_shape=jax.ShapeDtypeStruct((m, n), out_dtype),
        scratch_shapes=[pltpu.VMEM((tm, tn), F32)],
        compiler_params=_cparams(("parallel", "parallel", "arbitrary")),
        name=name,
    )(*args)


def _out_proj_kernel(a0, a1, a2, a3, w_ref, res_ref, o_ref):
    acc = res_ref[...]
    for g, a_ref in enumerate((a0, a1, a2, a3)):
        acc = acc + _dot(a_ref[...], w_ref[g * GROUP_WIDTH:(g + 1) * GROUP_WIDTH, :])
    o_ref[...] = acc


def _out_proj(mixes, w, res, tm=512, tn=512):
    m = res.shape[0]
    n = w.shape[1]
    a_spec = pl.BlockSpec((tm, GROUP_WIDTH), lambda i, j: (i, 0))
    return pl.pallas_call(
        _out_proj_kernel,
        grid=(m // tm, n // tn),
        in_specs=[a_spec, a_spec, a_spec, a_spec,
                  pl.BlockSpec((w.shape[0], tn), lambda i, j: (0, j)),
                  pl.BlockSpec((tm, tn), lambda i, j: (i, j))],
        out_specs=pl.BlockSpec((tm, tn), lambda i, j: (i, j)),
        out_shape=jax.ShapeDtypeStruct((m, n), F32),
        compiler_params=_cparams(("parallel", "parallel")),
        name="out_proj",
    )(*mixes, w, res)


def _rope_kernel(pos_ref, tab_ref, cos_ref, sin_ref):
    ang = pos_ref[...].astype(F32) * tab_ref[0:1, :]
    cos_ref[...] = jnp.cos(ang) * tab_ref[1:2, :]
    sin_ref[...] = jnp.sin(ang) * tab_ref[2:3, :]


def _rope_tables(pos_col, tm=1024):
    m = pos_col.shape[0]
    half = MLA_ROPE // 2
    inv = ROPE_THETA ** (-jnp.arange(half, dtype=F32) * 2.0 / MLA_ROPE)
    zeros = jnp.zeros((LANES - MLA_ROPE,), F32)
    ones = jnp.ones((half,), F32)
    tab = jnp.stack([jnp.concatenate([inv, inv, zeros]),
                     jnp.concatenate([ones, ones, zeros]),
                     jnp.concatenate([-ones, ones, zeros])]
                    + [jnp.zeros((LANES,), F32)] * 5)
    spec = pl.BlockSpec((tm, LANES), lambda i: (i, 0))
    return pl.pallas_call(
        _rope_kernel,
        grid=(m // tm,),
        in_specs=[pl.BlockSpec((tm, 1), lambda i: (i, 0)),
                  pl.BlockSpec((8, LANES), lambda i: (0, 0))],
        out_specs=[spec, spec],
        out_shape=[jax.ShapeDtypeStruct((m, LANES), F32)] * 2,
        compiler_params=_cparams(("parallel",)),
        name="rope_tables",
    )(pos_col, tab)


def _mla_q_kernel(cq_ref, g_ref, w_ref, cos_ref, sin_ref, o_ref):
    n = _rms(cq_ref[...].astype(F32), g_ref[...]).astype(BF16)
    t = _dot(n, w_ref[...])
    o_ref[:, :GROUP_WIDTH] = t[:, :GROUP_WIDTH].astype(o_ref.dtype)
    cos, sin = cos_ref[...], sin_ref[...]
    for h in range(GROUP_HEADS):
        lo = GROUP_WIDTH + h * LANES
        pe = t[:, lo:lo + LANES] * cos + t[:, lo + GROUP_WIDTH:lo + GROUP_WIDTH + LANES] * sin
        o_ref[:, lo:lo + LANES] = pe.astype(o_ref.dtype)


def _mla_q(proj, g, w, cos, sin, tm=512):
    m = proj.shape[0]
    cq_block = (2 * A_COLS + C_COLS) // MLA_Q_RANK
    return pl.pallas_call(
        _mla_q_kernel,
        grid=(m // tm,),
        in_specs=[pl.BlockSpec((tm, MLA_Q_RANK), lambda i: (i, cq_block)),
                  pl.BlockSpec((1, MLA_Q_RANK), lambda i: (0, 0)),
                  pl.BlockSpec(w.shape, lambda i: (0, 0)),
                  pl.BlockSpec((tm, LANES), lambda i: (i, 0)),
                  pl.BlockSpec((tm, LANES), lambda i: (i, 0))],
        out_specs=pl.BlockSpec((tm, 2 * GROUP_WIDTH), lambda i: (i, 0)),
        out_shape=jax.ShapeDtypeStruct((m, 2 * GROUP_WIDTH), BF16),
        compiler_params=_cparams(("parallel",)),
        name="mla_q",
    )(proj, g.reshape(1, -1).astype(F32), w, cos, sin)


def _mla_kv_kernel(ckv_ref, g_ref, w_ref, kr_ref, cos_ref, sin_ref, kv_ref, kpe_ref):
    n = _rms(ckv_ref[...].astype(F32), g_ref[...]).astype(BF16)
    kv_ref[...] = _dot(n, w_ref[...]).astype(kv_ref.dtype)
    kr = kr_ref[...]
    kpe_ref[...] = (kr[:, :LANES] * cos_ref[...] + kr[:, LANES:] * sin_ref[...]).astype(kpe_ref.dtype)


def _mla_kv(proj, g, w, kr, cos, sin, tm=512):
    m = proj.shape[0]
    ckv_block = (2 * A_COLS + C_COLS + MLA_Q_RANK) // MLA_KV_RANK
    return pl.pallas_call(
        _mla_kv_kernel,
        grid=(m // tm,),
        in_specs=[pl.BlockSpec((tm, MLA_KV_RANK), lambda i: (i, ckv_block)),
                  pl.BlockSpec((1, MLA_KV_RANK), lambda i: (0, 0)),
                  pl.BlockSpec(w.shape, lambda i: (0, 0)),
                  pl.BlockSpec((tm, 2 * LANES), lambda i: (i, 0)),
                  pl.BlockSpec((tm, LANES), lambda i: (i, 0)),
                  pl.BlockSpec((tm, LANES), lambda i: (i, 0))],
        out_specs=[pl.BlockSpec((tm, 2 * GROUP_WIDTH), lambda i: (i, 0)),
                   pl.BlockSpec((tm, LANES), lambda i: (i, 0))],
        out_shape=[jax.ShapeDtypeStruct((m, 2 * GROUP_WIDTH), BF16),
                   jax.ShapeDtypeStruct((m, LANES), BF16)],
        compiler_params=_cparams(("parallel",)),
        name="mla_kv",
    )(proj, g.reshape(1, -1).astype(F32), w, kr, cos, sin)


def _forget_logit_kernel(w_ref, h_ref, o_ref):
    o_ref[...] = _dot_nt(w_ref[...], h_ref[...])


def _forget_logits(w_t, h, tm=1024):
    m, d = h.shape
    rows = w_t.shape[0]
    return pl.pallas_call(
        _forget_logit_kernel,
        grid=(m // tm,),
        in_specs=[pl.BlockSpec((rows, d), lambda i: (0, 0)),
                  pl.BlockSpec((tm, d), lambda i: (i, 0))],
        out_specs=pl.BlockSpec((rows, tm), lambda i: (0, i)),
        out_shape=jax.ShapeDtypeStruct((rows, m), F32),
        compiler_params=_cparams(("parallel",)),
        name="forget_logits",
    )(w_t, h)


def _split3(x):
    x1 = x.astype(BF16)
    r1 = x - x1.astype(F32)
    x2 = r1.astype(BF16)
    x3 = (r1 - x2.astype(F32)).astype(BF16)
    return x1, x2, x3


def _forget_cum_kernel(f_ref, b_ref, o_ref, *, chunk):
    x = f_ref[...] + b_ref[...]
    logf = (jnp.minimum(x, 0.0) - jnp.log1p(jnp.exp(-jnp.abs(x)))) * LOG2E
    r = lax.broadcasted_iota(jnp.int32, (chunk, chunk), 0)
    c = lax.broadcasted_iota(jnp.int32, (chunk, chunk), 1)
    tri = jnp.where(r <= c, 1.0, 0.0).astype(BF16)
    carry = jnp.zeros((x.shape[0], 1), F32)
    for ci in range(x.shape[1] // chunk):
        parts = _split3(logf[:, ci * chunk:(ci + 1) * chunk])
        cs = carry + _dot(parts[0], tri) + _dot(parts[1], tri) + _dot(parts[2], tri)
        o_ref[:, ci * chunk:(ci + 1) * chunk] = cs
        carry = cs[:, chunk - 1:chunk]


def _forget_cum(f_t, b_f, batch, seq):
    rows = f_t.shape[0]
    return pl.pallas_call(
        functools.partial(_forget_cum_kernel, chunk=256),
        grid=(batch,),
        in_specs=[pl.BlockSpec((rows, seq), lambda b: (0, b)),
                  pl.BlockSpec((rows, 1), lambda b: (0, 0))],
        out_specs=pl.BlockSpec((None, rows, seq), lambda b: (b, 0, 0)),
        out_shape=jax.ShapeDtypeStruct((batch, rows, seq), F32),
        compiler_params=_cparams(("parallel",)),
        name="forget_cum",
    )(f_t, b_f)


def _tile_iota(t):
    return (lax.broadcasted_iota(jnp.int32, (t, t), 0),
            lax.broadcasted_iota(jnp.int32, (t, t), 1))


def _softmax_update(s, v, m, acc):
    m_new = jnp.maximum(m, jnp.max(s, axis=-1, keepdims=True))
    alpha = jnp.exp2(m - m_new)
    p = jnp.exp2(s - m_new).astype(BF16)
    v_aug = jnp.concatenate([v, jnp.ones_like(v)], axis=1)
    return m_new, alpha * acc + _dot(p, v_aug)


def _softmax_init(t):
    return (jnp.full((t, 1), NEG_INF, F32), jnp.zeros((t, 2 * HEAD_DIM), F32))


def _softmax_out(acc):
    return acc[:, :HEAD_DIM] / acc[:, HEAD_DIM:]


def _hs(h):
    return slice(h * HEAD_DIM, (h + 1) * HEAD_DIM)


def _rows(j, t):
    return pl.ds(pl.multiple_of(j * t, t), t)


def _causal_sweep(step, qi, init):
    carry = lax.fori_loop(0, qi, lambda j, c: step(j, c, False), init)
    return step(qi, carry, True)


def _fox_kernel(q_ref, k_ref, v_ref, cum_ref, o_ref, *, t):
    qi = pl.program_id(2)
    row, col = _tile_iota(t)
    qs = [q_ref[:, _hs(h)] for h in range(HEADS_PER_STEP)]

    def step(j, carry, masked):
        out = []
        for h in range(HEADS_PER_STEP):
            s = _dot_nt(qs[h], k_ref[_rows(j, t), _hs(h)]) - cum_ref[h, :, _rows(j, t)]
            if masked:
                s = jnp.where(col <= row, s, NEG_INF)
            out.append(_softmax_update(s, v_ref[_rows(j, t), _hs(h)], *carry[h]))
        return tuple(out)

    carry = _causal_sweep(step, qi, tuple(_softmax_init(t) for _ in range(HEADS_PER_STEP)))
    o_ref[...] = jnp.concatenate([_softmax_out(c[1]) for c in carry], axis=1).astype(o_ref.dtype)


def _mla_kernel(qn_ref, qp_ref, kn_ref, kp_ref, v_ref, o_ref, *, t):
    qi = pl.program_id(2)
    row, col = _tile_iota(t)
    qs = [jnp.concatenate([qn_ref[:, _hs(h)], qp_ref[:, _hs(h)]], axis=1)
          for h in range(HEADS_PER_STEP)]

    def step(j, carry, masked):
        kp = kp_ref[_rows(j, t), :]
        out = []
        for h in range(HEADS_PER_STEP):
            s = _dot_nt(qs[h], jnp.concatenate([kn_ref[_rows(j, t), _hs(h)], kp], axis=1))
            if masked:
                s = jnp.where(col <= row, s, NEG_INF)
            out.append(_softmax_update(s, v_ref[_rows(j, t), _hs(h)], *carry[h]))
        return tuple(out)

    carry = _causal_sweep(step, qi, tuple(_softmax_init(t) for _ in range(HEADS_PER_STEP)))
    o_ref[...] = jnp.concatenate([_softmax_out(c[1]) for c in carry], axis=1).astype(o_ref.dtype)


def _diff_kernel(lq1_ref, lk1_ref, lq2_ref, lk2_ref, q_ref, k_ref, v_ref, posq_ref, posk_ref,
                 bias_ref, subg_ref, o_ref, *, t, lam_init):
    qi = pl.program_id(2)
    lam = (jnp.exp(jnp.sum(lq1_ref[...] * lk1_ref[...], axis=-1, keepdims=True))
           - jnp.exp(jnp.sum(lq2_ref[...] * lk2_ref[...], axis=-1, keepdims=True)) + lam_init)
    row, col = _tile_iota(t)
    lane = lax.broadcasted_iota(jnp.int32, (t, HEAD_DIM), 1)
    qs = []
    for h in range(HEADS_PER_STEP):
        q = q_ref[:, _hs(h)]
        zero = jnp.zeros_like(q)
        qs.append((jnp.where(lane < DIFF_QK, q, zero), jnp.where(lane >= DIFF_QK, q, zero)))
    posq = posq_ref[...]
    posq_min = jnp.min(posq)
    bias_rows = [jnp.broadcast_to(bias_ref[h], (t, LANES)) for h in range(HEADS_PER_STEP)]
    last = REL_MAX_DIST - 1

    def chains(j, carry, biases, masked):
        out = []
        for h in range(HEADS_PER_STEP):
            k = k_ref[_rows(j, t), _hs(h)]
            v = v_ref[_rows(j, t), _hs(h)]
            for c in range(2):
                s = _dot_nt(qs[h][c], k) + biases[h]
                if masked:
                    s = jnp.where(col <= row, s, NEG_INF)
                out.append(_softmax_update(s, v, *carry[2 * h + c]))
        return tuple(out)

    def step(j, carry, masked):
        posk = posk_ref[:, _rows(j, t)]

        def near(c):
            dist = jnp.clip(posq - posk, 0, last)
            biases = [jnp.concatenate(
                [jnp.take_along_axis(bias_rows[h], dist[:, cc * LANES:(cc + 1) * LANES], axis=1,
                                     mode="promise_in_bounds")
                 for cc in range(t // LANES)], axis=1) for h in range(HEADS_PER_STEP)]
            return chains(j, c, biases, masked)

        def far(c):
            return chains(j, c, [bias_ref[h][:, last:] for h in range(HEADS_PER_STEP)], masked)

        return lax.cond(posq_min - jnp.max(posk) < last, near, far, carry)

    carry = _causal_sweep(step, qi, tuple(_softmax_init(t) for _ in range(2 * HEADS_PER_STEP)))
    outs = []
    for h in range(HEADS_PER_STEP):
        o = _softmax_out(carry[2 * h][1]) - lam * _softmax_out(carry[2 * h + 1][1])
        outs.append(_rms(o, subg_ref[...]) * (1.0 - lam_init))
    o_ref[...] = jnp.concatenate(outs, axis=1).astype(o_ref.dtype)


def _sb_kernel(q_ref, k_ref, v_ref, o_ref, *, t):
    qi = pl.program_id(2)
    row, col = _tile_iota(t)
    later = jnp.where(row > col, 1.0, 0.0).astype(BF16)
    qs = [q_ref[:, _hs(h)] for h in range(HEADS_PER_STEP)]

    def step(j, carry, masked):
        out = []
        for h in range(HEADS_PER_STEP):
            tail, acc = carry[h]
            z = _dot_nt(qs[h], k_ref[_rows(j, t), _hs(h)])
            sp = jnp.log(1.0 + jnp.exp2(-jnp.abs(z))) * LOG2E
            log_beta = jnp.minimum(z, 0.0) - sp
            log_1m = -jnp.maximum(z, 0.0) - sp
            if masked:
                log_1m = jnp.where(col < row, log_1m, 0.0)
            hi = log_1m.astype(BF16)
            lo = (log_1m - hi.astype(F32)).astype(BF16)
            suffix = _dot(hi, later) + _dot(lo, later)
            a = jnp.exp2(log_beta + suffix + tail)
            if masked:
                a = jnp.where(col < row, a, 0.0)
            acc = acc + _dot(a.astype(BF16), v_ref[_rows(j, t), _hs(h)])
            tail = tail + suffix[:, :1] + log_1m[:, :1]
            out.append((tail, acc))
        return tuple(out)

    def live(carry):
        worst = carry[0][0]
        for h in range(1, HEADS_PER_STEP):
            worst = jnp.maximum(worst, carry[h][0])
        return (jnp.max(worst) > SB_DEAD_LOG2).astype(jnp.int32)

    init = tuple((jnp.zeros((t, 1), F32), jnp.zeros((t, HEAD_DIM), F32)) for _ in range(HEADS_PER_STEP))
    carry = step(qi, init, True)

    def body(state):
        j, _, c = state
        c = step(j, c, False)
        return j - 1, live(c), c

    _, _, carry = lax.while_loop(lambda st: (st[0] >= 0) & (st[1] > 0), body,
                                 (qi - 1, live(carry), carry))
    o_ref[...] = jnp.concatenate([c[1] for c in carry], axis=1).astype(o_ref.dtype)


def _attn_call(kernel, t, batch, seq, in_arrays, in_specs, name):
    nq = seq // t
    width = HEADS_PER_STEP * HEAD_DIM
    return pl.pallas_call(
        kernel,
        grid=(batch, GROUP_HEADS // HEADS_PER_STEP, nq),
        in_specs=in_specs,
        out_specs=pl.BlockSpec((t, width), lambda b, h, i: (b * nq + i, h)),
        out_shape=jax.ShapeDtypeStruct((batch * seq, GROUP_WIDTH), BF16),
        compiler_params=_cparams(("parallel", "parallel", "arbitrary")),
        name=name,
    )(*in_arrays)


def _q_spec(t, seq, col0):
    nq = seq // t
    return pl.BlockSpec((t, HEADS_PER_STEP * HEAD_DIM), lambda b, h, i: (b * nq + i, col0 + h))


def _kv_spec(seq, col0):
    return pl.BlockSpec((seq, HEADS_PER_STEP * HEAD_DIM), lambda b, h, i: (b, col0 + h))


def _shared_spec(seq):
    return pl.BlockSpec((seq, HEAD_DIM), lambda b, h, i: (b, 0))


def _full_spec(shape):
    return pl.BlockSpec(shape, lambda b, h, i: (0,) * len(shape))


def _mem_kernel(x_ref, g_ref, wq_ref, k_ref, v_ref, wo_ref, o_ref):
    x = x_ref[...]
    h = _rms(x, g_ref[...]).astype(BF16)
    q = _dot(h, wq_ref[...]).astype(BF16)
    outs = []
    for hd in range(MEM_HEADS):
        sl = slice(hd * MEM_HEAD_DIM, (hd + 1) * MEM_HEAD_DIM)
        s = _dot_nt(q[:, sl], k_ref[:, sl])
        p = jnp.exp2(s - jnp.max(s, axis=-1, keepdims=True))
        o = _dot(p.astype(BF16), v_ref[:, sl]) / jnp.sum(p, axis=-1, keepdims=True)
        outs.append(o.astype(BF16))
    o_ref[...] = x + _dot(jnp.concatenate(outs, axis=-1), wo_ref[...])


def _mem_block(x, g, wq, k, v, wo, batch, seq, tm=256):
    m, d = x.shape
    n_mem = k.shape[0] // batch
    tiles_per_batch = seq // tm
    width = MEM_HEADS * MEM_HEAD_DIM
    return pl.pallas_call(
        _mem_kernel,
        grid=(m // tm,),
        in_specs=[pl.BlockSpec((tm, d), lambda i: (i, 0)),
                  pl.BlockSpec((1, d), lambda i: (0, 0)),
                  pl.BlockSpec((d, width), lambda i: (0, 0)),
                  pl.BlockSpec((n_mem, width), lambda i: (i // tiles_per_batch, 0)),
                  pl.BlockSpec((n_mem, width), lambda i: (i // tiles_per_batch, 0)),
                  pl.BlockSpec((width, d), lambda i: (0, 0))],
        out_specs=pl.BlockSpec((tm, d), lambda i: (i, 0)),
        out_shape=jax.ShapeDtypeStruct((m, d), F32),
        compiler_params=_cparams(("parallel",)),
        name="mem_block",
    )(x, g.reshape(1, d).astype(F32), wq, k, v, wo)


def _rel_bias_by_distance(rel_table):
    d = jnp.arange(REL_MAX_DIST, dtype=jnp.int32)
    max_exact = REL_BUCKETS // 2
    nf = jnp.maximum(d, 1).astype(F32)
    large = max_exact + (jnp.log(nf / max_exact) / math.log(REL_MAX_DIST / max_exact)
                         * (REL_BUCKETS - max_exact)).astype(jnp.int32)
    large = jnp.minimum(large, REL_BUCKETS - 1)
    bucket = jnp.where(d < max_exact, d, large)
    return jnp.take(rel_table, bucket, axis=0).T.astype(F32)


def _prep_in_proj(w):
    gw = GROUP_WIDTH
    a = w[:, :A_COLS]
    b = w[:, A_COLS:A_COLS + B_COLS]
    c = w[:, A_COLS + B_COLS:A_COLS + B_COLS + C_COLS]
    d = w[:, A_COLS + B_COLS + C_COLS:A_COLS + B_COLS + C_COLS + 3 * gw]
    f = w[:, A_COLS + B_COLS + C_COLS + 3 * gw:]
    main = jnp.concatenate([
        a[:, :gw] * (DIFF_QK ** -0.5 * LOG2E), a[:, gw:],
        c[:, :gw] * (HEAD_DIM ** -0.5 * LOG2E), c[:, gw:],
        d[:, :gw] * (HEAD_DIM ** -0.5 * LOG2E), d[:, gw:],
        b[:, :MLA_Q_RANK + MLA_KV_RANK]], axis=1).astype(BF16)
    kr = b[:, MLA_Q_RANK + MLA_KV_RANK:]
    half = MLA_ROPE // 2
    pad = jnp.zeros((w.shape[0], LANES - MLA_ROPE), w.dtype)
    rope_k = jnp.concatenate([kr, pad, kr[:, half:], kr[:, :half], pad], axis=1).astype(BF16)
    f_t = jnp.concatenate([f.T, jnp.zeros((16 - GROUP_HEADS, w.shape[0]), w.dtype)], axis=0).astype(BF16)
    return main, rope_k, f_t


def _prep_uq(w):
    qk = MLA_NOPE + MLA_ROPE
    half = MLA_ROPE // 2
    w = (w * (qk ** -0.5 * LOG2E)).reshape(w.shape[0], GROUP_HEADS, qk)
    pad = jnp.zeros((w.shape[0], GROUP_HEADS, LANES - MLA_ROPE), w.dtype)
    nope = w[:, :, :MLA_NOPE]
    r1 = w[:, :, MLA_NOPE:MLA_NOPE + half]
    r2 = w[:, :, MLA_NOPE + half:]
    rope = jnp.concatenate([r1, r2, pad], axis=-1)
    swapped = jnp.concatenate([r2, r1, pad], axis=-1)
    return jnp.concatenate([nope.reshape(w.shape[0], -1), rope.reshape(w.shape[0], -1),
                            swapped.reshape(w.shape[0], -1)], axis=1).astype(BF16)


def _prep_ukv(w):
    w = w.reshape(w.shape[0], GROUP_HEADS, MLA_NOPE + HEAD_DIM)
    return jnp.concatenate([w[:, :, :MLA_NOPE].reshape(w.shape[0], -1),
                            w[:, :, MLA_NOPE:].reshape(w.shape[0], -1)], axis=1).astype(BF16)


def kernel(x, mem, positions, attn_norm_g, w_in, w_out, rel_table, diff_lam_q1, diff_lam_k1, diff_lam_q2, diff_lam_k2, diff_sub_g, mla_q_norm_g, mla_kv_norm_g, mla_w_uq, mla_w_ukv, fox_b_f, mem_q_norm_g, mem_kv_norm_g, mem_w_q, mem_w_k, mem_w_v, mem_w_o, mlp_norm_g, w_up, w_down, final_norm_g):
    batch, seq, d = x.shape
    m = batch * seq
    gw = GROUP_WIDTH
    nh = GROUP_HEADS
    xf = x.reshape(m, d)
    mem2 = mem.reshape(batch * mem.shape[1], d)
    pos_col = positions.reshape(m, 1).astype(jnp.int32)
    pos_row = positions.reshape(batch, 1, seq).astype(jnp.int32)
    cos, sin = _rope_tables(pos_col)
    bias_by_dist = (_rel_bias_by_distance(rel_table) * LOG2E).reshape(nh, 1, LANES)
    npair = nh // HEADS_PER_STEP
    a0, c0, d0 = 0, 3 * npair, 6 * npair

    for l in range(DEPTH):
        w_main, w_rope_k, w_f_t = _prep_in_proj(w_in[l])
        h = _rmsnorm(xf, attn_norm_g[l], BF16)
        proj = _matmul(h, w_main, BF16, name="in_proj")
        k_rope = _matmul(h, w_rope_k, F32, name="in_proj_rope_k")
        f_t = _forget_logits(w_f_t, h)
        b_f = jnp.concatenate([fox_b_f[l].astype(F32), jnp.zeros((16 - nh,), F32)]).reshape(16, 1)
        cum = _forget_cum(f_t, b_f, batch, seq).reshape(batch, 16, 1, seq)

        lam_init = 0.8 - 0.6 * math.exp(-0.3 * l)
        lam_vecs = [v[l].reshape(1, DIFF_QK).astype(F32)
                    for v in (diff_lam_q1, diff_lam_k1, diff_lam_q2, diff_lam_k2)]
        o_a = _attn_call(
            functools.partial(_diff_kernel, t=ATT_T, lam_init=lam_init), ATT_T, batch, seq,
            lam_vecs + [proj, proj, proj, pos_col, pos_row, bias_by_dist,
                        diff_sub_g[l].reshape(1, HEAD_DIM).astype(F32)],
            [_full_spec((1, DIFF_QK))] * 4
            + [_q_spec(ATT_T, seq, a0), _kv_spec(seq, a0 + npair), _kv_spec(seq, a0 + 2 * npair),
               pl.BlockSpec((ATT_T, 1), lambda b, h, i: (b * (seq // ATT_T) + i, 0)),
               pl.BlockSpec((None, 1, seq), lambda b, h, i: (b, 0, 0)),
               pl.BlockSpec((HEADS_PER_STEP, 1, LANES), lambda b, h, i: (h, 0, 0)),
               _full_spec((1, HEAD_DIM))],
            "diff_attn")

        q_b = _mla_q(proj, mla_q_norm_g[l], _prep_uq(mla_w_uq[l]), cos, sin)
        kv_b, kpe_b = _mla_kv(proj, mla_kv_norm_g[l], _prep_ukv(mla_w_ukv[l]), k_rope, cos, sin)
        o_b = _attn_call(
            functools.partial(_mla_kernel, t=ATT_T), ATT_T, batch, seq,
            [q_b, q_b, kv_b, kpe_b, kv_b],
            [_q_spec(ATT_T, seq, 0), _q_spec(ATT_T, seq, npair), _kv_spec(seq, 0), _shared_spec(seq),
             _kv_spec(seq, npair)],
            "mla_attn")

        o_c = _attn_call(
            functools.partial(_sb_kernel, t=SB_T), SB_T, batch, seq,
            [proj, proj, proj],
            [_q_spec(SB_T, seq, c0), _kv_spec(seq, c0 + npair), _kv_spec(seq, c0 + 2 * npair)],
            "stickbreak_attn")

        o_d = _attn_call(
            functools.partial(_fox_kernel, t=ATT_T), ATT_T, batch, seq,
            [proj, proj, proj, cum],
            [_q_spec(ATT_T, seq, d0), _kv_spec(seq, d0 + npair), _kv_spec(seq, d0 + 2 * npair),
             pl.BlockSpec((None, HEADS_PER_STEP, 1, seq), lambda b, h, i: (b, h, 0, 0))],
            "fox_attn")

        xf = _out_proj([o_a, o_b, o_c, o_d], w_out[l].astype(BF16), xf)

        mem_n = _rmsnorm(mem2, mem_kv_norm_g[l], BF16)
        k_m = _matmul(mem_n, mem_w_k[l].astype(BF16), BF16, name="mem_k")
        v_m = _matmul(mem_n, mem_w_v[l].astype(BF16), BF16, name="mem_v")
        xf = _mem_block(xf, mem_q_norm_g[l], (mem_w_q[l] * (MEM_HEAD_DIM ** -0.5 * LOG2E)).astype(BF16),
                        k_m, v_m, mem_w_o[l].astype(BF16), batch, seq)

        h = _rmsnorm(xf, mlp_norm_g[l], BF16)
        u = _matmul(h, w_up[l].astype(BF16), BF16, relu2=True, name="mlp_up")
        xf = _matmul(u, w_down[l].astype(BF16), F32, res=xf, name="mlp_down")

    return _rmsnorm(xf, final_norm_g, F32).reshape(batch, seq, d)
```

```python
import functools
import math

import jax
import jax.numpy as jnp
from jax import lax
from jax.experimental import pallas as pl
from jax.experimental.pallas import tpu as pltpu

F32 = jnp.float32
BF16 = jnp.bfloat16

D_MODEL = 4096
DEPTH = 2
HEAD_DIM = 128
GROUP_HEADS = 8
GROUP_WIDTH = GROUP_HEADS * HEAD_DIM
DIFF_QK = HEAD_DIM // 2
MLA_Q_RANK = 1536
MLA_KV_RANK = 512
MLA_NOPE = 128
MLA_ROPE = 64
ROPE_THETA = 10000.0
REL_BUCKETS = 32
REL_MAX_DIST = 128
MEM_HEADS = 4
MEM_HEAD_DIM = 128
NORM_EPS = 1e-6
NEG_INF = -1e30

A_COLS = 3 * GROUP_WIDTH
B_COLS = MLA_Q_RANK + MLA_KV_RANK + MLA_ROPE
C_COLS = 3 * GROUP_WIDTH

LANES = 128
VMEM_LIMIT = 48 * 1024 * 1024

LOG2E = math.log2(math.e)

ATT_T = 512
SB_T = 256
HEADS_PER_STEP = 2
SB_DEAD_LOG2 = -160.0
TRUSTED_ROW_SUM = 2.0 ** -60


def _cparams(sem):
    return pltpu.CompilerParams(dimension_semantics=sem, vmem_limit_bytes=VMEM_LIMIT)


def _dot(a, b):
    return jnp.dot(a, b, preferred_element_type=F32)


def _dot_nt(a, b):
    return lax.dot_general(a, b, (((1,), (1,)), ((), ())), preferred_element_type=F32)


def _rms(x, g):
    return x * lax.rsqrt(jnp.mean(x * x, axis=-1, keepdims=True) + NORM_EPS) * g


def _rmsnorm_kernel(x_ref, g_ref, o_ref):
    o_ref[...] = _rms(x_ref[...].astype(F32), g_ref[...]).astype(o_ref.dtype)


def _rmsnorm(x, g, out_dtype, tm=256):
    m, d = x.shape
    return pl.pallas_call(
        _rmsnorm_kernel,
        grid=(m // tm,),
        in_specs=[pl.BlockSpec((tm, d), lambda i: (i, 0)),
                  pl.BlockSpec((1, d), lambda i: (0, 0))],
        out_specs=pl.BlockSpec((tm, d), lambda i: (i, 0)),
        out_shape=jax.ShapeDtypeStruct((m, d), out_dtype),
        compiler_params=_cparams(("parallel",)),
        name="rmsnorm",
    )(x, g.reshape(1, d).astype(F32))


def _mm_epilogue(r, res_ref, o_ref, relu2):
    if relu2:
        r = jnp.square(jnp.maximum(r, 0.0))
    if res_ref is not None:
        r = r + res_ref[...]
    o_ref[...] = r.astype(o_ref.dtype)


def _mm_kernel(a_ref, w_ref, *rest, nk, relu2, has_res):
    res_ref = rest[0] if has_res else None
    o_ref = rest[1] if has_res else rest[0]
    if nk == 1:
        _mm_epilogue(_dot(a_ref[...], w_ref[...]), res_ref, o_ref, relu2)
        return
    acc_ref = rest[-1]
    k = pl.program_id(2)

    @pl.when(k == 0)
    def _():
        acc_ref[...] = _dot(a_ref[...], w_ref[...])

    @pl.when(k > 0)
    def _():
        acc_ref[...] += _dot(a_ref[...], w_ref[...])

    @pl.when(k == nk - 1)
    def _():
        _mm_epilogue(acc_ref[...], res_ref, o_ref, relu2)


def _matmul(a, w, out_dtype, *, res=None, relu2=False, tm=1024, tn=512, tk=4096, name="matmul"):
    m, kd = a.shape
    n = w.shape[1]
    tm, tn, tk = min(tm, m), min(tn, n), min(tk, kd)
    assert m % tm == 0 and n % tn == 0 and kd % tk == 0
    nk = kd // tk
    in_specs = [pl.BlockSpec((tm, tk), lambda i, j, k: (i, k)),
                pl.BlockSpec((tk, tn), lambda i, j, k: (k, j))]
    args = [a, w]
    if res is not None:
        in_specs.append(pl.BlockSpec((tm, tn), lambda i, j, k: (i, j)))
        args.append(res)
    return pl.pallas_call(
        functools.partial(_mm_kernel, nk=nk, relu2=relu2, has_res=res is not None),
        grid=(m // tm, n // tn, nk),
        in_specs=in_specs,
        out_specs=pl.BlockSpec((tm, tn), lambda i, j, k: (i, j)),
        out_shape=jax.ShapeDtypeStruct((m, n), out_dtype),
        scratch_shapes=[pltpu.VMEM((tm, tn), F32)] if nk > 1 else [],
        compiler_params=_cparams(("parallel", "parallel", "arbitrary")),
        name=name,
    )(*args)


def _out_proj_kernel(a0, a1, a2, a3, w_ref, res_ref, o_ref):
    acc = res_ref[...]
    for g, a_ref in enumerate((a0, a1, a2, a3)):
        acc = acc + _dot(a_ref[...], w_ref[g * GROUP_WIDTH:(g + 1) * GROUP_WIDTH, :])
    o_ref[...] = acc


def _out_proj(mixes, w, res, tm=512, tn=512):
    m = res.shape[0]
    n = w.shape[1]
    a_spec = pl.BlockSpec((tm, GROUP_WIDTH), lambda i, j: (i, 0))
    return pl.pallas_call(
        _out_proj_kernel,
        grid=(m // tm, n // tn),
        in_specs=[a_spec, a_spec, a_spec, a_spec,
                  pl.BlockSpec((w.shape[0], tn), lambda i, j: (0, j)),
                  pl.BlockSpec((tm, tn), lambda i, j: (i, j))],
        out_specs=pl.BlockSpec((tm, tn), lambda i, j: (i, j)),
        out_shape=jax.ShapeDtypeStruct((m, n), F32),
        compiler_params=_cparams(("parallel", "parallel")),
        name="out_proj",
    )(*mixes, w, res)


def _rope_kernel(pos_ref, tab_ref, cos_ref, sin_ref):
    ang = pos_ref[...].astype(F32) * tab_ref[0:1, :]
    cos_ref[...] = jnp.cos(ang) * tab_ref[1:2, :]
    sin_ref[...] = jnp.sin(ang) * tab_ref[2:3, :]


def _rope_tables(pos_col, tm=1024):
    m = pos_col.shape[0]
    half = MLA_ROPE // 2
    inv = ROPE_THETA ** (-jnp.arange(half, dtype=F32) * 2.0 / MLA_ROPE)
    zeros = jnp.zeros((LANES - MLA_ROPE,), F32)
    ones = jnp.ones((half,), F32)
    tab = jnp.stack([jnp.concatenate([inv, inv, zeros]),
                     jnp.concatenate([ones, ones, zeros]),
                     jnp.concatenate([-ones, ones, zeros])]
                    + [jnp.zeros((LANES,), F32)] * 5)
    spec = pl.BlockSpec((tm, LANES), lambda i: (i, 0))
    return pl.pallas_call(
        _rope_kernel,
        grid=(m // tm,),
        in_specs=[pl.BlockSpec((tm, 1), lambda i: (i, 0)),
                  pl.BlockSpec((8, LANES), lambda i: (0, 0))],
        out_specs=[spec, spec],
        out_shape=[jax.ShapeDtypeStruct((m, LANES), F32)] * 2,
        compiler_params=_cparams(("parallel",)),
        name="rope_tables",
    )(pos_col, tab)


def _mla_q_kernel(cq_ref, g_ref, w_ref, cos_ref, sin_ref, o_ref):
    n = _rms(cq_ref[...].astype(F32), g_ref[...]).astype(BF16)
    t = _dot(n, w_ref[...])
    o_ref[:, :GROUP_WIDTH] = t[:, :GROUP_WIDTH].astype(o_ref.dtype)
    cos, sin = cos_ref[...], sin_ref[...]
    for h in range(GROUP_HEADS):
        lo = GROUP_WIDTH + h * LANES
        pe = t[:, lo:lo + LANES] * cos + t[:, lo + GROUP_WIDTH:lo + GROUP_WIDTH + LANES] * sin
        o_ref[:, lo:lo + LANES] = pe.astype(o_ref.dtype)


def _mla_q(proj, g, w, cos, sin, tm=512):
    m = proj.shape[0]
    cq_block = (2 * A_COLS + C_COLS) // MLA_Q_RANK
    return pl.pallas_call(
        _mla_q_kernel,
        grid=(m // tm,),
        in_specs=[pl.BlockSpec((tm, MLA_Q_RANK), lambda i: (i, cq_block)),
                  pl.BlockSpec((1, MLA_Q_RANK), lambda i: (0, 0)),
                  pl.BlockSpec(w.shape, lambda i: (0, 0)),
                  pl.BlockSpec((tm, LANES), lambda i: (i, 0)),
                  pl.BlockSpec((tm, LANES), lambda i: (i, 0))],
        out_specs=pl.BlockSpec((tm, 2 * GROUP_WIDTH), lambda i: (i, 0)),
        out_shape=jax.ShapeDtypeStruct((m, 2 * GROUP_WIDTH), BF16),
        compiler_params=_cparams(("parallel",)),
        name="mla_q",
    )(proj, g.reshape(1, -1).astype(F32), w, cos, sin)


def _mla_kv_kernel(ckv_ref, g_ref, w_ref, kr_ref, cos_ref, sin_ref, kv_ref, kpe_ref):
    n = _rms(ckv_ref[...].astype(F32), g_ref[...]).astype(BF16)
    kv_ref[...] = _dot(n, w_ref[...]).astype(kv_ref.dtype)
    kr = kr_ref[...]
    kpe = kr[:, :LANES] * cos_ref[...] + kr[:, LANES:] * sin_ref[...]
    lane = lax.broadcasted_iota(jnp.int32, kpe.shape, 1)
    kpe = jnp.where((lane >= MLA_ROPE) & (lane < MLA_ROPE + 3), 1.0, kpe)
    kpe_ref[...] = kpe.astype(kpe_ref.dtype)


def _mla_kv(proj, g, w, kr, cos, sin, tm=512):
    m = proj.shape[0]
    ckv_block = (2 * A_COLS + C_COLS + MLA_Q_RANK) // MLA_KV_RANK
    return pl.pallas_call(
        _mla_kv_kernel,
        grid=(m // tm,),
        in_specs=[pl.BlockSpec((tm, MLA_KV_RANK), lambda i: (i, ckv_block)),
                  pl.BlockSpec((1, MLA_KV_RANK), lambda i: (0, 0)),
                  pl.BlockSpec(w.shape, lambda i: (0, 0)),
                  pl.BlockSpec((tm, 2 * LANES), lambda i: (i, 0)),
                  pl.BlockSpec((tm, LANES), lambda i: (i, 0)),
                  pl.BlockSpec((tm, LANES), lambda i: (i, 0))],
        out_specs=[pl.BlockSpec((tm, 2 * GROUP_WIDTH), lambda i: (i, 0)),
                   pl.BlockSpec((tm, LANES), lambda i: (i, 0))],
        out_shape=[jax.ShapeDtypeStruct((m, 2 * GROUP_WIDTH), BF16),
                   jax.ShapeDtypeStruct((m, LANES), BF16)],
        compiler_params=_cparams(("parallel",)),
        name="mla_kv",
    )(proj, g.reshape(1, -1).astype(F32), w, kr, cos, sin)


def _split3(x):
    x1 = x.astype(BF16)
    r1 = x - x1.astype(F32)
    x2 = r1.astype(BF16)
    x3 = (r1 - x2.astype(F32)).astype(BF16)
    return x1, x2, x3


def _place(lane, cols, base, fill):
    out = fill
    for i, c in enumerate(cols):
        out = jnp.where(lane == base + i, c.astype(F32), out)
    return out


def _forget_cum_kernel(f_ref, b_ref, cum_ref, kx_ref, *, chunk):
    r = lax.broadcasted_iota(jnp.int32, (chunk, chunk), 0)
    c = lax.broadcasted_iota(jnp.int32, (chunk, chunk), 1)
    tri = jnp.where(r >= c, 1.0, 0.0).astype(BF16)
    lane = lax.broadcasted_iota(jnp.int32, (chunk, LANES), 1)
    unit_fill = jnp.where(lane < 3, 1.0, 0.0)

    def body(ci, carry):
        rows = _rows(ci, chunk)
        x = f_ref[rows, :] + b_ref[...]
        logf = (jnp.minimum(x, 0.0) - jnp.log1p(jnp.exp(-jnp.abs(x)))) * LOG2E
        parts = _split3(logf)
        cs = carry + _dot(tri, parts[0]) + _dot(tri, parts[1]) + _dot(tri, parts[2])
        cum_ref[rows, :] = cs
        for h in range(GROUP_HEADS):
            kx_ref[h, rows, :] = _place(lane, _split3(cs[:, h:h + 1]), 3, unit_fill).astype(kx_ref.dtype)
        return cs[chunk - 1:chunk, :]

    lax.fori_loop(0, f_ref.shape[0] // chunk, body, jnp.zeros((1, LANES), F32))


def _forget_cum(side, b_f, batch, seq):
    f_block = 2
    return pl.pallas_call(
        functools.partial(_forget_cum_kernel, chunk=256),
        grid=(batch,),
        in_specs=[pl.BlockSpec((seq, LANES), lambda b: (b, f_block)),
                  pl.BlockSpec((1, LANES), lambda b: (0, 0))],
        out_specs=[pl.BlockSpec((seq, LANES), lambda b: (b, 0)),
                   pl.BlockSpec((None, GROUP_HEADS, seq, LANES), lambda b: (b, 0, 0, 0))],
        out_shape=[jax.ShapeDtypeStruct((batch * seq, LANES), F32),
                   jax.ShapeDtypeStruct((batch, GROUP_HEADS, seq, LANES), BF16)],
        compiler_params=_cparams(("parallel",)),
        name="forget_cum",
    )(side, b_f)


def _tile_iota(t):
    return (lax.broadcasted_iota(jnp.int32, (t, t), 0),
            lax.broadcasted_iota(jnp.int32, (t, t), 1))


def _softmax_update(s, v, m, acc):
    m_new = jnp.maximum(m, jnp.max(s, axis=-1, keepdims=True))
    alpha = jnp.exp2(m - m_new)
    p = jnp.exp2(s - m_new).astype(BF16)
    v_aug = jnp.concatenate([v, jnp.ones_like(v)], axis=1)
    return m_new, alpha * acc + _dot(p, v_aug)


def _softmax_init(t):
    return (jnp.full((t, 1), NEG_INF, F32), jnp.zeros((t, 2 * HEAD_DIM), F32))


def _softmax_out(acc):
    return acc[:, :HEAD_DIM] / acc[:, HEAD_DIM:]


def _hs(h):
    return slice(h * HEAD_DIM, (h + 1) * HEAD_DIM)


def _rows(j, t):
    return pl.ds(pl.multiple_of(j * t, t), t)


def _causal_sweep(step, qi, init):
    carry = lax.fori_loop(0, qi, lambda j, c: step(j, c, False), init)
    return step(qi, carry, True)


def _update(s, v, state, mask, fast):
    if mask is not None:
        s = jnp.where(mask, s, NEG_INF)
    if not fast:
        return _softmax_update(s, v, *state)
    v_aug = jnp.concatenate([v, jnp.ones_like(v)], axis=1)
    return state + _dot(jnp.exp2(s).astype(BF16), v_aug)


def _run_softmax(qi, t, n_chains, make_step, finish):
    zeros = tuple(jnp.zeros((t, 2 * HEAD_DIM), F32) for _ in range(n_chains))
    accs = _causal_sweep(make_step(True), qi, zeros)
    l_min = accs[0][:, HEAD_DIM:HEAD_DIM + 1]
    for a in accs[1:]:
        l_min = jnp.minimum(l_min, a[:, HEAD_DIM:HEAD_DIM + 1])
    ok = jnp.min(l_min) >= TRUSTED_ROW_SUM

    @pl.when(ok)
    def _():
        finish([_softmax_out(a) for a in accs])

    @pl.when(jnp.logical_not(ok))
    def _():
        carry = _causal_sweep(make_step(False), qi, tuple(_softmax_init(t) for _ in range(n_chains)))
        finish([_softmax_out(c[1]) for c in carry])


def _key_norm_max(k_ref, h, seq, extra_ref=None):
    chunk = ATT_T

    def body(c, mx):
        kk = k_ref[_rows(c, chunk), _hs(h)].astype(F32)
        n2 = jnp.sum(kk * kk, axis=1, keepdims=True)
        if extra_ref is not None:
            e = extra_ref[_rows(c, chunk), :].astype(F32)
            n2 = n2 + jnp.sum(e * e, axis=1, keepdims=True)
        return jnp.maximum(mx, n2)

    mx = lax.fori_loop(0, seq // chunk, body, jnp.zeros((chunk, 1), F32))
    return jnp.sqrt(jnp.max(mx, axis=0, keepdims=True))


def _store_key_norms(kmax_ref, qi, norm_fn):
    @pl.when(qi == 0)
    def _():
        for h in range(HEADS_PER_STEP):
            kmax_ref[h] = jnp.broadcast_to(norm_fn(h), kmax_ref.shape[1:])


def _row_norm(x):
    xf = x.astype(F32)
    return jnp.sqrt(jnp.sum(xf * xf, axis=1, keepdims=True))


def _fox_kernel(q_ref, k_ref, v_ref, kx_ref, cumq_ref, o_ref, kmax_ref, *, t, seq):
    pair, qi = pl.program_id(1), pl.program_id(2)
    _store_key_norms(kmax_ref, qi, lambda h: _key_norm_max(k_ref, h, seq))
    row, col = _tile_iota(t)
    lane = lax.broadcasted_iota(jnp.int32, (t, LANES), 1)
    minus_fill = jnp.where((lane >= 3) & (lane < 6), -1.0, 0.0)
    cumq = cumq_ref[...]

    def q_aug(h, fast):
        q = q_ref[:, _hs(h)]
        c = jnp.sum(jnp.where(lane == pair * HEADS_PER_STEP + h, cumq, 0.0), axis=1, keepdims=True)
        if fast:
            c = c - _row_norm(q) * kmax_ref[h][0:1, 0:1]
        return jnp.concatenate([q, _place(lane, _split3(c), 0, minus_fill).astype(BF16)], axis=1)

    def make_step(fast):
        qs = [q_aug(h, fast) for h in range(HEADS_PER_STEP)]

        def step(j, carry, masked):
            mask = (col <= row) if masked else None
            out = []
            for h in range(HEADS_PER_STEP):
                k = jnp.concatenate([k_ref[_rows(j, t), _hs(h)], kx_ref[h, _rows(j, t), :]], axis=1)
                out.append(_update(_dot_nt(qs[h], k), v_ref[_rows(j, t), _hs(h)], carry[h], mask, fast))
            return tuple(out)

        return step

    def finish(outs):
        o_ref[...] = jnp.concatenate(outs, axis=1).astype(o_ref.dtype)

    _run_softmax(qi, t, HEADS_PER_STEP, make_step, finish)


def _mla_kernel(qn_ref, qp_ref, kn_ref, kp_ref, v_ref, o_ref, kmax_ref, *, t, seq):
    qi = pl.program_id(2)
    _store_key_norms(kmax_ref, qi, lambda h: _key_norm_max(kn_ref, h, seq, kp_ref))
    row, col = _tile_iota(t)
    lane = lax.broadcasted_iota(jnp.int32, (t, LANES), 1)

    def q_aug(h, fast):
        qn, qp = qn_ref[:, _hs(h)], qp_ref[:, _hs(h)]
        if fast:
            qn_f, qp_f = qn.astype(F32), qp.astype(F32)
            norm = jnp.sqrt(jnp.sum(qn_f * qn_f, axis=1, keepdims=True)
                            + jnp.sum(qp_f * qp_f, axis=1, keepdims=True))
            qp = _place(lane, _split3(-norm * kmax_ref[h][0:1, 0:1]), MLA_ROPE, qp_f).astype(BF16)
        return jnp.concatenate([qn, qp], axis=1)

    def make_step(fast):
        qs = [q_aug(h, fast) for h in range(HEADS_PER_STEP)]

        def step(j, carry, masked):
            mask = (col <= row) if masked else None
            kp = kp_ref[_rows(j, t), :]
            out = []
            for h in range(HEADS_PER_STEP):
                k = jnp.concatenate([kn_ref[_rows(j, t), _hs(h)], kp], axis=1)
                out.append(_update(_dot_nt(qs[h], k), v_ref[_rows(j, t), _hs(h)], carry[h], mask, fast))
            return tuple(out)

        return step

    def finish(outs):
        o_ref[...] = jnp.concatenate(outs, axis=1).astype(o_ref.dtype)

    _run_softmax(qi, t, HEADS_PER_STEP, make_step, finish)


def _diff_kernel(lq1_ref, lk1_ref, lq2_ref, lk2_ref, q_ref, k_ref, v_ref, posq_ref, posk_ref,
                 bias_ref, subg_ref, o_ref, kmax_ref, *, t, seq, lam_init):
    qi = pl.program_id(2)
    _store_key_norms(kmax_ref, qi, lambda h: _key_norm_max(k_ref, h, seq))
    lam = (jnp.exp(jnp.sum(lq1_ref[...] * lk1_ref[...], axis=-1, keepdims=True))
           - jnp.exp(jnp.sum(lq2_ref[...] * lk2_ref[...], axis=-1, keepdims=True)) + lam_init)
    row, col = _tile_iota(t)
    lane = lax.broadcasted_iota(jnp.int32, (t, LANES), 1)
    zero_fill = jnp.zeros((t, LANES), F32)
    key_units = jnp.where(lane < 3, 1.0, 0.0).astype(BF16)
    posq = posq_ref[...]
    posq_min = jnp.min(posq)
    bias_rows = [jnp.broadcast_to(bias_ref[h], (t, LANES)) for h in range(HEADS_PER_STEP)]
    last = REL_MAX_DIST - 1
    far_bias = [bias_ref[h][:, last:] for h in range(HEADS_PER_STEP)]

    def half_q(h, c):
        q = q_ref[:, _hs(h)]
        return jnp.where((lane < DIFF_QK) if c == 0 else (lane >= DIFF_QK), q, jnp.zeros_like(q))

    def make_step(fast):
        q_near, q_far = [], []
        for h in range(HEADS_PER_STEP):
            for c in range(2):
                qz = half_q(h, c)
                if fast:
                    bound = (_row_norm(qz) * kmax_ref[h][0:1, 0:1]
                             + jnp.max(bias_ref[h], axis=1, keepdims=True))
                    q_near.append(jnp.concatenate(
                        [qz, _place(lane, _split3(-bound), 0, zero_fill).astype(BF16)], axis=1))
                    q_far.append(jnp.concatenate(
                        [qz, _place(lane, _split3(far_bias[h] - bound), 0, zero_fill).astype(BF16)], axis=1))
                else:
                    q_near.append(qz)
                    q_far.append(qz)

        def chains(j, carry, masked, gathered):
            mask = (col <= row) if masked else None
            out = []
            for h in range(HEADS_PER_STEP):
                k = k_ref[_rows(j, t), _hs(h)]
                if fast:
                    k = jnp.concatenate([k, key_units], axis=1)
                v = v_ref[_rows(j, t), _hs(h)]
                for c in range(2):
                    i = 2 * h + c
                    if gathered is not None:
                        s = _dot_nt(q_near[i], k) + gathered[h]
                    elif fast:
                        s = _dot_nt(q_far[i], k)
                    else:
                        s = _dot_nt(q_far[i], k) + far_bias[h]
                    out.append(_update(s, v, carry[i], mask, fast))
            return tuple(out)

        def step(j, carry, masked):
            posk = posk_ref[:, _rows(j, t)]

            def near(c):
                dist = jnp.clip(posq - posk, 0, last)
                gathered = [jnp.concatenate(
                    [jnp.take_along_axis(bias_rows[h], dist[:, cc * LANES:(cc + 1) * LANES], axis=1,
                                         mode="promise_in_bounds")
                     for cc in range(t // LANES)], axis=1) for h in range(HEADS_PER_STEP)]
                return chains(j, c, masked, gathered)

            def far(c):
                return chains(j, c, masked, None)

            return lax.cond(posq_min - jnp.max(posk) < last, near, far, carry)

        return step

    def finish(outs):
        normed = []
        for h in range(HEADS_PER_STEP):
            o = outs[2 * h] - lam * outs[2 * h + 1]
            normed.append(_rms(o, subg_ref[...]) * (1.0 - lam_init))
        o_ref[...] = jnp.concatenate(normed, axis=1).astype(o_ref.dtype)

    _run_softmax(qi, t, 2 * HEADS_PER_STEP, make_step, finish)


def _sb_kernel(q_ref, k_ref, v_ref, o_ref, *, t):
    qi = pl.program_id(2)
    row, col = _tile_iota(t)
    later = jnp.where(row > col, 1.0, 0.0).astype(BF16)
    qs = [q_ref[:, _hs(h)] for h in range(HEADS_PER_STEP)]

    def step(j, carry, masked):
        out = []
        for h in range(HEADS_PER_STEP):
            tail, acc = carry[h]
            z = _dot_nt(qs[h], k_ref[_rows(j, t), _hs(h)])
            sp = jnp.log(1.0 + jnp.exp2(-jnp.abs(z))) * LOG2E
            log_beta = jnp.minimum(z, 0.0) - sp
            log_1m = -jnp.maximum(z, 0.0) - sp
            if masked:
                log_1m = jnp.where(col < row, log_1m, 0.0)
            hi = log_1m.astype(BF16)
            lo = (log_1m - hi.astype(F32)).astype(BF16)
            suffix = _dot(hi, later) + _dot(lo, later)
            a = jnp.exp2(log_beta + suffix + tail)
            if masked:
                a = jnp.where(col < row, a, 0.0)
            acc = acc + _dot(a.astype(BF16), v_ref[_rows(j, t), _hs(h)])
            tail = tail + suffix[:, :1] + log_1m[:, :1]
            out.append((tail, acc))
        return tuple(out)

    def live(carry):
        worst = carry[0][0]
        for h in range(1, HEADS_PER_STEP):
            worst = jnp.maximum(worst, carry[h][0])
        return (jnp.max(worst) > SB_DEAD_LOG2).astype(jnp.int32)

    init = tuple((jnp.zeros((t, 1), F32), jnp.zeros((t, HEAD_DIM), F32)) for _ in range(HEADS_PER_STEP))
    carry = step(qi, init, True)

    def body(state):
        j, _, c = state
        c = step(j, c, False)
        return j - 1, live(c), c

    _, _, carry = lax.while_loop(lambda st: (st[0] >= 0) & (st[1] > 0), body,
                                 (qi - 1, live(carry), carry))
    o_ref[...] = jnp.concatenate([c[1] for c in carry], axis=1).astype(o_ref.dtype)


def _attn_call(kernel, t, batch, seq, in_arrays, in_specs, name, key_norm_scratch=True):
    nq = seq // t
    width = HEADS_PER_STEP * HEAD_DIM
    scratch = [pltpu.VMEM((HEADS_PER_STEP, 8, LANES), F32)] if key_norm_scratch else []
    return pl.pallas_call(
        kernel,
        grid=(batch, GROUP_HEADS // HEADS_PER_STEP, nq),
        in_specs=in_specs,
        out_specs=pl.BlockSpec((t, width), lambda b, h, i: (b * nq + i, h)),
        out_shape=jax.ShapeDtypeStruct((batch * seq, GROUP_WIDTH), BF16),
        scratch_shapes=scratch,
        compiler_params=_cparams(("arbitrary", "arbitrary", "arbitrary")),
        name=name,
    )(*in_arrays)


def _q_spec(t, seq, col0):
    nq = seq // t
    return pl.BlockSpec((t, HEADS_PER_STEP * HEAD_DIM), lambda b, h, i: (b * nq + i, col0 + h))


def _kv_spec(seq, col0):
    return pl.BlockSpec((seq, HEADS_PER_STEP * HEAD_DIM), lambda b, h, i: (b, col0 + h))


def _shared_spec(seq):
    return pl.BlockSpec((seq, HEAD_DIM), lambda b, h, i: (b, 0))


def _full_spec(shape):
    return pl.BlockSpec(shape, lambda b, h, i: (0,) * len(shape))


def _mem_kernel(x_ref, g_ref, wq_ref, k_ref, v_ref, wo_ref, o_ref):
    x = x_ref[...]
    h = _rms(x, g_ref[...]).astype(BF16)
    q = _dot(h, wq_ref[...]).astype(BF16)
    outs = []
    for hd in range(MEM_HEADS):
        sl = slice(hd * MEM_HEAD_DIM, (hd + 1) * MEM_HEAD_DIM)
        s = _dot_nt(q[:, sl], k_ref[:, sl])
        p = jnp.exp2(s - jnp.max(s, axis=-1, keepdims=True))
        o = _dot(p.astype(BF16), v_ref[:, sl]) / jnp.sum(p, axis=-1, keepdims=True)
        outs.append(o.astype(BF16))
    o_ref[...] = x + _dot(jnp.concatenate(outs, axis=-1), wo_ref[...])


def _mem_block(x, g, wq, k, v, wo, batch, seq, tm=256):
    m, d = x.shape
    n_mem = k.shape[0] // batch
    tiles_per_batch = seq // tm
    width = MEM_HEADS * MEM_HEAD_DIM
    return pl.pallas_call(
        _mem_kernel,
        grid=(m // tm,),
        in_specs=[pl.BlockSpec((tm, d), lambda i: (i, 0)),
                  pl.BlockSpec((1, d), lambda i: (0, 0)),
                  pl.BlockSpec((d, width), lambda i: (0, 0)),
                  pl.BlockSpec((n_mem, width), lambda i: (i // tiles_per_batch, 0)),
                  pl.BlockSpec((n_mem, width), lambda i: (i // tiles_per_batch, 0)),
                  pl.BlockSpec((width, d), lambda i: (0, 0))],
        out_specs=pl.BlockSpec((tm, d), lambda i: (i, 0)),
        out_shape=jax.ShapeDtypeStruct((m, d), F32),
        compiler_params=_cparams(("parallel",)),
        name="mem_block",
    )(x, g.reshape(1, d).astype(F32), wq, k, v, wo)


def _rel_bias_by_distance(rel_table):
    d = jnp.arange(REL_MAX_DIST, dtype=jnp.int32)
    max_exact = REL_BUCKETS // 2
    nf = jnp.maximum(d, 1).astype(F32)
    large = max_exact + (jnp.log(nf / max_exact) / math.log(REL_MAX_DIST / max_exact)
                         * (REL_BUCKETS - max_exact)).astype(jnp.int32)
    large = jnp.minimum(large, REL_BUCKETS - 1)
    bucket = jnp.where(d < max_exact, d, large)
    return jnp.take(rel_table, bucket, axis=0).T.astype(F32)


def _prep_in_proj(w):
    gw = GROUP_WIDTH
    a = w[:, :A_COLS]
    b = w[:, A_COLS:A_COLS + B_COLS]
    c = w[:, A_COLS + B_COLS:A_COLS + B_COLS + C_COLS]
    d = w[:, A_COLS + B_COLS + C_COLS:A_COLS + B_COLS + C_COLS + 3 * gw]
    f = w[:, A_COLS + B_COLS + C_COLS + 3 * gw:]
    main = jnp.concatenate([
        a[:, :gw] * (DIFF_QK ** -0.5 * LOG2E), a[:, gw:],
        c[:, :gw] * (HEAD_DIM ** -0.5 * LOG2E), c[:, gw:],
        d[:, :gw] * (HEAD_DIM ** -0.5 * LOG2E), d[:, gw:],
        b[:, :MLA_Q_RANK + MLA_KV_RANK]], axis=1).astype(BF16)
    kr = b[:, MLA_Q_RANK + MLA_KV_RANK:]
    half = MLA_ROPE // 2
    pad = jnp.zeros((w.shape[0], LANES - MLA_ROPE), w.dtype)
    f_pad = jnp.zeros((w.shape[0], 2 * LANES - GROUP_HEADS), w.dtype)
    side = jnp.concatenate([kr, pad, kr[:, half:], kr[:, :half], pad, f, f_pad], axis=1).astype(BF16)
    return main, side


def _prep_uq(w):
    qk = MLA_NOPE + MLA_ROPE
    half = MLA_ROPE // 2
    w = (w * (qk ** -0.5 * LOG2E)).reshape(w.shape[0], GROUP_HEADS, qk)
    pad = jnp.zeros((w.shape[0], GROUP_HEADS, LANES - MLA_ROPE), w.dtype)
    nope = w[:, :, :MLA_NOPE]
    r1 = w[:, :, MLA_NOPE:MLA_NOPE + half]
    r2 = w[:, :, MLA_NOPE + half:]
    rope = jnp.concatenate([r1, r2, pad], axis=-1)
    swapped = jnp.concatenate([r2, r1, pad], axis=-1)
    return jnp.concatenate([nope.reshape(w.shape[0], -1), rope.reshape(w.shape[0], -1),
                            swapped.reshape(w.shape[0], -1)], axis=1).astype(BF16)


def _prep_ukv(w):
    w = w.reshape(w.shape[0], GROUP_HEADS, MLA_NOPE + HEAD_DIM)
    return jnp.concatenate([w[:, :, :MLA_NOPE].reshape(w.shape[0], -1),
                            w[:, :, MLA_NOPE:].reshape(w.shape[0], -1)], axis=1).astype(BF16)


def kernel(x, mem, positions, attn_norm_g, w_in, w_out, rel_table, diff_lam_q1, diff_lam_k1, diff_lam_q2, diff_lam_k2, diff_sub_g, mla_q_norm_g, mla_kv_norm_g, mla_w_uq, mla_w_ukv, fox_b_f, mem_q_norm_g, mem_kv_norm_g, mem_w_q, mem_w_k, mem_w_v, mem_w_o, mlp_norm_g, w_up, w_down, final_norm_g):
    batch, seq, d = x.shape
    m = batch * seq
    gw = GROUP_WIDTH
    nh = GROUP_HEADS
    xf = x.reshape(m, d)
    mem2 = mem.reshape(batch * mem.shape[1], d)
    pos_col = positions.reshape(m, 1).astype(jnp.int32)
    pos_row = positions.reshape(batch, 1, seq).astype(jnp.int32)
    cos, sin = _rope_tables(pos_col)
    bias_by_dist = (_rel_bias_by_distance(rel_table) * LOG2E).reshape(nh, 1, LANES)
    npair = nh // HEADS_PER_STEP
    a0, c0, d0 = 0, 3 * npair, 6 * npair

    for l in range(DEPTH):
        w_main, w_side = _prep_in_proj(w_in[l])
        h = _rmsnorm(xf, attn_norm_g[l], BF16)
        proj = _matmul(h, w_main, BF16, name="in_proj")
        side = _matmul(h, w_side, F32, name="in_proj_side")
        b_f = jnp.concatenate([fox_b_f[l].astype(F32), jnp.zeros((LANES - nh,), F32)]).reshape(1, LANES)
        cum, fox_kx = _forget_cum(side, b_f, batch, seq)

        lam_init = 0.8 - 0.6 * math.exp(-0.3 * l)
        lam_vecs = [v[l].reshape(1, DIFF_QK).astype(F32)
                    for v in (diff_lam_q1, diff_lam_k1, diff_lam_q2, diff_lam_k2)]
        o_a = _attn_call(
            functools.partial(_diff_kernel, t=ATT_T, seq=seq, lam_init=lam_init), ATT_T, batch, seq,
            lam_vecs + [proj, proj, proj, pos_col, pos_row, bias_by_dist,
                        diff_sub_g[l].reshape(1, HEAD_DIM).astype(F32)],
            [_full_spec((1, DIFF_QK))] * 4
            + [_q_spec(ATT_T, seq, a0), _kv_spec(seq, a0 + npair), _kv_spec(seq, a0 + 2 * npair),
               pl.BlockSpec((ATT_T, 1), lambda b, h, i: (b * (seq // ATT_T) + i, 0)),
               pl.BlockSpec((None, 1, seq), lambda b, h, i: (b, 0, 0)),
               pl.BlockSpec((HEADS_PER_STEP, 1, LANES), lambda b, h, i: (h, 0, 0)),
               _full_spec((1, HEAD_DIM))],
            "diff_attn")

        q_b = _mla_q(proj, mla_q_norm_g[l], _prep_uq(mla_w_uq[l]), cos, sin)
        kv_b, kpe_b = _mla_kv(proj, mla_kv_norm_g[l], _prep_ukv(mla_w_ukv[l]), side, cos, sin)
        o_b = _attn_call(
            functools.partial(_mla_kernel, t=ATT_T, seq=seq), ATT_T, batch, seq,
            [q_b, q_b, kv_b, kpe_b, kv_b],
            [_q_spec(ATT_T, seq, 0), _q_spec(ATT_T, seq, npair), _kv_spec(seq, 0), _shared_spec(seq),
             _kv_spec(seq, npair)],
            "mla_attn")

        o_c = _attn_call(
            functools.partial(_sb_kernel, t=SB_T), SB_T, batch, seq,
            [proj, proj, proj],
            [_q_spec(SB_T, seq, c0), _kv_spec(seq, c0 + npair), _kv_spec(seq, c0 + 2 * npair)],
            "stickbreak_attn", key_norm_scratch=False)

        o_d = _attn_call(
            functools.partial(_fox_kernel, t=ATT_T, seq=seq), ATT_T, batch, seq,
            [proj, proj, proj, fox_kx, cum],
            [_q_spec(ATT_T, seq, d0), _kv_spec(seq, d0 + npair), _kv_spec(seq, d0 + 2 * npair),
             pl.BlockSpec((None, HEADS_PER_STEP, seq, LANES), lambda b, h, i: (b, h, 0, 0)),
             pl.BlockSpec((ATT_T, LANES), lambda b, h, i: (b * (seq // ATT_T) + i, 0))],
            "fox_attn")

        xf = _out_proj([o_a, o_b, o_c, o_d], w_out[l].astype(BF16), xf)

        mem_n = _rmsnorm(mem2, mem_kv_norm_g[l], BF16)
        k_m = _matmul(mem_n, mem_w_k[l].astype(BF16), BF16, name="mem_k")
        v_m = _matmul(mem_n, mem_w_v[l].astype(BF16), BF16, name="mem_v")
        xf = _mem_block(xf, mem_q_norm_g[l], (mem_w_q[l] * (MEM_HEAD_DIM ** -0.5 * LOG2E)).astype(BF16),
                        k_m, v_m, mem_w_o[l].astype(BF16), batch, seq)

        h = _rmsnorm(xf, mlp_norm_g[l], BF16)
        u = _matmul(h, w_up[l].astype(BF16), BF16, relu2=True, name="mlp_up")
        xf = _matmul(u, w_down[l].astype(BF16), F32, res=xf, name="mlp_down")

    return _rmsnorm(xf, final_norm_g, F32).reshape(batch, seq, d)
```

```python
import functools
import math

import jax
import jax.numpy as jnp
from jax import lax
from jax.experimental import pallas as pl
from jax.experimental.pallas import tpu as pltpu

F32 = jnp.float32
BF16 = jnp.bfloat16

D_MODEL = 4096
DEPTH = 2
HEAD_DIM = 128
GROUP_HEADS = 8
GROUP_WIDTH = GROUP_HEADS * HEAD_DIM
DIFF_QK = HEAD_DIM // 2
MLA_Q_RANK = 1536
MLA_KV_RANK = 512
MLA_NOPE = 128
MLA_ROPE = 64
ROPE_THETA = 10000.0
REL_BUCKETS = 32
REL_MAX_DIST = 128
MEM_HEADS = 4
MEM_HEAD_DIM = 128
NORM_EPS = 1e-6
NEG_INF = -1e30

A_COLS = 3 * GROUP_WIDTH
B_COLS = MLA_Q_RANK + MLA_KV_RANK + MLA_ROPE
C_COLS = 3 * GROUP_WIDTH

LANES = 128
VMEM_LIMIT = 48 * 1024 * 1024

LOG2E = math.log2(math.e)

ATT_T = 512
SB_T = 256
HEADS_PER_STEP = 2
SB_HEADS_PER_STEP = 4
SB_DEAD_LOG2 = -160.0
TRUSTED_ROW_SUM = 2.0 ** -60


def _cparams(sem):
    return pltpu.CompilerParams(dimension_semantics=sem, vmem_limit_bytes=VMEM_LIMIT)


def _dot(a, b):
    return jnp.dot(a, b, preferred_element_type=F32)


def _dot_nt(a, b):
    return lax.dot_general(a, b, (((1,), (1,)), ((), ())), preferred_element_type=F32)


def _rms(x, g):
    return x * lax.rsqrt(jnp.mean(x * x, axis=-1, keepdims=True) + NORM_EPS) * g


def _rmsnorm_kernel(x_ref, g_ref, o_ref):
    o_ref[...] = _rms(x_ref[...].astype(F32), g_ref[...]).astype(o_ref.dtype)


def _rmsnorm(x, g, out_dtype, tm=256):
    m, d = x.shape
    return pl.pallas_call(
        _rmsnorm_kernel,
        grid=(m // tm,),
        in_specs=[pl.BlockSpec((tm, d), lambda i: (i, 0)),
                  pl.BlockSpec((1, d), lambda i: (0, 0))],
        out_specs=pl.BlockSpec((tm, d), lambda i: (i, 0)),
        out_shape=jax.ShapeDtypeStruct((m, d), out_dtype),
        compiler_params=_cparams(("parallel",)),
        name="rmsnorm",
    )(x, g.reshape(1, d).astype(F32))


def _mm_epilogue(r, res_ref, o_ref, relu2):
    if relu2:
        r = jnp.square(jnp.maximum(r, 0.0))
    if res_ref is not None:
        r = r + res_ref[...]
    o_ref[...] = r.astype(o_ref.dtype)


def _mm_kernel(a_ref, w_ref, *rest, nk, relu2, has_res):
    res_ref = rest[0] if has_res else None
    o_ref = rest[1] if has_res else rest[0]
    if nk == 1:
        _mm_epilogue(_dot(a_ref[...], w_ref[...]), res_ref, o_ref, relu2)
        return
    acc_ref = rest[-1]
    k = pl.program_id(2)

    @pl.when(k == 0)
    def _():
        acc_ref[...] = _dot(a_ref[...], w_ref[...])

    @pl.when(k > 0)
    def _():
        acc_ref[...] += _dot(a_ref[...], w_ref[...])

    @pl.when(k == nk - 1)
    def _():
        _mm_epilogue(acc_ref[...], res_ref, o_ref, relu2)


def _matmul(a, w, out_dtype, *, res=None, relu2=False, tm=1024, tn=512, tk=4096, name="matmul"):
    m, kd = a.shape
    n = w.shape[1]
    tm, tn, tk = min(tm, m), min(tn, n), min(tk, kd)
    assert m % tm == 0 and n % tn == 0 and kd % tk == 0
    nk = kd // tk
    in_specs = [pl.BlockSpec((tm, tk), lambda i, j, k: (i, k)),
                pl.BlockSpec((tk, tn), lambda i, j, k: (k, j))]
    args = [a, w]
    if res is not None:
        in_specs.append(pl.BlockSpec((tm, tn), lambda i, j, k: (i, j)))
        args.append(res)
    return pl.pallas_call(
        functools.partial(_mm_kernel, nk=nk, relu2=relu2, has_res=res is not None),
        grid=(m // tm, n // tn, nk),
        in_specs=in_specs,
        out_specs=pl.BlockSpec((tm, tn), lambda i, j, k: (i, j)),
        out_shape=jax.ShapeDtypeStruct((m, n), out_dtype),
        scratch_shapes=[pltpu.VMEM((tm, tn), F32)] if nk > 1 else [],
        compiler_params=_cparams(("parallel", "parallel", "arbitrary")),
        name=name,
    )(*args)


def _out_proj_kernel(a0, a1, a2, a3, w_ref, res_ref, o_ref):
    acc = res_ref[...]
    for g, a_ref in enumerate((a0, a1, a2, a3)):
        acc = acc + _dot(a_ref[...], w_ref[g * GROUP_WIDTH:(g + 1) * GROUP_WIDTH, :])
    o_ref[...] = acc


def _out_proj(mixes, w, res, tm=512, tn=512):
    m = res.shape[0]
    n = w.shape[1]
    a_spec = pl.BlockSpec((tm, GROUP_WIDTH), lambda i, j: (i, 0))
    return pl.pallas_call(
        _out_proj_kernel,
        grid=(m // tm, n // tn),
        in_specs=[a_spec, a_spec, a_spec, a_spec,
                  pl.BlockSpec((w.shape[0], tn), lambda i, j: (0, j)),
                  pl.BlockSpec((tm, tn), lambda i, j: (i, j))],
        out_specs=pl.BlockSpec((tm, tn), lambda i, j: (i, j)),
        out_shape=jax.ShapeDtypeStruct((m, n), F32),
        compiler_params=_cparams(("parallel", "parallel")),
        name="out_proj",
    )(*mixes, w, res)


def _rope_kernel(pos_ref, tab_ref, cos_ref, sin_ref):
    ang = pos_ref[...].astype(F32) * tab_ref[0:1, :]
    cos_ref[...] = jnp.cos(ang) * tab_ref[1:2, :]
    sin_ref[...] = jnp.sin(ang) * tab_ref[2:3, :]


def _rope_tables(pos_col, tm=1024):
    m = pos_col.shape[0]
    half = MLA_ROPE // 2
    inv = ROPE_THETA ** (-jnp.arange(half, dtype=F32) * 2.0 / MLA_ROPE)
    zeros = jnp.zeros((LANES - MLA_ROPE,), F32)
    ones = jnp.ones((half,), F32)
    tab = jnp.stack([jnp.concatenate([inv, inv, zeros]),
                     jnp.concatenate([ones, ones, zeros]),
                     jnp.concatenate([-ones, ones, zeros])]
                    + [jnp.zeros((LANES,), F32)] * 5)
    spec = pl.BlockSpec((tm, LANES), lambda i: (i, 0))
    return pl.pallas_call(
        _rope_kernel,
        grid=(m // tm,),
        in_specs=[pl.BlockSpec((tm, 1), lambda i: (i, 0)),
                  pl.BlockSpec((8, LANES), lambda i: (0, 0))],
        out_specs=[spec, spec],
        out_shape=[jax.ShapeDtypeStruct((m, LANES), F32)] * 2,
        compiler_params=_cparams(("parallel",)),
        name="rope_tables",
    )(pos_col, tab)


def _mla_q_kernel(cq_ref, g_ref, w_ref, cos_ref, sin_ref, o_ref):
    n = _rms(cq_ref[...].astype(F32), g_ref[...]).astype(BF16)
    t = _dot(n, w_ref[...])
    o_ref[:, :GROUP_WIDTH] = t[:, :GROUP_WIDTH].astype(o_ref.dtype)
    cos, sin = cos_ref[...], sin_ref[...]
    for h in range(GROUP_HEADS):
        lo = GROUP_WIDTH + h * LANES
        pe = t[:, lo:lo + LANES] * cos + t[:, lo + GROUP_WIDTH:lo + GROUP_WIDTH + LANES] * sin
        o_ref[:, lo:lo + LANES] = pe.astype(o_ref.dtype)


def _mla_q(proj, g, w, cos, sin, tm=512):
    m = proj.shape[0]
    cq_block = (2 * A_COLS + C_COLS) // MLA_Q_RANK
    return pl.pallas_call(
        _mla_q_kernel,
        grid=(m // tm,),
        in_specs=[pl.BlockSpec((tm, MLA_Q_RANK), lambda i: (i, cq_block)),
                  pl.BlockSpec((1, MLA_Q_RANK), lambda i: (0, 0)),
                  pl.BlockSpec(w.shape, lambda i: (0, 0)),
                  pl.BlockSpec((tm, LANES), lambda i: (i, 0)),
                  pl.BlockSpec((tm, LANES), lambda i: (i, 0))],
        out_specs=pl.BlockSpec((tm, 2 * GROUP_WIDTH), lambda i: (i, 0)),
        out_shape=jax.ShapeDtypeStruct((m, 2 * GROUP_WIDTH), BF16),
        compiler_params=_cparams(("parallel",)),
        name="mla_q",
    )(proj, g.reshape(1, -1).astype(F32), w, cos, sin)


def _mla_kv_kernel(ckv_ref, g_ref, w_ref, kr_ref, cos_ref, sin_ref, kv_ref, kpe_ref):
    n = _rms(ckv_ref[...].astype(F32), g_ref[...]).astype(BF16)
    kv_ref[...] = _dot(n, w_ref[...]).astype(kv_ref.dtype)
    kr = kr_ref[...]
    kpe = kr[:, :LANES] * cos_ref[...] + kr[:, LANES:] * sin_ref[...]
    lane = lax.broadcasted_iota(jnp.int32, kpe.shape, 1)
    kpe = jnp.where((lane >= MLA_ROPE) & (lane < MLA_ROPE + 3), 1.0, kpe)
    kpe_ref[...] = kpe.astype(kpe_ref.dtype)


def _mla_kv(proj, g, w, kr, cos, sin, tm=512):
    m = proj.shape[0]
    ckv_block = (2 * A_COLS + C_COLS + MLA_Q_RANK) // MLA_KV_RANK
    return pl.pallas_call(
        _mla_kv_kernel,
        grid=(m // tm,),
        in_specs=[pl.BlockSpec((tm, MLA_KV_RANK), lambda i: (i, ckv_block)),
                  pl.BlockSpec((1, MLA_KV_RANK), lambda i: (0, 0)),
                  pl.BlockSpec(w.shape, lambda i: (0, 0)),
                  pl.BlockSpec((tm, 2 * LANES), lambda i: (i, 0)),
                  pl.BlockSpec((tm, LANES), lambda i: (i, 0)),
                  pl.BlockSpec((tm, LANES), lambda i: (i, 0))],
        out_specs=[pl.BlockSpec((tm, 2 * GROUP_WIDTH), lambda i: (i, 0)),
                   pl.BlockSpec((tm, LANES), lambda i: (i, 0))],
        out_shape=[jax.ShapeDtypeStruct((m, 2 * GROUP_WIDTH), BF16),
                   jax.ShapeDtypeStruct((m, LANES), BF16)],
        compiler_params=_cparams(("parallel",)),
        name="mla_kv",
    )(proj, g.reshape(1, -1).astype(F32), w, kr, cos, sin)


def _split3(x):
    x1 = x.astype(BF16)
    r1 = x - x1.astype(F32)
    x2 = r1.astype(BF16)
    x3 = (r1 - x2.astype(F32)).astype(BF16)
    return x1, x2, x3


def _place(lane, cols, base, fill):
    out = fill
    for i, c in enumerate(cols):
        out = jnp.where(lane == base + i, c.astype(F32), out)
    return out


def _forget_cum_kernel(f_ref, b_ref, cum_ref, kx_ref, *, chunk):
    r = lax.broadcasted_iota(jnp.int32, (chunk, chunk), 0)
    c = lax.broadcasted_iota(jnp.int32, (chunk, chunk), 1)
    tri = jnp.where(r >= c, 1.0, 0.0).astype(BF16)
    lane = lax.broadcasted_iota(jnp.int32, (chunk, LANES), 1)
    unit_fill = jnp.where(lane < 3, 1.0, 0.0)

    def body(ci, carry):
        rows = _rows(ci, chunk)
        x = f_ref[rows, :] + b_ref[...]
        logf = (jnp.minimum(x, 0.0) - jnp.log1p(jnp.exp(-jnp.abs(x)))) * LOG2E
        parts = _split3(logf)
        cs = carry + _dot(tri, parts[0]) + _dot(tri, parts[1]) + _dot(tri, parts[2])
        cum_ref[rows, :] = cs
        for h in range(GROUP_HEADS):
            kx_ref[h, rows, :] = _place(lane, _split3(cs[:, h:h + 1]), 3, unit_fill).astype(kx_ref.dtype)
        return cs[chunk - 1:chunk, :]

    lax.fori_loop(0, f_ref.shape[0] // chunk, body, jnp.zeros((1, LANES), F32))


def _forget_cum(side, b_f, batch, seq):
    f_block = 2
    return pl.pallas_call(
        functools.partial(_forget_cum_kernel, chunk=256),
        grid=(batch,),
        in_specs=[pl.BlockSpec((seq, LANES), lambda b: (b, f_block)),
                  pl.BlockSpec((1, LANES), lambda b: (0, 0))],
        out_specs=[pl.BlockSpec((seq, LANES), lambda b: (b, 0)),
                   pl.BlockSpec((None, GROUP_HEADS, seq, LANES), lambda b: (b, 0, 0, 0))],
        out_shape=[jax.ShapeDtypeStruct((batch * seq, LANES), F32),
                   jax.ShapeDtypeStruct((batch, GROUP_HEADS, seq, LANES), BF16)],
        compiler_params=_cparams(("parallel",)),
        name="forget_cum",
    )(side, b_f)


def _tile_iota(t):
    return (lax.broadcasted_iota(jnp.int32, (t, t), 0),
            lax.broadcasted_iota(jnp.int32, (t, t), 1))


def _softmax_update(s, v, m, acc):
    m_new = jnp.maximum(m, jnp.max(s, axis=-1, keepdims=True))
    alpha = jnp.exp2(m - m_new)
    p = jnp.exp2(s - m_new).astype(BF16)
    v_aug = jnp.concatenate([v, jnp.ones_like(v)], axis=1)
    return m_new, alpha * acc + _dot(p, v_aug)


def _softmax_init(t):
    return (jnp.full((t, 1), NEG_INF, F32), jnp.zeros((t, 2 * HEAD_DIM), F32))


def _softmax_out(acc):
    return acc[:, :HEAD_DIM] / acc[:, HEAD_DIM:]


def _hs(h):
    return slice(h * HEAD_DIM, (h + 1) * HEAD_DIM)


def _rows(j, w):
    return pl.ds(pl.multiple_of(j * w, w), w)


def _causal_sweep(step, qi, init):
    carry = lax.fori_loop(0, qi, lambda j, c: step(j, c, False), init)
    return step(qi, carry, True)


def _update(s, v, state, mask, acc_ref, chain):
    if mask is not None:
        s = jnp.where(mask, s, NEG_INF)
    if acc_ref is None:
        return _softmax_update(s, v, *state)
    v_aug = jnp.concatenate([v, jnp.ones_like(v)], axis=1)
    acc_ref[chain] += _dot(jnp.exp2(s).astype(BF16), v_aug)
    return None


def _run_softmax(qi, t, n_chains, make_step, finish, acc_ref):
    acc_ref[...] = jnp.zeros_like(acc_ref)
    fast = make_step(acc_ref)

    def wide(j, carry):
        fast(j, None, False, 2 * t)
        return carry

    lax.fori_loop(0, qi // 2, wide, 0)

    @pl.when(qi % 2 == 1)
    def _():
        fast(qi - 1, None, False, t)

    fast(qi, None, True, t)
    accs = [acc_ref[c] for c in range(n_chains)]
    l_min = accs[0][:, HEAD_DIM:HEAD_DIM + 1]
    for a in accs[1:]:
        l_min = jnp.minimum(l_min, a[:, HEAD_DIM:HEAD_DIM + 1])
    ok = jnp.min(l_min) >= TRUSTED_ROW_SUM

    @pl.when(ok)
    def _():
        finish([_softmax_out(a) for a in accs])

    @pl.when(jnp.logical_not(ok))
    def _():
        slow = make_step(None)
        init = tuple(_softmax_init(t) for _ in range(n_chains))
        carry = lax.fori_loop(0, qi, lambda j, c: slow(j, c, False, t), init)
        finish([_softmax_out(c[1]) for c in slow(qi, carry, True, t)])


def _key_norm_max(k_ref, h, seq, extra_ref=None):
    chunk = ATT_T

    def body(c, mx):
        kk = k_ref[_rows(c, chunk), _hs(h)].astype(F32)
        n2 = jnp.sum(kk * kk, axis=1, keepdims=True)
        if extra_ref is not None:
            e = extra_ref[_rows(c, chunk), :].astype(F32)
            n2 = n2 + jnp.sum(e * e, axis=1, keepdims=True)
        return jnp.maximum(mx, n2)

    mx = lax.fori_loop(0, seq // chunk, body, jnp.zeros((chunk, 1), F32))
    return jnp.sqrt(jnp.max(mx, axis=0, keepdims=True))


def _store_key_norms(kmax_ref, qi, norm_fn):
    @pl.when(qi == 0)
    def _():
        for h in range(HEADS_PER_STEP):
            kmax_ref[h] = jnp.broadcast_to(norm_fn(h), kmax_ref.shape[1:])


def _row_norm(x):
    xf = x.astype(F32)
    return jnp.sqrt(jnp.sum(xf * xf, axis=1, keepdims=True))


def _fox_kernel(q_ref, k_ref, v_ref, kx_ref, cumq_ref, o_ref, kmax_ref, acc_ref, *, t, seq):
    pair, qi = pl.program_id(1), pl.program_id(2)
    _store_key_norms(kmax_ref, qi, lambda h: _key_norm_max(k_ref, h, seq))
    row, col = _tile_iota(t)
    lane = lax.broadcasted_iota(jnp.int32, (t, LANES), 1)
    minus_fill = jnp.where((lane >= 3) & (lane < 6), -1.0, 0.0)
    cumq = cumq_ref[...]

    def q_aug(h, fast):
        q = q_ref[:, _hs(h)]
        c = jnp.sum(jnp.where(lane == pair * HEADS_PER_STEP + h, cumq, 0.0), axis=1, keepdims=True)
        if fast:
            c = c - _row_norm(q) * kmax_ref[h][0:1, 0:1]
        return jnp.concatenate([q, _place(lane, _split3(c), 0, minus_fill).astype(BF16)], axis=1)

    def make_step(acc_ref):
        fast = acc_ref is not None
        qs = [q_aug(h, fast) for h in range(HEADS_PER_STEP)]

        def step(j, carry, masked, w):
            mask = (col <= row) if masked else None
            out = []
            for h in range(HEADS_PER_STEP):
                k = jnp.concatenate([k_ref[_rows(j, w), _hs(h)], kx_ref[h, _rows(j, w), :]], axis=1)
                out.append(_update(_dot_nt(qs[h], k), v_ref[_rows(j, w), _hs(h)], None if fast else carry[h], mask, acc_ref, h))
            return tuple(out)

        return step

    def finish(outs):
        o_ref[...] = jnp.concatenate(outs, axis=1).astype(o_ref.dtype)

    _run_softmax(qi, t, HEADS_PER_STEP, make_step, finish, acc_ref)


def _mla_kernel(qn_ref, qp_ref, kn_ref, kp_ref, v_ref, o_ref, kmax_ref, acc_ref, *, t, seq):
    qi = pl.program_id(2)
    _store_key_norms(kmax_ref, qi, lambda h: _key_norm_max(kn_ref, h, seq, kp_ref))
    row, col = _tile_iota(t)
    lane = lax.broadcasted_iota(jnp.int32, (t, LANES), 1)

    def q_aug(h, fast):
        qn, qp = qn_ref[:, _hs(h)], qp_ref[:, _hs(h)]
        if fast:
            qn_f, qp_f = qn.astype(F32), qp.astype(F32)
            norm = jnp.sqrt(jnp.sum(qn_f * qn_f, axis=1, keepdims=True)
                            + jnp.sum(qp_f * qp_f, axis=1, keepdims=True))
            qp = _place(lane, _split3(-norm * kmax_ref[h][0:1, 0:1]), MLA_ROPE, qp_f).astype(BF16)
        return jnp.concatenate([qn, qp], axis=1)

    def make_step(acc_ref):
        fast = acc_ref is not None
        qs = [q_aug(h, fast) for h in range(HEADS_PER_STEP)]

        def step(j, carry, masked, w):
            mask = (col <= row) if masked else None
            kp = kp_ref[_rows(j, w), :]
            out = []
            for h in range(HEADS_PER_STEP):
                k = jnp.concatenate([kn_ref[_rows(j, w), _hs(h)], kp], axis=1)
                out.append(_update(_dot_nt(qs[h], k), v_ref[_rows(j, w), _hs(h)], None if fast else carry[h], mask, acc_ref, h))
            return tuple(out)

        return step

    def finish(outs):
        o_ref[...] = jnp.concatenate(outs, axis=1).astype(o_ref.dtype)

    _run_softmax(qi, t, HEADS_PER_STEP, make_step, finish, acc_ref)


def _diff_kernel(lq1_ref, lk1_ref, lq2_ref, lk2_ref, q_ref, k_ref, v_ref, posq_ref, posk_ref,
                 bias_ref, subg_ref, o_ref, kmax_ref, acc_ref, *, t, seq, lam_init):
    qi = pl.program_id(2)
    _store_key_norms(kmax_ref, qi, lambda h: _key_norm_max(k_ref, h, seq))
    lam = (jnp.exp(jnp.sum(lq1_ref[...] * lk1_ref[...], axis=-1, keepdims=True))
           - jnp.exp(jnp.sum(lq2_ref[...] * lk2_ref[...], axis=-1, keepdims=True)) + lam_init)
    row, col = _tile_iota(t)
    lane = lax.broadcasted_iota(jnp.int32, (t, LANES), 1)
    zero_fill = jnp.zeros((t, LANES), F32)

    def key_units(w):
        return jnp.where(lax.broadcasted_iota(jnp.int32, (w, LANES), 1) < 3, 1.0, 0.0).astype(BF16)

    posq = posq_ref[...]
    posq_min = jnp.min(posq)
    bias_rows = [jnp.broadcast_to(bias_ref[h], (t, LANES)) for h in range(HEADS_PER_STEP)]
    last = REL_MAX_DIST - 1
    far_bias = [bias_ref[h][:, last:] for h in range(HEADS_PER_STEP)]

    def half_q(h, c):
        q = q_ref[:, _hs(h)]
        return jnp.where((lane < DIFF_QK) if c == 0 else (lane >= DIFF_QK), q, jnp.zeros_like(q))

    def make_step(acc_ref):
        fast = acc_ref is not None
        q_near, q_far = [], []
        for h in range(HEADS_PER_STEP):
            for c in range(2):
                qz = half_q(h, c)
                if fast:
                    bound = (_row_norm(qz) * kmax_ref[h][0:1, 0:1]
                             + jnp.max(bias_ref[h], axis=1, keepdims=True))
                    q_near.append(jnp.concatenate(
                        [qz, _place(lane, _split3(-bound), 0, zero_fill).astype(BF16)], axis=1))
                    q_far.append(jnp.concatenate(
                        [qz, _place(lane, _split3(far_bias[h] - bound), 0, zero_fill).astype(BF16)], axis=1))
                else:
                    q_near.append(qz)
                    q_far.append(qz)

        def chains(j, carry, masked, w, gathered):
            mask = (col <= row) if masked else None
            out = []
            for h in range(HEADS_PER_STEP):
                k = k_ref[_rows(j, w), _hs(h)]
                if fast:
                    k = jnp.concatenate([k, key_units(w)], axis=1)
                v = v_ref[_rows(j, w), _hs(h)]
                for c in range(2):
                    i = 2 * h + c
                    if gathered is not None:
                        s = _dot_nt(q_near[i], k) + gathered[h]
                    elif fast:
                        s = _dot_nt(q_far[i], k)
                    else:
                        s = _dot_nt(q_far[i], k) + far_bias[h]
                    out.append(_update(s, v, None if fast else carry[i], mask, acc_ref, i))
            return tuple(out)

        def step(j, carry, masked, w):
            posk = posk_ref[:, _rows(j, w)]

            def near(c):
                dist = jnp.clip(posq - posk, 0, last)
                gathered = [jnp.concatenate(
                    [jnp.take_along_axis(bias_rows[h], dist[:, cc * LANES:(cc + 1) * LANES], axis=1,
                                         mode="promise_in_bounds")
                     for cc in range(w // LANES)], axis=1) for h in range(HEADS_PER_STEP)]
                return chains(j, c, masked, w, gathered)

            def far(c):
                return chains(j, c, masked, w, None)

            return lax.cond(posq_min - jnp.max(posk) < last, near, far, carry)

        return step

    def finish(outs):
        normed = []
        for h in range(HEADS_PER_STEP):
            o = outs[2 * h] - lam * outs[2 * h + 1]
            normed.append(_rms(o, subg_ref[...]) * (1.0 - lam_init))
        o_ref[...] = jnp.concatenate(normed, axis=1).astype(o_ref.dtype)

    _run_softmax(qi, t, 2 * HEADS_PER_STEP, make_step, finish, acc_ref)


def _sb_kernel(q_ref, k_ref, v_ref, o_ref, *, t):
    qi = pl.program_id(2)
    w = t
    row, col = _tile_iota(t)
    later = jnp.where(row > col, 1.0, 0.0).astype(BF16)
    qs = [q_ref[:, _hs(h)] for h in range(SB_HEADS_PER_STEP)]

    def step(j, carry, masked):
        out = []
        for h in range(SB_HEADS_PER_STEP):
            tail, acc = carry[h]
            z = _dot_nt(qs[h], k_ref[_rows(j, w), _hs(h)])
            sp = jnp.log(1.0 + jnp.exp2(-jnp.abs(z))) * LOG2E
            log_beta = jnp.minimum(z, 0.0) - sp
            log_1m = -jnp.maximum(z, 0.0) - sp
            if masked:
                log_1m = jnp.where(col < row, log_1m, 0.0)
            hi = log_1m.astype(BF16)
            lo = (log_1m - hi.astype(F32)).astype(BF16)
            suffix = _dot(hi, later) + _dot(lo, later)
            a = jnp.exp2(log_beta + suffix + tail)
            if masked:
                a = jnp.where(col < row, a, 0.0)
            acc = acc + _dot(a.astype(BF16), v_ref[_rows(j, w), _hs(h)])
            tail = tail + suffix[:, :1] + log_1m[:, :1]
            out.append((tail, acc))
        return tuple(out)

    def live(carry):
        worst = carry[0][0]
        for h in range(1, SB_HEADS_PER_STEP):
            worst = jnp.maximum(worst, carry[h][0])
        return (jnp.max(worst) > SB_DEAD_LOG2).astype(jnp.int32)

    init = tuple((jnp.zeros((t, 1), F32), jnp.zeros((t, HEAD_DIM), F32)) for _ in range(SB_HEADS_PER_STEP))
    carry = step(qi, init, True)

    def body(state):
        j, _, c = state
        c = step(j, c, False)
        return j - 1, live(c), c

    _, _, carry = lax.while_loop(lambda st: (st[0] >= 0) & (st[1] > 0), body,
                                 (qi - 1, live(carry), carry))
    o_ref[...] = jnp.concatenate([c[1] for c in carry], axis=1).astype(o_ref.dtype)


def _attn_call(kernel, t, batch, seq, in_arrays, in_specs, name, hps=HEADS_PER_STEP, softmax_chains=0):
    nq = seq // t
    width = hps * HEAD_DIM
    scratch = []
    if softmax_chains:
        scratch = [pltpu.VMEM((hps, 8, LANES), F32),
                   pltpu.VMEM((softmax_chains, t, 2 * HEAD_DIM), F32)]
    return pl.pallas_call(
        kernel,
        grid=(batch, GROUP_HEADS // hps, nq),
        in_specs=in_specs,
        out_specs=pl.BlockSpec((t, width), lambda b, h, i: (b * nq + i, h)),
        out_shape=jax.ShapeDtypeStruct((batch * seq, GROUP_WIDTH), BF16),
        scratch_shapes=scratch,
        compiler_params=_cparams(("arbitrary", "arbitrary", "arbitrary")),
        name=name,
    )(*in_arrays)


def _q_spec(t, seq, col0, hps=HEADS_PER_STEP):
    nq = seq // t
    return pl.BlockSpec((t, hps * HEAD_DIM), lambda b, h, i: (b * nq + i, col0 + h))


def _kv_spec(seq, col0, hps=HEADS_PER_STEP):
    return pl.BlockSpec((seq, hps * HEAD_DIM), lambda b, h, i: (b, col0 + h))


def _shared_spec(seq):
    return pl.BlockSpec((seq, HEAD_DIM), lambda b, h, i: (b, 0))


def _full_spec(shape):
    return pl.BlockSpec(shape, lambda b, h, i: (0,) * len(shape))


def _mem_kernel(x_ref, g_ref, wq_ref, k_ref, v_ref, wo_ref, o_ref):
    x = x_ref[...]
    h = _rms(x, g_ref[...]).astype(BF16)
    q = _dot(h, wq_ref[...]).astype(BF16)
    outs = []
    for hd in range(MEM_HEADS):
        sl = slice(hd * MEM_HEAD_DIM, (hd + 1) * MEM_HEAD_DIM)
        s = _dot_nt(q[:, sl], k_ref[:, sl])
        p = jnp.exp2(s - jnp.max(s, axis=-1, keepdims=True))
        o = _dot(p.astype(BF16), v_ref[:, sl]) / jnp.sum(p, axis=-1, keepdims=True)
        outs.append(o.astype(BF16))
    o_ref[...] = x + _dot(jnp.concatenate(outs, axis=-1), wo_ref[...])


def _mem_block(x, g, wq, k, v, wo, batch, seq, tm=256):
    m, d = x.shape
    n_mem = k.shape[0] // batch
    tiles_per_batch = seq // tm
    width = MEM_HEADS * MEM_HEAD_DIM
    return pl.pallas_call(
        _mem_kernel,
        grid=(m // tm,),
        in_specs=[pl.BlockSpec((tm, d), lambda i: (i, 0)),
                  pl.BlockSpec((1, d), lambda i: (0, 0)),
                  pl.BlockSpec((d, width), lambda i: (0, 0)),
                  pl.BlockSpec((n_mem, width), lambda i: (i // tiles_per_batch, 0)),
                  pl.BlockSpec((n_mem, width), lambda i: (i // tiles_per_batch, 0)),
                  pl.BlockSpec((width, d), lambda i: (0, 0))],
        out_specs=pl.BlockSpec((tm, d), lambda i: (i, 0)),
        out_shape=jax.ShapeDtypeStruct((m, d), F32),
        compiler_params=_cparams(("parallel",)),
        name="mem_block",
    )(x, g.reshape(1, d).astype(F32), wq, k, v, wo)


def _rel_bias_by_distance(rel_table):
    d = jnp.arange(REL_MAX_DIST, dtype=jnp.int32)
    max_exact = REL_BUCKETS // 2
    nf = jnp.maximum(d, 1).astype(F32)
    large = max_exact + (jnp.log(nf / max_exact) / math.log(REL_MAX_DIST / max_exact)
                         * (REL_BUCKETS - max_exact)).astype(jnp.int32)
    large = jnp.minimum(large, REL_BUCKETS - 1)
    bucket = jnp.where(d < max_exact, d, large)
    return jnp.take(rel_table, bucket, axis=0).T.astype(F32)


def _prep_in_proj(w):
    gw = GROUP_WIDTH
    a = w[:, :A_COLS]
    b = w[:, A_COLS:A_COLS + B_COLS]
    c = w[:, A_COLS + B_COLS:A_COLS + B_COLS + C_COLS]
    d = w[:, A_COLS + B_COLS + C_COLS:A_COLS + B_COLS + C_COLS + 3 * gw]
    f = w[:, A_COLS + B_COLS + C_COLS + 3 * gw:]
    main = jnp.concatenate([
        a[:, :gw] * (DIFF_QK ** -0.5 * LOG2E), a[:, gw:],
        c[:, :gw] * (HEAD_DIM ** -0.5 * LOG2E), c[:, gw:],
        d[:, :gw] * (HEAD_DIM ** -0.5 * LOG2E), d[:, gw:],
        b[:, :MLA_Q_RANK + MLA_KV_RANK]], axis=1).astype(BF16)
    kr = b[:, MLA_Q_RANK + MLA_KV_RANK:]
    half = MLA_ROPE // 2
    pad = jnp.zeros((w.shape[0], LANES - MLA_ROPE), w.dtype)
    f_pad = jnp.zeros((w.shape[0], 2 * LANES - GROUP_HEADS), w.dtype)
    side = jnp.concatenate([kr, pad, kr[:, half:], kr[:, :half], pad, f, f_pad], axis=1).astype(BF16)
    return main, side


def _prep_uq(w):
    qk = MLA_NOPE + MLA_ROPE
    half = MLA_ROPE // 2
    w = (w * (qk ** -0.5 * LOG2E)).reshape(w.shape[0], GROUP_HEADS, qk)
    pad = jnp.zeros((w.shape[0], GROUP_HEADS, LANES - MLA_ROPE), w.dtype)
    nope = w[:, :, :MLA_NOPE]
    r1 = w[:, :, MLA_NOPE:MLA_NOPE + half]
    r2 = w[:, :, MLA_NOPE + half:]
    rope = jnp.concatenate([r1, r2, pad], axis=-1)
    swapped = jnp.concatenate([r2, r1, pad], axis=-1)
    return jnp.concatenate([nope.reshape(w.shape[0], -1), rope.reshape(w.shape[0], -1),
                            swapped.reshape(w.shape[0], -1)], axis=1).astype(BF16)


def _prep_ukv(w):
    w = w.reshape(w.shape[0], GROUP_HEADS, MLA_NOPE + HEAD_DIM)
    return jnp.concatenate([w[:, :, :MLA_NOPE].reshape(w.shape[0], -1),
                            w[:, :, MLA_NOPE:].reshape(w.shape[0], -1)], axis=1).astype(BF16)


def kernel(x, mem, positions, attn_norm_g, w_in, w_out, rel_table, diff_lam_q1, diff_lam_k1, diff_lam_q2, diff_lam_k2, diff_sub_g, mla_q_norm_g, mla_kv_norm_g, mla_w_uq, mla_w_ukv, fox_b_f, mem_q_norm_g, mem_kv_norm_g, mem_w_q, mem_w_k, mem_w_v, mem_w_o, mlp_norm_g, w_up, w_down, final_norm_g):
    batch, seq, d = x.shape
    m = batch * seq
    gw = GROUP_WIDTH
    nh = GROUP_HEADS
    xf = x.reshape(m, d)
    mem2 = mem.reshape(batch * mem.shape[1], d)
    pos_col = positions.reshape(m, 1).astype(jnp.int32)
    pos_row = positions.reshape(batch, 1, seq).astype(jnp.int32)
    cos, sin = _rope_tables(pos_col)
    bias_by_dist = (_rel_bias_by_distance(rel_table) * LOG2E).reshape(nh, 1, LANES)
    npair = nh // HEADS_PER_STEP
    a0, d0 = 0, 6 * npair
    nsb = nh // SB_HEADS_PER_STEP
    sb0 = 3 * nsb

    for l in range(DEPTH):
        w_main, w_side = _prep_in_proj(w_in[l])
        h = _rmsnorm(xf, attn_norm_g[l], BF16)
        proj = _matmul(h, w_main, BF16, name="in_proj")
        side = _matmul(h, w_side, F32, name="in_proj_side")
        b_f = jnp.concatenate([fox_b_f[l].astype(F32), jnp.zeros((LANES - nh,), F32)]).reshape(1, LANES)
        cum, fox_kx = _forget_cum(side, b_f, batch, seq)

        lam_init = 0.8 - 0.6 * math.exp(-0.3 * l)
        lam_vecs = [v[l].reshape(1, DIFF_QK).astype(F32)
                    for v in (diff_lam_q1, diff_lam_k1, diff_lam_q2, diff_lam_k2)]
        o_a = _attn_call(
            functools.partial(_diff_kernel, t=ATT_T, seq=seq, lam_init=lam_init), ATT_T, batch, seq,
            lam_vecs + [proj, proj, proj, pos_col, pos_row, bias_by_dist,
                        diff_sub_g[l].reshape(1, HEAD_DIM).astype(F32)],
            [_full_spec((1, DIFF_QK))] * 4
            + [_q_spec(ATT_T, seq, a0), _kv_spec(seq, a0 + npair), _kv_spec(seq, a0 + 2 * npair),
               pl.BlockSpec((ATT_T, 1), lambda b, h, i: (b * (seq // ATT_T) + i, 0)),
               pl.BlockSpec((None, 1, seq), lambda b, h, i: (b, 0, 0)),
               pl.BlockSpec((HEADS_PER_STEP, 1, LANES), lambda b, h, i: (h, 0, 0)),
               _full_spec((1, HEAD_DIM))],
            "diff_attn", softmax_chains=2 * HEADS_PER_STEP)

        q_b = _mla_q(proj, mla_q_norm_g[l], _prep_uq(mla_w_uq[l]), cos, sin)
        kv_b, kpe_b = _mla_kv(proj, mla_kv_norm_g[l], _prep_ukv(mla_w_ukv[l]), side, cos, sin)
        o_b = _attn_call(
            functools.partial(_mla_kernel, t=ATT_T, seq=seq), ATT_T, batch, seq,
            [q_b, q_b, kv_b, kpe_b, kv_b],
            [_q_spec(ATT_T, seq, 0), _q_spec(ATT_T, seq, npair), _kv_spec(seq, 0), _shared_spec(seq),
             _kv_spec(seq, npair)],
            "mla_attn", softmax_chains=HEADS_PER_STEP)

        o_c = _attn_call(
            functools.partial(_sb_kernel, t=SB_T), SB_T, batch, seq,
            [proj, proj, proj],
            [_q_spec(SB_T, seq, sb0, SB_HEADS_PER_STEP), _kv_spec(seq, sb0 + nsb, SB_HEADS_PER_STEP),
             _kv_spec(seq, sb0 + 2 * nsb, SB_HEADS_PER_STEP)],
            "stickbreak_attn", hps=SB_HEADS_PER_STEP)

        o_d = _attn_call(
            functools.partial(_fox_kernel, t=ATT_T, seq=seq), ATT_T, batch, seq,
            [proj, proj, proj, fox_kx, cum],
            [_q_spec(ATT_T, seq, d0), _kv_spec(seq, d0 + npair), _kv_spec(seq, d0 + 2 * npair),
             pl.BlockSpec((None, HEADS_PER_STEP, seq, LANES), lambda b, h, i: (b, h, 0, 0)),
             pl.BlockSpec((ATT_T, LANES), lambda b, h, i: (b * (seq // ATT_T) + i, 0))],
            "fox_attn", softmax_chains=HEADS_PER_STEP)

        xf = _out_proj([o_a, o_b, o_c, o_d], w_out[l].astype(BF16), xf)

        mem_n = _rmsnorm(mem2, mem_kv_norm_g[l], BF16)
        k_m = _matmul(mem_n, mem_w_k[l].astype(BF16), BF16, name="mem_k")
        v_m = _matmul(mem_n, mem_w_v[l].astype(BF16), BF16, name="mem_v")
        xf = _mem_block(xf, mem_q_norm_g[l], (mem_w_q[l] * (MEM_HEAD_DIM ** -0.5 * LOG2E)).astype(BF16),
                        k_m, v_m, mem_w_o[l].astype(BF16), batch, seq)

        h = _rmsnorm(xf, mlp_norm_g[l], BF16)
        u = _matmul(h, w_up[l].astype(BF16), BF16, relu2=True, name="mlp_up")
        xf = _matmul(u, w_down[l].astype(BF16), F32, res=xf, name="mlp_down")

    return _rmsnorm(xf, final_norm_g, F32).reshape(batch, seq, d)
```

```python
import functools
import math

import numpy as np
import jax
import jax.numpy as jnp
from jax import lax
from jax.experimental import pallas as pl
from jax.experimental.pallas import tpu as pltpu

F32 = jnp.float32
BF16 = jnp.bfloat16

D_MODEL = 4096
DEPTH = 2
HEAD_DIM = 128
GROUP_HEADS = 8
GROUP_WIDTH = GROUP_HEADS * HEAD_DIM
DIFF_QK = HEAD_DIM // 2
MLA_Q_RANK = 1536
MLA_KV_RANK = 512
MLA_NOPE = 128
MLA_ROPE = 64
ROPE_THETA = 10000.0
REL_BUCKETS = 32
REL_MAX_DIST = 128
MEM_HEADS = 4
MEM_HEAD_DIM = 128
NORM_EPS = 1e-6
NEG_INF = -1e30

A_COLS = 3 * GROUP_WIDTH
B_COLS = MLA_Q_RANK + MLA_KV_RANK + MLA_ROPE
C_COLS = 3 * GROUP_WIDTH

PROJ_A0 = 0
PROJ_CQ0 = A_COLS
PROJ_CKV0 = A_COLS + MLA_Q_RANK
PROJ_C0 = 5632
PROJ_D0 = PROJ_C0 + C_COLS
PROJ_COLS = PROJ_D0 + 3 * GROUP_WIDTH

LANES = 128
VMEM_LIMIT = 48 * 1024 * 1024

LOG2E = math.log2(math.e)

ATT_T = 512
SB_T = 256
HEADS_PER_STEP = 2
SB_HEADS_PER_STEP = 4
SB_DEAD_LOG2 = -160.0
TRUSTED_ROW_SUM = 2.0 ** -60


def _cparams(sem):
    return pltpu.CompilerParams(dimension_semantics=sem, vmem_limit_bytes=VMEM_LIMIT)


def _dot(a, b):
    return jnp.dot(a, b, preferred_element_type=F32)


def _dot_nt(a, b):
    return lax.dot_general(a, b, (((1,), (1,)), ((), ())), preferred_element_type=F32)


def _rms(x, g):
    return x * lax.rsqrt(jnp.mean(x * x, axis=-1, keepdims=True) + NORM_EPS) * g


def _rmsnorm_kernel(x_ref, g_ref, o_ref):
    o_ref[...] = _rms(x_ref[...].astype(F32), g_ref[...]).astype(o_ref.dtype)


def _rmsnorm(x, g, out_dtype, tm=256):
    m, d = x.shape
    return pl.pallas_call(
        _rmsnorm_kernel,
        grid=(m // tm,),
        in_specs=[pl.BlockSpec((tm, d), lambda i: (i, 0)),
                  pl.BlockSpec((1, d), lambda i: (0, 0))],
        out_specs=pl.BlockSpec((tm, d), lambda i: (i, 0)),
        out_shape=jax.ShapeDtypeStruct((m, d), out_dtype),
        compiler_params=_cparams(("parallel",)),
        name="rmsnorm",
    )(x, g.reshape(1, d).astype(F32))


def _mm_epilogue(r, res_ref, o_ref, relu2):
    if relu2:
        r = jnp.square(jnp.maximum(r, 0.0))
    if res_ref is not None:
        r = r + res_ref[...]
    o_ref[...] = r.astype(o_ref.dtype)


def _mm_kernel(a_ref, w_ref, *rest, nk, relu2, has_res):
    res_ref = rest[0] if has_res else None
    o_ref = rest[1] if has_res else rest[0]
    if nk == 1:
        _mm_epilogue(_dot(a_ref[...], w_ref[...]), res_ref, o_ref, relu2)
        return
    acc_ref = rest[-1]
    k = pl.program_id(2)

    @pl.when(k == 0)
    def _():
        acc_ref[...] = _dot(a_ref[...], w_ref[...])

    @pl.when(k > 0)
    def _():
        acc_ref[...] += _dot(a_ref[...], w_ref[...])

    @pl.when(k == nk - 1)
    def _():
        _mm_epilogue(acc_ref[...], res_ref, o_ref, relu2)


def _matmul(a, w, out_dtype, *, res=None, relu2=False, tm=1024, tn=512, tk=4096, name="matmul"):
    m, kd = a.shape
    n = w.shape[1]
    tm, tn, tk = min(tm, m), min(tn, n), min(tk, kd)
    assert m % tm == 0 and n % tn == 0 and kd % tk == 0
    nk = kd // tk
    in_specs = [pl.BlockSpec((tm, tk), lambda i, j, k: (i, k)),
                pl.BlockSpec((tk, tn), lambda i, j, k: (k, j))]
    args = [a, w]
    if res is not None:
        in_specs.append(pl.BlockSpec((tm, tn), lambda i, j, k: (i, j)))
        args.append(res)
    return pl.pallas_call(
        functools.partial(_mm_kernel, nk=nk, relu2=relu2, has_res=res is not None),
        grid=(m // tm, n // tn, nk),
        in_specs=in_specs,
        out_specs=pl.BlockSpec((tm, tn), lambda i, j, k: (i, j)),
        out_shape=jax.ShapeDtypeStruct((m, n), out_dtype),
        scratch_shapes=[pltpu.VMEM((tm, tn), F32)] if nk > 1 else [],
        compiler_params=_cparams(("parallel", "parallel", "arbitrary")),
        name=name,
    )(*args)


def _out_proj_kernel(a0, a1, a2, a3, w_ref, res_ref, o_ref):
    acc = res_ref[...]
    for g, a_ref in enumerate((a0, a1, a2, a3)):
        acc = acc + _dot(a_ref[...], w_ref[g * GROUP_WIDTH:(g + 1) * GROUP_WIDTH, :])
    o_ref[...] = acc


def _out_proj(mixes, w, res, tm=1024, tn=512):
    m = res.shape[0]
    n = w.shape[1]
    a_spec = pl.BlockSpec((tm, GROUP_WIDTH), lambda i, j: (i, 0))
    return pl.pallas_call(
        _out_proj_kernel,
        grid=(m // tm, n // tn),
        in_specs=[a_spec, a_spec, a_spec, a_spec,
                  pl.BlockSpec((w.shape[0], tn), lambda i, j: (0, j)),
                  pl.BlockSpec((tm, tn), lambda i, j: (i, j))],
        out_specs=pl.BlockSpec((tm, tn), lambda i, j: (i, j)),
        out_shape=jax.ShapeDtypeStruct((m, n), F32),
        compiler_params=_cparams(("parallel", "parallel")),
        name="out_proj",
    )(*mixes, w, res)


def _rope_kernel(pos_ref, tab_ref, cos_ref, sin_ref):
    ang = pos_ref[...].astype(F32) * tab_ref[0:1, :]
    cos_ref[...] = jnp.cos(ang) * tab_ref[1:2, :]
    sin_ref[...] = jnp.sin(ang) * tab_ref[2:3, :]


def _rope_tables(pos_col, tm=1024):
    m = pos_col.shape[0]
    half = MLA_ROPE // 2
    inv = ROPE_THETA ** (-jnp.arange(half, dtype=F32) * 2.0 / MLA_ROPE)
    zeros = jnp.zeros((LANES - MLA_ROPE,), F32)
    ones = jnp.ones((half,), F32)
    tab = jnp.stack([jnp.concatenate([inv, inv, zeros]),
                     jnp.concatenate([ones, ones, zeros]),
                     jnp.concatenate([-ones, ones, zeros])]
                    + [jnp.zeros((LANES,), F32)] * 5)
    spec = pl.BlockSpec((tm, LANES), lambda i: (i, 0))
    return pl.pallas_call(
        _rope_kernel,
        grid=(m // tm,),
        in_specs=[pl.BlockSpec((tm, 1), lambda i: (i, 0)),
                  pl.BlockSpec((8, LANES), lambda i: (0, 0))],
        out_specs=[spec, spec],
        out_shape=[jax.ShapeDtypeStruct((m, LANES), F32)] * 2,
        compiler_params=_cparams(("parallel",)),
        name="rope_tables",
    )(pos_col, tab)


def _mla_q_kernel(cq_ref, g_ref, w_ref, cos_ref, sin_ref, o_ref):
    n = _rms(cq_ref[...].astype(F32), g_ref[...]).astype(BF16)
    t = _dot(n, w_ref[...])
    o_ref[:, :GROUP_WIDTH] = t[:, :GROUP_WIDTH].astype(o_ref.dtype)
    cos, sin = cos_ref[...], sin_ref[...]
    for h in range(GROUP_HEADS):
        lo = GROUP_WIDTH + h * LANES
        pe = t[:, lo:lo + LANES] * cos + t[:, lo + GROUP_WIDTH:lo + GROUP_WIDTH + LANES] * sin
        o_ref[:, lo:lo + LANES] = pe.astype(o_ref.dtype)


def _mla_q(proj, g, w, cos, sin, tm=512):
    m = proj.shape[0]
    cq_block = PROJ_CQ0 // MLA_Q_RANK
    return pl.pallas_call(
        _mla_q_kernel,
        grid=(m // tm,),
        in_specs=[pl.BlockSpec((tm, MLA_Q_RANK), lambda i: (i, cq_block)),
                  pl.BlockSpec((1, MLA_Q_RANK), lambda i: (0, 0)),
                  pl.BlockSpec(w.shape, lambda i: (0, 0)),
                  pl.BlockSpec((tm, LANES), lambda i: (i, 0)),
                  pl.BlockSpec((tm, LANES), lambda i: (i, 0))],
        out_specs=pl.BlockSpec((tm, 2 * GROUP_WIDTH), lambda i: (i, 0)),
        out_shape=jax.ShapeDtypeStruct((m, 2 * GROUP_WIDTH), BF16),
        compiler_params=_cparams(("parallel",)),
        name="mla_q",
    )(proj, g.reshape(1, -1).astype(F32), w, cos, sin)


def _mla_kv_kernel(ckv_ref, g_ref, w_ref, kr_ref, cos_ref, sin_ref, kv_ref, kpe_ref):
    n = _rms(ckv_ref[...].astype(F32), g_ref[...]).astype(BF16)
    kv_ref[...] = _dot(n, w_ref[...]).astype(kv_ref.dtype)
    kr = kr_ref[...]
    kpe = kr[:, :LANES] * cos_ref[...] + kr[:, LANES:] * sin_ref[...]
    lane = lax.broadcasted_iota(jnp.int32, kpe.shape, 1)
    kpe = jnp.where((lane >= MLA_ROPE) & (lane < MLA_ROPE + 3), 1.0, kpe)
    kpe_ref[...] = kpe.astype(kpe_ref.dtype)


def _mla_kv(proj, g, w, kr, cos, sin, tm=512):
    m = proj.shape[0]
    ckv_block = PROJ_CKV0 // MLA_KV_RANK
    return pl.pallas_call(
        _mla_kv_kernel,
        grid=(m // tm,),
        in_specs=[pl.BlockSpec((tm, MLA_KV_RANK), lambda i: (i, ckv_block)),
                  pl.BlockSpec((1, MLA_KV_RANK), lambda i: (0, 0)),
                  pl.BlockSpec(w.shape, lambda i: (0, 0)),
                  pl.BlockSpec((tm, 2 * LANES), lambda i: (i, 0)),
                  pl.BlockSpec((tm, LANES), lambda i: (i, 0)),
                  pl.BlockSpec((tm, LANES), lambda i: (i, 0))],
        out_specs=[pl.BlockSpec((tm, 2 * GROUP_WIDTH), lambda i: (i, 0)),
                   pl.BlockSpec((tm, LANES), lambda i: (i, 0))],
        out_shape=[jax.ShapeDtypeStruct((m, 2 * GROUP_WIDTH), BF16),
                   jax.ShapeDtypeStruct((m, LANES), BF16)],
        compiler_params=_cparams(("parallel",)),
        name="mla_kv",
    )(proj, g.reshape(1, -1).astype(F32), w, kr, cos, sin)


def _split3(x):
    x1 = x.astype(BF16)
    r1 = x - x1.astype(F32)
    x2 = r1.astype(BF16)
    x3 = (r1 - x2.astype(F32)).astype(BF16)
    return x1, x2, x3


def _place(lane, cols, base, fill):
    out = fill
    for i, c in enumerate(cols):
        out = jnp.where(lane == base + i, c.astype(F32), out)
    return out


def _forget_cum_kernel(f_ref, b_ref, cum_ref, kx_ref, *, chunk):
    r = lax.broadcasted_iota(jnp.int32, (chunk, chunk), 0)
    c = lax.broadcasted_iota(jnp.int32, (chunk, chunk), 1)
    tri = jnp.where(r >= c, 1.0, 0.0).astype(BF16)
    lane = lax.broadcasted_iota(jnp.int32, (chunk, LANES), 1)
    unit_fill = jnp.where(lane < 3, 1.0, 0.0)

    def body(ci, carry):
        rows = _rows(ci, chunk)
        x = f_ref[rows, :] + b_ref[...]
        logf = (jnp.minimum(x, 0.0) - jnp.log1p(jnp.exp(-jnp.abs(x)))) * LOG2E
        parts = _split3(logf)
        cs = carry + _dot(tri, parts[0]) + _dot(tri, parts[1]) + _dot(tri, parts[2])
        cum_ref[rows, :] = cs
        for h in range(GROUP_HEADS):
            kx_ref[h, rows, :] = _place(lane, _split3(cs[:, h:h + 1]), 3, unit_fill).astype(kx_ref.dtype)
        return cs[chunk - 1:chunk, :]

    lax.fori_loop(0, f_ref.shape[0] // chunk, body, jnp.zeros((1, LANES), F32))


def _forget_cum(side, b_f, batch, seq):
    f_block = 2
    return pl.pallas_call(
        functools.partial(_forget_cum_kernel, chunk=256),
        grid=(batch,),
        in_specs=[pl.BlockSpec((seq, LANES), lambda b: (b, f_block)),
                  pl.BlockSpec((1, LANES), lambda b: (0, 0))],
        out_specs=[pl.BlockSpec((seq, LANES), lambda b: (b, 0)),
                   pl.BlockSpec((None, GROUP_HEADS, seq, LANES), lambda b: (b, 0, 0, 0))],
        out_shape=[jax.ShapeDtypeStruct((batch * seq, LANES), F32),
                   jax.ShapeDtypeStruct((batch, GROUP_HEADS, seq, LANES), BF16)],
        compiler_params=_cparams(("parallel",)),
        name="forget_cum",
    )(side, b_f)


def _tile_iota(t):
    return (lax.broadcasted_iota(jnp.int32, (t, t), 0),
            lax.broadcasted_iota(jnp.int32, (t, t), 1))


def _softmax_update(s, v, m, acc):
    m_new = jnp.maximum(m, jnp.max(s, axis=-1, keepdims=True))
    alpha = jnp.exp2(m - m_new)
    p = jnp.exp2(s - m_new).astype(BF16)
    v_aug = jnp.concatenate([v, jnp.ones_like(v)], axis=1)
    return m_new, alpha * acc + _dot(p, v_aug)


def _softmax_init(t):
    return (jnp.full((t, 1), NEG_INF, F32), jnp.zeros((t, 2 * HEAD_DIM), F32))


def _softmax_out(acc):
    return acc[:, :HEAD_DIM] / acc[:, HEAD_DIM:]


def _hs(h):
    return slice(h * HEAD_DIM, (h + 1) * HEAD_DIM)


def _rows(j, w):
    return pl.ds(pl.multiple_of(j * w, w), w)


def _causal_sweep(step, qi, init):
    carry = lax.fori_loop(0, qi, lambda j, c: step(j, c, False), init)
    return step(qi, carry, True)


def _update(s, v, state, mask, acc_ref, chain):
    if mask is not None:
        s = jnp.where(mask, s, NEG_INF)
    if acc_ref is None:
        return _softmax_update(s, v, *state)
    v_aug = jnp.concatenate([v, jnp.ones_like(v)], axis=1)
    acc_ref[chain] += _dot(jnp.exp2(s).astype(BF16), v_aug)
    return None


def _run_softmax(qi, t, n_chains, make_step, finish, acc_ref):
    acc_ref[...] = jnp.zeros_like(acc_ref)
    fast = make_step(acc_ref)

    def wide(j, carry):
        fast(j, None, False, 2 * t)
        return carry

    lax.fori_loop(0, qi // 2, wide, 0)

    @pl.when(qi % 2 == 1)
    def _():
        fast(qi - 1, None, False, t)

    fast(qi, None, True, t)
    accs = [acc_ref[c] for c in range(n_chains)]
    l_min = accs[0][:, HEAD_DIM:HEAD_DIM + 1]
    for a in accs[1:]:
        l_min = jnp.minimum(l_min, a[:, HEAD_DIM:HEAD_DIM + 1])
    ok = jnp.min(l_min) >= TRUSTED_ROW_SUM

    @pl.when(ok)
    def _():
        finish([_softmax_out(a) for a in accs])

    @pl.when(jnp.logical_not(ok))
    def _():
        slow = make_step(None)
        init = tuple(_softmax_init(t) for _ in range(n_chains))
        carry = lax.fori_loop(0, qi, lambda j, c: slow(j, c, False, t), init)
        finish([_softmax_out(c[1]) for c in slow(qi, carry, True, t)])


def _key_norm_max(k_ref, h, seq, extra_ref=None):
    chunk = ATT_T

    def body(c, mx):
        kk = k_ref[_rows(c, chunk), _hs(h)].astype(F32)
        n2 = jnp.sum(kk * kk, axis=1, keepdims=True)
        if extra_ref is not None:
            e = extra_ref[_rows(c, chunk), :].astype(F32)
            n2 = n2 + jnp.sum(e * e, axis=1, keepdims=True)
        return jnp.maximum(mx, n2)

    mx = lax.fori_loop(0, seq // chunk, body, jnp.zeros((chunk, 1), F32))
    return jnp.sqrt(jnp.max(mx, axis=0, keepdims=True))


def _store_key_norms(kmax_ref, qi, norm_fn):
    @pl.when(qi == 0)
    def _():
        for h in range(HEADS_PER_STEP):
            kmax_ref[h] = jnp.broadcast_to(norm_fn(h), kmax_ref.shape[1:])


def _row_norm(x):
    xf = x.astype(F32)
    return jnp.sqrt(jnp.sum(xf * xf, axis=1, keepdims=True))


def _fox_kernel(q_ref, k_ref, v_ref, kx_ref, cumq_ref, o_ref, kmax_ref, acc_ref, *, t, seq):
    pair, qi = pl.program_id(1), pl.program_id(2)
    _store_key_norms(kmax_ref, qi, lambda h: _key_norm_max(k_ref, h, seq))
    row, col = _tile_iota(t)
    lane = lax.broadcasted_iota(jnp.int32, (t, LANES), 1)
    minus_fill = jnp.where((lane >= 3) & (lane < 6), -1.0, 0.0)
    cumq = cumq_ref[...]

    def q_aug(h, fast):
        q = q_ref[:, _hs(h)]
        c = jnp.sum(jnp.where(lane == pair * HEADS_PER_STEP + h, cumq, 0.0), axis=1, keepdims=True)
        if fast:
            c = c - _row_norm(q) * kmax_ref[h][0:1, 0:1]
        return jnp.concatenate([q, _place(lane, _split3(c), 0, minus_fill).astype(BF16)], axis=1)

    def make_step(acc_ref):
        fast = acc_ref is not None
        qs = [q_aug(h, fast) for h in range(HEADS_PER_STEP)]

        def step(j, carry, masked, w):
            mask = (col <= row) if masked else None
            out = []
            for h in range(HEADS_PER_STEP):
                k = jnp.concatenate([k_ref[_rows(j, w), _hs(h)], kx_ref[h, _rows(j, w), :]], axis=1)
                out.append(_update(_dot_nt(qs[h], k), v_ref[_rows(j, w), _hs(h)], None if fast else carry[h], mask, acc_ref, h))
            return tuple(out)

        return step

    def finish(outs):
        o_ref[...] = jnp.concatenate(outs, axis=1).astype(o_ref.dtype)

    _run_softmax(qi, t, HEADS_PER_STEP, make_step, finish, acc_ref)


def _mla_kernel(qn_ref, qp_ref, kn_ref, kp_ref, v_ref, o_ref, kmax_ref, acc_ref, *, t, seq):
    qi = pl.program_id(2)
    _store_key_norms(kmax_ref, qi, lambda h: _key_norm_max(kn_ref, h, seq, kp_ref))
    row, col = _tile_iota(t)
    lane = lax.broadcasted_iota(jnp.int32, (t, LANES), 1)

    def q_aug(h, fast):
        qn, qp = qn_ref[:, _hs(h)], qp_ref[:, _hs(h)]
        if fast:
            qn_f, qp_f = qn.astype(F32), qp.astype(F32)
            norm = jnp.sqrt(jnp.sum(qn_f * qn_f, axis=1, keepdims=True)
                            + jnp.sum(qp_f * qp_f, axis=1, keepdims=True))
            qp = _place(lane, _split3(-norm * kmax_ref[h][0:1, 0:1]), MLA_ROPE, qp_f).astype(BF16)
        return jnp.concatenate([qn, qp], axis=1)

    def make_step(acc_ref):
        fast = acc_ref is not None
        qs = [q_aug(h, fast) for h in range(HEADS_PER_STEP)]

        def step(j, carry, masked, w):
            mask = (col <= row) if masked else None
            kp = kp_ref[_rows(j, w), :]
            out = []
            for h in range(HEADS_PER_STEP):
                k = jnp.concatenate([kn_ref[_rows(j, w), _hs(h)], kp], axis=1)
                out.append(_update(_dot_nt(qs[h], k), v_ref[_rows(j, w), _hs(h)], None if fast else carry[h], mask, acc_ref, h))
            return tuple(out)

        return step

    def finish(outs):
        o_ref[...] = jnp.concatenate(outs, axis=1).astype(o_ref.dtype)

    _run_softmax(qi, t, HEADS_PER_STEP, make_step, finish, acc_ref)


def _diff_kernel(lq1_ref, lk1_ref, lq2_ref, lk2_ref, q_ref, k_ref, v_ref, posq_ref, posk_ref,
                 bias_ref, subg_ref, o_ref, kmax_ref, acc_ref, *, t, seq, lam_init):
    qi = pl.program_id(2)
    _store_key_norms(kmax_ref, qi, lambda h: _key_norm_max(k_ref, h, seq))
    lam = (jnp.exp(jnp.sum(lq1_ref[...] * lk1_ref[...], axis=-1, keepdims=True))
           - jnp.exp(jnp.sum(lq2_ref[...] * lk2_ref[...], axis=-1, keepdims=True)) + lam_init)
    row, col = _tile_iota(t)
    lane = lax.broadcasted_iota(jnp.int32, (t, LANES), 1)
    zero_fill = jnp.zeros((t, LANES), F32)

    def key_units(w):
        return jnp.where(lax.broadcasted_iota(jnp.int32, (w, LANES), 1) < 3, 1.0, 0.0).astype(BF16)

    posq = posq_ref[...]
    posq_min = jnp.min(posq)
    bias_rows = [jnp.broadcast_to(bias_ref[h], (t, LANES)) for h in range(HEADS_PER_STEP)]
    last = REL_MAX_DIST - 1
    far_bias = [bias_ref[h][:, last:] for h in range(HEADS_PER_STEP)]

    def half_q(h, c):
        q = q_ref[:, _hs(h)]
        return jnp.where((lane < DIFF_QK) if c == 0 else (lane >= DIFF_QK), q, jnp.zeros_like(q))

    def make_step(acc_ref):
        fast = acc_ref is not None
        q_near, q_far = [], []
        for h in range(HEADS_PER_STEP):
            for c in range(2):
                qz = half_q(h, c)
                if fast:
                    bound = (_row_norm(qz) * kmax_ref[h][0:1, 0:1]
                             + jnp.max(bias_ref[h], axis=1, keepdims=True))
                    q_near.append(jnp.concatenate(
                        [qz, _place(lane, _split3(-bound), 0, zero_fill).astype(BF16)], axis=1))
                    q_far.append(jnp.concatenate(
                        [qz, _place(lane, _split3(far_bias[h] - bound), 0, zero_fill).astype(BF16)], axis=1))
                else:
                    q_near.append(qz)
                    q_far.append(qz)

        def chains(j, carry, masked, w, gathered):
            mask = (col <= row) if masked else None
            out = []
            for h in range(HEADS_PER_STEP):
                k = k_ref[_rows(j, w), _hs(h)]
                if fast:
                    k = jnp.concatenate([k, key_units(w)], axis=1)
                v = v_ref[_rows(j, w), _hs(h)]
                for c in range(2):
                    i = 2 * h + c
                    if gathered is not None:
                        s = _dot_nt(q_near[i], k) + gathered[h]
                    elif fast:
                        s = _dot_nt(q_far[i], k)
                    else:
                        s = _dot_nt(q_far[i], k) + far_bias[h]
                    out.append(_update(s, v, None if fast else carry[i], mask, acc_ref, i))
            return tuple(out)

        def step(j, carry, masked, w):
            posk = posk_ref[:, _rows(j, w)]

            def near(c):
                dist = jnp.clip(posq - posk, 0, last)
                gathered = [jnp.concatenate(
                    [jnp.take_along_axis(bias_rows[h], dist[:, cc * LANES:(cc + 1) * LANES], axis=1,
                                         mode="promise_in_bounds")
                     for cc in range(w // LANES)], axis=1) for h in range(HEADS_PER_STEP)]
                return chains(j, c, masked, w, gathered)

            def far(c):
                return chains(j, c, masked, w, None)

            return lax.cond(posq_min - jnp.max(posk) < last, near, far, carry)

        return step

    def finish(outs):
        normed = []
        for h in range(HEADS_PER_STEP):
            o = outs[2 * h] - lam * outs[2 * h + 1]
            normed.append(_rms(o, subg_ref[...]) * (1.0 - lam_init))
        o_ref[...] = jnp.concatenate(normed, axis=1).astype(o_ref.dtype)

    _run_softmax(qi, t, 2 * HEADS_PER_STEP, make_step, finish, acc_ref)


def _sb_kernel(q_ref, k_ref, v_ref, o_ref, *, t):
    qi = pl.program_id(2)
    w = t
    row, col = _tile_iota(t)
    later = jnp.where(row > col, 1.0, 0.0).astype(BF16)
    qs = [q_ref[:, _hs(h)] for h in range(SB_HEADS_PER_STEP)]

    def step(j, carry, masked):
        out = []
        for h in range(SB_HEADS_PER_STEP):
            tail, acc = carry[h]
            z = _dot_nt(qs[h], k_ref[_rows(j, w), _hs(h)])
            sp = jnp.log(1.0 + jnp.exp2(-jnp.abs(z))) * LOG2E
            log_beta = jnp.minimum(z, 0.0) - sp
            log_1m = -jnp.maximum(z, 0.0) - sp
            if masked:
                log_1m = jnp.where(col < row, log_1m, 0.0)
            hi = log_1m.astype(BF16)
            lo = (log_1m - hi.astype(F32)).astype(BF16)
            suffix = _dot(hi, later) + _dot(lo, later)
            a = jnp.exp2(log_beta + suffix + tail)
            if masked:
                a = jnp.where(col < row, a, 0.0)
            acc = acc + _dot(a.astype(BF16), v_ref[_rows(j, w), _hs(h)])
            tail = tail + suffix[:, :1] + log_1m[:, :1]
            out.append((tail, acc))
        return tuple(out)

    def live(carry):
        worst = carry[0][0]
        for h in range(1, SB_HEADS_PER_STEP):
            worst = jnp.maximum(worst, carry[h][0])
        return (jnp.max(worst) > SB_DEAD_LOG2).astype(jnp.int32)

    init = tuple((jnp.zeros((t, 1), F32), jnp.zeros((t, HEAD_DIM), F32)) for _ in range(SB_HEADS_PER_STEP))
    carry = step(qi, init, True)

    def body(state):
        j, _, c = state
        c = step(j, c, False)
        return j - 1, live(c), c

    _, _, carry = lax.while_loop(lambda st: (st[0] >= 0) & (st[1] > 0), body,
                                 (qi - 1, live(carry), carry))
    o_ref[...] = jnp.concatenate([c[1] for c in carry], axis=1).astype(o_ref.dtype)


def _attn_call(kernel, t, batch, seq, in_arrays, in_specs, name, hps=HEADS_PER_STEP, softmax_chains=0):
    nq = seq // t
    width = hps * HEAD_DIM
    scratch = []
    if softmax_chains:
        scratch = [pltpu.VMEM((hps, 8, LANES), F32),
                   pltpu.VMEM((softmax_chains, t, 2 * HEAD_DIM), F32)]
    return pl.pallas_call(
        kernel,
        grid=(batch, GROUP_HEADS // hps, nq),
        in_specs=in_specs,
        out_specs=pl.BlockSpec((t, width), lambda b, h, i: (b * nq + i, h)),
        out_shape=jax.ShapeDtypeStruct((batch * seq, GROUP_WIDTH), BF16),
        scratch_shapes=scratch,
        compiler_params=_cparams(("arbitrary", "arbitrary", "arbitrary")),
        name=name,
    )(*in_arrays)


def _q_spec(t, seq, col0, hps=HEADS_PER_STEP):
    nq = seq // t
    return pl.BlockSpec((t, hps * HEAD_DIM), lambda b, h, i: (b * nq + i, col0 + h))


def _kv_spec(seq, col0, hps=HEADS_PER_STEP):
    return pl.BlockSpec((seq, hps * HEAD_DIM), lambda b, h, i: (b, col0 + h))


def _shared_spec(seq):
    return pl.BlockSpec((seq, HEAD_DIM), lambda b, h, i: (b, 0))


def _full_spec(shape):
    return pl.BlockSpec(shape, lambda b, h, i: (0,) * len(shape))


def _mem_kernel(x_ref, g_ref, wq_ref, k_ref, v_ref, wo_ref, o_ref):
    x = x_ref[...]
    h = _rms(x, g_ref[...]).astype(BF16)
    q = _dot(h, wq_ref[...]).astype(BF16)
    outs = []
    for hd in range(MEM_HEADS):
        sl = slice(hd * MEM_HEAD_DIM, (hd + 1) * MEM_HEAD_DIM)
        s = _dot_nt(q[:, sl], k_ref[:, sl])
        p = jnp.exp2(s - jnp.max(s, axis=-1, keepdims=True))
        o = _dot(p.astype(BF16), v_ref[:, sl]) / jnp.sum(p, axis=-1, keepdims=True)
        outs.append(o.astype(BF16))
    o_ref[...] = x + _dot(jnp.concatenate(outs, axis=-1), wo_ref[...])


def _mem_block(x, g, wq, k, v, wo, batch, seq, tm=256):
    m, d = x.shape
    n_mem = k.shape[0] // batch
    tiles_per_batch = seq // tm
    width = MEM_HEADS * MEM_HEAD_DIM
    return pl.pallas_call(
        _mem_kernel,
        grid=(m // tm,),
        in_specs=[pl.BlockSpec((tm, d), lambda i: (i, 0)),
                  pl.BlockSpec((1, d), lambda i: (0, 0)),
                  pl.BlockSpec((d, width), lambda i: (0, 0)),
                  pl.BlockSpec((n_mem, width), lambda i: (i // tiles_per_batch, 0)),
                  pl.BlockSpec((n_mem, width), lambda i: (i // tiles_per_batch, 0)),
                  pl.BlockSpec((width, d), lambda i: (0, 0))],
        out_specs=pl.BlockSpec((tm, d), lambda i: (i, 0)),
        out_shape=jax.ShapeDtypeStruct((m, d), F32),
        compiler_params=_cparams(("parallel",)),
        name="mem_block",
    )(x, g.reshape(1, d).astype(F32), wq, k, v, wo)


def _rel_bias_by_distance(rel_table):
    d = jnp.arange(REL_MAX_DIST, dtype=jnp.int32)
    max_exact = REL_BUCKETS // 2
    nf = jnp.maximum(d, 1).astype(F32)
    large = max_exact + (jnp.log(nf / max_exact) / math.log(REL_MAX_DIST / max_exact)
                         * (REL_BUCKETS - max_exact)).astype(jnp.int32)
    large = jnp.minimum(large, REL_BUCKETS - 1)
    bucket = jnp.where(d < max_exact, d, large)
    return jnp.take(rel_table, bucket, axis=0).T.astype(F32)


def _in_proj_col_scale():
    gw = GROUP_WIDTH
    scale = np.ones((PROJ_COLS,), np.float32)
    scale[PROJ_A0:PROJ_A0 + gw] = DIFF_QK ** -0.5 * LOG2E
    scale[PROJ_C0:PROJ_C0 + gw] = HEAD_DIM ** -0.5 * LOG2E
    scale[PROJ_D0:PROJ_D0 + gw] = HEAD_DIM ** -0.5 * LOG2E
    return scale


def _prep_in_proj(w):
    ab = A_COLS + B_COLS
    cd = C_COLS + 3 * GROUP_WIDTH
    gap = jnp.zeros((w.shape[0], PROJ_C0 - ab), w.dtype)
    main = (jnp.concatenate([w[:, :ab], gap, w[:, ab:ab + cd]], axis=1)
            * _in_proj_col_scale()).astype(BF16)
    kr = w[:, A_COLS + MLA_Q_RANK + MLA_KV_RANK:ab]
    f = w[:, ab + cd:]
    half = MLA_ROPE // 2
    pad = jnp.zeros((w.shape[0], LANES - MLA_ROPE), w.dtype)
    f_pad = jnp.zeros((w.shape[0], 2 * LANES - GROUP_HEADS), w.dtype)
    side = jnp.concatenate([kr, pad, kr[:, half:], kr[:, :half], pad, f, f_pad], axis=1).astype(BF16)
    return main, side


def _prep_uq(w):
    qk = MLA_NOPE + MLA_ROPE
    half = MLA_ROPE // 2
    w = (w * (qk ** -0.5 * LOG2E)).reshape(w.shape[0], GROUP_HEADS, qk)
    pad = jnp.zeros((w.shape[0], GROUP_HEADS, LANES - MLA_ROPE), w.dtype)
    nope = w[:, :, :MLA_NOPE]
    r1 = w[:, :, MLA_NOPE:MLA_NOPE + half]
    r2 = w[:, :, MLA_NOPE + half:]
    rope = jnp.concatenate([r1, r2, pad], axis=-1)
    swapped = jnp.concatenate([r2, r1, pad], axis=-1)
    return jnp.concatenate([nope.reshape(w.shape[0], -1), rope.reshape(w.shape[0], -1),
                            swapped.reshape(w.shape[0], -1)], axis=1).astype(BF16)


def _prep_ukv(w):
    w = w.reshape(w.shape[0], GROUP_HEADS, MLA_NOPE + HEAD_DIM)
    return jnp.concatenate([w[:, :, :MLA_NOPE].reshape(w.shape[0], -1),
                            w[:, :, MLA_NOPE:].reshape(w.shape[0], -1)], axis=1).astype(BF16)


def kernel(x, mem, positions, attn_norm_g, w_in, w_out, rel_table, diff_lam_q1, diff_lam_k1, diff_lam_q2, diff_lam_k2, diff_sub_g, mla_q_norm_g, mla_kv_norm_g, mla_w_uq, mla_w_ukv, fox_b_f, mem_q_norm_g, mem_kv_norm_g, mem_w_q, mem_w_k, mem_w_v, mem_w_o, mlp_norm_g, w_up, w_down, final_norm_g):
    batch, seq, d = x.shape
    m = batch * seq
    gw = GROUP_WIDTH
    nh = GROUP_HEADS
    xf = x.reshape(m, d)
    mem2 = mem.reshape(batch * mem.shape[1], d)
    pos_col = positions.reshape(m, 1).astype(jnp.int32)
    pos_row = positions.reshape(batch, 1, seq).astype(jnp.int32)
    cos, sin = _rope_tables(pos_col)
    bias_by_dist = (_rel_bias_by_distance(rel_table) * LOG2E).reshape(nh, 1, LANES)
    npair = nh // HEADS_PER_STEP
    pair_cols = HEADS_PER_STEP * HEAD_DIM
    a0, d0 = PROJ_A0 // pair_cols, PROJ_D0 // pair_cols
    nsb = nh // SB_HEADS_PER_STEP
    sb0 = PROJ_C0 // (SB_HEADS_PER_STEP * HEAD_DIM)

    for l in range(DEPTH):
        w_main, w_side = _prep_in_proj(w_in[l])
        h = _rmsnorm(xf, attn_norm_g[l], BF16)
        proj = _matmul(h, w_main, BF16, name="in_proj")
        side = _matmul(h, w_side, F32, name="in_proj_side")
        b_f = jnp.concatenate([fox_b_f[l].astype(F32), jnp.zeros((LANES - nh,), F32)]).reshape(1, LANES)
        cum, fox_kx = _forget_cum(side, b_f, batch, seq)

        lam_init = 0.8 - 0.6 * math.exp(-0.3 * l)
        lam_vecs = [v[l].reshape(1, DIFF_QK).astype(F32)
                    for v in (diff_lam_q1, diff_lam_k1, diff_lam_q2, diff_lam_k2)]
        o_a = _attn_call(
            functools.partial(_diff_kernel, t=ATT_T, seq=seq, lam_init=lam_init), ATT_T, batch, seq,
            lam_vecs + [proj, proj, proj, pos_col, pos_row, bias_by_dist,
                        diff_sub_g[l].reshape(1, HEAD_DIM).astype(F32)],
            [_full_spec((1, DIFF_QK))] * 4
            + [_q_spec(ATT_T, seq, a0), _kv_spec(seq, a0 + npair), _kv_spec(seq, a0 + 2 * npair),
               pl.BlockSpec((ATT_T, 1), lambda b, h, i: (b * (seq // ATT_T) + i, 0)),
               pl.BlockSpec((None, 1, seq), lambda b, h, i: (b, 0, 0)),
               pl.BlockSpec((HEADS_PER_STEP, 1, LANES), lambda b, h, i: (h, 0, 0)),
               _full_spec((1, HEAD_DIM))],
            "diff_attn", softmax_chains=2 * HEADS_PER_STEP)

        q_b = _mla_q(proj, mla_q_norm_g[l], _prep_uq(mla_w_uq[l]), cos, sin)
        kv_b, kpe_b = _mla_kv(proj, mla_kv_norm_g[l], _prep_ukv(mla_w_ukv[l]), side, cos, sin)
        o_b = _attn_call(
            functools.partial(_mla_kernel, t=ATT_T, seq=seq), ATT_T, batch, seq,
            [q_b, q_b, kv_b, kpe_b, kv_b],
            [_q_spec(ATT_T, seq, 0), _q_spec(ATT_T, seq, npair), _kv_spec(seq, 0), _shared_spec(seq),
             _kv_spec(seq, npair)],
            "mla_attn", softmax_chains=HEADS_PER_STEP)

        o_c = _attn_call(
            functools.partial(_sb_kernel, t=SB_T), SB_T, batch, seq,
            [proj, proj, proj],
            [_q_spec(SB_T, seq, sb0, SB_HEADS_PER_STEP), _kv_spec(seq, sb0 + nsb, SB_HEADS_PER_STEP),
             _kv_spec(seq, sb0 + 2 * nsb, SB_HEADS_PER_STEP)],
            "stickbreak_attn", hps=SB_HEADS_PER_STEP)

        o_d = _attn_call(
            functools.partial(_fox_kernel, t=ATT_T, seq=seq), ATT_T, batch, seq,
            [proj, proj, proj, fox_kx, cum],
            [_q_spec(ATT_T, seq, d0), _kv_spec(seq, d0 + npair), _kv_spec(seq, d0 + 2 * npair),
             pl.BlockSpec((None, HEADS_PER_STEP, seq, LANES), lambda b, h, i: (b, h, 0, 0)),
             pl.BlockSpec((ATT_T, LANES), lambda b, h, i: (b * (seq // ATT_T) + i, 0))],
            "fox_attn", softmax_chains=HEADS_PER_STEP)

        xf = _out_proj([o_a, o_b, o_c, o_d], w_out[l].astype(BF16), xf)

        mem_n = _rmsnorm(mem2, mem_kv_norm_g[l], BF16)
        k_m = _matmul(mem_n, mem_w_k[l].astype(BF16), BF16, name="mem_k")
        v_m = _matmul(mem_n, mem_w_v[l].astype(BF16), BF16, name="mem_v")
        xf = _mem_block(xf, mem_q_norm_g[l], (mem_w_q[l] * (MEM_HEAD_DIM ** -0.5 * LOG2E)).astype(BF16),
                        k_m, v_m, mem_w_o[l].astype(BF16), batch, seq)

        h = _rmsnorm(xf, mlp_norm_g[l], BF16)
        u = _matmul(h, w_up[l].astype(BF16), BF16, relu2=True, name="mlp_up")
        xf = _matmul(u, w_down[l].astype(BF16), F32, res=xf, name="mlp_down")

    return _rmsnorm(xf, final_norm_g, F32).reshape(batch, seq, d)
```

```python
import functools
import math

import numpy as np
import jax
import jax.numpy as jnp
from jax import lax
from jax.experimental import pallas as pl
from jax.experimental.pallas import tpu as pltpu

F32 = jnp.float32
BF16 = jnp.bfloat16

D_MODEL = 4096
DEPTH = 2
HEAD_DIM = 128
GROUP_HEADS = 8
GROUP_WIDTH = GROUP_HEADS * HEAD_DIM
DIFF_QK = HEAD_DIM // 2
MLA_Q_RANK = 1536
MLA_KV_RANK = 512
MLA_NOPE = 128
MLA_ROPE = 64
ROPE_THETA = 10000.0
REL_BUCKETS = 32
REL_MAX_DIST = 128
MEM_HEADS = 4
MEM_HEAD_DIM = 128
NORM_EPS = 1e-6
NEG_INF = -1e30

A_COLS = 3 * GROUP_WIDTH
B_COLS = MLA_Q_RANK + MLA_KV_RANK + MLA_ROPE
C_COLS = 3 * GROUP_WIDTH

PROJ_A0 = 0
PROJ_CQ0 = A_COLS
PROJ_CKV0 = A_COLS + MLA_Q_RANK
PROJ_C0 = 5632
PROJ_D0 = PROJ_C0 + C_COLS
PROJ_COLS = PROJ_D0 + 3 * GROUP_WIDTH

LANES = 128
VMEM_LIMIT = 48 * 1024 * 1024

LOG2E = math.log2(math.e)

ATT_T = 512
SB_T = 256
HEADS_PER_STEP = 2
SB_HEADS_PER_STEP = 4
SB_DEAD_LOG2 = -160.0
TRUSTED_ROW_SUM = 2.0 ** -60


def _cparams(sem):
    return pltpu.CompilerParams(dimension_semantics=sem, vmem_limit_bytes=VMEM_LIMIT)


def _dot(a, b):
    return jnp.dot(a, b, preferred_element_type=F32)


def _dot_nt(a, b):
    return lax.dot_general(a, b, (((1,), (1,)), ((), ())), preferred_element_type=F32)


def _rms(x, g):
    return x * lax.rsqrt(jnp.mean(x * x, axis=-1, keepdims=True) + NORM_EPS) * g


def _rmsnorm_kernel(x_ref, g_ref, o_ref):
    o_ref[...] = _rms(x_ref[...].astype(F32), g_ref[...]).astype(o_ref.dtype)


def _rmsnorm(x, g, out_dtype, tm=256):
    m, d = x.shape
    return pl.pallas_call(
        _rmsnorm_kernel,
        grid=(m // tm,),
        in_specs=[pl.BlockSpec((tm, d), lambda i: (i, 0)),
                  pl.BlockSpec((1, d), lambda i: (0, 0))],
        out_specs=pl.BlockSpec((tm, d), lambda i: (i, 0)),
        out_shape=jax.ShapeDtypeStruct((m, d), out_dtype),
        compiler_params=_cparams(("parallel",)),
        name="rmsnorm",
    )(x, g.reshape(1, d).astype(F32))


def _mm_epilogue(r, res_ref, o_ref, relu2):
    if relu2:
        r = jnp.square(jnp.maximum(r, 0.0))
    if res_ref is not None:
        r = r + res_ref[...]
    o_ref[...] = r.astype(o_ref.dtype)


def _mm_kernel(a_ref, w_ref, *rest, nk, relu2, has_res):
    res_ref = rest[0] if has_res else None
    o_ref = rest[1] if has_res else rest[0]
    if nk == 1:
        _mm_epilogue(_dot(a_ref[...], w_ref[...]), res_ref, o_ref, relu2)
        return
    acc_ref = rest[-1]
    k = pl.program_id(2)

    @pl.when(k == 0)
    def _():
        acc_ref[...] = _dot(a_ref[...], w_ref[...])

    @pl.when(k > 0)
    def _():
        acc_ref[...] += _dot(a_ref[...], w_ref[...])

    @pl.when(k == nk - 1)
    def _():
        _mm_epilogue(acc_ref[...], res_ref, o_ref, relu2)


def _matmul(a, w, out_dtype, *, res=None, relu2=False, tm=1024, tn=512, tk=4096, name="matmul"):
    m, kd = a.shape
    n = w.shape[1]
    tm, tn, tk = min(tm, m), min(tn, n), min(tk, kd)
    assert m % tm == 0 and n % tn == 0 and kd % tk == 0
    nk = kd // tk
    in_specs = [pl.BlockSpec((tm, tk), lambda i, j, k: (i, k)),
                pl.BlockSpec((tk, tn), lambda i, j, k: (k, j))]
    args = [a, w]
    if res is not None:
        in_specs.append(pl.BlockSpec((tm, tn), lambda i, j, k: (i, j)))
        args.append(res)
    return pl.pallas_call(
        functools.partial(_mm_kernel, nk=nk, relu2=relu2, has_res=res is not None),
        grid=(m // tm, n // tn, nk),
        in_specs=in_specs,
        out_specs=pl.BlockSpec((tm, tn), lambda i, j, k: (i, j)),
        out_shape=jax.ShapeDtypeStruct((m, n), out_dtype),
        scratch_shapes=[pltpu.VMEM((tm, tn), F32)] if nk > 1 else [],
        compiler_params=_cparams(("parallel", "parallel", "arbitrary")),
        name=name,
    )(*args)


def _out_proj_kernel(a0, a1, a2, a3, w_ref, res_ref, o_ref):
    acc = res_ref[...]
    for g, a_ref in enumerate((a0, a1, a2, a3)):
        acc = acc + _dot(a_ref[...], w_ref[g * GROUP_WIDTH:(g + 1) * GROUP_WIDTH, :])
    o_ref[...] = acc


def _out_proj(mixes, w, res, tm=1024, tn=512):
    m = res.shape[0]
    n = w.shape[1]
    a_spec = pl.BlockSpec((tm, GROUP_WIDTH), lambda i, j: (i, 0))
    return pl.pallas_call(
        _out_proj_kernel,
        grid=(m // tm, n // tn),
        in_specs=[a_spec, a_spec, a_spec, a_spec,
                  pl.BlockSpec((w.shape[0], tn), lambda i, j: (0, j)),
                  pl.BlockSpec((tm, tn), lambda i, j: (i, j))],
        out_specs=pl.BlockSpec((tm, tn), lambda i, j: (i, j)),
        out_shape=jax.ShapeDtypeStruct((m, n), F32),
        compiler_params=_cparams(("parallel", "parallel")),
        name="out_proj",
    )(*mixes, w, res)


def _rope_kernel(pos_ref, tab_ref, cos_ref, sin_ref):
    ang = pos_ref[...].astype(F32) * tab_ref[0:1, :]
    cos_ref[...] = jnp.cos(ang) * tab_ref[1:2, :]
    sin_ref[...] = jnp.sin(ang) * tab_ref[2:3, :]


def _rope_tables(pos_col, tm=1024):
    m = pos_col.shape[0]
    half = MLA_ROPE // 2
    inv = ROPE_THETA ** (-jnp.arange(half, dtype=F32) * 2.0 / MLA_ROPE)
    zeros = jnp.zeros((LANES - MLA_ROPE,), F32)
    ones = jnp.ones((half,), F32)
    tab = jnp.stack([jnp.concatenate([inv, inv, zeros]),
                     jnp.concatenate([ones, ones, zeros]),
                     jnp.concatenate([-ones, ones, zeros])]
                    + [jnp.zeros((LANES,), F32)] * 5)
    spec = pl.BlockSpec((tm, LANES), lambda i: (i, 0))
    return pl.pallas_call(
        _rope_kernel,
        grid=(m // tm,),
        in_specs=[pl.BlockSpec((tm, 1), lambda i: (i, 0)),
                  pl.BlockSpec((8, LANES), lambda i: (0, 0))],
        out_specs=[spec, spec],
        out_shape=[jax.ShapeDtypeStruct((m, LANES), F32)] * 2,
        compiler_params=_cparams(("parallel",)),
        name="rope_tables",
    )(pos_col, tab)


def _mla_q_kernel(cq_ref, g_ref, w_ref, cos_ref, sin_ref, o_ref):
    n = _rms(cq_ref[...].astype(F32), g_ref[...]).astype(BF16)
    t = _dot(n, w_ref[...])
    o_ref[:, :GROUP_WIDTH] = t[:, :GROUP_WIDTH].astype(o_ref.dtype)
    cos, sin = cos_ref[...], sin_ref[...]
    for h in range(GROUP_HEADS):
        lo = GROUP_WIDTH + h * LANES
        pe = t[:, lo:lo + LANES] * cos + t[:, lo + GROUP_WIDTH:lo + GROUP_WIDTH + LANES] * sin
        o_ref[:, lo:lo + LANES] = pe.astype(o_ref.dtype)


def _mla_q(proj, g, w, cos, sin, tm=512):
    m = proj.shape[0]
    cq_block = PROJ_CQ0 // MLA_Q_RANK
    return pl.pallas_call(
        _mla_q_kernel,
        grid=(m // tm,),
        in_specs=[pl.BlockSpec((tm, MLA_Q_RANK), lambda i: (i, cq_block)),
                  pl.BlockSpec((1, MLA_Q_RANK), lambda i: (0, 0)),
                  pl.BlockSpec(w.shape, lambda i: (0, 0)),
                  pl.BlockSpec((tm, LANES), lambda i: (i, 0)),
                  pl.BlockSpec((tm, LANES), lambda i: (i, 0))],
        out_specs=pl.BlockSpec((tm, 2 * GROUP_WIDTH), lambda i: (i, 0)),
        out_shape=jax.ShapeDtypeStruct((m, 2 * GROUP_WIDTH), BF16),
        compiler_params=_cparams(("parallel",)),
        name="mla_q",
    )(proj, g.reshape(1, -1).astype(F32), w, cos, sin)


def _mla_kv_kernel(ckv_ref, g_ref, w_ref, kr_ref, cos_ref, sin_ref, kv_ref, kpe_ref):
    n = _rms(ckv_ref[...].astype(F32), g_ref[...]).astype(BF16)
    kv_ref[...] = _dot(n, w_ref[...]).astype(kv_ref.dtype)
    kr = kr_ref[...]
    kpe = kr[:, :LANES] * cos_ref[...] + kr[:, LANES:] * sin_ref[...]
    lane = lax.broadcasted_iota(jnp.int32, kpe.shape, 1)
    kpe = jnp.where((lane >= MLA_ROPE) & (lane < MLA_ROPE + 3), 1.0, kpe)
    kpe_ref[...] = kpe.astype(kpe_ref.dtype)


def _mla_kv(proj, g, w, kr, cos, sin, tm=512):
    m = proj.shape[0]
    ckv_block = PROJ_CKV0 // MLA_KV_RANK
    return pl.pallas_call(
        _mla_kv_kernel,
        grid=(m // tm,),
        in_specs=[pl.BlockSpec((tm, MLA_KV_RANK), lambda i: (i, ckv_block)),
                  pl.BlockSpec((1, MLA_KV_RANK), lambda i: (0, 0)),
                  pl.BlockSpec(w.shape, lambda i: (0, 0)),
                  pl.BlockSpec((tm, 2 * LANES), lambda i: (i, 0)),
                  pl.BlockSpec((tm, LANES), lambda i: (i, 0)),
                  pl.BlockSpec((tm, LANES), lambda i: (i, 0))],
        out_specs=[pl.BlockSpec((tm, 2 * GROUP_WIDTH), lambda i: (i, 0)),
                   pl.BlockSpec((tm, LANES), lambda i: (i, 0))],
        out_shape=[jax.ShapeDtypeStruct((m, 2 * GROUP_WIDTH), BF16),
                   jax.ShapeDtypeStruct((m, LANES), BF16)],
        compiler_params=_cparams(("parallel",)),
        name="mla_kv",
    )(proj, g.reshape(1, -1).astype(F32), w, kr, cos, sin)


def _split3(x):
    x1 = x.astype(BF16)
    r1 = x - x1.astype(F32)
    x2 = r1.astype(BF16)
    x3 = (r1 - x2.astype(F32)).astype(BF16)
    return x1, x2, x3


def _place(lane, cols, base, fill):
    out = fill
    for i, c in enumerate(cols):
        out = jnp.where(lane == base + i, c.astype(F32), out)
    return out


def _forget_cum_kernel(f_ref, b_ref, cum_ref, kx_ref, *, chunk):
    r = lax.broadcasted_iota(jnp.int32, (chunk, chunk), 0)
    c = lax.broadcasted_iota(jnp.int32, (chunk, chunk), 1)
    tri = jnp.where(r >= c, 1.0, 0.0).astype(BF16)
    lane = lax.broadcasted_iota(jnp.int32, (chunk, LANES), 1)
    unit_fill = jnp.where(lane < 3, 1.0, 0.0)

    def body(ci, carry):
        rows = _rows(ci, chunk)
        x = f_ref[rows, :] + b_ref[...]
        logf = (jnp.minimum(x, 0.0) - jnp.log1p(jnp.exp(-jnp.abs(x)))) * LOG2E
        parts = _split3(logf)
        cs = carry + _dot(tri, parts[0]) + _dot(tri, parts[1]) + _dot(tri, parts[2])
        cum_ref[rows, :] = cs
        for h in range(GROUP_HEADS):
            kx_ref[h, rows, :] = _place(lane, _split3(cs[:, h:h + 1]), 3, unit_fill).astype(kx_ref.dtype)
        return cs[chunk - 1:chunk, :]

    lax.fori_loop(0, f_ref.shape[0] // chunk, body, jnp.zeros((1, LANES), F32))


def _forget_cum(side, b_f, batch, seq):
    f_block = 2
    return pl.pallas_call(
        functools.partial(_forget_cum_kernel, chunk=256),
        grid=(batch,),
        in_specs=[pl.BlockSpec((seq, LANES), lambda b: (b, f_block)),
                  pl.BlockSpec((1, LANES), lambda b: (0, 0))],
        out_specs=[pl.BlockSpec((seq, LANES), lambda b: (b, 0)),
                   pl.BlockSpec((None, GROUP_HEADS, seq, LANES), lambda b: (b, 0, 0, 0))],
        out_shape=[jax.ShapeDtypeStruct((batch * seq, LANES), F32),
                   jax.ShapeDtypeStruct((batch, GROUP_HEADS, seq, LANES), BF16)],
        compiler_params=_cparams(("parallel",)),
        name="forget_cum",
    )(side, b_f)


def _tile_iota(t):
    return (lax.broadcasted_iota(jnp.int32, (t, t), 0),
            lax.broadcasted_iota(jnp.int32, (t, t), 1))


def _softmax_update(s, v, m, acc):
    m_new = jnp.maximum(m, jnp.max(s, axis=-1, keepdims=True))
    alpha = jnp.exp2(m - m_new)
    p = jnp.exp2(s - m_new).astype(BF16)
    v_aug = jnp.concatenate([v, jnp.ones_like(v)], axis=1)
    return m_new, alpha * acc + _dot(p, v_aug)


def _softmax_init(t):
    return (jnp.full((t, 1), NEG_INF, F32), jnp.zeros((t, 2 * HEAD_DIM), F32))


def _softmax_out(acc):
    return acc[:, :HEAD_DIM] / acc[:, HEAD_DIM:]


def _hs(h):
    return slice(h * HEAD_DIM, (h + 1) * HEAD_DIM)


def _rows(j, w):
    return pl.ds(pl.multiple_of(j * w, w), w)


def _causal_sweep(step, qi, init):
    carry = lax.fori_loop(0, qi, lambda j, c: step(j, c, False), init)
    return step(qi, carry, True)


def _update(s, v, state, mask, acc_ref, chain):
    if mask is not None:
        s = jnp.where(mask, s, NEG_INF)
    if acc_ref is None:
        return _softmax_update(s, v, *state)
    v_aug = jnp.concatenate([v, jnp.ones_like(v)], axis=1)
    acc_ref[chain] += _dot(jnp.exp2(s).astype(BF16), v_aug)
    return None


def _run_softmax(qi, t, n_chains, make_step, finish, acc_ref):
    acc_ref[...] = jnp.zeros_like(acc_ref)
    fast = make_step(acc_ref)

    def wide(j, carry):
        fast(j, None, False, 2 * t)
        return carry

    lax.fori_loop(0, qi // 2, wide, 0)

    @pl.when(qi % 2 == 1)
    def _():
        fast(qi - 1, None, False, t)

    fast(qi, None, True, t)
    accs = [acc_ref[c] for c in range(n_chains)]
    l_min = accs[0][:, HEAD_DIM:HEAD_DIM + 1]
    for a in accs[1:]:
        l_min = jnp.minimum(l_min, a[:, HEAD_DIM:HEAD_DIM + 1])
    ok = jnp.min(l_min) >= TRUSTED_ROW_SUM

    @pl.when(ok)
    def _():
        finish([_softmax_out(a) for a in accs])

    @pl.when(jnp.logical_not(ok))
    def _():
        slow = make_step(None)
        init = tuple(_softmax_init(t) for _ in range(n_chains))
        carry = lax.fori_loop(0, qi, lambda j, c: slow(j, c, False, t), init)
        finish([_softmax_out(c[1]) for c in slow(qi, carry, True, t)])


def _key_norm_max(k_ref, h, seq, extra_ref=None):
    chunk = ATT_T

    def body(c, mx):
        kk = k_ref[_rows(c, chunk), _hs(h)].astype(F32)
        n2 = jnp.sum(kk * kk, axis=1, keepdims=True)
        if extra_ref is not None:
            e = extra_ref[_rows(c, chunk), :].astype(F32)
            n2 = n2 + jnp.sum(e * e, axis=1, keepdims=True)
        return jnp.maximum(mx, n2)

    mx = lax.fori_loop(0, seq // chunk, body, jnp.zeros((chunk, 1), F32))
    return jnp.sqrt(jnp.max(mx, axis=0, keepdims=True))


def _store_key_norms(kmax_ref, qi, norm_fn):
    @pl.when(qi == 0)
    def _():
        for h in range(HEADS_PER_STEP):
            kmax_ref[h] = jnp.broadcast_to(norm_fn(h), kmax_ref.shape[1:])


def _row_norm(x):
    xf = x.astype(F32)
    return jnp.sqrt(jnp.sum(xf * xf, axis=1, keepdims=True))


def _fox_kernel(q_ref, k_ref, v_ref, kx_ref, cumq_ref, o_ref, kmax_ref, acc_ref, *, t, seq):
    pair, qi = pl.program_id(1), pl.program_id(2)
    _store_key_norms(kmax_ref, qi, lambda h: _key_norm_max(k_ref, h, seq))
    row, col = _tile_iota(t)
    lane = lax.broadcasted_iota(jnp.int32, (t, LANES), 1)
    minus_fill = jnp.where((lane >= 3) & (lane < 6), -1.0, 0.0)
    cumq = cumq_ref[...]

    def q_aug(h, fast):
        q = q_ref[:, _hs(h)]
        c = jnp.sum(jnp.where(lane == pair * HEADS_PER_STEP + h, cumq, 0.0), axis=1, keepdims=True)
        if fast:
            c = c - _row_norm(q) * kmax_ref[h][0:1, 0:1]
        return jnp.concatenate([q, _place(lane, _split3(c), 0, minus_fill).astype(BF16)], axis=1)

    def make_step(acc_ref):
        fast = acc_ref is not None
        qs = [q_aug(h, fast) for h in range(HEADS_PER_STEP)]

        def step(j, carry, masked, w):
            mask = (col <= row) if masked else None
            out = []
            for h in range(HEADS_PER_STEP):
                k = jnp.concatenate([k_ref[_rows(j, w), _hs(h)], kx_ref[h, _rows(j, w), :]], axis=1)
                out.append(_update(_dot_nt(qs[h], k), v_ref[_rows(j, w), _hs(h)], None if fast else carry[h], mask, acc_ref, h))
            return tuple(out)

        return step

    def finish(outs):
        o_ref[...] = jnp.concatenate(outs, axis=1).astype(o_ref.dtype)

    _run_softmax(qi, t, HEADS_PER_STEP, make_step, finish, acc_ref)


def _mla_kernel(qn_ref, qp_ref, kn_ref, kp_ref, v_ref, o_ref, kmax_ref, acc_ref, *, t, seq):
    qi = pl.program_id(2)
    _store_key_norms(kmax_ref, qi, lambda h: _key_norm_max(kn_ref, h, seq, kp_ref))
    row, col = _tile_iota(t)
    lane = lax.broadcasted_iota(jnp.int32, (t, LANES), 1)

    def q_aug(h, fast):
        qn, qp = qn_ref[:, _hs(h)], qp_ref[:, _hs(h)]
        if fast:
            qn_f, qp_f = qn.astype(F32), qp.astype(F32)
            norm = jnp.sqrt(jnp.sum(qn_f * qn_f, axis=1, keepdims=True)
                            + jnp.sum(qp_f * qp_f, axis=1, keepdims=True))
            qp = _place(lane, _split3(-norm * kmax_ref[h][0:1, 0:1]), MLA_ROPE, qp_f).astype(BF16)
        return jnp.concatenate([qn, qp], axis=1)

    def make_step(acc_ref):
        fast = acc_ref is not None
        qs = [q_aug(h, fast) for h in range(HEADS_PER_STEP)]

        def step(j, carry, masked, w):
            mask = (col <= row) if masked else None
            kp = kp_ref[_rows(j, w), :]
            out = []
            for h in range(HEADS_PER_STEP):
                k = jnp.concatenate([kn_ref[_rows(j, w), _hs(h)], kp], axis=1)
                out.append(_update(_dot_nt(qs[h], k), v_ref[_rows(j, w), _hs(h)], None if fast else carry[h], mask, acc_ref, h))
            return tuple(out)

        return step

    def finish(outs):
        o_ref[...] = jnp.concatenate(outs, axis=1).astype(o_ref.dtype)

    _run_softmax(qi, t, HEADS_PER_STEP, make_step, finish, acc_ref)


def _diff_kernel(lq1_ref, lk1_ref, lq2_ref, lk2_ref, q_ref, k_ref, v_ref, posq_ref, posk_ref,
                 bias_ref, subg_ref, o_ref, kmax_ref, acc_ref, *, t, seq, lam_init):
    qi = pl.program_id(2)
    _store_key_norms(kmax_ref, qi, lambda h: _key_norm_max(k_ref, h, seq))
    lam = (jnp.exp(jnp.sum(lq1_ref[...] * lk1_ref[...], axis=-1, keepdims=True))
           - jnp.exp(jnp.sum(lq2_ref[...] * lk2_ref[...], axis=-1, keepdims=True)) + lam_init)
    row, col = _tile_iota(t)
    lane = lax.broadcasted_iota(jnp.int32, (t, LANES), 1)
    zero_fill = jnp.zeros((t, LANES), F32)

    def key_units(w):
        return jnp.where(lax.broadcasted_iota(jnp.int32, (w, LANES), 1) < 3, 1.0, 0.0).astype(BF16)

    posq = posq_ref[...]
    posq_min = jnp.min(posq)
    bias_rows = [jnp.broadcast_to(bias_ref[h], (t, LANES)) for h in range(HEADS_PER_STEP)]
    last = REL_MAX_DIST - 1
    far_bias = [bias_ref[h][:, last:] for h in range(HEADS_PER_STEP)]

    def half_q(h, c):
        q = q_ref[:, _hs(h)]
        return jnp.where((lane < DIFF_QK) if c == 0 else (lane >= DIFF_QK), q, jnp.zeros_like(q))

    def make_step(acc_ref):
        fast = acc_ref is not None
        q_near, q_far = [], []
        for h in range(HEADS_PER_STEP):
            for c in range(2):
                qz = half_q(h, c)
                if fast:
                    bound = (_row_norm(qz) * kmax_ref[h][0:1, 0:1]
                             + jnp.max(bias_ref[h], axis=1, keepdims=True))
                    q_near.append(jnp.concatenate(
                        [qz, _place(lane, _split3(-bound), 0, zero_fill).astype(BF16)], axis=1))
                    q_far.append(jnp.concatenate(
                        [qz, _place(lane, _split3(far_bias[h] - bound), 0, zero_fill).astype(BF16)], axis=1))
                else:
                    q_near.append(qz)
                    q_far.append(qz)

        def chains(j, carry, masked, w, gathered):
            mask = (col <= row) if masked else None
            out = []
            for h in range(HEADS_PER_STEP):
                k = k_ref[_rows(j, w), _hs(h)]
                if fast:
                    k = jnp.concatenate([k, key_units(w)], axis=1)
                v = v_ref[_rows(j, w), _hs(h)]
                for c in range(2):
                    i = 2 * h + c
                    if gathered is not None:
                        s = _dot_nt(q_near[i], k) + gathered[h]
                    elif fast:
                        s = _dot_nt(q_far[i], k)
                    else:
                        s = _dot_nt(q_far[i], k) + far_bias[h]
                    out.append(_update(s, v, None if fast else carry[i], mask, acc_ref, i))
            return tuple(out)

        def step(j, carry, masked, w):
            posk = posk_ref[:, _rows(j, w)]

            def near(c):
                dist = jnp.clip(posq - posk, 0, last)
                gathered = [jnp.concatenate(
                    [jnp.take_along_axis(bias_rows[h], dist[:, cc * LANES:(cc + 1) * LANES], axis=1,
                                         mode="promise_in_bounds")
                     for cc in range(w // LANES)], axis=1) for h in range(HEADS_PER_STEP)]
                return chains(j, c, masked, w, gathered)

            def far(c):
                return chains(j, c, masked, w, None)

            return lax.cond(posq_min - jnp.max(posk) < last, near, far, carry)

        return step

    def finish(outs):
        normed = []
        for h in range(HEADS_PER_STEP):
            o = outs[2 * h] - lam * outs[2 * h + 1]
            normed.append(_rms(o, subg_ref[...]) * (1.0 - lam_init))
        o_ref[...] = jnp.concatenate(normed, axis=1).astype(o_ref.dtype)

    _run_softmax(qi, t, 2 * HEADS_PER_STEP, make_step, finish, acc_ref)


def _sb_kernel(q_ref, k_ref, v_ref, o_ref, *, t):
    qi = pl.program_id(2)
    w = t
    row, col = _tile_iota(t)
    later = jnp.where(row > col, 1.0, 0.0).astype(BF16)
    qs = [q_ref[:, _hs(h)] for h in range(SB_HEADS_PER_STEP)]

    def step(j, carry, masked):
        out = []
        for h in range(SB_HEADS_PER_STEP):
            tail, acc = carry[h]
            z = _dot_nt(qs[h], k_ref[_rows(j, w), _hs(h)])
            sp = jnp.log(1.0 + jnp.exp2(-jnp.abs(z))) * LOG2E
            log_beta = jnp.minimum(z, 0.0) - sp
            log_1m = -jnp.maximum(z, 0.0) - sp
            if masked:
                log_1m = jnp.where(col < row, log_1m, 0.0)
            hi = log_1m.astype(BF16)
            lo = (log_1m - hi.astype(F32)).astype(BF16)
            suffix = _dot(hi, later) + _dot(lo, later)
            a = jnp.exp2(log_beta + suffix + tail)
            if masked:
                a = jnp.where(col < row, a, 0.0)
            acc = acc + _dot(a.astype(BF16), v_ref[_rows(j, w), _hs(h)])
            tail = tail + suffix[:, :1] + log_1m[:, :1]
            out.append((tail, acc))
        return tuple(out)

    def live(carry):
        worst = carry[0][0]
        for h in range(1, SB_HEADS_PER_STEP):
            worst = jnp.maximum(worst, carry[h][0])
        return (jnp.max(worst) > SB_DEAD_LOG2).astype(jnp.int32)

    init = tuple((jnp.zeros((t, 1), F32), jnp.zeros((t, HEAD_DIM), F32)) for _ in range(SB_HEADS_PER_STEP))
    carry = step(qi, init, True)

    def body(state):
        j, _, c = state
        c = step(j, c, False)
        return j - 1, live(c), c

    _, _, carry = lax.while_loop(lambda st: (st[0] >= 0) & (st[1] > 0), body,
                                 (qi - 1, live(carry), carry))
    o_ref[...] = jnp.concatenate([c[1] for c in carry], axis=1).astype(o_ref.dtype)


def _attn_call(kernel, t, batch, seq, in_arrays, in_specs, name, hps=HEADS_PER_STEP, softmax_chains=0):
    nq = seq // t
    width = hps * HEAD_DIM
    scratch = []
    if softmax_chains:
        scratch = [pltpu.VMEM((hps, 8, LANES), F32),
                   pltpu.VMEM((softmax_chains, t, 2 * HEAD_DIM), F32)]
    return pl.pallas_call(
        kernel,
        grid=(batch, GROUP_HEADS // hps, nq),
        in_specs=in_specs,
        out_specs=pl.BlockSpec((t, width), lambda b, h, i: (b * nq + i, h)),
        out_shape=jax.ShapeDtypeStruct((batch * seq, GROUP_WIDTH), BF16),
        scratch_shapes=scratch,
        compiler_params=_cparams(("arbitrary", "arbitrary", "arbitrary")),
        name=name,
    )(*in_arrays)


def _q_spec(t, seq, col0, hps=HEADS_PER_STEP):
    nq = seq // t
    return pl.BlockSpec((t, hps * HEAD_DIM), lambda b, h, i: (b * nq + i, col0 + h))


def _kv_spec(seq, col0, hps=HEADS_PER_STEP):
    return pl.BlockSpec((seq, hps * HEAD_DIM), lambda b, h, i: (b, col0 + h))


def _shared_spec(seq):
    return pl.BlockSpec((seq, HEAD_DIM), lambda b, h, i: (b, 0))


def _full_spec(shape):
    return pl.BlockSpec(shape, lambda b, h, i: (0,) * len(shape))


def _mem_kernel(x_ref, g_ref, wq_ref, k_ref, v_ref, wo_ref, o_ref):
    x = x_ref[...]
    h = _rms(x, g_ref[...]).astype(BF16)
    q = _dot(h, wq_ref[...]).astype(BF16)
    outs = []
    for hd in range(MEM_HEADS):
        sl = slice(hd * MEM_HEAD_DIM, (hd + 1) * MEM_HEAD_DIM)
        s = _dot_nt(q[:, sl], k_ref[:, sl])
        p = jnp.exp2(s - jnp.max(s, axis=-1, keepdims=True))
        o = _dot(p.astype(BF16), v_ref[:, sl]) / jnp.sum(p, axis=-1, keepdims=True)
        outs.append(o.astype(BF16))
    o_ref[...] = x + _dot(jnp.concatenate(outs, axis=-1), wo_ref[...])


def _mem_block(x, g, wq, k, v, wo, batch, seq, tm=256):
    m, d = x.shape
    n_mem = k.shape[0] // batch
    tiles_per_batch = seq // tm
    width = MEM_HEADS * MEM_HEAD_DIM
    return pl.pallas_call(
        _mem_kernel,
        grid=(m // tm,),
        in_specs=[pl.BlockSpec((tm, d), lambda i: (i, 0)),
                  pl.BlockSpec((1, d), lambda i: (0, 0)),
                  pl.BlockSpec((d, width), lambda i: (0, 0)),
                  pl.BlockSpec((n_mem, width), lambda i: (i // tiles_per_batch, 0)),
                  pl.BlockSpec((n_mem, width), lambda i: (i // tiles_per_batch, 0)),
                  pl.BlockSpec((width, d), lambda i: (0, 0))],
        out_specs=pl.BlockSpec((tm, d), lambda i: (i, 0)),
        out_shape=jax.ShapeDtypeStruct((m, d), F32),
        compiler_params=_cparams(("parallel",)),
        name="mem_block",
    )(x, g.reshape(1, d).astype(F32), wq, k, v, wo)


def _rel_bias_by_distance(rel_table):
    d = jnp.arange(REL_MAX_DIST, dtype=jnp.int32)
    max_exact = REL_BUCKETS // 2
    nf = jnp.maximum(d, 1).astype(F32)
    large = max_exact + (jnp.log(nf / max_exact) / math.log(REL_MAX_DIST / max_exact)
                         * (REL_BUCKETS - max_exact)).astype(jnp.int32)
    large = jnp.minimum(large, REL_BUCKETS - 1)
    bucket = jnp.where(d < max_exact, d, large)
    return jnp.take(rel_table, bucket, axis=0).T.astype(F32)


def _in_proj_col_scale():
    gw = GROUP_WIDTH
    scale = np.ones((PROJ_COLS,), np.float32)
    scale[PROJ_A0:PROJ_A0 + gw] = DIFF_QK ** -0.5 * LOG2E
    scale[PROJ_C0:PROJ_C0 + gw] = HEAD_DIM ** -0.5 * LOG2E
    scale[PROJ_D0:PROJ_D0 + gw] = HEAD_DIM ** -0.5 * LOG2E
    return scale


PREP_CHUNK = 512
PREP_ALIGNED = PROJ_C0 // PREP_CHUNK
PREP_SKEW = PREP_CHUNK - (PROJ_C0 - (A_COLS + B_COLS))


def _regroup_kernel(x_ref, tail_ref, s_ref, o_ref):
    c = pl.program_id(1)

    @pl.when(c < PREP_ALIGNED)
    def _():
        o_ref[...] = (x_ref[...] * s_ref[...]).astype(o_ref.dtype)

    @pl.when(c >= PREP_ALIGNED)
    def _():
        x = jnp.concatenate([x_ref[:, PREP_SKEW:], tail_ref[:, :PREP_SKEW]], axis=1)
        o_ref[...] = (x * s_ref[...]).astype(o_ref.dtype)


def _regroup_in_proj(w_in, layer, tr=512):
    d = w_in.shape[1]
    tails_per_chunk = PREP_CHUNK // LANES
    scale = jnp.asarray(_in_proj_col_scale()).reshape(1, PROJ_COLS)
    return pl.pallas_call(
        _regroup_kernel,
        grid=(d // tr, PROJ_COLS // PREP_CHUNK),
        in_specs=[
            pl.BlockSpec((None, tr, PREP_CHUNK),
                         lambda r, c: (layer, r, jnp.where(c < PREP_ALIGNED, c, c - 1))),
            pl.BlockSpec((None, tr, LANES),
                         lambda r, c: (layer, r, jnp.where(c < PREP_ALIGNED, 0, c * tails_per_chunk))),
            pl.BlockSpec((1, PREP_CHUNK), lambda r, c: (0, c))],
        out_specs=pl.BlockSpec((tr, PREP_CHUNK), lambda r, c: (r, c)),
        out_shape=jax.ShapeDtypeStruct((d, PROJ_COLS), BF16),
        compiler_params=_cparams(("parallel", "parallel")),
        name="regroup_in_proj",
    )(w_in, w_in, scale)


def _prep_in_proj_side(w):
    ab = A_COLS + B_COLS
    cd = C_COLS + 3 * GROUP_WIDTH
    kr = w[:, A_COLS + MLA_Q_RANK + MLA_KV_RANK:ab]
    f = w[:, ab + cd:]
    half = MLA_ROPE // 2
    pad = jnp.zeros((w.shape[0], LANES - MLA_ROPE), w.dtype)
    f_pad = jnp.zeros((w.shape[0], 2 * LANES - GROUP_HEADS), w.dtype)
    return jnp.concatenate([kr, pad, kr[:, half:], kr[:, :half], pad, f, f_pad], axis=1).astype(BF16)


def _prep_uq(w):
    qk = MLA_NOPE + MLA_ROPE
    half = MLA_ROPE // 2
    w = (w * (qk ** -0.5 * LOG2E)).reshape(w.shape[0], GROUP_HEADS, qk)
    pad = jnp.zeros((w.shape[0], GROUP_HEADS, LANES - MLA_ROPE), w.dtype)
    nope = w[:, :, :MLA_NOPE]
    r1 = w[:, :, MLA_NOPE:MLA_NOPE + half]
    r2 = w[:, :, MLA_NOPE + half:]
    rope = jnp.concatenate([r1, r2, pad], axis=-1)
    swapped = jnp.concatenate([r2, r1, pad], axis=-1)
    return jnp.concatenate([nope.reshape(w.shape[0], -1), rope.reshape(w.shape[0], -1),
                            swapped.reshape(w.shape[0], -1)], axis=1).astype(BF16)


def _prep_ukv(w):
    w = w.reshape(w.shape[0], GROUP_HEADS, MLA_NOPE + HEAD_DIM)
    return jnp.concatenate([w[:, :, :MLA_NOPE].reshape(w.shape[0], -1),
                            w[:, :, MLA_NOPE:].reshape(w.shape[0], -1)], axis=1).astype(BF16)


def kernel(x, mem, positions, attn_norm_g, w_in, w_out, rel_table, diff_lam_q1, diff_lam_k1, diff_lam_q2, diff_lam_k2, diff_sub_g, mla_q_norm_g, mla_kv_norm_g, mla_w_uq, mla_w_ukv, fox_b_f, mem_q_norm_g, mem_kv_norm_g, mem_w_q, mem_w_k, mem_w_v, mem_w_o, mlp_norm_g, w_up, w_down, final_norm_g):
    batch, seq, d = x.shape
    m = batch * seq
    gw = GROUP_WIDTH
    nh = GROUP_HEADS
    xf = x.reshape(m, d)
    mem2 = mem.reshape(batch * mem.shape[1], d)
    pos_col = positions.reshape(m, 1).astype(jnp.int32)
    pos_row = positions.reshape(batch, 1, seq).astype(jnp.int32)
    cos, sin = _rope_tables(pos_col)
    bias_by_dist = (_rel_bias_by_distance(rel_table) * LOG2E).reshape(nh, 1, LANES)
    npair = nh // HEADS_PER_STEP
    pair_cols = HEADS_PER_STEP * HEAD_DIM
    a0, d0 = PROJ_A0 // pair_cols, PROJ_D0 // pair_cols
    nsb = nh // SB_HEADS_PER_STEP
    sb0 = PROJ_C0 // (SB_HEADS_PER_STEP * HEAD_DIM)

    for l in range(DEPTH):
        w_main = _regroup_in_proj(w_in, l)
        w_side = _prep_in_proj_side(w_in[l])
        h = _rmsnorm(xf, attn_norm_g[l], BF16)
        proj = _matmul(h, w_main, BF16, name="in_proj")
        side = _matmul(h, w_side, F32, name="in_proj_side")
        b_f = jnp.concatenate([fox_b_f[l].astype(F32), jnp.zeros((LANES - nh,), F32)]).reshape(1, LANES)
        cum, fox_kx = _forget_cum(side, b_f, batch, seq)

        lam_init = 0.8 - 0.6 * math.exp(-0.3 * l)
        lam_vecs = [v[l].reshape(1, DIFF_QK).astype(F32)
                    for v in (diff_lam_q1, diff_lam_k1, diff_lam_q2, diff_lam_k2)]
        o_a = _attn_call(
            functools.partial(_diff_kernel, t=ATT_T, seq=seq, lam_init=lam_init), ATT_T, batch, seq,
            lam_vecs + [proj, proj, proj, pos_col, pos_row, bias_by_dist,
                        diff_sub_g[l].reshape(1, HEAD_DIM).astype(F32)],
            [_full_spec((1, DIFF_QK))] * 4
            + [_q_spec(ATT_T, seq, a0), _kv_spec(seq, a0 + npair), _kv_spec(seq, a0 + 2 * npair),
               pl.BlockSpec((ATT_T, 1), lambda b, h, i: (b * (seq // ATT_T) + i, 0)),
               pl.BlockSpec((None, 1, seq), lambda b, h, i: (b, 0, 0)),
               pl.BlockSpec((HEADS_PER_STEP, 1, LANES), lambda b, h, i: (h, 0, 0)),
               _full_spec((1, HEAD_DIM))],
            "diff_attn", softmax_chains=2 * HEADS_PER_STEP)

        q_b = _mla_q(proj, mla_q_norm_g[l], _prep_uq(mla_w_uq[l]), cos, sin)
        kv_b, kpe_b = _mla_kv(proj, mla_kv_norm_g[l], _prep_ukv(mla_w_ukv[l]), side, cos, sin)
        o_b = _attn_call(
            functools.partial(_mla_kernel, t=ATT_T, seq=seq), ATT_T, batch, seq,
            [q_b, q_b, kv_b, kpe_b, kv_b],
            [_q_spec(ATT_T, seq, 0), _q_spec(ATT_T, seq, npair), _kv_spec(seq, 0), _shared_spec(seq),
             _kv_spec(seq, npair)],
            "mla_attn", softmax_chains=HEADS_PER_STEP)

        o_c = _attn_call(
            functools.partial(_sb_kernel, t=SB_T), SB_T, batch, seq,
            [proj, proj, proj],
            [_q_spec(SB_T, seq, sb0, SB_HEADS_PER_STEP), _kv_spec(seq, sb0 + nsb, SB_HEADS_PER_STEP),
             _kv_spec(seq, sb0 + 2 * nsb, SB_HEADS_PER_STEP)],
            "stickbreak_attn", hps=SB_HEADS_PER_STEP)

        o_d = _attn_call(
            functools.partial(_fox_kernel, t=ATT_T, seq=seq), ATT_T, batch, seq,
            [proj, proj, proj, fox_kx, cum],
            [_q_spec(ATT_T, seq, d0), _kv_spec(seq, d0 + npair), _kv_spec(seq, d0 + 2 * npair),
             pl.BlockSpec((None, HEADS_PER_STEP, seq, LANES), lambda b, h, i: (b, h, 0, 0)),
             pl.BlockSpec((ATT_T, LANES), lambda b, h, i: (b * (seq // ATT_T) + i, 0))],
            "fox_attn", softmax_chains=HEADS_PER_STEP)

        xf = _out_proj([o_a, o_b, o_c, o_d], w_out[l].astype(BF16), xf)

        mem_n = _rmsnorm(mem2, mem_kv_norm_g[l], BF16)
        k_m = _matmul(mem_n, mem_w_k[l].astype(BF16), BF16, name="mem_k")
        v_m = _matmul(mem_n, mem_w_v[l].astype(BF16), BF16, name="mem_v")
        xf = _mem_block(xf, mem_q_norm_g[l], (mem_w_q[l] * (MEM_HEAD_DIM ** -0.5 * LOG2E)).astype(BF16),
                        k_m, v_m, mem_w_o[l].astype(BF16), batch, seq)

        h = _rmsnorm(xf, mlp_norm_g[l], BF16)
        u = _matmul(h, w_up[l].astype(BF16), BF16, relu2=True, name="mlp_up")
        xf = _matmul(u, w_down[l].astype(BF16), F32, res=xf, name="mlp_down")

    return _rmsnorm(xf, final_norm_g, F32).reshape(batch, seq, d)
```

```python
import functools
import math

import numpy as np
import jax
import jax.numpy as jnp
from jax import lax
from jax.experimental import pallas as pl
from jax.experimental.pallas import tpu as pltpu

F32 = jnp.float32
BF16 = jnp.bfloat16

D_MODEL = 4096
DEPTH = 2
HEAD_DIM = 128
GROUP_HEADS = 8
GROUP_WIDTH = GROUP_HEADS * HEAD_DIM
DIFF_QK = HEAD_DIM // 2
MLA_Q_RANK = 1536
MLA_KV_RANK = 512
MLA_NOPE = 128
MLA_ROPE = 64
ROPE_THETA = 10000.0
REL_BUCKETS = 32
REL_MAX_DIST = 128
MEM_HEADS = 4
MEM_HEAD_DIM = 128
NORM_EPS = 1e-6
NEG_INF = -1e30

A_COLS = 3 * GROUP_WIDTH
B_COLS = MLA_Q_RANK + MLA_KV_RANK + MLA_ROPE
C_COLS = 3 * GROUP_WIDTH

PROJ_A0 = 0
PROJ_CQ0 = A_COLS
PROJ_CKV0 = A_COLS + MLA_Q_RANK
PROJ_C0 = 5632
PROJ_D0 = PROJ_C0 + C_COLS
PROJ_COLS = PROJ_D0 + 3 * GROUP_WIDTH

LANES = 128
VMEM_LIMIT = 48 * 1024 * 1024

LOG2E = math.log2(math.e)

ATT_T = 512
SB_T = 256
HEADS_PER_STEP = 2
SB_HEADS_PER_STEP = 4
SB_DEAD_LOG2 = -160.0
TRUSTED_ROW_SUM = 2.0 ** -60


def _cparams(sem):
    return pltpu.CompilerParams(dimension_semantics=sem, vmem_limit_bytes=VMEM_LIMIT)


def _dot(a, b):
    return jnp.dot(a, b, preferred_element_type=F32)


def _dot_nt(a, b):
    return lax.dot_general(a, b, (((1,), (1,)), ((), ())), preferred_element_type=F32)


def _rms(x, g):
    return x * lax.rsqrt(jnp.mean(x * x, axis=-1, keepdims=True) + NORM_EPS) * g


def _rmsnorm_kernel(x_ref, g_ref, o_ref):
    o_ref[...] = _rms(x_ref[...].astype(F32), g_ref[...]).astype(o_ref.dtype)


def _rmsnorm(x, g, out_dtype, tm=256):
    m, d = x.shape
    return pl.pallas_call(
        _rmsnorm_kernel,
        grid=(m // tm,),
        in_specs=[pl.BlockSpec((tm, d), lambda i: (i, 0)),
                  pl.BlockSpec((1, d), lambda i: (0, 0))],
        out_specs=pl.BlockSpec((tm, d), lambda i: (i, 0)),
        out_shape=jax.ShapeDtypeStruct((m, d), out_dtype),
        compiler_params=_cparams(("parallel",)),
        name="rmsnorm",
    )(x, g.reshape(1, d).astype(F32))


def _mm_epilogue(r, res_ref, o_ref, relu2):
    if relu2:
        r = jnp.square(jnp.maximum(r, 0.0))
    if res_ref is not None:
        r = r + res_ref[...]
    o_ref[...] = r.astype(o_ref.dtype)


def _mm_kernel(a_ref, w_ref, *rest, nk, relu2, has_res):
    res_ref = rest[0] if has_res else None
    o_ref = rest[1] if has_res else rest[0]
    if nk == 1:
        _mm_epilogue(_dot(a_ref[...], w_ref[...].astype(a_ref.dtype)), res_ref, o_ref, relu2)
        return
    acc_ref = rest[-1]
    k = pl.program_id(2)

    @pl.when(k == 0)
    def _():
        acc_ref[...] = _dot(a_ref[...], w_ref[...].astype(a_ref.dtype))

    @pl.when(k > 0)
    def _():
        acc_ref[...] += _dot(a_ref[...], w_ref[...].astype(a_ref.dtype))

    @pl.when(k == nk - 1)
    def _():
        _mm_epilogue(acc_ref[...], res_ref, o_ref, relu2)


def _w_spec(w, layer, block, index_map):
    if w.ndim == 2:
        return pl.BlockSpec(block, index_map)
    return pl.BlockSpec((None,) + block, lambda *g: (layer,) + index_map(*g))


def _matmul(a, w, out_dtype, *, layer=None, res=None, relu2=False, tm=1024, tn=512, tk=4096,
            name="matmul"):
    m, kd = a.shape
    n = w.shape[-1]
    tm, tn, tk = min(tm, m), min(tn, n), min(tk, kd)
    assert m % tm == 0 and n % tn == 0 and kd % tk == 0
    nk = kd // tk
    in_specs = [pl.BlockSpec((tm, tk), lambda i, j, k: (i, k)),
                _w_spec(w, layer, (tk, tn), lambda i, j, k: (k, j))]
    args = [a, w]
    if res is not None:
        in_specs.append(pl.BlockSpec((tm, tn), lambda i, j, k: (i, j)))
        args.append(res)
    return pl.pallas_call(
        functools.partial(_mm_kernel, nk=nk, relu2=relu2, has_res=res is not None),
        grid=(m // tm, n // tn, nk),
        in_specs=in_specs,
        out_specs=pl.BlockSpec((tm, tn), lambda i, j, k: (i, j)),
        out_shape=jax.ShapeDtypeStruct((m, n), out_dtype),
        scratch_shapes=[pltpu.VMEM((tm, tn), F32)] if nk > 1 else [],
        compiler_params=_cparams(("parallel", "parallel", "arbitrary")),
        name=name,
    )(*args)


def _out_proj_kernel(a0, a1, a2, a3, w_ref, res_ref, o_ref):
    acc = res_ref[...]
    for g, a_ref in enumerate((a0, a1, a2, a3)):
        acc = acc + _dot(a_ref[...], w_ref[g * GROUP_WIDTH:(g + 1) * GROUP_WIDTH, :])
    o_ref[...] = acc


def _out_proj(mixes, w, layer, res, tm=1024, tn=512):
    m = res.shape[0]
    n = w.shape[-1]
    a_spec = pl.BlockSpec((tm, GROUP_WIDTH), lambda i, j: (i, 0))
    return pl.pallas_call(
        _out_proj_kernel,
        grid=(m // tm, n // tn),
        in_specs=[a_spec, a_spec, a_spec, a_spec,
                  _w_spec(w, layer, (w.shape[-2], tn), lambda i, j: (0, j)),
                  pl.BlockSpec((tm, tn), lambda i, j: (i, j))],
        out_specs=pl.BlockSpec((tm, tn), lambda i, j: (i, j)),
        out_shape=jax.ShapeDtypeStruct((m, n), F32),
        compiler_params=_cparams(("parallel", "parallel")),
        name="out_proj",
    )(*mixes, w, res)


def _rope_kernel(pos_ref, tab_ref, cos_ref, sin_ref):
    ang = pos_ref[...].astype(F32) * tab_ref[0:1, :]
    cos_ref[...] = jnp.cos(ang) * tab_ref[1:2, :]
    sin_ref[...] = jnp.sin(ang) * tab_ref[2:3, :]


def _rope_tables(pos_col, tm=1024):
    m = pos_col.shape[0]
    half = MLA_ROPE // 2
    inv = ROPE_THETA ** (-jnp.arange(half, dtype=F32) * 2.0 / MLA_ROPE)
    zeros = jnp.zeros((LANES - MLA_ROPE,), F32)
    ones = jnp.ones((half,), F32)
    tab = jnp.stack([jnp.concatenate([inv, inv, zeros]),
                     jnp.concatenate([ones, ones, zeros]),
                     jnp.concatenate([-ones, ones, zeros])]
                    + [jnp.zeros((LANES,), F32)] * 5)
    spec = pl.BlockSpec((tm, LANES), lambda i: (i, 0))
    return pl.pallas_call(
        _rope_kernel,
        grid=(m // tm,),
        in_specs=[pl.BlockSpec((tm, 1), lambda i: (i, 0)),
                  pl.BlockSpec((8, LANES), lambda i: (0, 0))],
        out_specs=[spec, spec],
        out_shape=[jax.ShapeDtypeStruct((m, LANES), F32)] * 2,
        compiler_params=_cparams(("parallel",)),
        name="rope_tables",
    )(pos_col, tab)


def _mla_q_kernel(cq_ref, g_ref, w_ref, cos_ref, sin_ref, o_ref):
    n = _rms(cq_ref[...].astype(F32), g_ref[...]).astype(BF16)
    t = _dot(n, w_ref[...])
    o_ref[:, :GROUP_WIDTH] = t[:, :GROUP_WIDTH].astype(o_ref.dtype)
    cos, sin = cos_ref[...], sin_ref[...]
    for h in range(GROUP_HEADS):
        lo = GROUP_WIDTH + h * LANES
        pe = t[:, lo:lo + LANES] * cos + t[:, lo + GROUP_WIDTH:lo + GROUP_WIDTH + LANES] * sin
        o_ref[:, lo:lo + LANES] = pe.astype(o_ref.dtype)


def _mla_q(proj, g, w, cos, sin, tm=512):
    m = proj.shape[0]
    cq_block = PROJ_CQ0 // MLA_Q_RANK
    return pl.pallas_call(
        _mla_q_kernel,
        grid=(m // tm,),
        in_specs=[pl.BlockSpec((tm, MLA_Q_RANK), lambda i: (i, cq_block)),
                  pl.BlockSpec((1, MLA_Q_RANK), lambda i: (0, 0)),
                  pl.BlockSpec(w.shape, lambda i: (0, 0)),
                  pl.BlockSpec((tm, LANES), lambda i: (i, 0)),
                  pl.BlockSpec((tm, LANES), lambda i: (i, 0))],
        out_specs=pl.BlockSpec((tm, 2 * GROUP_WIDTH), lambda i: (i, 0)),
        out_shape=jax.ShapeDtypeStruct((m, 2 * GROUP_WIDTH), BF16),
        compiler_params=_cparams(("parallel",)),
        name="mla_q",
    )(proj, g.reshape(1, -1).astype(F32), w, cos, sin)


def _mla_kv_kernel(ckv_ref, g_ref, w_ref, kr_ref, cos_ref, sin_ref, kv_ref, kpe_ref):
    n = _rms(ckv_ref[...].astype(F32), g_ref[...]).astype(BF16)
    kv_ref[...] = _dot(n, w_ref[...]).astype(kv_ref.dtype)
    kr = kr_ref[...]
    kpe = kr[:, :LANES] * cos_ref[...] + kr[:, LANES:] * sin_ref[...]
    lane = lax.broadcasted_iota(jnp.int32, kpe.shape, 1)
    kpe = jnp.where((lane >= MLA_ROPE) & (lane < MLA_ROPE + 3), 1.0, kpe)
    kpe_ref[...] = kpe.astype(kpe_ref.dtype)


def _mla_kv(proj, g, w, kr, cos, sin, tm=512):
    m = proj.shape[0]
    ckv_block = PROJ_CKV0 // MLA_KV_RANK
    return pl.pallas_call(
        _mla_kv_kernel,
        grid=(m // tm,),
        in_specs=[pl.BlockSpec((tm, MLA_KV_RANK), lambda i: (i, ckv_block)),
                  pl.BlockSpec((1, MLA_KV_RANK), lambda i: (0, 0)),
                  pl.BlockSpec(w.shape, lambda i: (0, 0)),
                  pl.BlockSpec((tm, 2 * LANES), lambda i: (i, 0)),
                  pl.BlockSpec((tm, LANES), lambda i: (i, 0)),
                  pl.BlockSpec((tm, LANES), lambda i: (i, 0))],
        out_specs=[pl.BlockSpec((tm, 2 * GROUP_WIDTH), lambda i: (i, 0)),
                   pl.BlockSpec((tm, LANES), lambda i: (i, 0))],
        out_shape=[jax.ShapeDtypeStruct((m, 2 * GROUP_WIDTH), BF16),
                   jax.ShapeDtypeStruct((m, LANES), BF16)],
        compiler_params=_cparams(("parallel",)),
        name="mla_kv",
    )(proj, g.reshape(1, -1).astype(F32), w, kr, cos, sin)


def _split3(x):
    x1 = x.astype(BF16)
    r1 = x - x1.astype(F32)
    x2 = r1.astype(BF16)
    x3 = (r1 - x2.astype(F32)).astype(BF16)
    return x1, x2, x3


def _place(lane, cols, base, fill):
    out = fill
    for i, c in enumerate(cols):
        out = jnp.where(lane == base + i, c.astype(F32), out)
    return out


def _forget_cum_kernel(f_ref, b_ref, cum_ref, kx_ref, *, chunk):
    r = lax.broadcasted_iota(jnp.int32, (chunk, chunk), 0)
    c = lax.broadcasted_iota(jnp.int32, (chunk, chunk), 1)
    tri = jnp.where(r >= c, 1.0, 0.0).astype(BF16)
    lane = lax.broadcasted_iota(jnp.int32, (chunk, LANES), 1)
    unit_fill = jnp.where(lane < 3, 1.0, 0.0)

    def body(ci, carry):
        rows = _rows(ci, chunk)
        x = f_ref[rows, :] + b_ref[...]
        logf = (jnp.minimum(x, 0.0) - jnp.log1p(jnp.exp(-jnp.abs(x)))) * LOG2E
        parts = _split3(logf)
        cs = carry + _dot(tri, parts[0]) + _dot(tri, parts[1]) + _dot(tri, parts[2])
        cum_ref[rows, :] = cs
        for h in range(GROUP_HEADS):
            kx_ref[h, rows, :] = _place(lane, _split3(cs[:, h:h + 1]), 3, unit_fill).astype(kx_ref.dtype)
        return cs[chunk - 1:chunk, :]

    lax.fori_loop(0, f_ref.shape[0] // chunk, body, jnp.zeros((1, LANES), F32))


def _forget_cum(side, b_f, batch, seq):
    f_block = 2
    return pl.pallas_call(
        functools.partial(_forget_cum_kernel, chunk=256),
        grid=(batch,),
        in_specs=[pl.BlockSpec((seq, LANES), lambda b: (b, f_block)),
                  pl.BlockSpec((1, LANES), lambda b: (0, 0))],
        out_specs=[pl.BlockSpec((seq, LANES), lambda b: (b, 0)),
                   pl.BlockSpec((None, GROUP_HEADS, seq, LANES), lambda b: (b, 0, 0, 0))],
        out_shape=[jax.ShapeDtypeStruct((batch * seq, LANES), F32),
                   jax.ShapeDtypeStruct((batch, GROUP_HEADS, seq, LANES), BF16)],
        compiler_params=_cparams(("parallel",)),
        name="forget_cum",
    )(side, b_f)


def _tile_iota(t):
    return (lax.broadcasted_iota(jnp.int32, (t, t), 0),
            lax.broadcasted_iota(jnp.int32, (t, t), 1))


def _softmax_update(s, v, m, acc):
    m_new = jnp.maximum(m, jnp.max(s, axis=-1, keepdims=True))
    alpha = jnp.exp2(m - m_new)
    p = jnp.exp2(s - m_new).astype(BF16)
    v_aug = jnp.concatenate([v, jnp.ones_like(v)], axis=1)
    return m_new, alpha * acc + _dot(p, v_aug)


def _softmax_init(t):
    return (jnp.full((t, 1), NEG_INF, F32), jnp.zeros((t, 2 * HEAD_DIM), F32))


def _softmax_out(acc):
    return acc[:, :HEAD_DIM] / acc[:, HEAD_DIM:]


def _hs(h):
    return slice(h * HEAD_DIM, (h + 1) * HEAD_DIM)


def _rows(j, w):
    return pl.ds(pl.multiple_of(j * w, w), w)


def _causal_sweep(step, qi, init):
    carry = lax.fori_loop(0, qi, lambda j, c: step(j, c, False), init)
    return step(qi, carry, True)


def _update(s, v, state, mask, acc_ref, chain):
    if mask is not None:
        s = jnp.where(mask, s, NEG_INF)
    if acc_ref is None:
        return _softmax_update(s, v, *state)
    v_aug = jnp.concatenate([v, jnp.ones_like(v)], axis=1)
    acc_ref[chain] += _dot(jnp.exp2(s).astype(BF16), v_aug)
    return None


def _run_softmax(qi, t, n_chains, make_step, finish, acc_ref):
    acc_ref[...] = jnp.zeros_like(acc_ref)
    fast = make_step(acc_ref)

    def wide(j, carry):
        fast(j, None, False, 2 * t)
        return carry

    lax.fori_loop(0, qi // 2, wide, 0)

    @pl.when(qi % 2 == 1)
    def _():
        fast(qi - 1, None, False, t)

    fast(qi, None, True, t)
    accs = [acc_ref[c] for c in range(n_chains)]
    l_min = accs[0][:, HEAD_DIM:HEAD_DIM + 1]
    for a in accs[1:]:
        l_min = jnp.minimum(l_min, a[:, HEAD_DIM:HEAD_DIM + 1])
    ok = jnp.min(l_min) >= TRUSTED_ROW_SUM

    @pl.when(ok)
    def _():
        finish([_softmax_out(a) for a in accs])

    @pl.when(jnp.logical_not(ok))
    def _():
        slow = make_step(None)
        init = tuple(_softmax_init(t) for _ in range(n_chains))
        carry = lax.fori_loop(0, qi, lambda j, c: slow(j, c, False, t), init)
        finish([_softmax_out(c[1]) for c in slow(qi, carry, True, t)])


def _key_norm_max(k_ref, h, seq, extra_ref=None):
    chunk = ATT_T

    def body(c, mx):
        kk = k_ref[_rows(c, chunk), _hs(h)].astype(F32)
        n2 = jnp.sum(kk * kk, axis=1, keepdims=True)
        if extra_ref is not None:
            e = extra_ref[_rows(c, chunk), :].astype(F32)
            n2 = n2 + jnp.sum(e * e, axis=1, keepdims=True)
        return jnp.maximum(mx, n2)

    mx = lax.fori_loop(0, seq // chunk, body, jnp.zeros((chunk, 1), F32))
    return jnp.sqrt(jnp.max(mx, axis=0, keepdims=True))


def _store_key_norms(kmax_ref, qi, norm_fn):
    @pl.when(qi == 0)
    def _():
        for h in range(HEADS_PER_STEP):
            kmax_ref[h] = jnp.broadcast_to(norm_fn(h), kmax_ref.shape[1:])


def _row_norm(x):
    xf = x.astype(F32)
    return jnp.sqrt(jnp.sum(xf * xf, axis=1, keepdims=True))


def _fox_kernel(q_ref, k_ref, v_ref, kx_ref, cumq_ref, o_ref, kmax_ref, acc_ref, *, t, seq):
    pair, qi = pl.program_id(1), pl.program_id(2)
    _store_key_norms(kmax_ref, qi, lambda h: _key_norm_max(k_ref, h, seq))
    row, col = _tile_iota(t)
    lane = lax.broadcasted_iota(jnp.int32, (t, LANES), 1)
    minus_fill = jnp.where((lane >= 3) & (lane < 6), -1.0, 0.0)
    cumq = cumq_ref[...]

    def q_aug(h, fast):
        q = q_ref[:, _hs(h)]
        c = jnp.sum(jnp.where(lane == pair * HEADS_PER_STEP + h, cumq, 0.0), axis=1, keepdims=True)
        if fast:
            c = c - _row_norm(q) * kmax_ref[h][0:1, 0:1]
        return jnp.concatenate([q, _place(lane, _split3(c), 0, minus_fill).astype(BF16)], axis=1)

    def make_step(acc_ref):
        fast = acc_ref is not None
        qs = [q_aug(h, fast) for h in range(HEADS_PER_STEP)]

        def step(j, carry, masked, w):
            mask = (col <= row) if masked else None
            out = []
            for h in range(HEADS_PER_STEP):
                k = jnp.concatenate([k_ref[_rows(j, w), _hs(h)], kx_ref[h, _rows(j, w), :]], axis=1)
                out.append(_update(_dot_nt(qs[h], k), v_ref[_rows(j, w), _hs(h)], None if fast else carry[h], mask, acc_ref, h))
            return tuple(out)

        return step

    def finish(outs):
        o_ref[...] = jnp.concatenate(outs, axis=1).astype(o_ref.dtype)

    _run_softmax(qi, t, HEADS_PER_STEP, make_step, finish, acc_ref)


def _mla_kernel(qn_ref, qp_ref, kn_ref, kp_ref, v_ref, o_ref, kmax_ref, acc_ref, *, t, seq):
    qi = pl.program_id(2)
    _store_key_norms(kmax_ref, qi, lambda h: _key_norm_max(kn_ref, h, seq, kp_ref))
    row, col = _tile_iota(t)
    lane = lax.broadcasted_iota(jnp.int32, (t, LANES), 1)

    def q_aug(h, fast):
        qn, qp = qn_ref[:, _hs(h)], qp_ref[:, _hs(h)]
        if fast:
            qn_f, qp_f = qn.astype(F32), qp.astype(F32)
            norm = jnp.sqrt(jnp.sum(qn_f * qn_f, axis=1, keepdims=True)
                            + jnp.sum(qp_f * qp_f, axis=1, keepdims=True))
            qp = _place(lane, _split3(-norm * kmax_ref[h][0:1, 0:1]), MLA_ROPE, qp_f).astype(BF16)
        return jnp.concatenate([qn, qp], axis=1)

    def make_step(acc_ref):
        fast = acc_ref is not None
        qs = [q_aug(h, fast) for h in range(HEADS_PER_STEP)]

        def step(j, carry, masked, w):
            mask = (col <= row) if masked else None
            kp = kp_ref[_rows(j, w), :]
            out = []
            for h in range(HEADS_PER_STEP):
                k = jnp.concatenate([kn_ref[_rows(j, w), _hs(h)], kp], axis=1)
                out.append(_update(_dot_nt(qs[h], k), v_ref[_rows(j, w), _hs(h)], None if fast else carry[h], mask, acc_ref, h))
            return tuple(out)

        return step

    def finish(outs):
        o_ref[...] = jnp.concatenate(outs, axis=1).astype(o_ref.dtype)

    _run_softmax(qi, t, HEADS_PER_STEP, make_step, finish, acc_ref)


def _diff_kernel(lq1_ref, lk1_ref, lq2_ref, lk2_ref, q_ref, k_ref, v_ref, posq_ref, posk_ref,
                 bias_ref, subg_ref, o_ref, kmax_ref, acc_ref, *, t, seq, lam_init):
    qi = pl.program_id(2)
    _store_key_norms(kmax_ref, qi, lambda h: _key_norm_max(k_ref, h, seq))
    lam = (jnp.exp(jnp.sum(lq1_ref[...] * lk1_ref[...], axis=-1, keepdims=True))
           - jnp.exp(jnp.sum(lq2_ref[...] * lk2_ref[...], axis=-1, keepdims=True)) + lam_init)
    row, col = _tile_iota(t)
    lane = lax.broadcasted_iota(jnp.int32, (t, LANES), 1)
    zero_fill = jnp.zeros((t, LANES), F32)

    def key_units(w):
        return jnp.where(lax.broadcasted_iota(jnp.int32, (w, LANES), 1) < 3, 1.0, 0.0).astype(BF16)

    posq = posq_ref[...]
    posq_min = jnp.min(posq)
    bias_rows = [jnp.broadcast_to(bias_ref[h], (t, LANES)) for h in range(HEADS_PER_STEP)]
    last = REL_MAX_DIST - 1
    far_bias = [bias_ref[h][:, last:] for h in range(HEADS_PER_STEP)]

    def half_q(h, c):
        q = q_ref[:, _hs(h)]
        return jnp.where((lane < DIFF_QK) if c == 0 else (lane >= DIFF_QK), q, jnp.zeros_like(q))

    def make_step(acc_ref):
        fast = acc_ref is not None
        q_near, q_far = [], []
        for h in range(HEADS_PER_STEP):
            for c in range(2):
                qz = half_q(h, c)
                if fast:
                    bound = (_row_norm(qz) * kmax_ref[h][0:1, 0:1]
                             + jnp.max(bias_ref[h], axis=1, keepdims=True))
                    q_near.append(jnp.concatenate(
                        [qz, _place(lane, _split3(-bound), 0, zero_fill).astype(BF16)], axis=1))
                    q_far.append(jnp.concatenate(
                        [qz, _place(lane, _split3(far_bias[h] - bound), 0, zero_fill).astype(BF16)], axis=1))
                else:
                    q_near.append(qz)
                    q_far.append(qz)

        def chains(j, carry, masked, w, gathered):
            mask = (col <= row) if masked else None
            out = []
            for h in range(HEADS_PER_STEP):
                k = k_ref[_rows(j, w), _hs(h)]
                if fast:
                    k = jnp.concatenate([k, key_units(w)], axis=1)
                v = v_ref[_rows(j, w), _hs(h)]
                for c in range(2):
                    i = 2 * h + c
                    if gathered is not None:
                        s = _dot_nt(q_near[i], k) + gathered[h]
                    elif fast:
                        s = _dot_nt(q_far[i], k)
                    else:
                        s = _dot_nt(q_far[i], k) + far_bias[h]
                    out.append(_update(s, v, None if fast else carry[i], mask, acc_ref, i))
            return tuple(out)

        def step(j, carry, masked, w):
            posk = posk_ref[:, _rows(j, w)]

            def near(c):
                dist = jnp.clip(posq - posk, 0, last)
                gathered = [jnp.concatenate(
                    [jnp.take_along_axis(bias_rows[h], dist[:, cc * LANES:(cc + 1) * LANES], axis=1,
                                         mode="promise_in_bounds")
                     for cc in range(w // LANES)], axis=1) for h in range(HEADS_PER_STEP)]
                return chains(j, c, masked, w, gathered)

            def far(c):
                return chains(j, c, masked, w, None)

            return lax.cond(posq_min - jnp.max(posk) < last, near, far, carry)

        return step

    def finish(outs):
        normed = []
        for h in range(HEADS_PER_STEP):
            o = outs[2 * h] - lam * outs[2 * h + 1]
            normed.append(_rms(o, subg_ref[...]) * (1.0 - lam_init))
        o_ref[...] = jnp.concatenate(normed, axis=1).astype(o_ref.dtype)

    _run_softmax(qi, t, 2 * HEADS_PER_STEP, make_step, finish, acc_ref)


def _sb_kernel(q_ref, k_ref, v_ref, o_ref, *, t):
    qi = pl.program_id(2)
    w = t
    row, col = _tile_iota(t)
    later = jnp.where(row > col, 1.0, 0.0).astype(BF16)
    qs = [q_ref[:, _hs(h)] for h in range(SB_HEADS_PER_STEP)]

    def step(j, carry, masked):
        out = []
        for h in range(SB_HEADS_PER_STEP):
            tail, acc = carry[h]
            z = _dot_nt(qs[h], k_ref[_rows(j, w), _hs(h)])
            sp = jnp.log(1.0 + jnp.exp2(-jnp.abs(z))) * LOG2E
            log_beta = jnp.minimum(z, 0.0) - sp
            log_1m = -jnp.maximum(z, 0.0) - sp
            if masked:
                log_1m = jnp.where(col < row, log_1m, 0.0)
            hi = log_1m.astype(BF16)
            lo = (log_1m - hi.astype(F32)).astype(BF16)
            suffix = _dot(hi, later) + _dot(lo, later)
            a = jnp.exp2(log_beta + suffix + tail)
            if masked:
                a = jnp.where(col < row, a, 0.0)
            acc = acc + _dot(a.astype(BF16), v_ref[_rows(j, w), _hs(h)])
            tail = tail + suffix[:, :1] + log_1m[:, :1]
            out.append((tail, acc))
        return tuple(out)

    def live(carry):
        worst = carry[0][0]
        for h in range(1, SB_HEADS_PER_STEP):
            worst = jnp.maximum(worst, carry[h][0])
        return (jnp.max(worst) > SB_DEAD_LOG2).astype(jnp.int32)

    init = tuple((jnp.zeros((t, 1), F32), jnp.zeros((t, HEAD_DIM), F32)) for _ in range(SB_HEADS_PER_STEP))
    carry = step(qi, init, True)

    def body(state):
        j, _, c = state
        c = step(j, c, False)
        return j - 1, live(c), c

    _, _, carry = lax.while_loop(lambda st: (st[0] >= 0) & (st[1] > 0), body,
                                 (qi - 1, live(carry), carry))
    o_ref[...] = jnp.concatenate([c[1] for c in carry], axis=1).astype(o_ref.dtype)


def _attn_call(kernel, t, batch, seq, in_arrays, in_specs, name, hps=HEADS_PER_STEP, softmax_chains=0):
    nq = seq // t
    width = hps * HEAD_DIM
    scratch = []
    if softmax_chains:
        scratch = [pltpu.VMEM((hps, 8, LANES), F32),
                   pltpu.VMEM((softmax_chains, t, 2 * HEAD_DIM), F32)]
    return pl.pallas_call(
        kernel,
        grid=(batch, GROUP_HEADS // hps, nq),
        in_specs=in_specs,
        out_specs=pl.BlockSpec((t, width), lambda b, h, i: (b * nq + i, h)),
        out_shape=jax.ShapeDtypeStruct((batch * seq, GROUP_WIDTH), BF16),
        scratch_shapes=scratch,
        compiler_params=_cparams(("arbitrary", "arbitrary", "arbitrary")),
        name=name,
    )(*in_arrays)


def _q_spec(t, seq, col0, hps=HEADS_PER_STEP):
    nq = seq // t
    return pl.BlockSpec((t, hps * HEAD_DIM), lambda b, h, i: (b * nq + i, col0 + h))


def _kv_spec(seq, col0, hps=HEADS_PER_STEP):
    return pl.BlockSpec((seq, hps * HEAD_DIM), lambda b, h, i: (b, col0 + h))


def _shared_spec(seq):
    return pl.BlockSpec((seq, HEAD_DIM), lambda b, h, i: (b, 0))


def _full_spec(shape):
    return pl.BlockSpec(shape, lambda b, h, i: (0,) * len(shape))


def _mem_kernel(x_ref, g_ref, wq_ref, k_ref, v_ref, wo_ref, o_ref):
    x = x_ref[...]
    h = _rms(x, g_ref[...]).astype(BF16)
    q = _dot(h, wq_ref[...]).astype(BF16)
    outs = []
    for hd in range(MEM_HEADS):
        sl = slice(hd * MEM_HEAD_DIM, (hd + 1) * MEM_HEAD_DIM)
        s = _dot_nt(q[:, sl], k_ref[:, sl])
        p = jnp.exp2(s - jnp.max(s, axis=-1, keepdims=True))
        o = _dot(p.astype(BF16), v_ref[:, sl]) / jnp.sum(p, axis=-1, keepdims=True)
        outs.append(o.astype(BF16))
    o_ref[...] = x + _dot(jnp.concatenate(outs, axis=-1), wo_ref[...])


def _mem_block(x, g, wq, k, v, wo, layer, batch, seq, tm=256):
    m, d = x.shape
    n_mem = k.shape[0] // batch
    tiles_per_batch = seq // tm
    width = MEM_HEADS * MEM_HEAD_DIM
    return pl.pallas_call(
        _mem_kernel,
        grid=(m // tm,),
        in_specs=[pl.BlockSpec((tm, d), lambda i: (i, 0)),
                  pl.BlockSpec((1, d), lambda i: (0, 0)),
                  _w_spec(wq, layer, (d, width), lambda i: (0, 0)),
                  pl.BlockSpec((n_mem, width), lambda i: (i // tiles_per_batch, 0)),
                  pl.BlockSpec((n_mem, width), lambda i: (i // tiles_per_batch, 0)),
                  _w_spec(wo, layer, (width, d), lambda i: (0, 0))],
        out_specs=pl.BlockSpec((tm, d), lambda i: (i, 0)),
        out_shape=jax.ShapeDtypeStruct((m, d), F32),
        compiler_params=_cparams(("parallel",)),
        name="mem_block",
    )(x, g.reshape(1, d).astype(F32), wq, k, v, wo)


def _rel_bias_by_distance(rel_table):
    d = jnp.arange(REL_MAX_DIST, dtype=jnp.int32)
    max_exact = REL_BUCKETS // 2
    nf = jnp.maximum(d, 1).astype(F32)
    large = max_exact + (jnp.log(nf / max_exact) / math.log(REL_MAX_DIST / max_exact)
                         * (REL_BUCKETS - max_exact)).astype(jnp.int32)
    large = jnp.minimum(large, REL_BUCKETS - 1)
    bucket = jnp.where(d < max_exact, d, large)
    return jnp.take(rel_table, bucket, axis=0).T.astype(F32)


def _in_proj_col_scale():
    gw = GROUP_WIDTH
    scale = np.ones((PROJ_COLS,), np.float32)
    scale[PROJ_A0:PROJ_A0 + gw] = DIFF_QK ** -0.5 * LOG2E
    scale[PROJ_C0:PROJ_C0 + gw] = HEAD_DIM ** -0.5 * LOG2E
    scale[PROJ_D0:PROJ_D0 + gw] = HEAD_DIM ** -0.5 * LOG2E
    return scale


PREP_CHUNK = 512
PREP_ALIGNED = PROJ_C0 // PREP_CHUNK
PREP_SKEW = PREP_CHUNK - (PROJ_C0 - (A_COLS + B_COLS))


def _regroup_kernel(x_ref, tail_ref, s_ref, o_ref):
    c = pl.program_id(1)

    @pl.when(c < PREP_ALIGNED)
    def _():
        o_ref[...] = (x_ref[...].T * s_ref[...]).astype(o_ref.dtype)

    @pl.when(c >= PREP_ALIGNED)
    def _():
        x = jnp.concatenate([x_ref[PREP_SKEW:, :], tail_ref[...]], axis=0)
        o_ref[...] = (x.T * s_ref[...]).astype(o_ref.dtype)


def _regroup_in_proj(w_in_t, layer, tr=512):
    d = w_in_t.shape[2]
    tails_per_chunk = PREP_CHUNK // PREP_SKEW
    scale = jnp.asarray(_in_proj_col_scale()).reshape(1, PROJ_COLS)
    return pl.pallas_call(
        _regroup_kernel,
        grid=(d // tr, PROJ_COLS // PREP_CHUNK),
        in_specs=[
            pl.BlockSpec((None, PREP_CHUNK, tr),
                         lambda r, c: (layer, jnp.where(c < PREP_ALIGNED, c, c - 1), r)),
            pl.BlockSpec((None, PREP_SKEW, tr),
                         lambda r, c: (layer, jnp.where(c < PREP_ALIGNED, 0, c * tails_per_chunk), r)),
            pl.BlockSpec((1, PREP_CHUNK), lambda r, c: (0, c))],
        out_specs=pl.BlockSpec((tr, PREP_CHUNK), lambda r, c: (r, c)),
        out_shape=jax.ShapeDtypeStruct((d, PROJ_COLS), BF16),
        compiler_params=_cparams(("parallel", "parallel")),
        name="regroup_in_proj",
    )(w_in_t, w_in_t, scale)


SIDE_COLS = 4 * LANES


def _side_proj_kernel(h_ref, kr_ref, f_ref, o_ref):
    kr, f = kr_ref[...], f_ref[...]
    half = MLA_ROPE // 2

    def zeros(n):
        return jnp.zeros((n, kr.shape[1]), kr.dtype)

    w_t = jnp.concatenate([kr, zeros(LANES - MLA_ROPE), kr[half:], kr[:half], zeros(LANES - MLA_ROPE),
                           f, zeros(2 * LANES - GROUP_HEADS)], axis=0)
    o_ref[...] = _dot_nt(h_ref[...], w_t.astype(h_ref.dtype))


def _side_proj(h, w_in_t, layer, tm=1024):
    m, d = h.shape
    kr_row0 = A_COLS + MLA_Q_RANK + MLA_KV_RANK
    f_row0 = A_COLS + B_COLS + C_COLS + 3 * GROUP_WIDTH
    return pl.pallas_call(
        _side_proj_kernel,
        grid=(m // tm,),
        in_specs=[pl.BlockSpec((tm, d), lambda i: (i, 0)),
                  pl.BlockSpec((None, MLA_ROPE, d), lambda i: (layer, kr_row0 // MLA_ROPE, 0)),
                  pl.BlockSpec((None, GROUP_HEADS, d), lambda i: (layer, f_row0 // GROUP_HEADS, 0))],
        out_specs=pl.BlockSpec((tm, SIDE_COLS), lambda i: (i, 0)),
        out_shape=jax.ShapeDtypeStruct((m, SIDE_COLS), F32),
        compiler_params=_cparams(("parallel",)),
        name="in_proj_side",
    )(h, w_in_t, w_in_t)


def _prep_uq(w):
    qk = MLA_NOPE + MLA_ROPE
    half = MLA_ROPE // 2
    w = (w * (qk ** -0.5 * LOG2E)).reshape(w.shape[0], GROUP_HEADS, qk)
    pad = jnp.zeros((w.shape[0], GROUP_HEADS, LANES - MLA_ROPE), w.dtype)
    nope = w[:, :, :MLA_NOPE]
    r1 = w[:, :, MLA_NOPE:MLA_NOPE + half]
    r2 = w[:, :, MLA_NOPE + half:]
    rope = jnp.concatenate([r1, r2, pad], axis=-1)
    swapped = jnp.concatenate([r2, r1, pad], axis=-1)
    return jnp.concatenate([nope.reshape(w.shape[0], -1), rope.reshape(w.shape[0], -1),
                            swapped.reshape(w.shape[0], -1)], axis=1).astype(BF16)


def _prep_ukv(w):
    w = w.reshape(w.shape[0], GROUP_HEADS, MLA_NOPE + HEAD_DIM)
    return jnp.concatenate([w[:, :, :MLA_NOPE].reshape(w.shape[0], -1),
                            w[:, :, MLA_NOPE:].reshape(w.shape[0], -1)], axis=1).astype(BF16)


def kernel(x, mem, positions, attn_norm_g, w_in, w_out, rel_table, diff_lam_q1, diff_lam_k1, diff_lam_q2, diff_lam_k2, diff_sub_g, mla_q_norm_g, mla_kv_norm_g, mla_w_uq, mla_w_ukv, fox_b_f, mem_q_norm_g, mem_kv_norm_g, mem_w_q, mem_w_k, mem_w_v, mem_w_o, mlp_norm_g, w_up, w_down, final_norm_g):
    batch, seq, d = x.shape
    m = batch * seq
    gw = GROUP_WIDTH
    nh = GROUP_HEADS
    xf = x.reshape(m, d)
    mem2 = mem.reshape(batch * mem.shape[1], d)
    pos_col = positions.reshape(m, 1).astype(jnp.int32)
    pos_row = positions.reshape(batch, 1, seq).astype(jnp.int32)
    cos, sin = _rope_tables(pos_col)
    bias_by_dist = (_rel_bias_by_distance(rel_table) * LOG2E).reshape(nh, 1, LANES)
    npair = nh // HEADS_PER_STEP
    pair_cols = HEADS_PER_STEP * HEAD_DIM
    a0, d0 = PROJ_A0 // pair_cols, PROJ_D0 // pair_cols
    nsb = nh // SB_HEADS_PER_STEP
    sb0 = PROJ_C0 // (SB_HEADS_PER_STEP * HEAD_DIM)

    w_out_b, w_up_b, w_down_b = w_out.astype(BF16), w_up.astype(BF16), w_down.astype(BF16)
    mem_wq_b = (mem_w_q * (MEM_HEAD_DIM ** -0.5 * LOG2E)).astype(BF16)
    mem_wk_b, mem_wv_b, mem_wo_b = mem_w_k.astype(BF16), mem_w_v.astype(BF16), mem_w_o.astype(BF16)

    w_in_t = jnp.swapaxes(w_in, 1, 2)

    for l in range(DEPTH):
        w_main = _regroup_in_proj(w_in_t, l)
        h = _rmsnorm(xf, attn_norm_g[l], BF16)
        proj = _matmul(h, w_main, BF16, name="in_proj")
        side = _side_proj(h, w_in_t, l)
        b_f = jnp.concatenate([fox_b_f[l].astype(F32), jnp.zeros((LANES - nh,), F32)]).reshape(1, LANES)
        cum, fox_kx = _forget_cum(side, b_f, batch, seq)

        lam_init = 0.8 - 0.6 * math.exp(-0.3 * l)
        lam_vecs = [v[l].reshape(1, DIFF_QK).astype(F32)
                    for v in (diff_lam_q1, diff_lam_k1, diff_lam_q2, diff_lam_k2)]
        o_a = _attn_call(
            functools.partial(_diff_kernel, t=ATT_T, seq=seq, lam_init=lam_init), ATT_T, batch, seq,
            lam_vecs + [proj, proj, proj, pos_col, pos_row, bias_by_dist,
                        diff_sub_g[l].reshape(1, HEAD_DIM).astype(F32)],
            [_full_spec((1, DIFF_QK))] * 4
            + [_q_spec(ATT_T, seq, a0), _kv_spec(seq, a0 + npair), _kv_spec(seq, a0 + 2 * npair),
               pl.BlockSpec((ATT_T, 1), lambda b, h, i: (b * (seq // ATT_T) + i, 0)),
               pl.BlockSpec((None, 1, seq), lambda b, h, i: (b, 0, 0)),
               pl.BlockSpec((HEADS_PER_STEP, 1, LANES), lambda b, h, i: (h, 0, 0)),
               _full_spec((1, HEAD_DIM))],
            "diff_attn", softmax_chains=2 * HEADS_PER_STEP)

        q_b = _mla_q(proj, mla_q_norm_g[l], _prep_uq(mla_w_uq[l]), cos, sin)
        kv_b, kpe_b = _mla_kv(proj, mla_kv_norm_g[l], _prep_ukv(mla_w_ukv[l]), side, cos, sin)
        o_b = _attn_call(
            functools.partial(_mla_kernel, t=ATT_T, seq=seq), ATT_T, batch, seq,
            [q_b, q_b, kv_b, kpe_b, kv_b],
            [_q_spec(ATT_T, seq, 0), _q_spec(ATT_T, seq, npair), _kv_spec(seq, 0), _shared_spec(seq),
             _kv_spec(seq, npair)],
            "mla_attn", softmax_chains=HEADS_PER_STEP)

        o_c = _attn_call(
            functools.partial(_sb_kernel, t=SB_T), SB_T, batch, seq,
            [proj, proj, proj],
            [_q_spec(SB_T, seq, sb0, SB_HEADS_PER_STEP), _kv_spec(seq, sb0 + nsb, SB_HEADS_PER_STEP),
             _kv_spec(seq, sb0 + 2 * nsb, SB_HEADS_PER_STEP)],
            "stickbreak_attn", hps=SB_HEADS_PER_STEP)

        o_d = _attn_call(
            functools.partial(_fox_kernel, t=ATT_T, seq=seq), ATT_T, batch, seq,
            [proj, proj, proj, fox_kx, cum],
            [_q_spec(ATT_T, seq, d0), _kv_spec(seq, d0 + npair), _kv_spec(seq, d0 + 2 * npair),
             pl.BlockSpec((None, HEADS_PER_STEP, seq, LANES), lambda b, h, i: (b, h, 0, 0)),
             pl.BlockSpec((ATT_T, LANES), lambda b, h, i: (b * (seq // ATT_T) + i, 0))],
            "fox_attn", softmax_chains=HEADS_PER_STEP)

        xf = _out_proj([o_a, o_b, o_c, o_d], w_out_b, l, xf)

        mem_n = _rmsnorm(mem2, mem_kv_norm_g[l], BF16)
        k_m = _matmul(mem_n, mem_wk_b, BF16, layer=l, name="mem_k")
        v_m = _matmul(mem_n, mem_wv_b, BF16, layer=l, name="mem_v")
        xf = _mem_block(xf, mem_q_norm_g[l], mem_wq_b, k_m, v_m, mem_wo_b, l, batch, seq)

        h = _rmsnorm(xf, mlp_norm_g[l], BF16)
        u = _matmul(h, w_up_b, BF16, layer=l, relu2=True, name="mlp_up")
        xf = _matmul(u, w_down_b, F32, layer=l, res=xf, name="mlp_down")

    return _rmsnorm(xf, final_norm_g, F32).reshape(batch, seq, d)
```

```python
import functools
import math

import numpy as np
import jax
import jax.numpy as jnp
from jax import lax
from jax.experimental import pallas as pl
from jax.experimental.pallas import tpu as pltpu

F32 = jnp.float32
BF16 = jnp.bfloat16

D_MODEL = 4096
DEPTH = 2
HEAD_DIM = 128
GROUP_HEADS = 8
GROUP_WIDTH = GROUP_HEADS * HEAD_DIM
DIFF_QK = HEAD_DIM // 2
MLA_Q_RANK = 1536
MLA_KV_RANK = 512
MLA_NOPE = 128
MLA_ROPE = 64
ROPE_THETA = 10000.0
REL_BUCKETS = 32
REL_MAX_DIST = 128
MEM_HEADS = 4
MEM_HEAD_DIM = 128
NORM_EPS = 1e-6
NEG_INF = -1e30

A_COLS = 3 * GROUP_WIDTH
B_COLS = MLA_Q_RANK + MLA_KV_RANK + MLA_ROPE
C_COLS = 3 * GROUP_WIDTH

PROJ_A0 = 0
PROJ_CQ0 = A_COLS
PROJ_CKV0 = A_COLS + MLA_Q_RANK
PROJ_C0 = 5632
PROJ_D0 = PROJ_C0 + C_COLS
PROJ_COLS = PROJ_D0 + 3 * GROUP_WIDTH

LANES = 128
VMEM_LIMIT = 48 * 1024 * 1024

LOG2E = math.log2(math.e)

ATT_T = 512
SB_T = 256
HEADS_PER_STEP = 2
WIDE_HEADS_PER_STEP = 4
SB_HEADS_PER_STEP = 4
SB_DEAD_LOG2 = -160.0
TRUSTED_ROW_SUM = 2.0 ** -60


def _cparams(sem):
    return pltpu.CompilerParams(dimension_semantics=sem, vmem_limit_bytes=VMEM_LIMIT)


def _dot(a, b):
    return jnp.dot(a, b, preferred_element_type=F32)


def _dot_nt(a, b):
    return lax.dot_general(a, b, (((1,), (1,)), ((), ())), preferred_element_type=F32)


def _rms(x, g):
    return x * lax.rsqrt(jnp.mean(x * x, axis=-1, keepdims=True) + NORM_EPS) * g


def _rmsnorm_kernel(x_ref, g_ref, o_ref):
    o_ref[...] = _rms(x_ref[...].astype(F32), g_ref[...]).astype(o_ref.dtype)


def _rmsnorm(x, g, out_dtype, tm=256):
    m, d = x.shape
    return pl.pallas_call(
        _rmsnorm_kernel,
        grid=(m // tm,),
        in_specs=[pl.BlockSpec((tm, d), lambda i: (i, 0)),
                  pl.BlockSpec((1, d), lambda i: (0, 0))],
        out_specs=pl.BlockSpec((tm, d), lambda i: (i, 0)),
        out_shape=jax.ShapeDtypeStruct((m, d), out_dtype),
        compiler_params=_cparams(("parallel",)),
        name="rmsnorm",
    )(x, g.reshape(1, d).astype(F32))


def _mm_epilogue(r, res_ref, o_ref, relu2):
    if relu2:
        r = jnp.square(jnp.maximum(r, 0.0))
    if res_ref is not None:
        r = r + res_ref[...]
    o_ref[...] = r.astype(o_ref.dtype)


def _mm_kernel(a_ref, w_ref, *rest, nk, relu2, has_res):
    res_ref = rest[0] if has_res else None
    o_ref = rest[1] if has_res else rest[0]
    if nk == 1:
        _mm_epilogue(_dot(a_ref[...], w_ref[...].astype(a_ref.dtype)), res_ref, o_ref, relu2)
        return
    acc_ref = rest[-1]
    k = pl.program_id(2)

    @pl.when(k == 0)
    def _():
        acc_ref[...] = _dot(a_ref[...], w_ref[...].astype(a_ref.dtype))

    @pl.when(k > 0)
    def _():
        acc_ref[...] += _dot(a_ref[...], w_ref[...].astype(a_ref.dtype))

    @pl.when(k == nk - 1)
    def _():
        _mm_epilogue(acc_ref[...], res_ref, o_ref, relu2)


def _w_spec(w, layer, block, index_map):
    if w.ndim == 2:
        return pl.BlockSpec(block, index_map)
    return pl.BlockSpec((None,) + block, lambda *g: (layer,) + index_map(*g))


def _matmul(a, w, out_dtype, *, layer=None, res=None, relu2=False, tm=1024, tn=512, tk=4096,
            name="matmul"):
    m, kd = a.shape
    n = w.shape[-1]
    tm, tn, tk = min(tm, m), min(tn, n), min(tk, kd)
    assert m % tm == 0 and n % tn == 0 and kd % tk == 0
    nk = kd // tk
    in_specs = [pl.BlockSpec((tm, tk), lambda i, j, k: (i, k)),
                _w_spec(w, layer, (tk, tn), lambda i, j, k: (k, j))]
    args = [a, w]
    if res is not None:
        in_specs.append(pl.BlockSpec((tm, tn), lambda i, j, k: (i, j)))
        args.append(res)
    return pl.pallas_call(
        functools.partial(_mm_kernel, nk=nk, relu2=relu2, has_res=res is not None),
        grid=(m // tm, n // tn, nk),
        in_specs=in_specs,
        out_specs=pl.BlockSpec((tm, tn), lambda i, j, k: (i, j)),
        out_shape=jax.ShapeDtypeStruct((m, n), out_dtype),
        scratch_shapes=[pltpu.VMEM((tm, tn), F32)] if nk > 1 else [],
        compiler_params=_cparams(("parallel", "parallel", "arbitrary")),
        name=name,
    )(*args)


def _out_proj_kernel(a0, a1, a2, a3, w_ref, res_ref, o_ref):
    acc = res_ref[...]
    for g, a_ref in enumerate((a0, a1, a2, a3)):
        acc = acc + _dot(a_ref[...], w_ref[g * GROUP_WIDTH:(g + 1) * GROUP_WIDTH, :])
    o_ref[...] = acc


def _out_proj(mixes, w, layer, res, tm=1024, tn=512):
    m = res.shape[0]
    n = w.shape[-1]
    a_spec = pl.BlockSpec((tm, GROUP_WIDTH), lambda i, j: (i, 0))
    return pl.pallas_call(
        _out_proj_kernel,
        grid=(m // tm, n // tn),
        in_specs=[a_spec, a_spec, a_spec, a_spec,
                  _w_spec(w, layer, (w.shape[-2], tn), lambda i, j: (0, j)),
                  pl.BlockSpec((tm, tn), lambda i, j: (i, j))],
        out_specs=pl.BlockSpec((tm, tn), lambda i, j: (i, j)),
        out_shape=jax.ShapeDtypeStruct((m, n), F32),
        compiler_params=_cparams(("parallel", "parallel")),
        name="out_proj",
    )(*mixes, w, res)


def _rope_kernel(pos_ref, tab_ref, cos_ref, sin_ref):
    ang = pos_ref[...].astype(F32) * tab_ref[0:1, :]
    cos_ref[...] = jnp.cos(ang) * tab_ref[1:2, :]
    sin_ref[...] = jnp.sin(ang) * tab_ref[2:3, :]


def _rope_tables(pos_col, tm=1024):
    m = pos_col.shape[0]
    half = MLA_ROPE // 2
    inv = ROPE_THETA ** (-jnp.arange(half, dtype=F32) * 2.0 / MLA_ROPE)
    zeros = jnp.zeros((LANES - MLA_ROPE,), F32)
    ones = jnp.ones((half,), F32)
    tab = jnp.stack([jnp.concatenate([inv, inv, zeros]),
                     jnp.concatenate([ones, ones, zeros]),
                     jnp.concatenate([-ones, ones, zeros])]
                    + [jnp.zeros((LANES,), F32)] * 5)
    spec = pl.BlockSpec((tm, LANES), lambda i: (i, 0))
    return pl.pallas_call(
        _rope_kernel,
        grid=(m // tm,),
        in_specs=[pl.BlockSpec((tm, 1), lambda i: (i, 0)),
                  pl.BlockSpec((8, LANES), lambda i: (0, 0))],
        out_specs=[spec, spec],
        out_shape=[jax.ShapeDtypeStruct((m, LANES), F32)] * 2,
        compiler_params=_cparams(("parallel",)),
        name="rope_tables",
    )(pos_col, tab)


def _mla_q_kernel(cq_ref, g_ref, w_ref, cos_ref, sin_ref, o_ref):
    n = _rms(cq_ref[...].astype(F32), g_ref[...]).astype(BF16)
    t = _dot(n, w_ref[...])
    o_ref[:, :GROUP_WIDTH] = t[:, :GROUP_WIDTH].astype(o_ref.dtype)
    cos, sin = cos_ref[...], sin_ref[...]
    for h in range(GROUP_HEADS):
        lo = GROUP_WIDTH + h * LANES
        pe = t[:, lo:lo + LANES] * cos + t[:, lo + GROUP_WIDTH:lo + GROUP_WIDTH + LANES] * sin
        o_ref[:, lo:lo + LANES] = pe.astype(o_ref.dtype)


def _mla_q(proj, g, w, cos, sin, tm=512):
    m = proj.shape[0]
    cq_block = PROJ_CQ0 // MLA_Q_RANK
    return pl.pallas_call(
        _mla_q_kernel,
        grid=(m // tm,),
        in_specs=[pl.BlockSpec((tm, MLA_Q_RANK), lambda i: (i, cq_block)),
                  pl.BlockSpec((1, MLA_Q_RANK), lambda i: (0, 0)),
                  pl.BlockSpec(w.shape, lambda i: (0, 0)),
                  pl.BlockSpec((tm, LANES), lambda i: (i, 0)),
                  pl.BlockSpec((tm, LANES), lambda i: (i, 0))],
        out_specs=pl.BlockSpec((tm, 2 * GROUP_WIDTH), lambda i: (i, 0)),
        out_shape=jax.ShapeDtypeStruct((m, 2 * GROUP_WIDTH), BF16),
        compiler_params=_cparams(("parallel",)),
        name="mla_q",
    )(proj, g.reshape(1, -1).astype(F32), w, cos, sin)


def _mla_kv_kernel(ckv_ref, g_ref, w_ref, kr_ref, cos_ref, sin_ref, kv_ref, kpe_ref):
    n = _rms(ckv_ref[...].astype(F32), g_ref[...]).astype(BF16)
    kv_ref[...] = _dot(n, w_ref[...]).astype(kv_ref.dtype)
    kr = kr_ref[...]
    kpe = kr[:, :LANES] * cos_ref[...] + kr[:, LANES:] * sin_ref[...]
    lane = lax.broadcasted_iota(jnp.int32, kpe.shape, 1)
    kpe = jnp.where((lane >= MLA_ROPE) & (lane < MLA_ROPE + 3), 1.0, kpe)
    kpe_ref[...] = kpe.astype(kpe_ref.dtype)


def _mla_kv(proj, g, w, kr, cos, sin, tm=512):
    m = proj.shape[0]
    ckv_block = PROJ_CKV0 // MLA_KV_RANK
    return pl.pallas_call(
        _mla_kv_kernel,
        grid=(m // tm,),
        in_specs=[pl.BlockSpec((tm, MLA_KV_RANK), lambda i: (i, ckv_block)),
                  pl.BlockSpec((1, MLA_KV_RANK), lambda i: (0, 0)),
                  pl.BlockSpec(w.shape, lambda i: (0, 0)),
                  pl.BlockSpec((tm, 2 * LANES), lambda i: (i, 0)),
                  pl.BlockSpec((tm, LANES), lambda i: (i, 0)),
                  pl.BlockSpec((tm, LANES), lambda i: (i, 0))],
        out_specs=[pl.BlockSpec((tm, 2 * GROUP_WIDTH), lambda i: (i, 0)),
                   pl.BlockSpec((tm, LANES), lambda i: (i, 0))],
        out_shape=[jax.ShapeDtypeStruct((m, 2 * GROUP_WIDTH), BF16),
                   jax.ShapeDtypeStruct((m, LANES), BF16)],
        compiler_params=_cparams(("parallel",)),
        name="mla_kv",
    )(proj, g.reshape(1, -1).astype(F32), w, kr, cos, sin)


def _split3(x):
    x1 = x.astype(BF16)
    r1 = x - x1.astype(F32)
    x2 = r1.astype(BF16)
    x3 = (r1 - x2.astype(F32)).astype(BF16)
    return x1, x2, x3


def _place(lane, cols, base, fill):
    out = fill
    for i, c in enumerate(cols):
        out = jnp.where(lane == base + i, c.astype(F32), out)
    return out


def _forget_cum_kernel(f_ref, b_ref, cum_ref, kx_ref, *, chunk):
    r = lax.broadcasted_iota(jnp.int32, (chunk, chunk), 0)
    c = lax.broadcasted_iota(jnp.int32, (chunk, chunk), 1)
    tri = jnp.where(r >= c, 1.0, 0.0).astype(BF16)
    lane = lax.broadcasted_iota(jnp.int32, (chunk, LANES), 1)
    unit_fill = jnp.where(lane < 3, 1.0, 0.0)

    def body(ci, carry):
        rows = _rows(ci, chunk)
        x = f_ref[rows, :] + b_ref[...]
        logf = (jnp.minimum(x, 0.0) - jnp.log1p(jnp.exp(-jnp.abs(x)))) * LOG2E
        parts = _split3(logf)
        cs = carry + _dot(tri, parts[0]) + _dot(tri, parts[1]) + _dot(tri, parts[2])
        cum_ref[rows, :] = cs
        for h in range(GROUP_HEADS):
            kx_ref[h, rows, :] = _place(lane, _split3(cs[:, h:h + 1]), 3, unit_fill).astype(kx_ref.dtype)
        return cs[chunk - 1:chunk, :]

    lax.fori_loop(0, f_ref.shape[0] // chunk, body, jnp.zeros((1, LANES), F32))


def _forget_cum(side, b_f, batch, seq):
    f_block = 2
    return pl.pallas_call(
        functools.partial(_forget_cum_kernel, chunk=256),
        grid=(batch,),
        in_specs=[pl.BlockSpec((seq, LANES), lambda b: (b, f_block)),
                  pl.BlockSpec((1, LANES), lambda b: (0, 0))],
        out_specs=[pl.BlockSpec((seq, LANES), lambda b: (b, 0)),
                   pl.BlockSpec((None, GROUP_HEADS, seq, LANES), lambda b: (b, 0, 0, 0))],
        out_shape=[jax.ShapeDtypeStruct((batch * seq, LANES), F32),
                   jax.ShapeDtypeStruct((batch, GROUP_HEADS, seq, LANES), BF16)],
        compiler_params=_cparams(("parallel",)),
        name="forget_cum",
    )(side, b_f)


def _tile_iota(t):
    return (lax.broadcasted_iota(jnp.int32, (t, t), 0),
            lax.broadcasted_iota(jnp.int32, (t, t), 1))


def _softmax_update(s, v, m, acc):
    m_new = jnp.maximum(m, jnp.max(s, axis=-1, keepdims=True))
    alpha = jnp.exp2(m - m_new)
    p = jnp.exp2(s - m_new).astype(BF16)
    v_aug = jnp.concatenate([v, jnp.ones_like(v)], axis=1)
    return m_new, alpha * acc + _dot(p, v_aug)


def _softmax_init(t):
    return (jnp.full((t, 1), NEG_INF, F32), jnp.zeros((t, 2 * HEAD_DIM), F32))


def _softmax_out(acc):
    return acc[:, :HEAD_DIM] / acc[:, HEAD_DIM:]


def _hs(h):
    return slice(h * HEAD_DIM, (h + 1) * HEAD_DIM)


def _rows(j, w):
    return pl.ds(pl.multiple_of(j * w, w), w)


def _causal_sweep(step, qi, init):
    carry = lax.fori_loop(0, qi, lambda j, c: step(j, c, False), init)
    return step(qi, carry, True)


def _update(s, v, state, mask, acc_ref, chain):
    if mask is not None:
        s = jnp.where(mask, s, NEG_INF)
    if acc_ref is None:
        return _softmax_update(s, v, *state)
    v_aug = jnp.concatenate([v, jnp.ones_like(v)], axis=1)
    acc_ref[chain] += _dot(jnp.exp2(s).astype(BF16), v_aug)
    return None


def _run_softmax(qi, t, n_chains, make_step, finish, acc_ref):
    acc_ref[...] = jnp.zeros_like(acc_ref)
    fast = make_step(acc_ref)

    def wide(j, carry):
        fast(j, None, False, 2 * t)
        return carry

    lax.fori_loop(0, qi // 2, wide, 0)

    @pl.when(qi % 2 == 1)
    def _():
        fast(qi - 1, None, False, t)

    fast(qi, None, True, t)
    accs = [acc_ref[c] for c in range(n_chains)]
    l_min = accs[0][:, HEAD_DIM:HEAD_DIM + 1]
    for a in accs[1:]:
        l_min = jnp.minimum(l_min, a[:, HEAD_DIM:HEAD_DIM + 1])
    ok = jnp.min(l_min) >= TRUSTED_ROW_SUM

    @pl.when(ok)
    def _():
        finish([_softmax_out(a) for a in accs])

    @pl.when(jnp.logical_not(ok))
    def _():
        slow = make_step(None)
        init = tuple(_softmax_init(t) for _ in range(n_chains))
        carry = lax.fori_loop(0, qi, lambda j, c: slow(j, c, False, t), init)
        finish([_softmax_out(c[1]) for c in slow(qi, carry, True, t)])


def _key_norm_max(k_ref, h, seq, extra_ref=None):
    chunk = ATT_T

    def body(c, mx):
        kk = k_ref[_rows(c, chunk), _hs(h)].astype(F32)
        n2 = jnp.sum(kk * kk, axis=1, keepdims=True)
        if extra_ref is not None:
            e = extra_ref[_rows(c, chunk), :].astype(F32)
            n2 = n2 + jnp.sum(e * e, axis=1, keepdims=True)
        return jnp.maximum(mx, n2)

    mx = lax.fori_loop(0, seq // chunk, body, jnp.zeros((chunk, 1), F32))
    return jnp.sqrt(jnp.max(mx, axis=0, keepdims=True))


def _store_key_norms(kmax_ref, qi, norm_fn):
    @pl.when(qi == 0)
    def _():
        for h in range(kmax_ref.shape[0]):
            kmax_ref[h] = jnp.broadcast_to(norm_fn(h), kmax_ref.shape[1:])


def _row_norm(x):
    xf = x.astype(F32)
    return jnp.sqrt(jnp.sum(xf * xf, axis=1, keepdims=True))


def _fox_kernel(q_ref, k_ref, v_ref, kx_ref, cumq_ref, o_ref, kmax_ref, acc_ref, *, t, seq):
    group, qi = pl.program_id(1), pl.program_id(2)
    hps = kmax_ref.shape[0]
    _store_key_norms(kmax_ref, qi, lambda h: _key_norm_max(k_ref, h, seq))
    row, col = _tile_iota(t)
    lane = lax.broadcasted_iota(jnp.int32, (t, LANES), 1)
    minus_fill = jnp.where((lane >= 3) & (lane < 6), -1.0, 0.0)
    cumq = cumq_ref[...]

    def q_aug(h, fast):
        q = q_ref[:, _hs(h)]
        c = jnp.sum(jnp.where(lane == group * hps + h, cumq, 0.0), axis=1, keepdims=True)
        if fast:
            c = c - _row_norm(q) * kmax_ref[h][0:1, 0:1]
        return jnp.concatenate([q, _place(lane, _split3(c), 0, minus_fill).astype(BF16)], axis=1)

    def make_step(acc_ref):
        fast = acc_ref is not None
        qs = [q_aug(h, fast) for h in range(hps)]

        def step(j, carry, masked, w):
            mask = (col <= row) if masked else None
            out = []
            for h in range(hps):
                k = jnp.concatenate([k_ref[_rows(j, w), _hs(h)], kx_ref[h, _rows(j, w), :]], axis=1)
                out.append(_update(_dot_nt(qs[h], k), v_ref[_rows(j, w), _hs(h)], None if fast else carry[h], mask, acc_ref, h))
            return tuple(out)

        return step

    def finish(outs):
        o_ref[...] = jnp.concatenate(outs, axis=1).astype(o_ref.dtype)

    _run_softmax(qi, t, hps, make_step, finish, acc_ref)


def _mla_kernel(qn_ref, qp_ref, kn_ref, kp_ref, v_ref, o_ref, kmax_ref, acc_ref, *, t, seq):
    qi = pl.program_id(2)
    hps = kmax_ref.shape[0]
    _store_key_norms(kmax_ref, qi, lambda h: _key_norm_max(kn_ref, h, seq, kp_ref))
    row, col = _tile_iota(t)
    lane = lax.broadcasted_iota(jnp.int32, (t, LANES), 1)

    def q_aug(h, fast):
        qn, qp = qn_ref[:, _hs(h)], qp_ref[:, _hs(h)]
        if fast:
            qn_f, qp_f = qn.astype(F32), qp.astype(F32)
            norm = jnp.sqrt(jnp.sum(qn_f * qn_f, axis=1, keepdims=True)
                            + jnp.sum(qp_f * qp_f, axis=1, keepdims=True))
            qp = _place(lane, _split3(-norm * kmax_ref[h][0:1, 0:1]), MLA_ROPE, qp_f).astype(BF16)
        return jnp.concatenate([qn, qp], axis=1)

    def make_step(acc_ref):
        fast = acc_ref is not None
        qs = [q_aug(h, fast) for h in range(hps)]

        def step(j, carry, masked, w):
            mask = (col <= row) if masked else None
            kp = kp_ref[_rows(j, w), :]
            out = []
            for h in range(hps):
                k = jnp.concatenate([kn_ref[_rows(j, w), _hs(h)], kp], axis=1)
                out.append(_update(_dot_nt(qs[h], k), v_ref[_rows(j, w), _hs(h)], None if fast else carry[h], mask, acc_ref, h))
            return tuple(out)

        return step

    def finish(outs):
        o_ref[...] = jnp.concatenate(outs, axis=1).astype(o_ref.dtype)

    _run_softmax(qi, t, hps, make_step, finish, acc_ref)


def _diff_kernel(lq1_ref, lk1_ref, lq2_ref, lk2_ref, q_ref, k_ref, v_ref, posq_ref, posk_ref,
                 bias_ref, subg_ref, o_ref, kmax_ref, acc_ref, *, t, seq, lam_init):
    qi = pl.program_id(2)
    _store_key_norms(kmax_ref, qi, lambda h: _key_norm_max(k_ref, h, seq))
    lam = (jnp.exp(jnp.sum(lq1_ref[...] * lk1_ref[...], axis=-1, keepdims=True))
           - jnp.exp(jnp.sum(lq2_ref[...] * lk2_ref[...], axis=-1, keepdims=True)) + lam_init)
    row, col = _tile_iota(t)
    lane = lax.broadcasted_iota(jnp.int32, (t, LANES), 1)
    zero_fill = jnp.zeros((t, LANES), F32)

    def key_units(w):
        return jnp.where(lax.broadcasted_iota(jnp.int32, (w, LANES), 1) < 3, 1.0, 0.0).astype(BF16)

    posq = posq_ref[...]
    posq_min = jnp.min(posq)
    bias_rows = [jnp.broadcast_to(bias_ref[h], (t, LANES)) for h in range(HEADS_PER_STEP)]
    last = REL_MAX_DIST - 1
    far_bias = [bias_ref[h][:, last:] for h in range(HEADS_PER_STEP)]

    def half_q(h, c):
        q = q_ref[:, _hs(h)]
        return jnp.where((lane < DIFF_QK) if c == 0 else (lane >= DIFF_QK), q, jnp.zeros_like(q))

    def make_step(acc_ref):
        fast = acc_ref is not None
        q_near, q_far = [], []
        for h in range(HEADS_PER_STEP):
            for c in range(2):
                qz = half_q(h, c)
                if fast:
                    bound = (_row_norm(qz) * kmax_ref[h][0:1, 0:1]
                             + jnp.max(bias_ref[h], axis=1, keepdims=True))
                    q_near.append(jnp.concatenate(
                        [qz, _place(lane, _split3(-bound), 0, zero_fill).astype(BF16)], axis=1))
                    q_far.append(jnp.concatenate(
                        [qz, _place(lane, _split3(far_bias[h] - bound), 0, zero_fill).astype(BF16)], axis=1))
                else:
                    q_near.append(qz)
                    q_far.append(qz)

        def chains(j, carry, masked, w, gathered):
            mask = (col <= row) if masked else None
            out = []
            for h in range(HEADS_PER_STEP):
                k = k_ref[_rows(j, w), _hs(h)]
                if fast:
                    k = jnp.concatenate([k, key_units(w)], axis=1)
                v = v_ref[_rows(j, w), _hs(h)]
                for c in range(2):
                    i = 2 * h + c
                    if gathered is not None:
                        s = _dot_nt(q_near[i], k) + gathered[h]
                    elif fast:
                        s = _dot_nt(q_far[i], k)
                    else:
                        s = _dot_nt(q_far[i], k) + far_bias[h]
                    out.append(_update(s, v, None if fast else carry[i], mask, acc_ref, i))
            return tuple(out)

        def step(j, carry, masked, w):
            posk = posk_ref[:, _rows(j, w)]

            def near(c):
                dist = jnp.clip(posq - posk, 0, last)
                gathered = [jnp.concatenate(
                    [jnp.take_along_axis(bias_rows[h], dist[:, cc * LANES:(cc + 1) * LANES], axis=1,
                                         mode="promise_in_bounds")
                     for cc in range(w // LANES)], axis=1) for h in range(HEADS_PER_STEP)]
                return chains(j, c, masked, w, gathered)

            def far(c):
                return chains(j, c, masked, w, None)

            return lax.cond(posq_min - jnp.max(posk) < last, near, far, carry)

        return step

    def finish(outs):
        normed = []
        for h in range(HEADS_PER_STEP):
            o = outs[2 * h] - lam * outs[2 * h + 1]
            normed.append(_rms(o, subg_ref[...]) * (1.0 - lam_init))
        o_ref[...] = jnp.concatenate(normed, axis=1).astype(o_ref.dtype)

    _run_softmax(qi, t, 2 * HEADS_PER_STEP, make_step, finish, acc_ref)


def _sb_kernel(q_ref, k_ref, v_ref, o_ref, *, t):
    qi = pl.program_id(2)
    w = t
    row, col = _tile_iota(t)
    later = jnp.where(row > col, 1.0, 0.0).astype(BF16)
    qs = [q_ref[:, _hs(h)] for h in range(SB_HEADS_PER_STEP)]

    def step(j, carry, masked):
        out = []
        for h in range(SB_HEADS_PER_STEP):
            tail, acc = carry[h]
            z = _dot_nt(qs[h], k_ref[_rows(j, w), _hs(h)])
            sp = jnp.log(1.0 + jnp.exp2(-jnp.abs(z))) * LOG2E
            log_beta = jnp.minimum(z, 0.0) - sp
            log_1m = -jnp.maximum(z, 0.0) - sp
            if masked:
                log_1m = jnp.where(col < row, log_1m, 0.0)
            hi = log_1m.astype(BF16)
            lo = (log_1m - hi.astype(F32)).astype(BF16)
            both = _dot(jnp.concatenate([hi, lo], axis=0), later)
            suffix = both[:t] + both[t:]
            a = jnp.exp2(log_beta + suffix + tail)
            if masked:
                a = jnp.where(col < row, a, 0.0)
            acc = acc + _dot(a.astype(BF16), v_ref[_rows(j, w), _hs(h)])
            tail = tail + suffix[:, :1] + log_1m[:, :1]
            out.append((tail, acc))
        return tuple(out)

    def live(carry):
        worst = carry[0][0]
        for h in range(1, SB_HEADS_PER_STEP):
            worst = jnp.maximum(worst, carry[h][0])
        return (jnp.max(worst) > SB_DEAD_LOG2).astype(jnp.int32)

    init = tuple((jnp.zeros((t, 1), F32), jnp.zeros((t, HEAD_DIM), F32)) for _ in range(SB_HEADS_PER_STEP))
    carry = step(qi, init, True)

    def body(state):
        j, _, c = state
        c = step(j, c, False)
        return j - 1, live(c), c

    _, _, carry = lax.while_loop(lambda st: (st[0] >= 0) & (st[1] > 0), body,
                                 (qi - 1, live(carry), carry))
    o_ref[...] = jnp.concatenate([c[1] for c in carry], axis=1).astype(o_ref.dtype)


def _attn_call(kernel, t, batch, seq, in_arrays, in_specs, name, hps=HEADS_PER_STEP, softmax_chains=0):
    nq = seq // t
    width = hps * HEAD_DIM
    scratch = []
    if softmax_chains:
        scratch = [pltpu.VMEM((hps, 8, LANES), F32),
                   pltpu.VMEM((softmax_chains, t, 2 * HEAD_DIM), F32)]
    return pl.pallas_call(
        kernel,
        grid=(batch, GROUP_HEADS // hps, nq),
        in_specs=in_specs,
        out_specs=pl.BlockSpec((t, width), lambda b, h, i: (b * nq + i, h)),
        out_shape=jax.ShapeDtypeStruct((batch * seq, GROUP_WIDTH), BF16),
        scratch_shapes=scratch,
        compiler_params=_cparams(("arbitrary", "arbitrary", "arbitrary")),
        name=name,
    )(*in_arrays)


def _q_spec(t, seq, col0, hps=HEADS_PER_STEP):
    nq = seq // t
    return pl.BlockSpec((t, hps * HEAD_DIM), lambda b, h, i: (b * nq + i, col0 + h))


def _kv_spec(seq, col0, hps=HEADS_PER_STEP):
    return pl.BlockSpec((seq, hps * HEAD_DIM), lambda b, h, i: (b, col0 + h))


def _shared_spec(seq):
    return pl.BlockSpec((seq, HEAD_DIM), lambda b, h, i: (b, 0))


def _full_spec(shape):
    return pl.BlockSpec(shape, lambda b, h, i: (0,) * len(shape))


def _mem_kernel(x_ref, g_ref, wq_ref, k_ref, v_ref, wo_ref, o_ref):
    x = x_ref[...]
    h = _rms(x, g_ref[...]).astype(BF16)
    q = _dot(h, wq_ref[...]).astype(BF16)
    outs = []
    for hd in range(MEM_HEADS):
        sl = slice(hd * MEM_HEAD_DIM, (hd + 1) * MEM_HEAD_DIM)
        s = _dot_nt(q[:, sl], k_ref[:, sl])
        p = jnp.exp2(s - jnp.max(s, axis=-1, keepdims=True))
        o = _dot(p.astype(BF16), v_ref[:, sl]) / jnp.sum(p, axis=-1, keepdims=True)
        outs.append(o.astype(BF16))
    o_ref[...] = x + _dot(jnp.concatenate(outs, axis=-1), wo_ref[...])


def _mem_block(x, g, wq, k, v, wo, layer, batch, seq, tm=256):
    m, d = x.shape
    n_mem = k.shape[0] // batch
    tiles_per_batch = seq // tm
    width = MEM_HEADS * MEM_HEAD_DIM
    return pl.pallas_call(
        _mem_kernel,
        grid=(m // tm,),
        in_specs=[pl.BlockSpec((tm, d), lambda i: (i, 0)),
                  pl.BlockSpec((1, d), lambda i: (0, 0)),
                  _w_spec(wq, layer, (d, width), lambda i: (0, 0)),
                  pl.BlockSpec((n_mem, width), lambda i: (i // tiles_per_batch, 0)),
                  pl.BlockSpec((n_mem, width), lambda i: (i // tiles_per_batch, 0)),
                  _w_spec(wo, layer, (width, d), lambda i: (0, 0))],
        out_specs=pl.BlockSpec((tm, d), lambda i: (i, 0)),
        out_shape=jax.ShapeDtypeStruct((m, d), F32),
        compiler_params=_cparams(("parallel",)),
        name="mem_block",
    )(x, g.reshape(1, d).astype(F32), wq, k, v, wo)


def _rel_bias_by_distance(rel_table):
    d = jnp.arange(REL_MAX_DIST, dtype=jnp.int32)
    max_exact = REL_BUCKETS // 2
    nf = jnp.maximum(d, 1).astype(F32)
    large = max_exact + (jnp.log(nf / max_exact) / math.log(REL_MAX_DIST / max_exact)
                         * (REL_BUCKETS - max_exact)).astype(jnp.int32)
    large = jnp.minimum(large, REL_BUCKETS - 1)
    bucket = jnp.where(d < max_exact, d, large)
    return jnp.take(rel_table, bucket, axis=0).T.astype(F32)


def _in_proj_col_scale():
    gw = GROUP_WIDTH
    scale = np.ones((PROJ_COLS,), np.float32)
    scale[PROJ_A0:PROJ_A0 + gw] = DIFF_QK ** -0.5 * LOG2E
    scale[PROJ_C0:PROJ_C0 + gw] = HEAD_DIM ** -0.5 * LOG2E
    scale[PROJ_D0:PROJ_D0 + gw] = HEAD_DIM ** -0.5 * LOG2E
    return scale


PREP_CHUNK = 512
PREP_ALIGNED = PROJ_C0 // PREP_CHUNK
PREP_SKEW = PREP_CHUNK - (PROJ_C0 - (A_COLS + B_COLS))


def _regroup_kernel(x_ref, tail_ref, s_ref, o_ref):
    c = pl.program_id(1)

    @pl.when(c < PREP_ALIGNED)
    def _():
        o_ref[...] = (x_ref[...].T * s_ref[...]).astype(o_ref.dtype)

    @pl.when(c >= PREP_ALIGNED)
    def _():
        x = jnp.concatenate([x_ref[PREP_SKEW:, :], tail_ref[...]], axis=0)
        o_ref[...] = (x.T * s_ref[...]).astype(o_ref.dtype)


def _regroup_in_proj(w_in_t, layer, tr=512):
    d = w_in_t.shape[2]
    tails_per_chunk = PREP_CHUNK // PREP_SKEW
    scale = jnp.asarray(_in_proj_col_scale()).reshape(1, PROJ_COLS)
    return pl.pallas_call(
        _regroup_kernel,
        grid=(d // tr, PROJ_COLS // PREP_CHUNK),
        in_specs=[
            pl.BlockSpec((None, PREP_CHUNK, tr),
                         lambda r, c: (layer, jnp.where(c < PREP_ALIGNED, c, c - 1), r)),
            pl.BlockSpec((None, PREP_SKEW, tr),
                         lambda r, c: (layer, jnp.where(c < PREP_ALIGNED, 0, c * tails_per_chunk), r)),
            pl.BlockSpec((1, PREP_CHUNK), lambda r, c: (0, c))],
        out_specs=pl.BlockSpec((tr, PREP_CHUNK), lambda r, c: (r, c)),
        out_shape=jax.ShapeDtypeStruct((d, PROJ_COLS), BF16),
        compiler_params=_cparams(("parallel", "parallel")),
        name="regroup_in_proj",
    )(w_in_t, w_in_t, scale)


SIDE_COLS = 4 * LANES


def _side_proj_kernel(h_ref, kr_ref, f_ref, o_ref):
    kr, f = kr_ref[...], f_ref[...]
    half = MLA_ROPE // 2

    def zeros(n):
        return jnp.zeros((n, kr.shape[1]), kr.dtype)

    w_t = jnp.concatenate([kr, zeros(LANES - MLA_ROPE), kr[half:], kr[:half], zeros(LANES - MLA_ROPE),
                           f, zeros(2 * LANES - GROUP_HEADS)], axis=0)
    o_ref[...] = _dot_nt(h_ref[...], w_t.astype(h_ref.dtype))


def _side_proj(h, w_in_t, layer, tm=1024):
    m, d = h.shape
    kr_row0 = A_COLS + MLA_Q_RANK + MLA_KV_RANK
    f_row0 = A_COLS + B_COLS + C_COLS + 3 * GROUP_WIDTH
    return pl.pallas_call(
        _side_proj_kernel,
        grid=(m // tm,),
        in_specs=[pl.BlockSpec((tm, d), lambda i: (i, 0)),
                  pl.BlockSpec((None, MLA_ROPE, d), lambda i: (layer, kr_row0 // MLA_ROPE, 0)),
                  pl.BlockSpec((None, GROUP_HEADS, d), lambda i: (layer, f_row0 // GROUP_HEADS, 0))],
        out_specs=pl.BlockSpec((tm, SIDE_COLS), lambda i: (i, 0)),
        out_shape=jax.ShapeDtypeStruct((m, SIDE_COLS), F32),
        compiler_params=_cparams(("parallel",)),
        name="in_proj_side",
    )(h, w_in_t, w_in_t)


def _prep_uq(w):
    qk = MLA_NOPE + MLA_ROPE
    half = MLA_ROPE // 2
    w = (w * (qk ** -0.5 * LOG2E)).reshape(w.shape[0], GROUP_HEADS, qk)
    pad = jnp.zeros((w.shape[0], GROUP_HEADS, LANES - MLA_ROPE), w.dtype)
    nope = w[:, :, :MLA_NOPE]
    r1 = w[:, :, MLA_NOPE:MLA_NOPE + half]
    r2 = w[:, :, MLA_NOPE + half:]
    rope = jnp.concatenate([r1, r2, pad], axis=-1)
    swapped = jnp.concatenate([r2, r1, pad], axis=-1)
    return jnp.concatenate([nope.reshape(w.shape[0], -1), rope.reshape(w.shape[0], -1),
                            swapped.reshape(w.shape[0], -1)], axis=1).astype(BF16)


def _prep_ukv(w):
    w = w.reshape(w.shape[0], GROUP_HEADS, MLA_NOPE + HEAD_DIM)
    return jnp.concatenate([w[:, :, :MLA_NOPE].reshape(w.shape[0], -1),
                            w[:, :, MLA_NOPE:].reshape(w.shape[0], -1)], axis=1).astype(BF16)


def kernel(x, mem, positions, attn_norm_g, w_in, w_out, rel_table, diff_lam_q1, diff_lam_k1, diff_lam_q2, diff_lam_k2, diff_sub_g, mla_q_norm_g, mla_kv_norm_g, mla_w_uq, mla_w_ukv, fox_b_f, mem_q_norm_g, mem_kv_norm_g, mem_w_q, mem_w_k, mem_w_v, mem_w_o, mlp_norm_g, w_up, w_down, final_norm_g):
    batch, seq, d = x.shape
    m = batch * seq
    gw = GROUP_WIDTH
    nh = GROUP_HEADS
    xf = x.reshape(m, d)
    mem2 = mem.reshape(batch * mem.shape[1], d)
    pos_col = positions.reshape(m, 1).astype(jnp.int32)
    pos_row = positions.reshape(batch, 1, seq).astype(jnp.int32)
    cos, sin = _rope_tables(pos_col)
    bias_by_dist = (_rel_bias_by_distance(rel_table) * LOG2E).reshape(nh, 1, LANES)
    npair = nh // HEADS_PER_STEP
    pair_cols = HEADS_PER_STEP * HEAD_DIM
    a0 = PROJ_A0 // pair_cols
    nwide = nh // WIDE_HEADS_PER_STEP
    fox0 = PROJ_D0 // (WIDE_HEADS_PER_STEP * HEAD_DIM)
    nsb = nh // SB_HEADS_PER_STEP
    sb0 = PROJ_C0 // (SB_HEADS_PER_STEP * HEAD_DIM)

    w_out_b, w_up_b, w_down_b = w_out.astype(BF16), w_up.astype(BF16), w_down.astype(BF16)
    mem_wq_b = (mem_w_q * (MEM_HEAD_DIM ** -0.5 * LOG2E)).astype(BF16)
    mem_wk_b, mem_wv_b, mem_wo_b = mem_w_k.astype(BF16), mem_w_v.astype(BF16), mem_w_o.astype(BF16)

    w_in_t = jnp.swapaxes(w_in, 1, 2)

    for l in range(DEPTH):
        w_main = _regroup_in_proj(w_in_t, l)
        h = _rmsnorm(xf, attn_norm_g[l], BF16)
        proj = _matmul(h, w_main, BF16, name="in_proj")
        side = _side_proj(h, w_in_t, l)
        b_f = jnp.concatenate([fox_b_f[l].astype(F32), jnp.zeros((LANES - nh,), F32)]).reshape(1, LANES)
        cum, fox_kx = _forget_cum(side, b_f, batch, seq)

        lam_init = 0.8 - 0.6 * math.exp(-0.3 * l)
        lam_vecs = [v[l].reshape(1, DIFF_QK).astype(F32)
                    for v in (diff_lam_q1, diff_lam_k1, diff_lam_q2, diff_lam_k2)]
        o_a = _attn_call(
            functools.partial(_diff_kernel, t=ATT_T, seq=seq, lam_init=lam_init), ATT_T, batch, seq,
            lam_vecs + [proj, proj, proj, pos_col, pos_row, bias_by_dist,
                        diff_sub_g[l].reshape(1, HEAD_DIM).astype(F32)],
            [_full_spec((1, DIFF_QK))] * 4
            + [_q_spec(ATT_T, seq, a0), _kv_spec(seq, a0 + npair), _kv_spec(seq, a0 + 2 * npair),
               pl.BlockSpec((ATT_T, 1), lambda b, h, i: (b * (seq // ATT_T) + i, 0)),
               pl.BlockSpec((None, 1, seq), lambda b, h, i: (b, 0, 0)),
               pl.BlockSpec((HEADS_PER_STEP, 1, LANES), lambda b, h, i: (h, 0, 0)),
               _full_spec((1, HEAD_DIM))],
            "diff_attn", softmax_chains=2 * HEADS_PER_STEP)

        q_b = _mla_q(proj, mla_q_norm_g[l], _prep_uq(mla_w_uq[l]), cos, sin)
        kv_b, kpe_b = _mla_kv(proj, mla_kv_norm_g[l], _prep_ukv(mla_w_ukv[l]), side, cos, sin)
        o_b = _attn_call(
            functools.partial(_mla_kernel, t=ATT_T, seq=seq), ATT_T, batch, seq,
            [q_b, q_b, kv_b, kpe_b, kv_b],
            [_q_spec(ATT_T, seq, 0, WIDE_HEADS_PER_STEP), _q_spec(ATT_T, seq, nwide, WIDE_HEADS_PER_STEP),
             _kv_spec(seq, 0, WIDE_HEADS_PER_STEP), _shared_spec(seq),
             _kv_spec(seq, nwide, WIDE_HEADS_PER_STEP)],
            "mla_attn", hps=WIDE_HEADS_PER_STEP, softmax_chains=WIDE_HEADS_PER_STEP)

        o_c = _attn_call(
            functools.partial(_sb_kernel, t=SB_T), SB_T, batch, seq,
            [proj, proj, proj],
            [_q_spec(SB_T, seq, sb0, SB_HEADS_PER_STEP), _kv_spec(seq, sb0 + nsb, SB_HEADS_PER_STEP),
             _kv_spec(seq, sb0 + 2 * nsb, SB_HEADS_PER_STEP)],
            "stickbreak_attn", hps=SB_HEADS_PER_STEP)

        o_d = _attn_call(
            functools.partial(_fox_kernel, t=ATT_T, seq=seq), ATT_T, batch, seq,
            [proj, proj, proj, fox_kx, cum],
            [_q_spec(ATT_T, seq, fox0, WIDE_HEADS_PER_STEP), _kv_spec(seq, fox0 + nwide, WIDE_HEADS_PER_STEP),
             _kv_spec(seq, fox0 + 2 * nwide, WIDE_HEADS_PER_STEP),
             pl.BlockSpec((None, WIDE_HEADS_PER_STEP, seq, LANES), lambda b, h, i: (b, h, 0, 0)),
             pl.BlockSpec((ATT_T, LANES), lambda b, h, i: (b * (seq // ATT_T) + i, 0))],
            "fox_attn", hps=WIDE_HEADS_PER_STEP, softmax_chains=WIDE_HEADS_PER_STEP)

        xf = _out_proj([o_a, o_b, o_c, o_d], w_out_b, l, xf)

        mem_n = _rmsnorm(mem2, mem_kv_norm_g[l], BF16)
        k_m = _matmul(mem_n, mem_wk_b, BF16, layer=l, name="mem_k")
        v_m = _matmul(mem_n, mem_wv_b, BF16, layer=l, name="mem_v")
        xf = _mem_block(xf, mem_q_norm_g[l], mem_wq_b, k_m, v_m, mem_wo_b, l, batch, seq)

        h = _rmsnorm(xf, mlp_norm_g[l], BF16)
        u = _matmul(h, w_up_b, BF16, layer=l, relu2=True, name="mlp_up")
        xf = _matmul(u, w_down_b, F32, layer=l, res=xf, name="mlp_down")

    return _rmsnorm(xf, final_norm_g, F32).reshape(batch, seq, d)
```

```python
import functools
import math

import numpy as np
import jax
import jax.numpy as jnp
from jax import lax
from jax.experimental import pallas as pl
from jax.experimental.pallas import tpu as pltpu

F32 = jnp.float32
BF16 = jnp.bfloat16

D_MODEL = 4096
DEPTH = 2
HEAD_DIM = 128
GROUP_HEADS = 8
GROUP_WIDTH = GROUP_HEADS * HEAD_DIM
DIFF_QK = HEAD_DIM // 2
MLA_Q_RANK = 1536
MLA_KV_RANK = 512
MLA_NOPE = 128
MLA_ROPE = 64
ROPE_THETA = 10000.0
REL_BUCKETS = 32
REL_MAX_DIST = 128
MEM_HEADS = 4
MEM_HEAD_DIM = 128
NORM_EPS = 1e-6
NEG_INF = -1e30

A_COLS = 3 * GROUP_WIDTH
B_COLS = MLA_Q_RANK + MLA_KV_RANK + MLA_ROPE
C_COLS = 3 * GROUP_WIDTH

PROJ_A0 = 0
PROJ_CQ0 = A_COLS
PROJ_CKV0 = A_COLS + MLA_Q_RANK
PROJ_C0 = 5632
PROJ_D0 = PROJ_C0 + C_COLS
PROJ_COLS = PROJ_D0 + 3 * GROUP_WIDTH

LANES = 128
VMEM_LIMIT = 56 * 1024 * 1024

LOG2E = math.log2(math.e)

ATT_T = 512
SB_T = 256
HEADS_PER_STEP = 4
WIDE_HEADS_PER_STEP = 4
SB_HEADS_PER_STEP = 4
SB_DEAD_LOG2 = -160.0
TRUSTED_ROW_SUM = 2.0 ** -60


def _cparams(sem):
    return pltpu.CompilerParams(dimension_semantics=sem, vmem_limit_bytes=VMEM_LIMIT)


def _dot(a, b):
    return jnp.dot(a, b, preferred_element_type=F32)


def _dot_nt(a, b):
    return lax.dot_general(a, b, (((1,), (1,)), ((), ())), preferred_element_type=F32)


def _rms(x, g):
    return x * lax.rsqrt(jnp.mean(x * x, axis=-1, keepdims=True) + NORM_EPS) * g


def _rmsnorm_kernel(x_ref, g_ref, o_ref):
    o_ref[...] = _rms(x_ref[...].astype(F32), g_ref[...]).astype(o_ref.dtype)


def _rmsnorm(x, g, out_dtype, tm=256):
    m, d = x.shape
    return pl.pallas_call(
        _rmsnorm_kernel,
        grid=(m // tm,),
        in_specs=[pl.BlockSpec((tm, d), lambda i: (i, 0)),
                  pl.BlockSpec((1, d), lambda i: (0, 0))],
        out_specs=pl.BlockSpec((tm, d), lambda i: (i, 0)),
        out_shape=jax.ShapeDtypeStruct((m, d), out_dtype),
        compiler_params=_cparams(("parallel",)),
        name="rmsnorm",
    )(x, g.reshape(1, d).astype(F32))


def _mm_epilogue(r, res_ref, o_ref, relu2):
    if relu2:
        r = jnp.square(jnp.maximum(r, 0.0))
    if res_ref is not None:
        r = r + res_ref[...]
    o_ref[...] = r.astype(o_ref.dtype)


def _mm_kernel(a_ref, w_ref, *rest, nk, relu2, has_res):
    res_ref = rest[0] if has_res else None
    o_ref = rest[1] if has_res else rest[0]
    if nk == 1:
        _mm_epilogue(_dot(a_ref[...], w_ref[...].astype(a_ref.dtype)), res_ref, o_ref, relu2)
        return
    acc_ref = rest[-1]
    k = pl.program_id(2)

    @pl.when(k == 0)
    def _():
        acc_ref[...] = _dot(a_ref[...], w_ref[...].astype(a_ref.dtype))

    @pl.when(k > 0)
    def _():
        acc_ref[...] += _dot(a_ref[...], w_ref[...].astype(a_ref.dtype))

    @pl.when(k == nk - 1)
    def _():
        _mm_epilogue(acc_ref[...], res_ref, o_ref, relu2)


def _w_spec(w, layer, block, index_map):
    if w.ndim == 2:
        return pl.BlockSpec(block, index_map)
    return pl.BlockSpec((None,) + block, lambda *g: (layer,) + index_map(*g))


def _matmul(a, w, out_dtype, *, layer=None, res=None, relu2=False, tm=1024, tn=512, tk=4096,
            name="matmul"):
    m, kd = a.shape
    n = w.shape[-1]
    tm, tn, tk = min(tm, m), min(tn, n), min(tk, kd)
    assert m % tm == 0 and n % tn == 0 and kd % tk == 0
    nk = kd // tk
    in_specs = [pl.BlockSpec((tm, tk), lambda i, j, k: (i, k)),
                _w_spec(w, layer, (tk, tn), lambda i, j, k: (k, j))]
    args = [a, w]
    if res is not None:
        in_specs.append(pl.BlockSpec((tm, tn), lambda i, j, k: (i, j)))
        args.append(res)
    return pl.pallas_call(
        functools.partial(_mm_kernel, nk=nk, relu2=relu2, has_res=res is not None),
        grid=(m // tm, n // tn, nk),
        in_specs=in_specs,
        out_specs=pl.BlockSpec((tm, tn), lambda i, j, k: (i, j)),
        out_shape=jax.ShapeDtypeStruct((m, n), out_dtype),
        scratch_shapes=[pltpu.VMEM((tm, tn), F32)] if nk > 1 else [],
        compiler_params=_cparams(("parallel", "parallel", "arbitrary")),
        name=name,
    )(*args)


def _out_proj_kernel(a0, a1, a2, a3, w_ref, res_ref, o_ref):
    acc = res_ref[...]
    for g, a_ref in enumerate((a0, a1, a2, a3)):
        acc = acc + _dot(a_ref[...], w_ref[g * GROUP_WIDTH:(g + 1) * GROUP_WIDTH, :])
    o_ref[...] = acc


def _out_proj(mixes, w, layer, res, tm=1024, tn=512):
    m = res.shape[0]
    n = w.shape[-1]
    a_spec = pl.BlockSpec((tm, GROUP_WIDTH), lambda i, j: (i, 0))
    return pl.pallas_call(
        _out_proj_kernel,
        grid=(m // tm, n // tn),
        in_specs=[a_spec, a_spec, a_spec, a_spec,
                  _w_spec(w, layer, (w.shape[-2], tn), lambda i, j: (0, j)),
                  pl.BlockSpec((tm, tn), lambda i, j: (i, j))],
        out_specs=pl.BlockSpec((tm, tn), lambda i, j: (i, j)),
        out_shape=jax.ShapeDtypeStruct((m, n), F32),
        compiler_params=_cparams(("parallel", "parallel")),
        name="out_proj",
    )(*mixes, w, res)


def _rope_kernel(pos_ref, tab_ref, cos_ref, sin_ref):
    ang = pos_ref[...].astype(F32) * tab_ref[0:1, :]
    cos_ref[...] = jnp.cos(ang) * tab_ref[1:2, :]
    sin_ref[...] = jnp.sin(ang) * tab_ref[2:3, :]


def _rope_tables(pos_col, tm=1024):
    m = pos_col.shape[0]
    half = MLA_ROPE // 2
    inv = ROPE_THETA ** (-jnp.arange(half, dtype=F32) * 2.0 / MLA_ROPE)
    zeros = jnp.zeros((LANES - MLA_ROPE,), F32)
    ones = jnp.ones((half,), F32)
    tab = jnp.stack([jnp.concatenate([inv, inv, zeros]),
                     jnp.concatenate([ones, ones, zeros]),
                     jnp.concatenate([-ones, ones, zeros])]
                    + [jnp.zeros((LANES,), F32)] * 5)
    spec = pl.BlockSpec((tm, LANES), lambda i: (i, 0))
    return pl.pallas_call(
        _rope_kernel,
        grid=(m // tm,),
        in_specs=[pl.BlockSpec((tm, 1), lambda i: (i, 0)),
                  pl.BlockSpec((8, LANES), lambda i: (0, 0))],
        out_specs=[spec, spec],
        out_shape=[jax.ShapeDtypeStruct((m, LANES), F32)] * 2,
        compiler_params=_cparams(("parallel",)),
        name="rope_tables",
    )(pos_col, tab)


def _mla_q_kernel(cq_ref, g_ref, w_ref, cos_ref, sin_ref, o_ref):
    n = _rms(cq_ref[...].astype(F32), g_ref[...]).astype(BF16)
    t = _dot(n, w_ref[...])
    o_ref[:, :GROUP_WIDTH] = t[:, :GROUP_WIDTH].astype(o_ref.dtype)
    cos, sin = cos_ref[...], sin_ref[...]
    for h in range(GROUP_HEADS):
        lo = GROUP_WIDTH + h * LANES
        pe = t[:, lo:lo + LANES] * cos + t[:, lo + GROUP_WIDTH:lo + GROUP_WIDTH + LANES] * sin
        o_ref[:, lo:lo + LANES] = pe.astype(o_ref.dtype)


def _mla_q(proj, g, w, cos, sin, tm=512):
    m = proj.shape[0]
    cq_block = PROJ_CQ0 // MLA_Q_RANK
    return pl.pallas_call(
        _mla_q_kernel,
        grid=(m // tm,),
        in_specs=[pl.BlockSpec((tm, MLA_Q_RANK), lambda i: (i, cq_block)),
                  pl.BlockSpec((1, MLA_Q_RANK), lambda i: (0, 0)),
                  pl.BlockSpec(w.shape, lambda i: (0, 0)),
                  pl.BlockSpec((tm, LANES), lambda i: (i, 0)),
                  pl.BlockSpec((tm, LANES), lambda i: (i, 0))],
        out_specs=pl.BlockSpec((tm, 2 * GROUP_WIDTH), lambda i: (i, 0)),
        out_shape=jax.ShapeDtypeStruct((m, 2 * GROUP_WIDTH), BF16),
        compiler_params=_cparams(("parallel",)),
        name="mla_q",
    )(proj, g.reshape(1, -1).astype(F32), w, cos, sin)


def _mla_kv_kernel(ckv_ref, g_ref, w_ref, kr_ref, cos_ref, sin_ref, kv_ref, kpe_ref):
    n = _rms(ckv_ref[...].astype(F32), g_ref[...]).astype(BF16)
    kv_ref[...] = _dot(n, w_ref[...]).astype(kv_ref.dtype)
    kr = kr_ref[...]
    kpe = kr[:, :LANES] * cos_ref[...] + kr[:, LANES:] * sin_ref[...]
    lane = lax.broadcasted_iota(jnp.int32, kpe.shape, 1)
    kpe = jnp.where((lane >= MLA_ROPE) & (lane < MLA_ROPE + 3), 1.0, kpe)
    kpe_ref[...] = kpe.astype(kpe_ref.dtype)


def _mla_kv(proj, g, w, kr, cos, sin, tm=512):
    m = proj.shape[0]
    ckv_block = PROJ_CKV0 // MLA_KV_RANK
    return pl.pallas_call(
        _mla_kv_kernel,
        grid=(m // tm,),
        in_specs=[pl.BlockSpec((tm, MLA_KV_RANK), lambda i: (i, ckv_block)),
                  pl.BlockSpec((1, MLA_KV_RANK), lambda i: (0, 0)),
                  pl.BlockSpec(w.shape, lambda i: (0, 0)),
                  pl.BlockSpec((tm, 2 * LANES), lambda i: (i, 0)),
                  pl.BlockSpec((tm, LANES), lambda i: (i, 0)),
                  pl.BlockSpec((tm, LANES), lambda i: (i, 0))],
        out_specs=[pl.BlockSpec((tm, 2 * GROUP_WIDTH), lambda i: (i, 0)),
                   pl.BlockSpec((tm, LANES), lambda i: (i, 0))],
        out_shape=[jax.ShapeDtypeStruct((m, 2 * GROUP_WIDTH), BF16),
                   jax.ShapeDtypeStruct((m, LANES), BF16)],
        compiler_params=_cparams(("parallel",)),
        name="mla_kv",
    )(proj, g.reshape(1, -1).astype(F32), w, kr, cos, sin)


def _split3(x):
    x1 = x.astype(BF16)
    r1 = x - x1.astype(F32)
    x2 = r1.astype(BF16)
    x3 = (r1 - x2.astype(F32)).astype(BF16)
    return x1, x2, x3


def _place(lane, cols, base, fill):
    out = fill
    for i, c in enumerate(cols):
        out = jnp.where(lane == base + i, c.astype(F32), out)
    return out


def _forget_cum_kernel(f_ref, b_ref, cum_ref, kx_ref, *, chunk):
    r = lax.broadcasted_iota(jnp.int32, (chunk, chunk), 0)
    c = lax.broadcasted_iota(jnp.int32, (chunk, chunk), 1)
    tri = jnp.where(r >= c, 1.0, 0.0).astype(BF16)
    lane = lax.broadcasted_iota(jnp.int32, (chunk, LANES), 1)
    unit_fill = jnp.where(lane < 3, 1.0, 0.0)

    def body(ci, carry):
        rows = _rows(ci, chunk)
        x = f_ref[rows, :] + b_ref[...]
        logf = (jnp.minimum(x, 0.0) - jnp.log1p(jnp.exp(-jnp.abs(x)))) * LOG2E
        parts = _split3(logf)
        cs = carry + _dot(tri, parts[0]) + _dot(tri, parts[1]) + _dot(tri, parts[2])
        cum_ref[rows, :] = cs
        for h in range(GROUP_HEADS):
            kx_ref[h, rows, :] = _place(lane, _split3(cs[:, h:h + 1]), 3, unit_fill).astype(kx_ref.dtype)
        return cs[chunk - 1:chunk, :]

    lax.fori_loop(0, f_ref.shape[0] // chunk, body, jnp.zeros((1, LANES), F32))


def _forget_cum(side, b_f, batch, seq):
    f_block = 2
    return pl.pallas_call(
        functools.partial(_forget_cum_kernel, chunk=256),
        grid=(batch,),
        in_specs=[pl.BlockSpec((seq, LANES), lambda b: (b, f_block)),
                  pl.BlockSpec((1, LANES), lambda b: (0, 0))],
        out_specs=[pl.BlockSpec((seq, LANES), lambda b: (b, 0)),
                   pl.BlockSpec((None, GROUP_HEADS, seq, LANES), lambda b: (b, 0, 0, 0))],
        out_shape=[jax.ShapeDtypeStruct((batch * seq, LANES), F32),
                   jax.ShapeDtypeStruct((batch, GROUP_HEADS, seq, LANES), BF16)],
        compiler_params=_cparams(("parallel",)),
        name="forget_cum",
    )(side, b_f)


def _tile_iota(t):
    return (lax.broadcasted_iota(jnp.int32, (t, t), 0),
            lax.broadcasted_iota(jnp.int32, (t, t), 1))


def _diag_mask(t, w):
    row = lax.broadcasted_iota(jnp.int32, (t, w), 0)
    col = lax.broadcasted_iota(jnp.int32, (t, w), 1)
    return col <= row + (w - t)


def _softmax_update(s, v, m, acc):
    m_new = jnp.maximum(m, jnp.max(s, axis=-1, keepdims=True))
    alpha = jnp.exp2(m - m_new)
    p = jnp.exp2(s - m_new).astype(BF16)
    v_aug = jnp.concatenate([v, jnp.ones_like(v)], axis=1)
    return m_new, alpha * acc + _dot(p, v_aug)


def _softmax_init(t):
    return (jnp.full((t, 1), NEG_INF, F32), jnp.zeros((t, 2 * HEAD_DIM), F32))


def _softmax_out(acc):
    return acc[:, :HEAD_DIM] / acc[:, HEAD_DIM:]


def _hs(h):
    return slice(h * HEAD_DIM, (h + 1) * HEAD_DIM)


def _rows(j, w):
    return pl.ds(pl.multiple_of(j * w, w), w)


def _causal_sweep(step, qi, init):
    carry = lax.fori_loop(0, qi, lambda j, c: step(j, c, False), init)
    return step(qi, carry, True)


def _update(s, v, state, mask, acc_ref, chain):
    if mask is not None:
        s = jnp.where(mask, s, NEG_INF)
    if acc_ref is None:
        return _softmax_update(s, v, *state)
    v_aug = jnp.concatenate([v, jnp.ones_like(v)], axis=1)
    acc_ref[chain] += _dot(jnp.exp2(s).astype(BF16), v_aug)
    return None


def _run_softmax(qi, t, n_chains, make_step, finish, acc_ref):
    acc_ref[...] = jnp.zeros_like(acc_ref)
    fast = make_step(acc_ref)

    def wide(j, carry):
        fast(j, None, False, 2 * t)
        return carry

    lax.fori_loop(0, qi // 2, wide, 0)

    @pl.when(qi % 2 == 1)
    def _():
        fast(qi // 2, None, True, 2 * t)

    @pl.when(qi % 2 == 0)
    def _():
        fast(qi, None, True, t)

    accs = [acc_ref[c] for c in range(n_chains)]
    l_min = accs[0][:, HEAD_DIM:HEAD_DIM + 1]
    for a in accs[1:]:
        l_min = jnp.minimum(l_min, a[:, HEAD_DIM:HEAD_DIM + 1])
    ok = jnp.min(l_min) >= TRUSTED_ROW_SUM

    @pl.when(ok)
    def _():
        finish([_softmax_out(a) for a in accs])

    @pl.when(jnp.logical_not(ok))
    def _():
        slow = make_step(None)
        init = tuple(_softmax_init(t) for _ in range(n_chains))
        carry = lax.fori_loop(0, qi, lambda j, c: slow(j, c, False, t), init)
        finish([_softmax_out(c[1]) for c in slow(qi, carry, True, t)])


def _key_norm_max(k_ref, h, seq, extra_ref=None):
    chunk = ATT_T

    def body(c, mx):
        kk = k_ref[_rows(c, chunk), _hs(h)].astype(F32)
        n2 = jnp.sum(kk * kk, axis=1, keepdims=True)
        if extra_ref is not None:
            e = extra_ref[_rows(c, chunk), :].astype(F32)
            n2 = n2 + jnp.sum(e * e, axis=1, keepdims=True)
        return jnp.maximum(mx, n2)

    mx = lax.fori_loop(0, seq // chunk, body, jnp.zeros((chunk, 1), F32))
    return jnp.sqrt(jnp.max(mx, axis=0, keepdims=True))


def _store_key_norms(kmax_ref, qi, norm_fn):
    @pl.when(qi == 0)
    def _():
        for h in range(kmax_ref.shape[0]):
            kmax_ref[h] = jnp.broadcast_to(norm_fn(h), kmax_ref.shape[1:])


def _row_norm(x):
    xf = x.astype(F32)
    return jnp.sqrt(jnp.sum(xf * xf, axis=1, keepdims=True))


def _fox_kernel(q_ref, k_ref, v_ref, kx_ref, cumq_ref, o_ref, kmax_ref, acc_ref, *, t, seq):
    group, qi = pl.program_id(1), pl.program_id(2)
    hps = kmax_ref.shape[0]
    _store_key_norms(kmax_ref, qi, lambda h: _key_norm_max(k_ref, h, seq))
    lane = lax.broadcasted_iota(jnp.int32, (t, LANES), 1)
    minus_fill = jnp.where((lane >= 3) & (lane < 6), -1.0, 0.0)
    cumq = cumq_ref[...]

    def q_aug(h, fast):
        q = q_ref[:, _hs(h)]
        c = jnp.sum(jnp.where(lane == group * hps + h, cumq, 0.0), axis=1, keepdims=True)
        if fast:
            c = c - _row_norm(q) * kmax_ref[h][0:1, 0:1]
        return jnp.concatenate([q, _place(lane, _split3(c), 0, minus_fill).astype(BF16)], axis=1)

    def make_step(acc_ref):
        fast = acc_ref is not None
        qs = [q_aug(h, fast) for h in range(hps)]

        def step(j, carry, masked, w):
            mask = _diag_mask(t, w) if masked else None
            out = []
            for h in range(hps):
                k = jnp.concatenate([k_ref[_rows(j, w), _hs(h)], kx_ref[h, _rows(j, w), :]], axis=1)
                out.append(_update(_dot_nt(qs[h], k), v_ref[_rows(j, w), _hs(h)], None if fast else carry[h], mask, acc_ref, h))
            return tuple(out)

        return step

    def finish(outs):
        o_ref[...] = jnp.concatenate(outs, axis=1).astype(o_ref.dtype)

    _run_softmax(qi, t, hps, make_step, finish, acc_ref)


def _mla_kernel(qn_ref, qp_ref, kn_ref, kp_ref, v_ref, o_ref, kmax_ref, acc_ref, *, t, seq):
    qi = pl.program_id(2)
    hps = kmax_ref.shape[0]
    _store_key_norms(kmax_ref, qi, lambda h: _key_norm_max(kn_ref, h, seq, kp_ref))
    lane = lax.broadcasted_iota(jnp.int32, (t, LANES), 1)

    def q_aug(h, fast):
        qn, qp = qn_ref[:, _hs(h)], qp_ref[:, _hs(h)]
        if fast:
            qn_f, qp_f = qn.astype(F32), qp.astype(F32)
            norm = jnp.sqrt(jnp.sum(qn_f * qn_f, axis=1, keepdims=True)
                            + jnp.sum(qp_f * qp_f, axis=1, keepdims=True))
            qp = _place(lane, _split3(-norm * kmax_ref[h][0:1, 0:1]), MLA_ROPE, qp_f).astype(BF16)
        return jnp.concatenate([qn, qp], axis=1)

    def make_step(acc_ref):
        fast = acc_ref is not None
        qs = [q_aug(h, fast) for h in range(hps)]

        def step(j, carry, masked, w):
            mask = _diag_mask(t, w) if masked else None
            kp = kp_ref[_rows(j, w), :]
            out = []
            for h in range(hps):
                k = jnp.concatenate([kn_ref[_rows(j, w), _hs(h)], kp], axis=1)
                out.append(_update(_dot_nt(qs[h], k), v_ref[_rows(j, w), _hs(h)], None if fast else carry[h], mask, acc_ref, h))
            return tuple(out)

        return step

    def finish(outs):
        o_ref[...] = jnp.concatenate(outs, axis=1).astype(o_ref.dtype)

    _run_softmax(qi, t, hps, make_step, finish, acc_ref)


def _diff_kernel(lq1_ref, lk1_ref, lq2_ref, lk2_ref, q_ref, k_ref, v_ref, posq_ref, posk_ref,
                 bias_ref, subg_ref, o_ref, kmax_ref, acc_ref, *, t, seq, lam_init):
    qi = pl.program_id(2)
    _store_key_norms(kmax_ref, qi, lambda h: _key_norm_max(k_ref, h, seq))
    lam = (jnp.exp(jnp.sum(lq1_ref[...] * lk1_ref[...], axis=-1, keepdims=True))
           - jnp.exp(jnp.sum(lq2_ref[...] * lk2_ref[...], axis=-1, keepdims=True)) + lam_init)
    lane = lax.broadcasted_iota(jnp.int32, (t, LANES), 1)
    zero_fill = jnp.zeros((t, LANES), F32)

    def key_units(w):
        return jnp.where(lax.broadcasted_iota(jnp.int32, (w, LANES), 1) < 3, 1.0, 0.0).astype(BF16)

    posq = posq_ref[...]
    posq_min = jnp.min(posq)
    bias_rows = [jnp.broadcast_to(bias_ref[h], (t, LANES)) for h in range(HEADS_PER_STEP)]
    last = REL_MAX_DIST - 1
    far_bias = [bias_ref[h][:, last:] for h in range(HEADS_PER_STEP)]

    def half_q(h, c):
        q = q_ref[:, _hs(h)]
        return jnp.where((lane < DIFF_QK) if c == 0 else (lane >= DIFF_QK), q, jnp.zeros_like(q))

    def make_step(acc_ref):
        fast = acc_ref is not None
        q_near, q_far = [], []
        for h in range(HEADS_PER_STEP):
            for c in range(2):
                qz = half_q(h, c)
                if fast:
                    bound = (_row_norm(qz) * kmax_ref[h][0:1, 0:1]
                             + jnp.max(bias_ref[h], axis=1, keepdims=True))
                    q_near.append(jnp.concatenate(
                        [qz, _place(lane, _split3(-bound), 0, zero_fill).astype(BF16)], axis=1))
                    q_far.append(jnp.concatenate(
                        [qz, _place(lane, _split3(far_bias[h] - bound), 0, zero_fill).astype(BF16)], axis=1))
                else:
                    q_near.append(qz)
                    q_far.append(qz)

        def chains(j, carry, masked, w, gathered):
            mask = _diag_mask(t, w) if masked else None
            out = []
            for h in range(HEADS_PER_STEP):
                k = k_ref[_rows(j, w), _hs(h)]
                if fast:
                    k = jnp.concatenate([k, key_units(w)], axis=1)
                v = v_ref[_rows(j, w), _hs(h)]
                for c in range(2):
                    i = 2 * h + c
                    if gathered is not None:
                        s = _dot_nt(q_near[i], k) + gathered[h]
                    elif fast:
                        s = _dot_nt(q_far[i], k)
                    else:
                        s = _dot_nt(q_far[i], k) + far_bias[h]
                    out.append(_update(s, v, None if fast else carry[i], mask, acc_ref, i))
            return tuple(out)

        def step(j, carry, masked, w):
            posk = posk_ref[:, _rows(j, w)]

            def near(c):
                dist = jnp.clip(posq - posk, 0, last)
                gathered = [jnp.concatenate(
                    [jnp.take_along_axis(bias_rows[h], dist[:, cc * LANES:(cc + 1) * LANES], axis=1,
                                         mode="promise_in_bounds")
                     for cc in range(w // LANES)], axis=1) for h in range(HEADS_PER_STEP)]
                return chains(j, c, masked, w, gathered)

            def far(c):
                return chains(j, c, masked, w, None)

            return lax.cond(posq_min - jnp.max(posk) < last, near, far, carry)

        return step

    def finish(outs):
        normed = []
        for h in range(HEADS_PER_STEP):
            o = outs[2 * h] - lam * outs[2 * h + 1]
            normed.append(_rms(o, subg_ref[...]) * (1.0 - lam_init))
        o_ref[...] = jnp.concatenate(normed, axis=1).astype(o_ref.dtype)

    _run_softmax(qi, t, 2 * HEADS_PER_STEP, make_step, finish, acc_ref)


def _sb_kernel(q_ref, k_ref, v_ref, o_ref, *, t):
    qi = pl.program_id(2)
    w = t
    row, col = _tile_iota(t)
    later = jnp.where(row > col, 1.0, 0.0).astype(BF16)
    qs = [q_ref[:, _hs(h)] for h in range(SB_HEADS_PER_STEP)]

    def step(j, carry, masked):
        out = []
        for h in range(SB_HEADS_PER_STEP):
            tail, acc = carry[h]
            z = _dot_nt(qs[h], k_ref[_rows(j, w), _hs(h)])
            sp = jnp.log(1.0 + jnp.exp2(-jnp.abs(z))) * LOG2E
            log_beta = jnp.minimum(z, 0.0) - sp
            log_1m = -jnp.maximum(z, 0.0) - sp
            if masked:
                log_1m = jnp.where(col < row, log_1m, 0.0)
            hi = log_1m.astype(BF16)
            lo = (log_1m - hi.astype(F32)).astype(BF16)
            both = _dot(jnp.concatenate([hi, lo], axis=0), later)
            suffix = both[:t] + both[t:]
            a = jnp.exp2(log_beta + suffix + tail)
            if masked:
                a = jnp.where(col < row, a, 0.0)
            acc = acc + _dot(a.astype(BF16), v_ref[_rows(j, w), _hs(h)])
            tail = tail + suffix[:, :1] + log_1m[:, :1]
            out.append((tail, acc))
        return tuple(out)

    def live(carry):
        worst = carry[0][0]
        for h in range(1, SB_HEADS_PER_STEP):
            worst = jnp.maximum(worst, carry[h][0])
        return (jnp.max(worst) > SB_DEAD_LOG2).astype(jnp.int32)

    init = tuple((jnp.zeros((t, 1), F32), jnp.zeros((t, HEAD_DIM), F32)) for _ in range(SB_HEADS_PER_STEP))
    carry = step(qi, init, True)

    def body(state):
        j, _, c = state
        c = step(j, c, False)
        return j - 1, live(c), c

    _, _, carry = lax.while_loop(lambda st: (st[0] >= 0) & (st[1] > 0), body,
                                 (qi - 1, live(carry), carry))
    o_ref[...] = jnp.concatenate([c[1] for c in carry], axis=1).astype(o_ref.dtype)


def _attn_call(kernel, t, batch, seq, in_arrays, in_specs, name, hps=HEADS_PER_STEP, softmax_chains=0):
    nq = seq // t
    width = hps * HEAD_DIM
    scratch = []
    if softmax_chains:
        scratch = [pltpu.VMEM((hps, 8, LANES), F32),
                   pltpu.VMEM((softmax_chains, t, 2 * HEAD_DIM), F32)]
    return pl.pallas_call(
        kernel,
        grid=(batch, GROUP_HEADS // hps, nq),
        in_specs=in_specs,
        out_specs=pl.BlockSpec((t, width), lambda b, h, i: (b * nq + i, h)),
        out_shape=jax.ShapeDtypeStruct((batch * seq, GROUP_WIDTH), BF16),
        scratch_shapes=scratch,
        compiler_params=_cparams(("arbitrary", "arbitrary", "arbitrary")),
        name=name,
    )(*in_arrays)


def _q_spec(t, seq, col0, hps=HEADS_PER_STEP):
    nq = seq // t
    return pl.BlockSpec((t, hps * HEAD_DIM), lambda b, h, i: (b * nq + i, col0 + h))


def _kv_spec(seq, col0, hps=HEADS_PER_STEP):
    return pl.BlockSpec((seq, hps * HEAD_DIM), lambda b, h, i: (b, col0 + h))


def _shared_spec(seq):
    return pl.BlockSpec((seq, HEAD_DIM), lambda b, h, i: (b, 0))


def _full_spec(shape):
    return pl.BlockSpec(shape, lambda b, h, i: (0,) * len(shape))


def _mem_kernel(x_ref, g_ref, wq_ref, k_ref, v_ref, wo_ref, o_ref):
    x = x_ref[...]
    h = _rms(x, g_ref[...]).astype(BF16)
    q = _dot(h, wq_ref[...]).astype(BF16)
    outs = []
    for hd in range(MEM_HEADS):
        sl = slice(hd * MEM_HEAD_DIM, (hd + 1) * MEM_HEAD_DIM)
        s = _dot_nt(q[:, sl], k_ref[:, sl])
        p = jnp.exp2(s - jnp.max(s, axis=-1, keepdims=True))
        o = _dot(p.astype(BF16), v_ref[:, sl]) / jnp.sum(p, axis=-1, keepdims=True)
        outs.append(o.astype(BF16))
    o_ref[...] = x + _dot(jnp.concatenate(outs, axis=-1), wo_ref[...])


def _mem_block(x, g, wq, k, v, wo, layer, batch, seq, tm=256):
    m, d = x.shape
    n_mem = k.shape[0] // batch
    tiles_per_batch = seq // tm
    width = MEM_HEADS * MEM_HEAD_DIM
    return pl.pallas_call(
        _mem_kernel,
        grid=(m // tm,),
        in_specs=[pl.BlockSpec((tm, d), lambda i: (i, 0)),
                  pl.BlockSpec((1, d), lambda i: (0, 0)),
                  _w_spec(wq, layer, (d, width), lambda i: (0, 0)),
                  pl.BlockSpec((n_mem, width), lambda i: (i // tiles_per_batch, 0)),
                  pl.BlockSpec((n_mem, width), lambda i: (i // tiles_per_batch, 0)),
                  _w_spec(wo, layer, (width, d), lambda i: (0, 0))],
        out_specs=pl.BlockSpec((tm, d), lambda i: (i, 0)),
        out_shape=jax.ShapeDtypeStruct((m, d), F32),
        compiler_params=_cparams(("parallel",)),
        name="mem_block",
    )(x, g.reshape(1, d).astype(F32), wq, k, v, wo)


def _rel_bias_by_distance(rel_table):
    d = jnp.arange(REL_MAX_DIST, dtype=jnp.int32)
    max_exact = REL_BUCKETS // 2
    nf = jnp.maximum(d, 1).astype(F32)
    large = max_exact + (jnp.log(nf / max_exact) / math.log(REL_MAX_DIST / max_exact)
                         * (REL_BUCKETS - max_exact)).astype(jnp.int32)
    large = jnp.minimum(large, REL_BUCKETS - 1)
    bucket = jnp.where(d < max_exact, d, large)
    return jnp.take(rel_table, bucket, axis=0).T.astype(F32)


def _in_proj_col_scale():
    gw = GROUP_WIDTH
    scale = np.ones((PROJ_COLS,), np.float32)
    scale[PROJ_A0:PROJ_A0 + gw] = DIFF_QK ** -0.5 * LOG2E
    scale[PROJ_C0:PROJ_C0 + gw] = HEAD_DIM ** -0.5 * LOG2E
    scale[PROJ_D0:PROJ_D0 + gw] = HEAD_DIM ** -0.5 * LOG2E
    return scale


PREP_CHUNK = 512
PREP_ALIGNED = PROJ_C0 // PREP_CHUNK
PREP_SKEW = PREP_CHUNK - (PROJ_C0 - (A_COLS + B_COLS))


def _regroup_kernel(x_ref, tail_ref, s_ref, o_ref):
    c = pl.program_id(1)

    @pl.when(c < PREP_ALIGNED)
    def _():
        o_ref[...] = (x_ref[...].T * s_ref[...]).astype(o_ref.dtype)

    @pl.when(c >= PREP_ALIGNED)
    def _():
        x = jnp.concatenate([x_ref[PREP_SKEW:, :], tail_ref[...]], axis=0)
        o_ref[...] = (x.T * s_ref[...]).astype(o_ref.dtype)


def _regroup_in_proj(w_in_t, layer, tr=2048):
    d = w_in_t.shape[2]
    tails_per_chunk = PREP_CHUNK // PREP_SKEW
    scale = jnp.asarray(_in_proj_col_scale()).reshape(1, PROJ_COLS)
    return pl.pallas_call(
        _regroup_kernel,
        grid=(d // tr, PROJ_COLS // PREP_CHUNK),
        in_specs=[
            pl.BlockSpec((None, PREP_CHUNK, tr),
                         lambda r, c: (layer, jnp.where(c < PREP_ALIGNED, c, c - 1), r)),
            pl.BlockSpec((None, PREP_SKEW, tr),
                         lambda r, c: (layer, jnp.where(c < PREP_ALIGNED, 0, c * tails_per_chunk), r)),
            pl.BlockSpec((1, PREP_CHUNK), lambda r, c: (0, c))],
        out_specs=pl.BlockSpec((tr, PREP_CHUNK), lambda r, c: (r, c)),
        out_shape=jax.ShapeDtypeStruct((d, PROJ_COLS), BF16),
        compiler_params=_cparams(("parallel", "parallel")),
        name="regroup_in_proj",
    )(w_in_t, w_in_t, scale)


SIDE_COLS = 4 * LANES


def _side_proj_kernel(h_ref, kr_ref, f_ref, o_ref):
    kr, f = kr_ref[...], f_ref[...]
    half = MLA_ROPE // 2

    def zeros(n):
        return jnp.zeros((n, kr.shape[1]), kr.dtype)

    w_t = jnp.concatenate([kr, zeros(LANES - MLA_ROPE), kr[half:], kr[:half], zeros(LANES - MLA_ROPE),
                           f, zeros(2 * LANES - GROUP_HEADS)], axis=0)
    o_ref[...] = _dot_nt(h_ref[...], w_t.astype(h_ref.dtype))


def _side_proj(h, w_in_t, layer, tm=1024):
    m, d = h.shape
    kr_row0 = A_COLS + MLA_Q_RANK + MLA_KV_RANK
    f_row0 = A_COLS + B_COLS + C_COLS + 3 * GROUP_WIDTH
    return pl.pallas_call(
        _side_proj_kernel,
        grid=(m // tm,),
        in_specs=[pl.BlockSpec((tm, d), lambda i: (i, 0)),
                  pl.BlockSpec((None, MLA_ROPE, d), lambda i: (layer, kr_row0 // MLA_ROPE, 0)),
                  pl.BlockSpec((None, GROUP_HEADS, d), lambda i: (layer, f_row0 // GROUP_HEADS, 0))],
        out_specs=pl.BlockSpec((tm, SIDE_COLS), lambda i: (i, 0)),
        out_shape=jax.ShapeDtypeStruct((m, SIDE_COLS), F32),
        compiler_params=_cparams(("parallel",)),
        name="in_proj_side",
    )(h, w_in_t, w_in_t)


def _prep_uq(w):
    qk = MLA_NOPE + MLA_ROPE
    half = MLA_ROPE // 2
    w = (w * (qk ** -0.5 * LOG2E)).reshape(w.shape[0], GROUP_HEADS, qk)
    pad = jnp.zeros((w.shape[0], GROUP_HEADS, LANES - MLA_ROPE), w.dtype)
    nope = w[:, :, :MLA_NOPE]
    r1 = w[:, :, MLA_NOPE:MLA_NOPE + half]
    r2 = w[:, :, MLA_NOPE + half:]
    rope = jnp.concatenate([r1, r2, pad], axis=-1)
    swapped = jnp.concatenate([r2, r1, pad], axis=-1)
    return jnp.concatenate([nope.reshape(w.shape[0], -1), rope.reshape(w.shape[0], -1),
                            swapped.reshape(w.shape[0], -1)], axis=1).astype(BF16)


def _prep_ukv(w):
    w = w.reshape(w.shape[0], GROUP_HEADS, MLA_NOPE + HEAD_DIM)
    return jnp.concatenate([w[:, :, :MLA_NOPE].reshape(w.shape[0], -1),
                            w[:, :, MLA_NOPE:].reshape(w.shape[0], -1)], axis=1).astype(BF16)


def kernel(x, mem, positions, attn_norm_g, w_in, w_out, rel_table, diff_lam_q1, diff_lam_k1, diff_lam_q2, diff_lam_k2, diff_sub_g, mla_q_norm_g, mla_kv_norm_g, mla_w_uq, mla_w_ukv, fox_b_f, mem_q_norm_g, mem_kv_norm_g, mem_w_q, mem_w_k, mem_w_v, mem_w_o, mlp_norm_g, w_up, w_down, final_norm_g):
    batch, seq, d = x.shape
    m = batch * seq
    gw = GROUP_WIDTH
    nh = GROUP_HEADS
    xf = x.reshape(m, d)
    mem2 = mem.reshape(batch * mem.shape[1], d)
    pos_col = positions.reshape(m, 1).astype(jnp.int32)
    pos_row = positions.reshape(batch, 1, seq).astype(jnp.int32)
    cos, sin = _rope_tables(pos_col)
    bias_by_dist = (_rel_bias_by_distance(rel_table) * LOG2E).reshape(nh, 1, LANES)
    npair = nh // HEADS_PER_STEP
    pair_cols = HEADS_PER_STEP * HEAD_DIM
    a0 = PROJ_A0 // pair_cols
    nwide = nh // WIDE_HEADS_PER_STEP
    fox0 = PROJ_D0 // (WIDE_HEADS_PER_STEP * HEAD_DIM)
    nsb = nh // SB_HEADS_PER_STEP
    sb0 = PROJ_C0 // (SB_HEADS_PER_STEP * HEAD_DIM)

    w_out_b, w_up_b, w_down_b = w_out.astype(BF16), w_up.astype(BF16), w_down.astype(BF16)
    mem_wq_b = (mem_w_q * (MEM_HEAD_DIM ** -0.5 * LOG2E)).astype(BF16)
    mem_wk_b, mem_wv_b, mem_wo_b = mem_w_k.astype(BF16), mem_w_v.astype(BF16), mem_w_o.astype(BF16)

    w_in_t = jnp.swapaxes(w_in, 1, 2)

    for l in range(DEPTH):
        w_main = _regroup_in_proj(w_in_t, l)
        h = _rmsnorm(xf, attn_norm_g[l], BF16)
        proj = _matmul(h, w_main, BF16, name="in_proj")
        side = _side_proj(h, w_in_t, l)
        b_f = jnp.concatenate([fox_b_f[l].astype(F32), jnp.zeros((LANES - nh,), F32)]).reshape(1, LANES)
        cum, fox_kx = _forget_cum(side, b_f, batch, seq)

        lam_init = 0.8 - 0.6 * math.exp(-0.3 * l)
        lam_vecs = [v[l].reshape(1, DIFF_QK).astype(F32)
                    for v in (diff_lam_q1, diff_lam_k1, diff_lam_q2, diff_lam_k2)]
        o_a = _attn_call(
            functools.partial(_diff_kernel, t=ATT_T, seq=seq, lam_init=lam_init), ATT_T, batch, seq,
            lam_vecs + [proj, proj, proj, pos_col, pos_row, bias_by_dist,
                        diff_sub_g[l].reshape(1, HEAD_DIM).astype(F32)],
            [_full_spec((1, DIFF_QK))] * 4
            + [_q_spec(ATT_T, seq, a0), _kv_spec(seq, a0 + npair), _kv_spec(seq, a0 + 2 * npair),
               pl.BlockSpec((ATT_T, 1), lambda b, h, i: (b * (seq // ATT_T) + i, 0)),
               pl.BlockSpec((None, 1, seq), lambda b, h, i: (b, 0, 0)),
               pl.BlockSpec((HEADS_PER_STEP, 1, LANES), lambda b, h, i: (h, 0, 0)),
               _full_spec((1, HEAD_DIM))],
            "diff_attn", softmax_chains=2 * HEADS_PER_STEP)

        q_b = _mla_q(proj, mla_q_norm_g[l], _prep_uq(mla_w_uq[l]), cos, sin)
        kv_b, kpe_b = _mla_kv(proj, mla_kv_norm_g[l], _prep_ukv(mla_w_ukv[l]), side, cos, sin)
        o_b = _attn_call(
            functools.partial(_mla_kernel, t=ATT_T, seq=seq), ATT_T, batch, seq,
            [q_b, q_b, kv_b, kpe_b, kv_b],
            [_q_spec(ATT_T, seq, 0, WIDE_HEADS_PER_STEP), _q_spec(ATT_T, seq, nwide, WIDE_HEADS_PER_STEP),
             _kv_spec(seq, 0, WIDE_HEADS_PER_STEP), _shared_spec(seq),
             _kv_spec(seq, nwide, WIDE_HEADS_PER_STEP)],
            "mla_attn", hps=WIDE_HEADS_PER_STEP, softmax_chains=WIDE_HEADS_PER_STEP)

        o_c = _attn_call(
            functools.partial(_sb_kernel, t=SB_T), SB_T, batch, seq,
            [proj, proj, proj],
            [_q_spec(SB_T, seq, sb0, SB_HEADS_PER_STEP), _kv_spec(seq, sb0 + nsb, SB_HEADS_PER_STEP),
             _kv_spec(seq, sb0 + 2 * nsb, SB_HEADS_PER_STEP)],
            "stickbreak_attn", hps=SB_HEADS_PER_STEP)

        o_d = _attn_call(
            functools.partial(_fox_kernel, t=ATT_T, seq=seq), ATT_T, batch, seq,
            [proj, proj, proj, fox_kx, cum],
            [_q_spec(ATT_T, seq, fox0, WIDE_HEADS_PER_STEP), _kv_spec(seq, fox0 + nwide, WIDE_HEADS_PER_STEP),
             _kv_spec(seq, fox0 + 2 * nwide, WIDE_HEADS_PER_STEP),
             pl.BlockSpec((None, WIDE_HEADS_PER_STEP, seq, LANES), lambda b, h, i: (b, h, 0, 0)),
             pl.BlockSpec((ATT_T, LANES), lambda b, h, i: (b * (seq // ATT_T) + i, 0))],
            "fox_attn", hps=WIDE_HEADS_PER_STEP, softmax_chains=WIDE_HEADS_PER_STEP)

        xf = _out_proj([o_a, o_b, o_c, o_d], w_out_b, l, xf)

        mem_n = _rmsnorm(mem2, mem_kv_norm_g[l], BF16)
        k_m = _matmul(mem_n, mem_wk_b, BF16, layer=l, name="mem_k")
        v_m = _matmul(mem_n, mem_wv_b, BF16, layer=l, name="mem_v")
        xf = _mem_block(xf, mem_q_norm_g[l], mem_wq_b, k_m, v_m, mem_wo_b, l, batch, seq)

        h = _rmsnorm(xf, mlp_norm_g[l], BF16)
        u = _matmul(h, w_up_b, BF16, layer=l, relu2=True, name="mlp_up")
        xf = _matmul(u, w_down_b, F32, layer=l, res=xf, name="mlp_down")

    return _rmsnorm(xf, final_norm_g, F32).reshape(batch, seq, d)
```

```python
import functools
import math

import numpy as np
import jax
import jax.numpy as jnp
from jax import lax
from jax.experimental import pallas as pl
from jax.experimental.pallas import tpu as pltpu

F32 = jnp.float32
BF16 = jnp.bfloat16

D_MODEL = 4096
DEPTH = 2
HEAD_DIM = 128
GROUP_HEADS = 8
GROUP_WIDTH = GROUP_HEADS * HEAD_DIM
DIFF_QK = HEAD_DIM // 2
MLA_Q_RANK = 1536
MLA_KV_RANK = 512
MLA_NOPE = 128
MLA_ROPE = 64
ROPE_THETA = 10000.0
REL_BUCKETS = 32
REL_MAX_DIST = 128
MEM_HEADS = 4
MEM_HEAD_DIM = 128
NORM_EPS = 1e-6
NEG_INF = -1e30

A_COLS = 3 * GROUP_WIDTH
B_COLS = MLA_Q_RANK + MLA_KV_RANK + MLA_ROPE
C_COLS = 3 * GROUP_WIDTH

PROJ_A0 = 0
PROJ_CQ0 = A_COLS
PROJ_CKV0 = A_COLS + MLA_Q_RANK
PROJ_C0 = 5632
PROJ_D0 = PROJ_C0 + C_COLS
PROJ_COLS = PROJ_D0 + 3 * GROUP_WIDTH

LANES = 128
VMEM_LIMIT = 48 * 1024 * 1024

LOG2E = math.log2(math.e)

ATT_T = 512
SB_T = 256
HEADS_PER_STEP = 2
WIDE_HEADS_PER_STEP = 4
SB_HEADS_PER_STEP = 4
SB_DEAD_LOG2 = -160.0
TRUSTED_ROW_SUM = 2.0 ** -60


def _cparams(sem):
    return pltpu.CompilerParams(dimension_semantics=sem, vmem_limit_bytes=VMEM_LIMIT)


def _dot(a, b):
    return jnp.dot(a, b, preferred_element_type=F32)


def _dot_nt(a, b):
    return lax.dot_general(a, b, (((1,), (1,)), ((), ())), preferred_element_type=F32)


def _rms(x, g):
    return x * lax.rsqrt(jnp.mean(x * x, axis=-1, keepdims=True) + NORM_EPS) * g


def _rmsnorm_kernel(x_ref, g_ref, o_ref):
    o_ref[...] = _rms(x_ref[...].astype(F32), g_ref[...]).astype(o_ref.dtype)


def _rmsnorm(x, g, out_dtype, tm=256):
    m, d = x.shape
    return pl.pallas_call(
        _rmsnorm_kernel,
        grid=(m // tm,),
        in_specs=[pl.BlockSpec((tm, d), lambda i: (i, 0)),
                  pl.BlockSpec((1, d), lambda i: (0, 0))],
        out_specs=pl.BlockSpec((tm, d), lambda i: (i, 0)),
        out_shape=jax.ShapeDtypeStruct((m, d), out_dtype),
        compiler_params=_cparams(("parallel",)),
        name="rmsnorm",
    )(x, g.reshape(1, d).astype(F32))


def _mm_epilogue(r, res_ref, o_ref, relu2):
    if relu2:
        r = jnp.square(jnp.maximum(r, 0.0))
    if res_ref is not None:
        r = r + res_ref[...]
    o_ref[...] = r.astype(o_ref.dtype)


def _mm_kernel(a_ref, w_ref, *rest, nk, relu2, has_res):
    res_ref = rest[0] if has_res else None
    o_ref = rest[1] if has_res else rest[0]
    if nk == 1:
        _mm_epilogue(_dot(a_ref[...], w_ref[...].astype(a_ref.dtype)), res_ref, o_ref, relu2)
        return
    acc_ref = rest[-1]
    k = pl.program_id(2)

    @pl.when(k == 0)
    def _():
        acc_ref[...] = _dot(a_ref[...], w_ref[...].astype(a_ref.dtype))

    @pl.when(k > 0)
    def _():
        acc_ref[...] += _dot(a_ref[...], w_ref[...].astype(a_ref.dtype))

    @pl.when(k == nk - 1)
    def _():
        _mm_epilogue(acc_ref[...], res_ref, o_ref, relu2)


def _w_spec(w, layer, block, index_map):
    if w.ndim == 2:
        return pl.BlockSpec(block, index_map)
    return pl.BlockSpec((None,) + block, lambda *g: (layer,) + index_map(*g))


def _matmul(a, w, out_dtype, *, layer=None, res=None, relu2=False, tm=1024, tn=512, tk=4096,
            name="matmul"):
    m, kd = a.shape
    n = w.shape[-1]
    tm, tn, tk = min(tm, m), min(tn, n), min(tk, kd)
    assert m % tm == 0 and n % tn == 0 and kd % tk == 0
    nk = kd // tk
    in_specs = [pl.BlockSpec((tm, tk), lambda i, j, k: (i, k)),
                _w_spec(w, layer, (tk, tn), lambda i, j, k: (k, j))]
    args = [a, w]
    if res is not None:
        in_specs.append(pl.BlockSpec((tm, tn), lambda i, j, k: (i, j)))
        args.append(res)
    return pl.pallas_call(
        functools.partial(_mm_kernel, nk=nk, relu2=relu2, has_res=res is not None),
        grid=(m // tm, n // tn, nk),
        in_specs=in_specs,
        out_specs=pl.BlockSpec((tm, tn), lambda i, j, k: (i, j)),
        out_shape=jax.ShapeDtypeStruct((m, n), out_dtype),
        scratch_shapes=[pltpu.VMEM((tm, tn), F32)] if nk > 1 else [],
        compiler_params=_cparams(("parallel", "parallel", "arbitrary")),
        name=name,
    )(*args)


def _out_proj_kernel(a0, a1, a2, a3, w_ref, res_ref, o_ref):
    acc = res_ref[...]
    for g, a_ref in enumerate((a0, a1, a2, a3)):
        acc = acc + _dot(a_ref[...], w_ref[g * GROUP_WIDTH:(g + 1) * GROUP_WIDTH, :])
    o_ref[...] = acc


def _out_proj(mixes, w, layer, res, tm=1024, tn=512):
    m = res.shape[0]
    n = w.shape[-1]
    a_spec = pl.BlockSpec((tm, GROUP_WIDTH), lambda i, j: (i, 0))
    return pl.pallas_call(
        _out_proj_kernel,
        grid=(m // tm, n // tn),
        in_specs=[a_spec, a_spec, a_spec, a_spec,
                  _w_spec(w, layer, (w.shape[-2], tn), lambda i, j: (0, j)),
                  pl.BlockSpec((tm, tn), lambda i, j: (i, j))],
        out_specs=pl.BlockSpec((tm, tn), lambda i, j: (i, j)),
        out_shape=jax.ShapeDtypeStruct((m, n), F32),
        compiler_params=_cparams(("parallel", "parallel")),
        name="out_proj",
    )(*mixes, w, res)


def _rope_kernel(pos_ref, tab_ref, cos_ref, sin_ref):
    ang = pos_ref[...].astype(F32) * tab_ref[0:1, :]
    cos_ref[...] = jnp.cos(ang) * tab_ref[1:2, :]
    sin_ref[...] = jnp.sin(ang) * tab_ref[2:3, :]


def _rope_tables(pos_col, tm=1024):
    m = pos_col.shape[0]
    half = MLA_ROPE // 2
    inv = ROPE_THETA ** (-jnp.arange(half, dtype=F32) * 2.0 / MLA_ROPE)
    zeros = jnp.zeros((LANES - MLA_ROPE,), F32)
    ones = jnp.ones((half,), F32)
    tab = jnp.stack([jnp.concatenate([inv, inv, zeros]),
                     jnp.concatenate([ones, ones, zeros]),
                     jnp.concatenate([-ones, ones, zeros])]
                    + [jnp.zeros((LANES,), F32)] * 5)
    spec = pl.BlockSpec((tm, LANES), lambda i: (i, 0))
    return pl.pallas_call(
        _rope_kernel,
        grid=(m // tm,),
        in_specs=[pl.BlockSpec((tm, 1), lambda i: (i, 0)),
                  pl.BlockSpec((8, LANES), lambda i: (0, 0))],
        out_specs=[spec, spec],
        out_shape=[jax.ShapeDtypeStruct((m, LANES), F32)] * 2,
        compiler_params=_cparams(("parallel",)),
        name="rope_tables",
    )(pos_col, tab)


def _mla_q_kernel(cq_ref, g_ref, w_ref, cos_ref, sin_ref, o_ref):
    n = _rms(cq_ref[...].astype(F32), g_ref[...]).astype(BF16)
    t = _dot(n, w_ref[...])
    o_ref[:, :GROUP_WIDTH] = t[:, :GROUP_WIDTH].astype(o_ref.dtype)
    cos, sin = cos_ref[...], sin_ref[...]
    for h in range(GROUP_HEADS):
        lo = GROUP_WIDTH + h * LANES
        pe = t[:, lo:lo + LANES] * cos + t[:, lo + GROUP_WIDTH:lo + GROUP_WIDTH + LANES] * sin
        o_ref[:, lo:lo + LANES] = pe.astype(o_ref.dtype)


def _mla_q(proj, g, w, cos, sin, tm=512):
    m = proj.shape[0]
    cq_block = PROJ_CQ0 // MLA_Q_RANK
    return pl.pallas_call(
        _mla_q_kernel,
        grid=(m // tm,),
        in_specs=[pl.BlockSpec((tm, MLA_Q_RANK), lambda i: (i, cq_block)),
                  pl.BlockSpec((1, MLA_Q_RANK), lambda i: (0, 0)),
                  pl.BlockSpec(w.shape, lambda i: (0, 0)),
                  pl.BlockSpec((tm, LANES), lambda i: (i, 0)),
                  pl.BlockSpec((tm, LANES), lambda i: (i, 0))],
        out_specs=pl.BlockSpec((tm, 2 * GROUP_WIDTH), lambda i: (i, 0)),
        out_shape=jax.ShapeDtypeStruct((m, 2 * GROUP_WIDTH), BF16),
        compiler_params=_cparams(("parallel",)),
        name="mla_q",
    )(proj, g.reshape(1, -1).astype(F32), w, cos, sin)


def _mla_kv_kernel(ckv_ref, g_ref, w_ref, kr_ref, cos_ref, sin_ref, kv_ref, kpe_ref):
    n = _rms(ckv_ref[...].astype(F32), g_ref[...]).astype(BF16)
    kv_ref[...] = _dot(n, w_ref[...]).astype(kv_ref.dtype)
    kr = kr_ref[...]
    kpe = kr[:, :LANES] * cos_ref[...] + kr[:, LANES:] * sin_ref[...]
    lane = lax.broadcasted_iota(jnp.int32, kpe.shape, 1)
    kpe = jnp.where((lane >= MLA_ROPE) & (lane < MLA_ROPE + 3), 1.0, kpe)
    kpe_ref[...] = kpe.astype(kpe_ref.dtype)


def _mla_kv(proj, g, w, kr, cos, sin, tm=512):
    m = proj.shape[0]
    ckv_block = PROJ_CKV0 // MLA_KV_RANK
    return pl.pallas_call(
        _mla_kv_kernel,
        grid=(m // tm,),
        in_specs=[pl.BlockSpec((tm, MLA_KV_RANK), lambda i: (i, ckv_block)),
                  pl.BlockSpec((1, MLA_KV_RANK), lambda i: (0, 0)),
                  pl.BlockSpec(w.shape, lambda i: (0, 0)),
                  pl.BlockSpec((tm, 2 * LANES), lambda i: (i, 0)),
                  pl.BlockSpec((tm, LANES), lambda i: (i, 0)),
                  pl.BlockSpec((tm, LANES), lambda i: (i, 0))],
        out_specs=[pl.BlockSpec((tm, 2 * GROUP_WIDTH), lambda i: (i, 0)),
                   pl.BlockSpec((tm, LANES), lambda i: (i, 0))],
        out_shape=[jax.ShapeDtypeStruct((m, 2 * GROUP_WIDTH), BF16),
                   jax.ShapeDtypeStruct((m, LANES), BF16)],
        compiler_params=_cparams(("parallel",)),
        name="mla_kv",
    )(proj, g.reshape(1, -1).astype(F32), w, kr, cos, sin)


def _split3(x):
    x1 = x.astype(BF16)
    r1 = x - x1.astype(F32)
    x2 = r1.astype(BF16)
    x3 = (r1 - x2.astype(F32)).astype(BF16)
    return x1, x2, x3


def _place(lane, cols, base, fill):
    out = fill
    for i, c in enumerate(cols):
        out = jnp.where(lane == base + i, c.astype(F32), out)
    return out


def _forget_cum_kernel(f_ref, b_ref, cum_ref, kx_ref, *, chunk):
    r = lax.broadcasted_iota(jnp.int32, (chunk, chunk), 0)
    c = lax.broadcasted_iota(jnp.int32, (chunk, chunk), 1)
    tri = jnp.where(r >= c, 1.0, 0.0).astype(BF16)
    lane = lax.broadcasted_iota(jnp.int32, (chunk, LANES), 1)
    unit_fill = jnp.where(lane < 3, 1.0, 0.0)

    def body(ci, carry):
        rows = _rows(ci, chunk)
        x = f_ref[rows, :] + b_ref[...]
        logf = (jnp.minimum(x, 0.0) - jnp.log1p(jnp.exp(-jnp.abs(x)))) * LOG2E
        parts = _split3(logf)
        cs = carry + _dot(tri, parts[0]) + _dot(tri, parts[1]) + _dot(tri, parts[2])
        cum_ref[rows, :] = cs
        for h in range(GROUP_HEADS):
            kx_ref[h, rows, :] = _place(lane, _split3(cs[:, h:h + 1]), 3, unit_fill).astype(kx_ref.dtype)
        return cs[chunk - 1:chunk, :]

    lax.fori_loop(0, f_ref.shape[0] // chunk, body, jnp.zeros((1, LANES), F32))


def _forget_cum(side, b_f, batch, seq):
    f_block = 2
    return pl.pallas_call(
        functools.partial(_forget_cum_kernel, chunk=256),
        grid=(batch,),
        in_specs=[pl.BlockSpec((seq, LANES), lambda b: (b, f_block)),
                  pl.BlockSpec((1, LANES), lambda b: (0, 0))],
        out_specs=[pl.BlockSpec((seq, LANES), lambda b: (b, 0)),
                   pl.BlockSpec((None, GROUP_HEADS, seq, LANES), lambda b: (b, 0, 0, 0))],
        out_shape=[jax.ShapeDtypeStruct((batch * seq, LANES), F32),
                   jax.ShapeDtypeStruct((batch, GROUP_HEADS, seq, LANES), BF16)],
        compiler_params=_cparams(("parallel",)),
        name="forget_cum",
    )(side, b_f)


def _tile_iota(t):
    return (lax.broadcasted_iota(jnp.int32, (t, t), 0),
            lax.broadcasted_iota(jnp.int32, (t, t), 1))


def _diag_mask(t, w):
    row = lax.broadcasted_iota(jnp.int32, (t, w), 0)
    col = lax.broadcasted_iota(jnp.int32, (t, w), 1)
    return col <= row + (w - t)


def _softmax_update(s, v, m, acc):
    m_new = jnp.maximum(m, jnp.max(s, axis=-1, keepdims=True))
    alpha = jnp.exp2(m - m_new)
    p = jnp.exp2(s - m_new).astype(BF16)
    v_aug = jnp.concatenate([v, jnp.ones_like(v)], axis=1)
    return m_new, alpha * acc + _dot(p, v_aug)


def _softmax_init(t):
    return (jnp.full((t, 1), NEG_INF, F32), jnp.zeros((t, 2 * HEAD_DIM), F32))


def _softmax_out(acc):
    return acc[:, :HEAD_DIM] / acc[:, HEAD_DIM:]


def _hs(h):
    return slice(h * HEAD_DIM, (h + 1) * HEAD_DIM)


def _rows(j, w):
    return pl.ds(pl.multiple_of(j * w, w), w)


def _causal_sweep(step, qi, init):
    carry = lax.fori_loop(0, qi, lambda j, c: step(j, c, False), init)
    return step(qi, carry, True)


def _update(s, v, state, mask, acc_ref, chain):
    if mask is not None:
        s = jnp.where(mask, s, NEG_INF)
    if acc_ref is None:
        return _softmax_update(s, v, *state)
    v_aug = jnp.concatenate([v, jnp.ones_like(v)], axis=1)
    acc_ref[chain] += _dot(jnp.exp2(s).astype(BF16), v_aug)
    return None


def _run_softmax(qi, t, n_chains, make_step, finish, acc_ref):
    acc_ref[...] = jnp.zeros_like(acc_ref)
    fast = make_step(acc_ref)

    def wide(j, carry):
        fast(j, None, False, 2 * t)
        return carry

    lax.fori_loop(0, qi // 2, wide, 0)

    @pl.when(qi % 2 == 1)
    def _():
        fast(qi // 2, None, True, 2 * t)

    @pl.when(qi % 2 == 0)
    def _():
        fast(qi, None, True, t)

    accs = [acc_ref[c] for c in range(n_chains)]
    l_min = accs[0][:, HEAD_DIM:HEAD_DIM + 1]
    for a in accs[1:]:
        l_min = jnp.minimum(l_min, a[:, HEAD_DIM:HEAD_DIM + 1])
    ok = jnp.min(l_min) >= TRUSTED_ROW_SUM

    @pl.when(ok)
    def _():
        finish([_softmax_out(a) for a in accs])

    @pl.when(jnp.logical_not(ok))
    def _():
        slow = make_step(None)
        init = tuple(_softmax_init(t) for _ in range(n_chains))
        carry = lax.fori_loop(0, qi, lambda j, c: slow(j, c, False, t), init)
        finish([_softmax_out(c[1]) for c in slow(qi, carry, True, t)])


def _key_norm_max(k_ref, h, seq, extra_ref=None):
    chunk = ATT_T

    def body(c, mx):
        kk = k_ref[_rows(c, chunk), _hs(h)].astype(F32)
        n2 = jnp.sum(kk * kk, axis=1, keepdims=True)
        if extra_ref is not None:
            e = extra_ref[_rows(c, chunk), :].astype(F32)
            n2 = n2 + jnp.sum(e * e, axis=1, keepdims=True)
        return jnp.maximum(mx, n2)

    mx = lax.fori_loop(0, seq // chunk, body, jnp.zeros((chunk, 1), F32))
    return jnp.sqrt(jnp.max(mx, axis=0, keepdims=True))


def _store_key_norms(kmax_ref, qi, norm_fn):
    @pl.when(qi == 0)
    def _():
        for h in range(kmax_ref.shape[0]):
            kmax_ref[h] = jnp.broadcast_to(norm_fn(h), kmax_ref.shape[1:])


def _row_norm(x):
    xf = x.astype(F32)
    return jnp.sqrt(jnp.sum(xf * xf, axis=1, keepdims=True))


def _fox_kernel(q_ref, k_ref, v_ref, kx_ref, cumq_ref, o_ref, kmax_ref, acc_ref, *, t, seq):
    group, qi = pl.program_id(1), pl.program_id(2)
    hps = kmax_ref.shape[0]
    _store_key_norms(kmax_ref, qi, lambda h: _key_norm_max(k_ref, h, seq))
    lane = lax.broadcasted_iota(jnp.int32, (t, LANES), 1)
    minus_fill = jnp.where((lane >= 3) & (lane < 6), -1.0, 0.0)
    cumq = cumq_ref[...]

    def q_aug(h, fast):
        q = q_ref[:, _hs(h)]
        c = jnp.sum(jnp.where(lane == group * hps + h, cumq, 0.0), axis=1, keepdims=True)
        if fast:
            c = c - _row_norm(q) * kmax_ref[h][0:1, 0:1]
        return jnp.concatenate([q, _place(lane, _split3(c), 0, minus_fill).astype(BF16)], axis=1)

    def make_step(acc_ref):
        fast = acc_ref is not None
        qs = [q_aug(h, fast) for h in range(hps)]

        def step(j, carry, masked, w):
            mask = _diag_mask(t, w) if masked else None
            out = []
            for h in range(hps):
                k = jnp.concatenate([k_ref[_rows(j, w), _hs(h)], kx_ref[h, _rows(j, w), :]], axis=1)
                out.append(_update(_dot_nt(qs[h], k), v_ref[_rows(j, w), _hs(h)], None if fast else carry[h], mask, acc_ref, h))
            return tuple(out)

        return step

    def finish(outs):
        o_ref[...] = jnp.concatenate(outs, axis=1).astype(o_ref.dtype)

    _run_softmax(qi, t, hps, make_step, finish, acc_ref)


def _mla_kernel(qn_ref, qp_ref, kn_ref, kp_ref, v_ref, o_ref, kmax_ref, acc_ref, *, t, seq):
    qi = pl.program_id(2)
    hps = kmax_ref.shape[0]
    _store_key_norms(kmax_ref, qi, lambda h: _key_norm_max(kn_ref, h, seq, kp_ref))
    lane = lax.broadcasted_iota(jnp.int32, (t, LANES), 1)

    def q_aug(h, fast):
        qn, qp = qn_ref[:, _hs(h)], qp_ref[:, _hs(h)]
        if fast:
            qn_f, qp_f = qn.astype(F32), qp.astype(F32)
            norm = jnp.sqrt(jnp.sum(qn_f * qn_f, axis=1, keepdims=True)
                            + jnp.sum(qp_f * qp_f, axis=1, keepdims=True))
            qp = _place(lane, _split3(-norm * kmax_ref[h][0:1, 0:1]), MLA_ROPE, qp_f).astype(BF16)
        return jnp.concatenate([qn, qp], axis=1)

    def make_step(acc_ref):
        fast = acc_ref is not None
        qs = [q_aug(h, fast) for h in range(hps)]

        def step(j, carry, masked, w):
            mask = _diag_mask(t, w) if masked else None
            kp = kp_ref[_rows(j, w), :]
            out = []
            for h in range(hps):
                k = jnp.concatenate([kn_ref[_rows(j, w), _hs(h)], kp], axis=1)
                out.append(_update(_dot_nt(qs[h], k), v_ref[_rows(j, w), _hs(h)], None if fast else carry[h], mask, acc_ref, h))
            return tuple(out)

        return step

    def finish(outs):
        o_ref[...] = jnp.concatenate(outs, axis=1).astype(o_ref.dtype)

    _run_softmax(qi, t, hps, make_step, finish, acc_ref)


def _diff_kernel(lq1_ref, lk1_ref, lq2_ref, lk2_ref, q_ref, k_ref, v_ref, posq_ref, posk_ref,
                 bias_ref, subg_ref, o_ref, kmax_ref, acc_ref, *, t, seq, lam_init):
    qi = pl.program_id(2)
    _store_key_norms(kmax_ref, qi, lambda h: _key_norm_max(k_ref, h, seq))
    lam = (jnp.exp(jnp.sum(lq1_ref[...] * lk1_ref[...], axis=-1, keepdims=True))
           - jnp.exp(jnp.sum(lq2_ref[...] * lk2_ref[...], axis=-1, keepdims=True)) + lam_init)
    lane = lax.broadcasted_iota(jnp.int32, (t, LANES), 1)
    zero_fill = jnp.zeros((t, LANES), F32)

    def key_units(w):
        return jnp.where(lax.broadcasted_iota(jnp.int32, (w, LANES), 1) < 3, 1.0, 0.0).astype(BF16)

    posq = posq_ref[...]
    posq_min = jnp.min(posq)
    bias_rows = [jnp.broadcast_to(bias_ref[h], (t, LANES)) for h in range(HEADS_PER_STEP)]
    last = REL_MAX_DIST - 1
    far_bias = [bias_ref[h][:, last:] for h in range(HEADS_PER_STEP)]

    def half_q(h, c):
        q = q_ref[:, _hs(h)]
        return jnp.where((lane < DIFF_QK) if c == 0 else (lane >= DIFF_QK), q, jnp.zeros_like(q))

    def make_step(acc_ref):
        fast = acc_ref is not None
        q_near, q_far = [], []
        for h in range(HEADS_PER_STEP):
            for c in range(2):
                qz = half_q(h, c)
                if fast:
                    bound = (_row_norm(qz) * kmax_ref[h][0:1, 0:1]
                             + jnp.max(bias_ref[h], axis=1, keepdims=True))
                    q_near.append(jnp.concatenate(
                        [qz, _place(lane, _split3(-bound), 0, zero_fill).astype(BF16)], axis=1))
                    q_far.append(jnp.concatenate(
                        [qz, _place(lane, _split3(far_bias[h] - bound), 0, zero_fill).astype(BF16)], axis=1))
                else:
                    q_near.append(qz)
                    q_far.append(qz)

        def chains(j, carry, masked, w, gathered):
            mask = _diag_mask(t, w) if masked else None
            out = []
            for h in range(HEADS_PER_STEP):
                k = k_ref[_rows(j, w), _hs(h)]
                if fast:
                    k = jnp.concatenate([k, key_units(w)], axis=1)
                v = v_ref[_rows(j, w), _hs(h)]
                for c in range(2):
                    i = 2 * h + c
                    if gathered is not None:
                        s = _dot_nt(q_near[i], k) + gathered[h]
                    elif fast:
                        s = _dot_nt(q_far[i], k)
                    else:
                        s = _dot_nt(q_far[i], k) + far_bias[h]
                    out.append(_update(s, v, None if fast else carry[i], mask, acc_ref, i))
            return tuple(out)

        def step(j, carry, masked, w):
            posk = posk_ref[:, _rows(j, w)]

            def near(c):
                dist = jnp.clip(posq - posk, 0, last)
                gathered = [jnp.concatenate(
                    [jnp.take_along_axis(bias_rows[h], dist[:, cc * LANES:(cc + 1) * LANES], axis=1,
                                         mode="promise_in_bounds")
                     for cc in range(w // LANES)], axis=1) for h in range(HEADS_PER_STEP)]
                return chains(j, c, masked, w, gathered)

            def far(c):
                return chains(j, c, masked, w, None)

            return lax.cond(posq_min - jnp.max(posk) < last, near, far, carry)

        return step

    def finish(outs):
        normed = []
        for h in range(HEADS_PER_STEP):
            o = outs[2 * h] - lam * outs[2 * h + 1]
            normed.append(_rms(o, subg_ref[...]) * (1.0 - lam_init))
        o_ref[...] = jnp.concatenate(normed, axis=1).astype(o_ref.dtype)

    _run_softmax(qi, t, 2 * HEADS_PER_STEP, make_step, finish, acc_ref)


def _sb_kernel(q_ref, k_ref, v_ref, o_ref, *, t):
    qi = pl.program_id(2)
    w = t
    row, col = _tile_iota(t)
    later = jnp.where(row > col, 1.0, 0.0).astype(BF16)
    qs = [q_ref[:, _hs(h)] for h in range(SB_HEADS_PER_STEP)]

    def step(j, carry, masked):
        out = []
        for h in range(SB_HEADS_PER_STEP):
            tail, acc = carry[h]
            z = _dot_nt(qs[h], k_ref[_rows(j, w), _hs(h)])
            sp = jnp.log(1.0 + jnp.exp2(-jnp.abs(z))) * LOG2E
            log_beta = jnp.minimum(z, 0.0) - sp
            log_1m = -jnp.maximum(z, 0.0) - sp
            if masked:
                log_1m = jnp.where(col < row, log_1m, 0.0)
            hi = log_1m.astype(BF16)
            lo = (log_1m - hi.astype(F32)).astype(BF16)
            both = _dot(jnp.concatenate([hi, lo], axis=0), later)
            suffix = both[:t] + both[t:]
            a = jnp.exp2(log_beta + suffix + tail)
            if masked:
                a = jnp.where(col < row, a, 0.0)
            acc = acc + _dot(a.astype(BF16), v_ref[_rows(j, w), _hs(h)])
            tail = tail + suffix[:, :1] + log_1m[:, :1]
            out.append((tail, acc))
        return tuple(out)

    def live(carry):
        worst = carry[0][0]
        for h in range(1, SB_HEADS_PER_STEP):
            worst = jnp.maximum(worst, carry[h][0])
        return (jnp.max(worst) > SB_DEAD_LOG2).astype(jnp.int32)

    init = tuple((jnp.zeros((t, 1), F32), jnp.zeros((t, HEAD_DIM), F32)) for _ in range(SB_HEADS_PER_STEP))
    carry = step(qi, init, True)

    def body(state):
        j, _, c = state
        c = step(j, c, False)
        return j - 1, live(c), c

    _, _, carry = lax.while_loop(lambda st: (st[0] >= 0) & (st[1] > 0), body,
                                 (qi - 1, live(carry), carry))
    o_ref[...] = jnp.concatenate([c[1] for c in carry], axis=1).astype(o_ref.dtype)


def _attn_call(kernel, t, batch, seq, in_arrays, in_specs, name, hps=HEADS_PER_STEP, softmax_chains=0):
    nq = seq // t
    width = hps * HEAD_DIM
    scratch = []
    if softmax_chains:
        scratch = [pltpu.VMEM((hps, 8, LANES), F32),
                   pltpu.VMEM((softmax_chains, t, 2 * HEAD_DIM), F32)]
    return pl.pallas_call(
        kernel,
        grid=(batch, GROUP_HEADS // hps, nq),
        in_specs=in_specs,
        out_specs=pl.BlockSpec((t, width), lambda b, h, i: (b * nq + i, h)),
        out_shape=jax.ShapeDtypeStruct((batch * seq, GROUP_WIDTH), BF16),
        scratch_shapes=scratch,
        compiler_params=_cparams(("arbitrary", "arbitrary", "arbitrary")),
        name=name,
    )(*in_arrays)


def _q_spec(t, seq, col0, hps=HEADS_PER_STEP):
    nq = seq // t
    return pl.BlockSpec((t, hps * HEAD_DIM), lambda b, h, i: (b * nq + i, col0 + h))


def _kv_spec(seq, col0, hps=HEADS_PER_STEP):
    return pl.BlockSpec((seq, hps * HEAD_DIM), lambda b, h, i: (b, col0 + h))


def _shared_spec(seq):
    return pl.BlockSpec((seq, HEAD_DIM), lambda b, h, i: (b, 0))


def _full_spec(shape):
    return pl.BlockSpec(shape, lambda b, h, i: (0,) * len(shape))


def _mem_kernel(x_ref, g_ref, wq_ref, k_ref, v_ref, wo_ref, o_ref):
    x = x_ref[...]
    h = _rms(x, g_ref[...]).astype(BF16)
    q = _dot(h, wq_ref[...]).astype(BF16)
    outs = []
    for hd in range(MEM_HEADS):
        sl = slice(hd * MEM_HEAD_DIM, (hd + 1) * MEM_HEAD_DIM)
        s = _dot_nt(q[:, sl], k_ref[:, sl])
        p = jnp.exp2(s - jnp.max(s, axis=-1, keepdims=True))
        o = _dot(p.astype(BF16), v_ref[:, sl]) / jnp.sum(p, axis=-1, keepdims=True)
        outs.append(o.astype(BF16))
    o_ref[...] = x + _dot(jnp.concatenate(outs, axis=-1), wo_ref[...])


def _mem_block(x, g, wq, k, v, wo, layer, batch, seq, tm=256):
    m, d = x.shape
    n_mem = k.shape[0] // batch
    tiles_per_batch = seq // tm
    width = MEM_HEADS * MEM_HEAD_DIM
    return pl.pallas_call(
        _mem_kernel,
        grid=(m // tm,),
        in_specs=[pl.BlockSpec((tm, d), lambda i: (i, 0)),
                  pl.BlockSpec((1, d), lambda i: (0, 0)),
                  _w_spec(wq, layer, (d, width), lambda i: (0, 0)),
                  pl.BlockSpec((n_mem, width), lambda i: (i // tiles_per_batch, 0)),
                  pl.BlockSpec((n_mem, width), lambda i: (i // tiles_per_batch, 0)),
                  _w_spec(wo, layer, (width, d), lambda i: (0, 0))],
        out_specs=pl.BlockSpec((tm, d), lambda i: (i, 0)),
        out_shape=jax.ShapeDtypeStruct((m, d), F32),
        compiler_params=_cparams(("parallel",)),
        name="mem_block",
    )(x, g.reshape(1, d).astype(F32), wq, k, v, wo)


def _rel_bias_by_distance(rel_table):
    d = jnp.arange(REL_MAX_DIST, dtype=jnp.int32)
    max_exact = REL_BUCKETS // 2
    nf = jnp.maximum(d, 1).astype(F32)
    large = max_exact + (jnp.log(nf / max_exact) / math.log(REL_MAX_DIST / max_exact)
                         * (REL_BUCKETS - max_exact)).astype(jnp.int32)
    large = jnp.minimum(large, REL_BUCKETS - 1)
    bucket = jnp.where(d < max_exact, d, large)
    return jnp.take(rel_table, bucket, axis=0).T.astype(F32)


def _in_proj_col_scale():
    gw = GROUP_WIDTH
    scale = np.ones((PROJ_COLS,), np.float32)
    scale[PROJ_A0:PROJ_A0 + gw] = DIFF_QK ** -0.5 * LOG2E
    scale[PROJ_C0:PROJ_C0 + gw] = HEAD_DIM ** -0.5 * LOG2E
    scale[PROJ_D0:PROJ_D0 + gw] = HEAD_DIM ** -0.5 * LOG2E
    return scale


PREP_CHUNK = 512
PREP_ALIGNED = PROJ_C0 // PREP_CHUNK
PREP_SKEW = PREP_CHUNK - (PROJ_C0 - (A_COLS + B_COLS))


def _regroup_kernel(x_ref, tail_ref, s_ref, o_ref):
    c = pl.program_id(1)

    @pl.when(c < PREP_ALIGNED)
    def _():
        o_ref[...] = (x_ref[...].T * s_ref[...]).astype(o_ref.dtype)

    @pl.when(c >= PREP_ALIGNED)
    def _():
        x = jnp.concatenate([x_ref[PREP_SKEW:, :], tail_ref[...]], axis=0)
        o_ref[...] = (x.T * s_ref[...]).astype(o_ref.dtype)


def _regroup_in_proj(w_in_t, layer, tr=2048):
    d = w_in_t.shape[2]
    tails_per_chunk = PREP_CHUNK // PREP_SKEW
    scale = jnp.asarray(_in_proj_col_scale()).reshape(1, PROJ_COLS)
    return pl.pallas_call(
        _regroup_kernel,
        grid=(d // tr, PROJ_COLS // PREP_CHUNK),
        in_specs=[
            pl.BlockSpec((None, PREP_CHUNK, tr),
                         lambda r, c: (layer, jnp.where(c < PREP_ALIGNED, c, c - 1), r)),
            pl.BlockSpec((None, PREP_SKEW, tr),
                         lambda r, c: (layer, jnp.where(c < PREP_ALIGNED, 0, c * tails_per_chunk), r)),
            pl.BlockSpec((1, PREP_CHUNK), lambda r, c: (0, c))],
        out_specs=pl.BlockSpec((tr, PREP_CHUNK), lambda r, c: (r, c)),
        out_shape=jax.ShapeDtypeStruct((d, PROJ_COLS), BF16),
        compiler_params=_cparams(("parallel", "parallel")),
        name="regroup_in_proj",
    )(w_in_t, w_in_t, scale)


SIDE_COLS = 4 * LANES


def _side_proj_kernel(h_ref, kr_ref, f_ref, o_ref):
    kr, f = kr_ref[...], f_ref[...]
    half = MLA_ROPE // 2

    def zeros(n):
        return jnp.zeros((n, kr.shape[1]), kr.dtype)

    w_t = jnp.concatenate([kr, zeros(LANES - MLA_ROPE), kr[half:], kr[:half], zeros(LANES - MLA_ROPE),
                           f, zeros(2 * LANES - GROUP_HEADS)], axis=0)
    o_ref[...] = _dot_nt(h_ref[...], w_t.astype(h_ref.dtype))


def _side_proj(h, w_in_t, layer, tm=1024):
    m, d = h.shape
    kr_row0 = A_COLS + MLA_Q_RANK + MLA_KV_RANK
    f_row0 = A_COLS + B_COLS + C_COLS + 3 * GROUP_WIDTH
    return pl.pallas_call(
        _side_proj_kernel,
        grid=(m // tm,),
        in_specs=[pl.BlockSpec((tm, d), lambda i: (i, 0)),
                  pl.BlockSpec((None, MLA_ROPE, d), lambda i: (layer, kr_row0 // MLA_ROPE, 0)),
                  pl.BlockSpec((None, GROUP_HEADS, d), lambda i: (layer, f_row0 // GROUP_HEADS, 0))],
        out_specs=pl.BlockSpec((tm, SIDE_COLS), lambda i: (i, 0)),
        out_shape=jax.ShapeDtypeStruct((m, SIDE_COLS), F32),
        compiler_params=_cparams(("parallel",)),
        name="in_proj_side",
    )(h, w_in_t, w_in_t)


def _prep_uq(w):
    qk = MLA_NOPE + MLA_ROPE
    half = MLA_ROPE // 2
    w = (w * (qk ** -0.5 * LOG2E)).reshape(w.shape[0], GROUP_HEADS, qk)
    pad = jnp.zeros((w.shape[0], GROUP_HEADS, LANES - MLA_ROPE), w.dtype)
    nope = w[:, :, :MLA_NOPE]
    r1 = w[:, :, MLA_NOPE:MLA_NOPE + half]
    r2 = w[:, :, MLA_NOPE + half:]
    rope = jnp.concatenate([r1, r2, pad], axis=-1)
    swapped = jnp.concatenate([r2, r1, pad], axis=-1)
    return jnp.concatenate([nope.reshape(w.shape[0], -1), rope.reshape(w.shape[0], -1),
                            swapped.reshape(w.shape[0], -1)], axis=1).astype(BF16)


def _prep_ukv(w):
    w = w.reshape(w.shape[0], GROUP_HEADS, MLA_NOPE + HEAD_DIM)
    return jnp.concatenate([w[:, :, :MLA_NOPE].reshape(w.shape[0], -1),
                            w[:, :, MLA_NOPE:].reshape(w.shape[0], -1)], axis=1).astype(BF16)


def kernel(x, mem, positions, attn_norm_g, w_in, w_out, rel_table, diff_lam_q1, diff_lam_k1, diff_lam_q2, diff_lam_k2, diff_sub_g, mla_q_norm_g, mla_kv_norm_g, mla_w_uq, mla_w_ukv, fox_b_f, mem_q_norm_g, mem_kv_norm_g, mem_w_q, mem_w_k, mem_w_v, mem_w_o, mlp_norm_g, w_up, w_down, final_norm_g):
    batch, seq, d = x.shape
    m = batch * seq
    gw = GROUP_WIDTH
    nh = GROUP_HEADS
    xf = x.reshape(m, d)
    mem2 = mem.reshape(batch * mem.shape[1], d)
    pos_col = positions.reshape(m, 1).astype(jnp.int32)
    pos_row = positions.reshape(batch, 1, seq).astype(jnp.int32)
    cos, sin = _rope_tables(pos_col)
    bias_by_dist = (_rel_bias_by_distance(rel_table) * LOG2E).reshape(nh, 1, LANES)
    npair = nh // HEADS_PER_STEP
    pair_cols = HEADS_PER_STEP * HEAD_DIM
    a0 = PROJ_A0 // pair_cols
    nwide = nh // WIDE_HEADS_PER_STEP
    fox0 = PROJ_D0 // (WIDE_HEADS_PER_STEP * HEAD_DIM)
    nsb = nh // SB_HEADS_PER_STEP
    sb0 = PROJ_C0 // (SB_HEADS_PER_STEP * HEAD_DIM)

    w_out_b, w_up_b, w_down_b = w_out.astype(BF16), w_up.astype(BF16), w_down.astype(BF16)
    mem_wq_b = (mem_w_q * (MEM_HEAD_DIM ** -0.5 * LOG2E)).astype(BF16)
    mem_wk_b, mem_wv_b, mem_wo_b = mem_w_k.astype(BF16), mem_w_v.astype(BF16), mem_w_o.astype(BF16)

    w_in_t = jnp.swapaxes(w_in, 1, 2)

    for l in range(DEPTH):
        w_main = _regroup_in_proj(w_in_t, l)
        h = _rmsnorm(xf, attn_norm_g[l], BF16)
        proj = _matmul(h, w_main, BF16, name="in_proj")
        side = _side_proj(h, w_in_t, l)
        b_f = jnp.concatenate([fox_b_f[l].astype(F32), jnp.zeros((LANES - nh,), F32)]).reshape(1, LANES)
        cum, fox_kx = _forget_cum(side, b_f, batch, seq)

        lam_init = 0.8 - 0.6 * math.exp(-0.3 * l)
        lam_vecs = [v[l].reshape(1, DIFF_QK).astype(F32)
                    for v in (diff_lam_q1, diff_lam_k1, diff_lam_q2, diff_lam_k2)]
        o_a = _attn_call(
            functools.partial(_diff_kernel, t=ATT_T, seq=seq, lam_init=lam_init), ATT_T, batch, seq,
            lam_vecs + [proj, proj, proj, pos_col, pos_row, bias_by_dist,
                        diff_sub_g[l].reshape(1, HEAD_DIM).astype(F32)],
            [_full_spec((1, DIFF_QK))] * 4
            + [_q_spec(ATT_T, seq, a0), _kv_spec(seq, a0 + npair), _kv_spec(seq, a0 + 2 * npair),
               pl.BlockSpec((ATT_T, 1), lambda b, h, i: (b * (seq // ATT_T) + i, 0)),
               pl.BlockSpec((None, 1, seq), lambda b, h, i: (b, 0, 0)),
               pl.BlockSpec((HEADS_PER_STEP, 1, LANES), lambda b, h, i: (h, 0, 0)),
               _full_spec((1, HEAD_DIM))],
            "diff_attn", softmax_chains=2 * HEADS_PER_STEP)

        q_b = _mla_q(proj, mla_q_norm_g[l], _prep_uq(mla_w_uq[l]), cos, sin)
        kv_b, kpe_b = _mla_kv(proj, mla_kv_norm_g[l], _prep_ukv(mla_w_ukv[l]), side, cos, sin)
        o_b = _attn_call(
            functools.partial(_mla_kernel, t=ATT_T, seq=seq), ATT_T, batch, seq,
            [q_b, q_b, kv_b, kpe_b, kv_b],
            [_q_spec(ATT_T, seq, 0, WIDE_HEADS_PER_STEP), _q_spec(ATT_T, seq, nwide, WIDE_HEADS_PER_STEP),
             _kv_spec(seq, 0, WIDE_HEADS_PER_STEP), _shared_spec(seq),
             _kv_spec(seq, nwide, WIDE_HEADS_PER_STEP)],
            "mla_attn", hps=WIDE_HEADS_PER_STEP, softmax_chains=WIDE_HEADS_PER_STEP)

        o_c = _attn_call(
            functools.partial(_sb_kernel, t=SB_T), SB_T, batch, seq,
            [proj, proj, proj],
            [_q_spec(SB_T, seq, sb0, SB_HEADS_PER_STEP), _kv_spec(seq, sb0 + nsb, SB_HEADS_PER_STEP),
             _kv_spec(seq, sb0 + 2 * nsb, SB_HEADS_PER_STEP)],
            "stickbreak_attn", hps=SB_HEADS_PER_STEP)

        o_d = _attn_call(
            functools.partial(_fox_kernel, t=ATT_T, seq=seq), ATT_T, batch, seq,
            [proj, proj, proj, fox_kx, cum],
            [_q_spec(ATT_T, seq, fox0, WIDE_HEADS_PER_STEP), _kv_spec(seq, fox0 + nwide, WIDE_HEADS_PER_STEP),
             _kv_spec(seq, fox0 + 2 * nwide, WIDE_HEADS_PER_STEP),
             pl.BlockSpec((None, WIDE_HEADS_PER_STEP, seq, LANES), lambda b, h, i: (b, h, 0, 0)),
             pl.BlockSpec((ATT_T, LANES), lambda b, h, i: (b * (seq // ATT_T) + i, 0))],
            "fox_attn", hps=WIDE_HEADS_PER_STEP, softmax_chains=WIDE_HEADS_PER_STEP)

        xf = _out_proj([o_a, o_b, o_c, o_d], w_out_b, l, xf)

        mem_n = _rmsnorm(mem2, mem_kv_norm_g[l], BF16)
        k_m = _matmul(mem_n, mem_wk_b, BF16, layer=l, name="mem_k")
        v_m = _matmul(mem_n, mem_wv_b, BF16, layer=l, name="mem_v")
        xf = _mem_block(xf, mem_q_norm_g[l], mem_wq_b, k_m, v_m, mem_wo_b, l, batch, seq)

        h = _rmsnorm(xf, mlp_norm_g[l], BF16)
        u = _matmul(h, w_up_b, BF16, layer=l, relu2=True, name="mlp_up")
        xf = _matmul(u, w_down_b, F32, layer=l, res=xf, name="mlp_down")

    return _rmsnorm(xf, final_norm_g, F32).reshape(batch, seq, d)
```

```python
import functools
import math

import numpy as np
import jax
import jax.numpy as jnp
from jax import lax
from jax.experimental import pallas as pl
from jax.experimental.pallas import tpu as pltpu

F32 = jnp.float32
BF16 = jnp.bfloat16

D_MODEL = 4096
DEPTH = 2
HEAD_DIM = 128
GROUP_HEADS = 8
GROUP_WIDTH = GROUP_HEADS * HEAD_DIM
DIFF_QK = HEAD_DIM // 2
MLA_Q_RANK = 1536
MLA_KV_RANK = 512
MLA_NOPE = 128
MLA_ROPE = 64
ROPE_THETA = 10000.0
REL_BUCKETS = 32
REL_MAX_DIST = 128
MEM_HEADS = 4
MEM_HEAD_DIM = 128
NORM_EPS = 1e-6
NEG_INF = -1e30

A_COLS = 3 * GROUP_WIDTH
B_COLS = MLA_Q_RANK + MLA_KV_RANK + MLA_ROPE
C_COLS = 3 * GROUP_WIDTH

PROJ_A0 = 0
PROJ_CQ0 = A_COLS
PROJ_CKV0 = A_COLS + MLA_Q_RANK
PROJ_C0 = 5632
PROJ_D0 = PROJ_C0 + C_COLS
PROJ_COLS = PROJ_D0 + 3 * GROUP_WIDTH

LANES = 128
VMEM_LIMIT = 48 * 1024 * 1024

LOG2E = math.log2(math.e)

ATT_T = 512
SB_T = 256
HEADS_PER_STEP = 2
WIDE_HEADS_PER_STEP = 4
SB_HEADS_PER_STEP = 4
DEAD_LOG2 = -160.0
TRUSTED_ROW_SUM = 2.0 ** -60


def _cparams(sem):
    return pltpu.CompilerParams(dimension_semantics=sem, vmem_limit_bytes=VMEM_LIMIT)


def _dot(a, b):
    return jnp.dot(a, b, preferred_element_type=F32)


def _dot_nt(a, b):
    return lax.dot_general(a, b, (((1,), (1,)), ((), ())), preferred_element_type=F32)


def _rms(x, g):
    return x * lax.rsqrt(jnp.mean(x * x, axis=-1, keepdims=True) + NORM_EPS) * g


def _rmsnorm_kernel(x_ref, g_ref, o_ref):
    o_ref[...] = _rms(x_ref[...].astype(F32), g_ref[...]).astype(o_ref.dtype)


def _rmsnorm(x, g, out_dtype, tm=256):
    m, d = x.shape
    return pl.pallas_call(
        _rmsnorm_kernel,
        grid=(m // tm,),
        in_specs=[pl.BlockSpec((tm, d), lambda i: (i, 0)),
                  pl.BlockSpec((1, d), lambda i: (0, 0))],
        out_specs=pl.BlockSpec((tm, d), lambda i: (i, 0)),
        out_shape=jax.ShapeDtypeStruct((m, d), out_dtype),
        compiler_params=_cparams(("parallel",)),
        name="rmsnorm",
    )(x, g.reshape(1, d).astype(F32))


def _mm_epilogue(r, res_ref, o_ref, relu2):
    if relu2:
        r = jnp.square(jnp.maximum(r, 0.0))
    if res_ref is not None:
        r = r + res_ref[...]
    o_ref[...] = r.astype(o_ref.dtype)


def _mm_kernel(a_ref, w_ref, *rest, nk, relu2, has_res):
    res_ref = rest[0] if has_res else None
    o_ref = rest[1] if has_res else rest[0]
    if nk == 1:
        _mm_epilogue(_dot(a_ref[...], w_ref[...].astype(a_ref.dtype)), res_ref, o_ref, relu2)
        return
    acc_ref = rest[-1]
    k = pl.program_id(2)

    @pl.when(k == 0)
    def _():
        acc_ref[...] = _dot(a_ref[...], w_ref[...].astype(a_ref.dtype))

    @pl.when(k > 0)
    def _():
        acc_ref[...] += _dot(a_ref[...], w_ref[...].astype(a_ref.dtype))

    @pl.when(k == nk - 1)
    def _():
        _mm_epilogue(acc_ref[...], res_ref, o_ref, relu2)


def _w_spec(w, layer, block, index_map):
    if w.ndim == 2:
        return pl.BlockSpec(block, index_map)
    return pl.BlockSpec((None,) + block, lambda *g: (layer,) + index_map(*g))


def _matmul(a, w, out_dtype, *, layer=None, res=None, relu2=False, tm=1024, tn=512, tk=4096,
            name="matmul"):
    m, kd = a.shape
    n = w.shape[-1]
    tm, tn, tk = min(tm, m), min(tn, n), min(tk, kd)
    assert m % tm == 0 and n % tn == 0 and kd % tk == 0
    nk = kd // tk
    in_specs = [pl.BlockSpec((tm, tk), lambda i, j, k: (i, k)),
                _w_spec(w, layer, (tk, tn), lambda i, j, k: (k, j))]
    args = [a, w]
    if res is not None:
        in_specs.append(pl.BlockSpec((tm, tn), lambda i, j, k: (i, j)))
        args.append(res)
    return pl.pallas_call(
        functools.partial(_mm_kernel, nk=nk, relu2=relu2, has_res=res is not None),
        grid=(m // tm, n // tn, nk),
        in_specs=in_specs,
        out_specs=pl.BlockSpec((tm, tn), lambda i, j, k: (i, j)),
        out_shape=jax.ShapeDtypeStruct((m, n), out_dtype),
        scratch_shapes=[pltpu.VMEM((tm, tn), F32)] if nk > 1 else [],
        compiler_params=_cparams(("parallel", "parallel", "arbitrary")),
        name=name,
    )(*args)


def _out_proj_kernel(a0, a1, a2, a3, w_ref, res_ref, o_ref):
    acc = res_ref[...]
    for g, a_ref in enumerate((a0, a1, a2, a3)):
        acc = acc + _dot(a_ref[...], w_ref[g * GROUP_WIDTH:(g + 1) * GROUP_WIDTH, :])
    o_ref[...] = acc


def _out_proj(mixes, w, layer, res, tm=1024, tn=512):
    m = res.shape[0]
    n = w.shape[-1]
    a_spec = pl.BlockSpec((tm, GROUP_WIDTH), lambda i, j: (i, 0))
    return pl.pallas_call(
        _out_proj_kernel,
        grid=(m // tm, n // tn),
        in_specs=[a_spec, a_spec, a_spec, a_spec,
                  _w_spec(w, layer, (w.shape[-2], tn), lambda i, j: (0, j)),
                  pl.BlockSpec((tm, tn), lambda i, j: (i, j))],
        out_specs=pl.BlockSpec((tm, tn), lambda i, j: (i, j)),
        out_shape=jax.ShapeDtypeStruct((m, n), F32),
        compiler_params=_cparams(("parallel", "parallel")),
        name="out_proj",
    )(*mixes, w, res)


def _rope_kernel(pos_ref, tab_ref, cos_ref, sin_ref):
    ang = pos_ref[...].astype(F32) * tab_ref[0:1, :]
    cos_ref[...] = jnp.cos(ang) * tab_ref[1:2, :]
    sin_ref[...] = jnp.sin(ang) * tab_ref[2:3, :]


def _rope_tables(pos_col, tm=1024):
    m = pos_col.shape[0]
    half = MLA_ROPE // 2
    inv = ROPE_THETA ** (-jnp.arange(half, dtype=F32) * 2.0 / MLA_ROPE)
    zeros = jnp.zeros((LANES - MLA_ROPE,), F32)
    ones = jnp.ones((half,), F32)
    tab = jnp.stack([jnp.concatenate([inv, inv, zeros]),
                     jnp.concatenate([ones, ones, zeros]),
                     jnp.concatenate([-ones, ones, zeros])]
                    + [jnp.zeros((LANES,), F32)] * 5)
    spec = pl.BlockSpec((tm, LANES), lambda i: (i, 0))
    return pl.pallas_call(
        _rope_kernel,
        grid=(m // tm,),
        in_specs=[pl.BlockSpec((tm, 1), lambda i: (i, 0)),
                  pl.BlockSpec((8, LANES), lambda i: (0, 0))],
        out_specs=[spec, spec],
        out_shape=[jax.ShapeDtypeStruct((m, LANES), F32)] * 2,
        compiler_params=_cparams(("parallel",)),
        name="rope_tables",
    )(pos_col, tab)


def _mla_q_kernel(cq_ref, g_ref, w_ref, cos_ref, sin_ref, o_ref):
    n = _rms(cq_ref[...].astype(F32), g_ref[...]).astype(BF16)
    t = _dot(n, w_ref[...])
    o_ref[:, :GROUP_WIDTH] = t[:, :GROUP_WIDTH].astype(o_ref.dtype)
    cos, sin = cos_ref[...], sin_ref[...]
    for h in range(GROUP_HEADS):
        lo = GROUP_WIDTH + h * LANES
        pe = t[:, lo:lo + LANES] * cos + t[:, lo + GROUP_WIDTH:lo + GROUP_WIDTH + LANES] * sin
        o_ref[:, lo:lo + LANES] = pe.astype(o_ref.dtype)


def _mla_q(proj, g, w, cos, sin, tm=512):
    m = proj.shape[0]
    cq_block = PROJ_CQ0 // MLA_Q_RANK
    return pl.pallas_call(
        _mla_q_kernel,
        grid=(m // tm,),
        in_specs=[pl.BlockSpec((tm, MLA_Q_RANK), lambda i: (i, cq_block)),
                  pl.BlockSpec((1, MLA_Q_RANK), lambda i: (0, 0)),
                  pl.BlockSpec(w.shape, lambda i: (0, 0)),
                  pl.BlockSpec((tm, LANES), lambda i: (i, 0)),
                  pl.BlockSpec((tm, LANES), lambda i: (i, 0))],
        out_specs=pl.BlockSpec((tm, 2 * GROUP_WIDTH), lambda i: (i, 0)),
        out_shape=jax.ShapeDtypeStruct((m, 2 * GROUP_WIDTH), BF16),
        compiler_params=_cparams(("parallel",)),
        name="mla_q",
    )(proj, g.reshape(1, -1).astype(F32), w, cos, sin)


def _mla_kv_kernel(ckv_ref, g_ref, w_ref, kr_ref, cos_ref, sin_ref, kv_ref, kpe_ref):
    n = _rms(ckv_ref[...].astype(F32), g_ref[...]).astype(BF16)
    kv_ref[...] = _dot(n, w_ref[...]).astype(kv_ref.dtype)
    kr = kr_ref[...]
    kpe = kr[:, :LANES] * cos_ref[...] + kr[:, LANES:] * sin_ref[...]
    lane = lax.broadcasted_iota(jnp.int32, kpe.shape, 1)
    kpe = jnp.where((lane >= MLA_ROPE) & (lane < MLA_ROPE + 3), 1.0, kpe)
    kpe_ref[...] = kpe.astype(kpe_ref.dtype)


def _mla_kv(proj, g, w, kr, cos, sin, tm=512):
    m = proj.shape[0]
    ckv_block = PROJ_CKV0 // MLA_KV_RANK
    return pl.pallas_call(
        _mla_kv_kernel,
        grid=(m // tm,),
        in_specs=[pl.BlockSpec((tm, MLA_KV_RANK), lambda i: (i, ckv_block)),
                  pl.BlockSpec((1, MLA_KV_RANK), lambda i: (0, 0)),
                  pl.BlockSpec(w.shape, lambda i: (0, 0)),
                  pl.BlockSpec((tm, 2 * LANES), lambda i: (i, 0)),
                  pl.BlockSpec((tm, LANES), lambda i: (i, 0)),
                  pl.BlockSpec((tm, LANES), lambda i: (i, 0))],
        out_specs=[pl.BlockSpec((tm, 2 * GROUP_WIDTH), lambda i: (i, 0)),
                   pl.BlockSpec((tm, LANES), lambda i: (i, 0))],
        out_shape=[jax.ShapeDtypeStruct((m, 2 * GROUP_WIDTH), BF16),
                   jax.ShapeDtypeStruct((m, LANES), BF16)],
        compiler_params=_cparams(("parallel",)),
        name="mla_kv",
    )(proj, g.reshape(1, -1).astype(F32), w, kr, cos, sin)


def _split3(x):
    x1 = x.astype(BF16)
    r1 = x - x1.astype(F32)
    x2 = r1.astype(BF16)
    x3 = (r1 - x2.astype(F32)).astype(BF16)
    return x1, x2, x3


def _place(lane, cols, base, fill):
    out = fill
    for i, c in enumerate(cols):
        out = jnp.where(lane == base + i, c.astype(F32), out)
    return out


def _forget_cum_kernel(f_ref, b_ref, cum_ref, kx_ref, *, chunk):
    r = lax.broadcasted_iota(jnp.int32, (chunk, chunk), 0)
    c = lax.broadcasted_iota(jnp.int32, (chunk, chunk), 1)
    tri = jnp.where(r >= c, 1.0, 0.0).astype(BF16)
    lane = lax.broadcasted_iota(jnp.int32, (chunk, LANES), 1)
    unit_fill = jnp.where(lane < 3, 1.0, 0.0)

    def body(ci, carry):
        rows = _rows(ci, chunk)
        x = f_ref[rows, :] + b_ref[...]
        logf = (jnp.minimum(x, 0.0) - jnp.log1p(jnp.exp(-jnp.abs(x)))) * LOG2E
        parts = _split3(logf)
        cs = carry + _dot(tri, parts[0]) + _dot(tri, parts[1]) + _dot(tri, parts[2])
        cum_ref[rows, :] = cs
        for h in range(GROUP_HEADS):
            kx_ref[h, rows, :] = _place(lane, _split3(cs[:, h:h + 1]), 3, unit_fill).astype(kx_ref.dtype)
        return cs[chunk - 1:chunk, :]

    lax.fori_loop(0, f_ref.shape[0] // chunk, body, jnp.zeros((1, LANES), F32))


def _forget_cum(side, b_f, batch, seq):
    f_block = 2
    return pl.pallas_call(
        functools.partial(_forget_cum_kernel, chunk=256),
        grid=(batch,),
        in_specs=[pl.BlockSpec((seq, LANES), lambda b: (b, f_block)),
                  pl.BlockSpec((1, LANES), lambda b: (0, 0))],
        out_specs=[pl.BlockSpec((seq, LANES), lambda b: (b, 0)),
                   pl.BlockSpec((None, GROUP_HEADS, seq, LANES), lambda b: (b, 0, 0, 0))],
        out_shape=[jax.ShapeDtypeStruct((batch * seq, LANES), F32),
                   jax.ShapeDtypeStruct((batch, GROUP_HEADS, seq, LANES), BF16)],
        compiler_params=_cparams(("parallel",)),
        name="forget_cum",
    )(side, b_f)


def _tile_iota(t):
    return (lax.broadcasted_iota(jnp.int32, (t, t), 0),
            lax.broadcasted_iota(jnp.int32, (t, t), 1))


def _diag_mask(t, w):
    row = lax.broadcasted_iota(jnp.int32, (t, w), 0)
    col = lax.broadcasted_iota(jnp.int32, (t, w), 1)
    return col <= row + (w - t)


def _softmax_update(s, v, m, acc):
    m_new = jnp.maximum(m, jnp.max(s, axis=-1, keepdims=True))
    alpha = jnp.exp2(m - m_new)
    p = jnp.exp2(s - m_new).astype(BF16)
    v_aug = jnp.concatenate([v, jnp.ones_like(v)], axis=1)
    return m_new, alpha * acc + _dot(p, v_aug)


def _softmax_init(t):
    return (jnp.full((t, 1), NEG_INF, F32), jnp.zeros((t, 2 * HEAD_DIM), F32))


def _softmax_out(acc):
    return acc[:, :HEAD_DIM] / acc[:, HEAD_DIM:]


def _hs(h):
    return slice(h * HEAD_DIM, (h + 1) * HEAD_DIM)


def _rows(j, w):
    return pl.ds(pl.multiple_of(j * w, w), w)


def _causal_sweep(step, qi, init):
    carry = lax.fori_loop(0, qi, lambda j, c: step(j, c, False), init)
    return step(qi, carry, True)


def _update(s, v, state, mask, acc_ref, chain):
    if mask is not None:
        s = jnp.where(mask, s, NEG_INF)
    if acc_ref is None:
        return _softmax_update(s, v, *state)
    v_aug = jnp.concatenate([v, jnp.ones_like(v)], axis=1)
    acc_ref[chain] += _dot(jnp.exp2(s).astype(BF16), v_aug)
    return None


def _run_softmax(qi, t, n_chains, make_step, finish, acc_ref, all_zero=None):
    acc_ref[...] = jnp.zeros_like(acc_ref)
    fast = make_step(acc_ref)

    def wide(j, carry):
        if all_zero is None:
            fast(j, None, False, 2 * t)
        else:
            @pl.when(jnp.logical_not(all_zero(j, 2 * t)))
            def _():
                fast(j, None, False, 2 * t)
        return carry

    lax.fori_loop(0, qi // 2, wide, 0)

    @pl.when(qi % 2 == 1)
    def _():
        fast(qi // 2, None, True, 2 * t)

    @pl.when(qi % 2 == 0)
    def _():
        fast(qi, None, True, t)

    accs = [acc_ref[c] for c in range(n_chains)]
    l_min = accs[0][:, HEAD_DIM:HEAD_DIM + 1]
    for a in accs[1:]:
        l_min = jnp.minimum(l_min, a[:, HEAD_DIM:HEAD_DIM + 1])
    ok = jnp.min(l_min) >= TRUSTED_ROW_SUM

    @pl.when(ok)
    def _():
        finish([_softmax_out(a) for a in accs])

    @pl.when(jnp.logical_not(ok))
    def _():
        slow = make_step(None)
        init = tuple(_softmax_init(t) for _ in range(n_chains))
        carry = lax.fori_loop(0, qi, lambda j, c: slow(j, c, False, t), init)
        finish([_softmax_out(c[1]) for c in slow(qi, carry, True, t)])


def _key_norm_max(k_ref, h, seq, extra_ref=None):
    chunk = ATT_T

    def body(c, mx):
        kk = k_ref[_rows(c, chunk), _hs(h)].astype(F32)
        n2 = jnp.sum(kk * kk, axis=1, keepdims=True)
        if extra_ref is not None:
            e = extra_ref[_rows(c, chunk), :].astype(F32)
            n2 = n2 + jnp.sum(e * e, axis=1, keepdims=True)
        return jnp.maximum(mx, n2)

    mx = lax.fori_loop(0, seq // chunk, body, jnp.zeros((chunk, 1), F32))
    return jnp.sqrt(jnp.max(mx, axis=0, keepdims=True))


def _store_key_norms(kmax_ref, qi, norm_fn):
    @pl.when(qi == 0)
    def _():
        for h in range(kmax_ref.shape[0]):
            kmax_ref[h] = jnp.broadcast_to(norm_fn(h), kmax_ref.shape[1:])


def _row_norm(x):
    xf = x.astype(F32)
    return jnp.sqrt(jnp.sum(xf * xf, axis=1, keepdims=True))


def _fox_kernel(q_ref, k_ref, v_ref, kx_ref, cumq_ref, cumk_ref, o_ref, kmax_ref, acc_ref, *, t, seq):
    group, qi = pl.program_id(1), pl.program_id(2)
    hps = kmax_ref.shape[0]
    _store_key_norms(kmax_ref, qi, lambda h: _key_norm_max(k_ref, h, seq))
    lane = lax.broadcasted_iota(jnp.int32, (t, LANES), 1)
    minus_fill = jnp.where((lane >= 3) & (lane < 6), -1.0, 0.0)
    cumq = cumq_ref[...]

    def all_zero(j, w):
        head_lane = lax.broadcasted_iota(jnp.int32, (1, LANES), 1)
        mine = (head_lane >= group * hps) & (head_lane < (group + 1) * hps)
        gap = cumq_ref[0:1, :] - cumk_ref[pl.ds((j + 1) * w - 1, 1), :]
        return jnp.max(jnp.where(mine, gap, NEG_INF)) < DEAD_LOG2

    def q_aug(h, fast):
        q = q_ref[:, _hs(h)]
        c = jnp.sum(jnp.where(lane == group * hps + h, cumq, 0.0), axis=1, keepdims=True)
        if fast:
            c = c - _row_norm(q) * kmax_ref[h][0:1, 0:1]
        return jnp.concatenate([q, _place(lane, _split3(c), 0, minus_fill).astype(BF16)], axis=1)

    def make_step(acc_ref):
        fast = acc_ref is not None
        qs = [q_aug(h, fast) for h in range(hps)]

        def step(j, carry, masked, w):
            mask = _diag_mask(t, w) if masked else None
            out = []
            for h in range(hps):
                k = jnp.concatenate([k_ref[_rows(j, w), _hs(h)], kx_ref[h, _rows(j, w), :]], axis=1)
                out.append(_update(_dot_nt(qs[h], k), v_ref[_rows(j, w), _hs(h)], None if fast else carry[h], mask, acc_ref, h))
            return tuple(out)

        return step

    def finish(outs):
        o_ref[...] = jnp.concatenate(outs, axis=1).astype(o_ref.dtype)

    _run_softmax(qi, t, hps, make_step, finish, acc_ref, all_zero)


def _mla_kernel(qn_ref, qp_ref, kn_ref, kp_ref, v_ref, o_ref, kmax_ref, acc_ref, *, t, seq):
    qi = pl.program_id(2)
    hps = kmax_ref.shape[0]
    _store_key_norms(kmax_ref, qi, lambda h: _key_norm_max(kn_ref, h, seq, kp_ref))
    lane = lax.broadcasted_iota(jnp.int32, (t, LANES), 1)

    def q_aug(h, fast):
        qn, qp = qn_ref[:, _hs(h)], qp_ref[:, _hs(h)]
        if fast:
            qn_f, qp_f = qn.astype(F32), qp.astype(F32)
            norm = jnp.sqrt(jnp.sum(qn_f * qn_f, axis=1, keepdims=True)
                            + jnp.sum(qp_f * qp_f, axis=1, keepdims=True))
            qp = _place(lane, _split3(-norm * kmax_ref[h][0:1, 0:1]), MLA_ROPE, qp_f).astype(BF16)
        return jnp.concatenate([qn, qp], axis=1)

    def make_step(acc_ref):
        fast = acc_ref is not None
        qs = [q_aug(h, fast) for h in range(hps)]

        def step(j, carry, masked, w):
            mask = _diag_mask(t, w) if masked else None
            kp = kp_ref[_rows(j, w), :]
            out = []
            for h in range(hps):
                k = jnp.concatenate([kn_ref[_rows(j, w), _hs(h)], kp], axis=1)
                out.append(_update(_dot_nt(qs[h], k), v_ref[_rows(j, w), _hs(h)], None if fast else carry[h], mask, acc_ref, h))
            return tuple(out)

        return step

    def finish(outs):
        o_ref[...] = jnp.concatenate(outs, axis=1).astype(o_ref.dtype)

    _run_softmax(qi, t, hps, make_step, finish, acc_ref)


def _diff_kernel(lq1_ref, lk1_ref, lq2_ref, lk2_ref, q_ref, k_ref, v_ref, posq_ref, posk_ref,
                 bias_ref, subg_ref, o_ref, kmax_ref, acc_ref, *, t, seq, lam_init):
    qi = pl.program_id(2)
    _store_key_norms(kmax_ref, qi, lambda h: _key_norm_max(k_ref, h, seq))
    lam = (jnp.exp(jnp.sum(lq1_ref[...] * lk1_ref[...], axis=-1, keepdims=True))
           - jnp.exp(jnp.sum(lq2_ref[...] * lk2_ref[...], axis=-1, keepdims=True)) + lam_init)
    lane = lax.broadcasted_iota(jnp.int32, (t, LANES), 1)
    zero_fill = jnp.zeros((t, LANES), F32)

    def key_units(w):
        return jnp.where(lax.broadcasted_iota(jnp.int32, (w, LANES), 1) < 3, 1.0, 0.0).astype(BF16)

    posq = posq_ref[...]
    posq_min = jnp.min(posq)
    bias_rows = [jnp.broadcast_to(bias_ref[h], (t, LANES)) for h in range(HEADS_PER_STEP)]
    last = REL_MAX_DIST - 1
    far_bias = [bias_ref[h][:, last:] for h in range(HEADS_PER_STEP)]

    def half_q(h, c):
        q = q_ref[:, _hs(h)]
        return jnp.where((lane < DIFF_QK) if c == 0 else (lane >= DIFF_QK), q, jnp.zeros_like(q))

    def make_step(acc_ref):
        fast = acc_ref is not None
        q_near, q_far = [], []
        for h in range(HEADS_PER_STEP):
            for c in range(2):
                qz = half_q(h, c)
                if fast:
                    bound = (_row_norm(qz) * kmax_ref[h][0:1, 0:1]
                             + jnp.max(bias_ref[h], axis=1, keepdims=True))
                    q_near.append(jnp.concatenate(
                        [qz, _place(lane, _split3(-bound), 0, zero_fill).astype(BF16)], axis=1))
                    q_far.append(jnp.concatenate(
                        [qz, _place(lane, _split3(far_bias[h] - bound), 0, zero_fill).astype(BF16)], axis=1))
                else:
                    q_near.append(qz)
                    q_far.append(qz)

        def chains(j, carry, masked, w, gathered):
            mask = _diag_mask(t, w) if masked else None
            out = []
            for h in range(HEADS_PER_STEP):
                k = k_ref[_rows(j, w), _hs(h)]
                if fast:
                    k = jnp.concatenate([k, key_units(w)], axis=1)
                v = v_ref[_rows(j, w), _hs(h)]
                for c in range(2):
                    i = 2 * h + c
                    if gathered is not None:
                        s = _dot_nt(q_near[i], k) + gathered[h]
                    elif fast:
                        s = _dot_nt(q_far[i], k)
                    else:
                        s = _dot_nt(q_far[i], k) + far_bias[h]
                    out.append(_update(s, v, None if fast else carry[i], mask, acc_ref, i))
            return tuple(out)

        def step(j, carry, masked, w):
            posk = posk_ref[:, _rows(j, w)]

            def near(c):
                dist = jnp.clip(posq - posk, 0, last)
                gathered = [jnp.concatenate(
                    [jnp.take_along_axis(bias_rows[h], dist[:, cc * LANES:(cc + 1) * LANES], axis=1,
                                         mode="promise_in_bounds")
                     for cc in range(w // LANES)], axis=1) for h in range(HEADS_PER_STEP)]
                return chains(j, c, masked, w, gathered)

            def far(c):
                return chains(j, c, masked, w, None)

            return lax.cond(posq_min - jnp.max(posk) < last, near, far, carry)

        return step

    def finish(outs):
        normed = []
        for h in range(HEADS_PER_STEP):
            o = outs[2 * h] - lam * outs[2 * h + 1]
            normed.append(_rms(o, subg_ref[...]) * (1.0 - lam_init))
        o_ref[...] = jnp.concatenate(normed, axis=1).astype(o_ref.dtype)

    _run_softmax(qi, t, 2 * HEADS_PER_STEP, make_step, finish, acc_ref)


def _sb_kernel(q_ref, k_ref, v_ref, o_ref, *, t):
    qi = pl.program_id(2)
    w = t
    row, col = _tile_iota(t)
    later = jnp.where(row > col, 1.0, 0.0).astype(BF16)
    qs = [q_ref[:, _hs(h)] for h in range(SB_HEADS_PER_STEP)]

    def step(j, carry, masked):
        out = []
        for h in range(SB_HEADS_PER_STEP):
            tail, acc = carry[h]
            z = _dot_nt(qs[h], k_ref[_rows(j, w), _hs(h)])
            sp = jnp.log(1.0 + jnp.exp2(-jnp.abs(z))) * LOG2E
            log_beta = jnp.minimum(z, 0.0) - sp
            log_1m = -jnp.maximum(z, 0.0) - sp
            if masked:
                log_1m = jnp.where(col < row, log_1m, 0.0)
            hi = log_1m.astype(BF16)
            lo = (log_1m - hi.astype(F32)).astype(BF16)
            both = _dot(jnp.concatenate([hi, lo], axis=0), later)
            suffix = both[:t] + both[t:]
            a = jnp.exp2(log_beta + suffix + tail)
            if masked:
                a = jnp.where(col < row, a, 0.0)
            acc = acc + _dot(a.astype(BF16), v_ref[_rows(j, w), _hs(h)])
            tail = tail + suffix[:, :1] + log_1m[:, :1]
            out.append((tail, acc))
        return tuple(out)

    def live(carry):
        worst = carry[0][0]
        for h in range(1, SB_HEADS_PER_STEP):
            worst = jnp.maximum(worst, carry[h][0])
        return (jnp.max(worst) > DEAD_LOG2).astype(jnp.int32)

    init = tuple((jnp.zeros((t, 1), F32), jnp.zeros((t, HEAD_DIM), F32)) for _ in range(SB_HEADS_PER_STEP))
    carry = step(qi, init, True)

    def body(state):
        j, _, c = state
        c = step(j, c, False)
        return j - 1, live(c), c

    _, _, carry = lax.while_loop(lambda st: (st[0] >= 0) & (st[1] > 0), body,
                                 (qi - 1, live(carry), carry))
    o_ref[...] = jnp.concatenate([c[1] for c in carry], axis=1).astype(o_ref.dtype)


def _attn_call(kernel, t, batch, seq, in_arrays, in_specs, name, hps=HEADS_PER_STEP, softmax_chains=0):
    nq = seq // t
    width = hps * HEAD_DIM
    scratch = []
    if softmax_chains:
        scratch = [pltpu.VMEM((hps, 8, LANES), F32),
                   pltpu.VMEM((softmax_chains, t, 2 * HEAD_DIM), F32)]
    return pl.pallas_call(
        kernel,
        grid=(batch, GROUP_HEADS // hps, nq),
        in_specs=in_specs,
        out_specs=pl.BlockSpec((t, width), lambda b, h, i: (b * nq + i, h)),
        out_shape=jax.ShapeDtypeStruct((batch * seq, GROUP_WIDTH), BF16),
        scratch_shapes=scratch,
        compiler_params=_cparams(("arbitrary", "arbitrary", "arbitrary")),
        name=name,
    )(*in_arrays)


def _q_spec(t, seq, col0, hps=HEADS_PER_STEP):
    nq = seq // t
    return pl.BlockSpec((t, hps * HEAD_DIM), lambda b, h, i: (b * nq + i, col0 + h))


def _kv_spec(seq, col0, hps=HEADS_PER_STEP):
    return pl.BlockSpec((seq, hps * HEAD_DIM), lambda b, h, i: (b, col0 + h))


def _shared_spec(seq):
    return pl.BlockSpec((seq, HEAD_DIM), lambda b, h, i: (b, 0))


def _full_spec(shape):
    return pl.BlockSpec(shape, lambda b, h, i: (0,) * len(shape))


def _mem_kernel(x_ref, g_ref, wq_ref, k_ref, v_ref, wo_ref, o_ref):
    x = x_ref[...]
    h = _rms(x, g_ref[...]).astype(BF16)
    q = _dot(h, wq_ref[...]).astype(BF16)
    outs = []
    for hd in range(MEM_HEADS):
        sl = slice(hd * MEM_HEAD_DIM, (hd + 1) * MEM_HEAD_DIM)
        s = _dot_nt(q[:, sl], k_ref[:, sl])
        p = jnp.exp2(s - jnp.max(s, axis=-1, keepdims=True))
        o = _dot(p.astype(BF16), v_ref[:, sl]) / jnp.sum(p, axis=-1, keepdims=True)
        outs.append(o.astype(BF16))
    o_ref[...] = x + _dot(jnp.concatenate(outs, axis=-1), wo_ref[...])


def _mem_block(x, g, wq, k, v, wo, layer, batch, seq, tm=256):
    m, d = x.shape
    n_mem = k.shape[0] // batch
    tiles_per_batch = seq // tm
    width = MEM_HEADS * MEM_HEAD_DIM
    return pl.pallas_call(
        _mem_kernel,
        grid=(m // tm,),
        in_specs=[pl.BlockSpec((tm, d), lambda i: (i, 0)),
                  pl.BlockSpec((1, d), lambda i: (0, 0)),
                  _w_spec(wq, layer, (d, width), lambda i: (0, 0)),
                  pl.BlockSpec((n_mem, width), lambda i: (i // tiles_per_batch, 0)),
                  pl.BlockSpec((n_mem, width), lambda i: (i // tiles_per_batch, 0)),
                  _w_spec(wo, layer, (width, d), lambda i: (0, 0))],
        out_specs=pl.BlockSpec((tm, d), lambda i: (i, 0)),
        out_shape=jax.ShapeDtypeStruct((m, d), F32),
        compiler_params=_cparams(("parallel",)),
        name="mem_block",
    )(x, g.reshape(1, d).astype(F32), wq, k, v, wo)


def _rel_bias_by_distance(rel_table):
    d = jnp.arange(REL_MAX_DIST, dtype=jnp.int32)
    max_exact = REL_BUCKETS // 2
    nf = jnp.maximum(d, 1).astype(F32)
    large = max_exact + (jnp.log(nf / max_exact) / math.log(REL_MAX_DIST / max_exact)
                         * (REL_BUCKETS - max_exact)).astype(jnp.int32)
    large = jnp.minimum(large, REL_BUCKETS - 1)
    bucket = jnp.where(d < max_exact, d, large)
    return jnp.take(rel_table, bucket, axis=0).T.astype(F32)


def _in_proj_col_scale():
    gw = GROUP_WIDTH
    scale = np.ones((PROJ_COLS,), np.float32)
    scale[PROJ_A0:PROJ_A0 + gw] = DIFF_QK ** -0.5 * LOG2E
    scale[PROJ_C0:PROJ_C0 + gw] = HEAD_DIM ** -0.5 * LOG2E
    scale[PROJ_D0:PROJ_D0 + gw] = HEAD_DIM ** -0.5 * LOG2E
    return scale


PREP_CHUNK = 512
PREP_ALIGNED = PROJ_C0 // PREP_CHUNK
PREP_SKEW = PREP_CHUNK - (PROJ_C0 - (A_COLS + B_COLS))


def _regroup_kernel(x_ref, tail_ref, s_ref, o_ref):
    c = pl.program_id(1)

    @pl.when(c < PREP_ALIGNED)
    def _():
        o_ref[...] = (x_ref[...].T * s_ref[...]).astype(o_ref.dtype)

    @pl.when(c >= PREP_ALIGNED)
    def _():
        x = jnp.concatenate([x_ref[PREP_SKEW:, :], tail_ref[...]], axis=0)
        o_ref[...] = (x.T * s_ref[...]).astype(o_ref.dtype)


def _regroup_in_proj(w_in_t, layer, tr=2048):
    d = w_in_t.shape[2]
    tails_per_chunk = PREP_CHUNK // PREP_SKEW
    scale = jnp.asarray(_in_proj_col_scale()).reshape(1, PROJ_COLS)
    return pl.pallas_call(
        _regroup_kernel,
        grid=(d // tr, PROJ_COLS // PREP_CHUNK),
        in_specs=[
            pl.BlockSpec((None, PREP_CHUNK, tr),
                         lambda r, c: (layer, jnp.where(c < PREP_ALIGNED, c, c - 1), r)),
            pl.BlockSpec((None, PREP_SKEW, tr),
                         lambda r, c: (layer, jnp.where(c < PREP_ALIGNED, 0, c * tails_per_chunk), r)),
            pl.BlockSpec((1, PREP_CHUNK), lambda r, c: (0, c))],
        out_specs=pl.BlockSpec((tr, PREP_CHUNK), lambda r, c: (r, c)),
        out_shape=jax.ShapeDtypeStruct((d, PROJ_COLS), BF16),
        compiler_params=_cparams(("parallel", "parallel")),
        name="regroup_in_proj",
    )(w_in_t, w_in_t, scale)


SIDE_COLS = 4 * LANES


def _side_proj_kernel(h_ref, kr_ref, f_ref, o_ref):
    kr, f = kr_ref[...], f_ref[...]
    half = MLA_ROPE // 2

    def zeros(n):
        return jnp.zeros((n, kr.shape[1]), kr.dtype)

    w_t = jnp.concatenate([kr, zeros(LANES - MLA_ROPE), kr[half:], kr[:half], zeros(LANES - MLA_ROPE),
                           f, zeros(2 * LANES - GROUP_HEADS)], axis=0)
    o_ref[...] = _dot_nt(h_ref[...], w_t.astype(h_ref.dtype))


def _side_proj(h, w_in_t, layer, tm=1024):
    m, d = h.shape
    kr_row0 = A_COLS + MLA_Q_RANK + MLA_KV_RANK
    f_row0 = A_COLS + B_COLS + C_COLS + 3 * GROUP_WIDTH
    return pl.pallas_call(
        _side_proj_kernel,
        grid=(m // tm,),
        in_specs=[pl.BlockSpec((tm, d), lambda i: (i, 0)),
                  pl.BlockSpec((None, MLA_ROPE, d), lambda i: (layer, kr_row0 // MLA_ROPE, 0)),
                  pl.BlockSpec((None, GROUP_HEADS, d), lambda i: (layer, f_row0 // GROUP_HEADS, 0))],
        out_specs=pl.BlockSpec((tm, SIDE_COLS), lambda i: (i, 0)),
        out_shape=jax.ShapeDtypeStruct((m, SIDE_COLS), F32),
        compiler_params=_cparams(("parallel",)),
        name="in_proj_side",
    )(h, w_in_t, w_in_t)


def _prep_uq(w):
    qk = MLA_NOPE + MLA_ROPE
    half = MLA_ROPE // 2
    w = (w * (qk ** -0.5 * LOG2E)).reshape(w.shape[0], GROUP_HEADS, qk)
    pad = jnp.zeros((w.shape[0], GROUP_HEADS, LANES - MLA_ROPE), w.dtype)
    nope = w[:, :, :MLA_NOPE]
    r1 = w[:, :, MLA_NOPE:MLA_NOPE + half]
    r2 = w[:, :, MLA_NOPE + half:]
    rope = jnp.concatenate([r1, r2, pad], axis=-1)
    swapped = jnp.concatenate([r2, r1, pad], axis=-1)
    return jnp.concatenate([nope.reshape(w.shape[0], -1), rope.reshape(w.shape[0], -1),
                            swapped.reshape(w.shape[0], -1)], axis=1).astype(BF16)


def _prep_ukv(w):
    w = w.reshape(w.shape[0], GROUP_HEADS, MLA_NOPE + HEAD_DIM)
    return jnp.concatenate([w[:, :, :MLA_NOPE].reshape(w.shape[0], -1),
                            w[:, :, MLA_NOPE:].reshape(w.shape[0], -1)], axis=1).astype(BF16)


def kernel(x, mem, positions, attn_norm_g, w_in, w_out, rel_table, diff_lam_q1, diff_lam_k1, diff_lam_q2, diff_lam_k2, diff_sub_g, mla_q_norm_g, mla_kv_norm_g, mla_w_uq, mla_w_ukv, fox_b_f, mem_q_norm_g, mem_kv_norm_g, mem_w_q, mem_w_k, mem_w_v, mem_w_o, mlp_norm_g, w_up, w_down, final_norm_g):
    batch, seq, d = x.shape
    m = batch * seq
    gw = GROUP_WIDTH
    nh = GROUP_HEADS
    xf = x.reshape(m, d)
    mem2 = mem.reshape(batch * mem.shape[1], d)
    pos_col = positions.reshape(m, 1).astype(jnp.int32)
    pos_row = positions.reshape(batch, 1, seq).astype(jnp.int32)
    cos, sin = _rope_tables(pos_col)
    bias_by_dist = (_rel_bias_by_distance(rel_table) * LOG2E).reshape(nh, 1, LANES)
    npair = nh // HEADS_PER_STEP
    pair_cols = HEADS_PER_STEP * HEAD_DIM
    a0 = PROJ_A0 // pair_cols
    nwide = nh // WIDE_HEADS_PER_STEP
    fox0 = PROJ_D0 // (WIDE_HEADS_PER_STEP * HEAD_DIM)
    nsb = nh // SB_HEADS_PER_STEP
    sb0 = PROJ_C0 // (SB_HEADS_PER_STEP * HEAD_DIM)

    w_out_b, w_up_b, w_down_b = w_out.astype(BF16), w_up.astype(BF16), w_down.astype(BF16)
    mem_wq_b = (mem_w_q * (MEM_HEAD_DIM ** -0.5 * LOG2E)).astype(BF16)
    mem_wk_b, mem_wv_b, mem_wo_b = mem_w_k.astype(BF16), mem_w_v.astype(BF16), mem_w_o.astype(BF16)

    w_in_t = jnp.swapaxes(w_in, 1, 2)

    for l in range(DEPTH):
        w_main = _regroup_in_proj(w_in_t, l)
        h = _rmsnorm(xf, attn_norm_g[l], BF16)
        proj = _matmul(h, w_main, BF16, name="in_proj")
        side = _side_proj(h, w_in_t, l)
        b_f = jnp.concatenate([fox_b_f[l].astype(F32), jnp.zeros((LANES - nh,), F32)]).reshape(1, LANES)
        cum, fox_kx = _forget_cum(side, b_f, batch, seq)

        lam_init = 0.8 - 0.6 * math.exp(-0.3 * l)
        lam_vecs = [v[l].reshape(1, DIFF_QK).astype(F32)
                    for v in (diff_lam_q1, diff_lam_k1, diff_lam_q2, diff_lam_k2)]
        o_a = _attn_call(
            functools.partial(_diff_kernel, t=ATT_T, seq=seq, lam_init=lam_init), ATT_T, batch, seq,
            lam_vecs + [proj, proj, proj, pos_col, pos_row, bias_by_dist,
                        diff_sub_g[l].reshape(1, HEAD_DIM).astype(F32)],
            [_full_spec((1, DIFF_QK))] * 4
            + [_q_spec(ATT_T, seq, a0), _kv_spec(seq, a0 + npair), _kv_spec(seq, a0 + 2 * npair),
               pl.BlockSpec((ATT_T, 1), lambda b, h, i: (b * (seq // ATT_T) + i, 0)),
               pl.BlockSpec((None, 1, seq), lambda b, h, i: (b, 0, 0)),
               pl.BlockSpec((HEADS_PER_STEP, 1, LANES), lambda b, h, i: (h, 0, 0)),
               _full_spec((1, HEAD_DIM))],
            "diff_attn", softmax_chains=2 * HEADS_PER_STEP)

        q_b = _mla_q(proj, mla_q_norm_g[l], _prep_uq(mla_w_uq[l]), cos, sin)
        kv_b, kpe_b = _mla_kv(proj, mla_kv_norm_g[l], _prep_ukv(mla_w_ukv[l]), side, cos, sin)
        o_b = _attn_call(
            functools.partial(_mla_kernel, t=ATT_T, seq=seq), ATT_T, batch, seq,
            [q_b, q_b, kv_b, kpe_b, kv_b],
            [_q_spec(ATT_T, seq, 0, WIDE_HEADS_PER_STEP), _q_spec(ATT_T, seq, nwide, WIDE_HEADS_PER_STEP),
             _kv_spec(seq, 0, WIDE_HEADS_PER_STEP), _shared_spec(seq),
             _kv_spec(seq, nwide, WIDE_HEADS_PER_STEP)],
            "mla_attn", hps=WIDE_HEADS_PER_STEP, softmax_chains=WIDE_HEADS_PER_STEP)

        o_c = _attn_call(
            functools.partial(_sb_kernel, t=SB_T), SB_T, batch, seq,
            [proj, proj, proj],
            [_q_spec(SB_T, seq, sb0, SB_HEADS_PER_STEP), _kv_spec(seq, sb0 + nsb, SB_HEADS_PER_STEP),
             _kv_spec(seq, sb0 + 2 * nsb, SB_HEADS_PER_STEP)],
            "stickbreak_attn", hps=SB_HEADS_PER_STEP)

        o_d = _attn_call(
            functools.partial(_fox_kernel, t=ATT_T, seq=seq), ATT_T, batch, seq,
            [proj, proj, proj, fox_kx, cum, cum],
            [_q_spec(ATT_T, seq, fox0, WIDE_HEADS_PER_STEP), _kv_spec(seq, fox0 + nwide, WIDE_HEADS_PER_STEP),
             _kv_spec(seq, fox0 + 2 * nwide, WIDE_HEADS_PER_STEP),
             pl.BlockSpec((None, WIDE_HEADS_PER_STEP, seq, LANES), lambda b, h, i: (b, h, 0, 0)),
             pl.BlockSpec((ATT_T, LANES), lambda b, h, i: (b * (seq // ATT_T) + i, 0)),
             pl.BlockSpec((seq, LANES), lambda b, h, i: (b, 0))],
            "fox_attn", hps=WIDE_HEADS_PER_STEP, softmax_chains=WIDE_HEADS_PER_STEP)

        xf = _out_proj([o_a, o_b, o_c, o_d], w_out_b, l, xf)

        mem_n = _rmsnorm(mem2, mem_kv_norm_g[l], BF16)
        k_m = _matmul(mem_n, mem_wk_b, BF16, layer=l, name="mem_k")
        v_m = _matmul(mem_n, mem_wv_b, BF16, layer=l, name="mem_v")
        xf = _mem_block(xf, mem_q_norm_g[l], mem_wq_b, k_m, v_m, mem_wo_b, l, batch, seq)

        h = _rmsnorm(xf, mlp_norm_g[l], BF16)
        u = _matmul(h, w_up_b, BF16, layer=l, relu2=True, name="mlp_up")
        xf = _matmul(u, w_down_b, F32, layer=l, res=xf, name="mlp_down")

    return _rmsnorm(xf, final_norm_g, F32).reshape(batch, seq, d)
```

```python
import functools
import math

import numpy as np
import jax
import jax.numpy as jnp
from jax import lax
from jax.experimental import pallas as pl
from jax.experimental.pallas import tpu as pltpu

F32 = jnp.float32
BF16 = jnp.bfloat16

DEPTH = 2
HEAD_DIM = 128
GROUP_HEADS = 8
GROUP_WIDTH = GROUP_HEADS * HEAD_DIM
DIFF_QK = HEAD_DIM // 2
MLA_Q_RANK = 1536
MLA_KV_RANK = 512
MLA_NOPE = 128
MLA_ROPE = 64
ROPE_THETA = 10000.0
REL_BUCKETS = 32
REL_MAX_DIST = 128
MEM_HEADS = 4
MEM_HEAD_DIM = 128
NORM_EPS = 1e-6
NEG_INF = -1e30

A_COLS = 3 * GROUP_WIDTH
B_COLS = MLA_Q_RANK + MLA_KV_RANK + MLA_ROPE
C_COLS = 3 * GROUP_WIDTH

PROJ_A0 = 0
PROJ_CQ0 = A_COLS
PROJ_CKV0 = A_COLS + MLA_Q_RANK
PROJ_ALIGN = 512
PROJ_C0 = -(-(A_COLS + B_COLS) // PROJ_ALIGN) * PROJ_ALIGN
PROJ_D0 = PROJ_C0 + C_COLS
PROJ_COLS = PROJ_D0 + 3 * GROUP_WIDTH

LANES = 128
SUBLANES = 8
VMEM_LIMIT = 48 * 1024 * 1024
OFFSET_PARTS = 3

LOG2E = math.log2(math.e)

ATT_T = 512
SB_T = 256
HEADS_PER_STEP = 2
WIDE_HEADS_PER_STEP = 4
SB_HEADS_PER_STEP = 4
DEAD_LOG2 = -160.0
TRUSTED_ROW_SUM = 2.0 ** -60


def _cparams(sem):
    return pltpu.CompilerParams(dimension_semantics=sem, vmem_limit_bytes=VMEM_LIMIT)


def _dot(a, b):
    return jnp.dot(a, b, preferred_element_type=F32)


def _dot_nt(a, b):
    return lax.dot_general(a, b, (((1,), (1,)), ((), ())), preferred_element_type=F32)


def _rms(x, g):
    return x * lax.rsqrt(jnp.mean(x * x, axis=-1, keepdims=True) + NORM_EPS) * g


def _rmsnorm_kernel(x_ref, g_ref, o_ref):
    o_ref[...] = _rms(x_ref[...].astype(F32), g_ref[...]).astype(o_ref.dtype)


def _rmsnorm(x, g, out_dtype, tm=256):
    m, d = x.shape
    return pl.pallas_call(
        _rmsnorm_kernel,
        grid=(m // tm,),
        in_specs=[pl.BlockSpec((tm, d), lambda i: (i, 0)),
                  pl.BlockSpec((1, d), lambda i: (0, 0))],
        out_specs=pl.BlockSpec((tm, d), lambda i: (i, 0)),
        out_shape=jax.ShapeDtypeStruct((m, d), out_dtype),
        compiler_params=_cparams(("parallel",)),
        name="rmsnorm",
    )(x, g.reshape(1, d).astype(F32))


def _mm_epilogue(r, res_ref, o_ref, relu2):
    if relu2:
        r = jnp.square(jnp.maximum(r, 0.0))
    if res_ref is not None:
        r = r + res_ref[...]
    o_ref[...] = r.astype(o_ref.dtype)


def _mm_kernel(a_ref, w_ref, *rest, nk, relu2, has_res):
    res_ref = rest[0] if has_res else None
    o_ref = rest[1] if has_res else rest[0]
    if nk == 1:
        _mm_epilogue(_dot(a_ref[...], w_ref[...]), res_ref, o_ref, relu2)
        return
    acc_ref = rest[-1]
    k = pl.program_id(2)

    @pl.when(k == 0)
    def _():
        acc_ref[...] = _dot(a_ref[...], w_ref[...])

    @pl.when(k > 0)
    def _():
        acc_ref[...] += _dot(a_ref[...], w_ref[...])

    @pl.when(k == nk - 1)
    def _():
        _mm_epilogue(acc_ref[...], res_ref, o_ref, relu2)


def _w_spec(w, layer, block, index_map):
    if w.ndim == 2:
        return pl.BlockSpec(block, index_map)
    return pl.BlockSpec((None,) + block, lambda *g: (layer,) + index_map(*g))


def _matmul(a, w, out_dtype, *, layer=None, res=None, relu2=False, tm=1024, tn=512, tk=4096,
            name="matmul"):
    m, kd = a.shape
    n = w.shape[-1]
    tm, tn, tk = min(tm, m), min(tn, n), min(tk, kd)
    assert m % tm == 0 and n % tn == 0 and kd % tk == 0
    nk = kd // tk
    in_specs = [pl.BlockSpec((tm, tk), lambda i, j, k: (i, k)),
                _w_spec(w, layer, (tk, tn), lambda i, j, k: (k, j))]
    args = [a, w]
    if res is not None:
        in_specs.append(pl.BlockSpec((tm, tn), lambda i, j, k: (i, j)))
        args.append(res)
    return pl.pallas_call(
        functools.partial(_mm_kernel, nk=nk, relu2=relu2, has_res=res is not None),
        grid=(m // tm, n // tn, nk),
        in_specs=in_specs,
        out_specs=pl.BlockSpec((tm, tn), lambda i, j, k: (i, j)),
        out_shape=jax.ShapeDtypeStruct((m, n), out_dtype),
        scratch_shapes=[pltpu.VMEM((tm, tn), F32)] if nk > 1 else [],
        compiler_params=_cparams(("parallel", "parallel", "arbitrary")),
        name=name,
    )(*args)


def _out_proj_kernel(a0, a1, a2, a3, w_ref, res_ref, o_ref):
    acc = res_ref[...]
    for g, a_ref in enumerate((a0, a1, a2, a3)):
        acc = acc + _dot(a_ref[...], w_ref[g * GROUP_WIDTH:(g + 1) * GROUP_WIDTH, :])
    o_ref[...] = acc


def _out_proj(mixes, w, layer, res, tm=1024, tn=512):
    m = res.shape[0]
    n = w.shape[-1]
    a_spec = pl.BlockSpec((tm, GROUP_WIDTH), lambda i, j: (i, 0))
    return pl.pallas_call(
        _out_proj_kernel,
        grid=(m // tm, n // tn),
        in_specs=[a_spec, a_spec, a_spec, a_spec,
                  _w_spec(w, layer, (w.shape[-2], tn), lambda i, j: (0, j)),
                  pl.BlockSpec((tm, tn), lambda i, j: (i, j))],
        out_specs=pl.BlockSpec((tm, tn), lambda i, j: (i, j)),
        out_shape=jax.ShapeDtypeStruct((m, n), F32),
        compiler_params=_cparams(("parallel", "parallel")),
        name="out_proj",
    )(*mixes, w, res)


def _rope_kernel(pos_ref, tab_ref, cos_ref, sin_ref):
    ang = pos_ref[...].astype(F32) * tab_ref[0:1, :]
    cos_ref[...] = jnp.cos(ang) * tab_ref[1:2, :]
    sin_ref[...] = jnp.sin(ang) * tab_ref[2:3, :]


def _rope_tables(pos_col, tm=1024):
    m = pos_col.shape[0]
    half = MLA_ROPE // 2
    inv = ROPE_THETA ** (-jnp.arange(half, dtype=F32) * 2.0 / MLA_ROPE)
    zeros = jnp.zeros((LANES - MLA_ROPE,), F32)
    ones = jnp.ones((half,), F32)
    tab = jnp.stack([jnp.concatenate([inv, inv, zeros]),
                     jnp.concatenate([ones, ones, zeros]),
                     jnp.concatenate([-ones, ones, zeros])]
                    + [jnp.zeros((LANES,), F32)] * 5)
    spec = pl.BlockSpec((tm, LANES), lambda i: (i, 0))
    return pl.pallas_call(
        _rope_kernel,
        grid=(m // tm,),
        in_specs=[pl.BlockSpec((tm, 1), lambda i: (i, 0)),
                  pl.BlockSpec((SUBLANES, LANES), lambda i: (0, 0))],
        out_specs=[spec, spec],
        out_shape=[jax.ShapeDtypeStruct((m, LANES), F32)] * 2,
        compiler_params=_cparams(("parallel",)),
        name="rope_tables",
    )(pos_col, tab)


def _mla_q_kernel(cq_ref, g_ref, w_ref, cos_ref, sin_ref, o_ref):
    n = _rms(cq_ref[...].astype(F32), g_ref[...]).astype(BF16)
    t = _dot(n, w_ref[...])
    o_ref[:, :GROUP_WIDTH] = t[:, :GROUP_WIDTH].astype(o_ref.dtype)
    cos, sin = cos_ref[...], sin_ref[...]
    for h in range(GROUP_HEADS):
        lo = GROUP_WIDTH + h * LANES
        pe = t[:, lo:lo + LANES] * cos + t[:, lo + GROUP_WIDTH:lo + GROUP_WIDTH + LANES] * sin
        o_ref[:, lo:lo + LANES] = pe.astype(o_ref.dtype)


def _mla_q(proj, g, w, cos, sin, tm=512):
    m = proj.shape[0]
    cq_block = PROJ_CQ0 // MLA_Q_RANK
    return pl.pallas_call(
        _mla_q_kernel,
        grid=(m // tm,),
        in_specs=[pl.BlockSpec((tm, MLA_Q_RANK), lambda i: (i, cq_block)),
                  pl.BlockSpec((1, MLA_Q_RANK), lambda i: (0, 0)),
                  pl.BlockSpec(w.shape, lambda i: (0, 0)),
                  pl.BlockSpec((tm, LANES), lambda i: (i, 0)),
                  pl.BlockSpec((tm, LANES), lambda i: (i, 0))],
        out_specs=pl.BlockSpec((tm, 2 * GROUP_WIDTH), lambda i: (i, 0)),
        out_shape=jax.ShapeDtypeStruct((m, 2 * GROUP_WIDTH), BF16),
        compiler_params=_cparams(("parallel",)),
        name="mla_q",
    )(proj, g.reshape(1, -1).astype(F32), w, cos, sin)


def _mla_kv_kernel(ckv_ref, g_ref, w_ref, kr_ref, cos_ref, sin_ref, kv_ref, kpe_ref):
    n = _rms(ckv_ref[...].astype(F32), g_ref[...]).astype(BF16)
    kv_ref[...] = _dot(n, w_ref[...]).astype(kv_ref.dtype)
    kr = kr_ref[...]
    kpe = kr[:, :LANES] * cos_ref[...] + kr[:, LANES:] * sin_ref[...]
    lane = lax.broadcasted_iota(jnp.int32, kpe.shape, 1)
    kpe = jnp.where((lane >= MLA_ROPE) & (lane < MLA_ROPE + OFFSET_PARTS), 1.0, kpe)
    kpe_ref[...] = kpe.astype(kpe_ref.dtype)


def _mla_kv(proj, g, w, kr, cos, sin, tm=512):
    m = proj.shape[0]
    ckv_block = PROJ_CKV0 // MLA_KV_RANK
    return pl.pallas_call(
        _mla_kv_kernel,
        grid=(m // tm,),
        in_specs=[pl.BlockSpec((tm, MLA_KV_RANK), lambda i: (i, ckv_block)),
                  pl.BlockSpec((1, MLA_KV_RANK), lambda i: (0, 0)),
                  pl.BlockSpec(w.shape, lambda i: (0, 0)),
                  pl.BlockSpec((tm, 2 * LANES), lambda i: (i, 0)),
                  pl.BlockSpec((tm, LANES), lambda i: (i, 0)),
                  pl.BlockSpec((tm, LANES), lambda i: (i, 0))],
        out_specs=[pl.BlockSpec((tm, 2 * GROUP_WIDTH), lambda i: (i, 0)),
                   pl.BlockSpec((tm, LANES), lambda i: (i, 0))],
        out_shape=[jax.ShapeDtypeStruct((m, 2 * GROUP_WIDTH), BF16),
                   jax.ShapeDtypeStruct((m, LANES), BF16)],
        compiler_params=_cparams(("parallel",)),
        name="mla_kv",
    )(proj, g.reshape(1, -1).astype(F32), w, kr, cos, sin)


def _split3(x):
    x1 = x.astype(BF16)
    r1 = x - x1.astype(F32)
    x2 = r1.astype(BF16)
    x3 = (r1 - x2.astype(F32)).astype(BF16)
    return x1, x2, x3


def _place(lane, cols, base, fill):
    out = fill
    for i, c in enumerate(cols):
        out = jnp.where(lane == base + i, c.astype(F32), out)
    return out


def _forget_cum_kernel(f_ref, b_ref, cum_ref, kx_ref, *, chunk):
    r = lax.broadcasted_iota(jnp.int32, (chunk, chunk), 0)
    c = lax.broadcasted_iota(jnp.int32, (chunk, chunk), 1)
    tri = jnp.where(r >= c, 1.0, 0.0).astype(BF16)
    lane = lax.broadcasted_iota(jnp.int32, (chunk, LANES), 1)
    unit_fill = jnp.where(lane < OFFSET_PARTS, 1.0, 0.0)

    def body(ci, carry):
        rows = _rows(ci, chunk)
        x = f_ref[rows, :] + b_ref[...]
        logf = (jnp.minimum(x, 0.0) - jnp.log1p(jnp.exp(-jnp.abs(x)))) * LOG2E
        parts = _split3(logf)
        cs = carry + _dot(tri, parts[0]) + _dot(tri, parts[1]) + _dot(tri, parts[2])
        cum_ref[rows, :] = cs
        for h in range(GROUP_HEADS):
            kx_ref[h, rows, :] = _place(lane, _split3(cs[:, h:h + 1]), OFFSET_PARTS,
                                        unit_fill).astype(kx_ref.dtype)
        return cs[chunk - 1:chunk, :]

    lax.fori_loop(0, f_ref.shape[0] // chunk, body, jnp.zeros((1, LANES), F32))


def _forget_cum(side, b_f, batch, seq):
    f_block = 2
    return pl.pallas_call(
        functools.partial(_forget_cum_kernel, chunk=256),
        grid=(batch,),
        in_specs=[pl.BlockSpec((seq, LANES), lambda b: (b, f_block)),
                  pl.BlockSpec((1, LANES), lambda b: (0, 0))],
        out_specs=[pl.BlockSpec((seq, LANES), lambda b: (b, 0)),
                   pl.BlockSpec((None, GROUP_HEADS, seq, LANES), lambda b: (b, 0, 0, 0))],
        out_shape=[jax.ShapeDtypeStruct((batch * seq, LANES), F32),
                   jax.ShapeDtypeStruct((batch, GROUP_HEADS, seq, LANES), BF16)],
        compiler_params=_cparams(("parallel",)),
        name="forget_cum",
    )(side, b_f)


def _tile_iota(t):
    return (lax.broadcasted_iota(jnp.int32, (t, t), 0),
            lax.broadcasted_iota(jnp.int32, (t, t), 1))


def _diag_mask(t, w):
    row = lax.broadcasted_iota(jnp.int32, (t, w), 0)
    col = lax.broadcasted_iota(jnp.int32, (t, w), 1)
    return col <= row + (w - t)


def _softmax_update(s, v, m, acc):
    m_new = jnp.maximum(m, jnp.max(s, axis=-1, keepdims=True))
    alpha = jnp.exp2(m - m_new)
    p = jnp.exp2(s - m_new).astype(BF16)
    v_aug = jnp.concatenate([v, jnp.ones_like(v)], axis=1)
    return m_new, alpha * acc + _dot(p, v_aug)


def _softmax_init(t):
    return (jnp.full((t, 1), NEG_INF, F32), jnp.zeros((t, 2 * HEAD_DIM), F32))


def _softmax_out(acc):
    return acc[:, :HEAD_DIM] / acc[:, HEAD_DIM:]


def _hs(h):
    return slice(h * HEAD_DIM, (h + 1) * HEAD_DIM)


def _rows(j, w):
    return pl.ds(pl.multiple_of(j * w, w), w)


def _update(s, v, state, mask, acc_ref, chain):
    if mask is not None:
        s = jnp.where(mask, s, NEG_INF)
    if acc_ref is None:
        return _softmax_update(s, v, *state)
    v_aug = jnp.concatenate([v, jnp.ones_like(v)], axis=1)
    acc_ref[chain] += _dot(jnp.exp2(s).astype(BF16), v_aug)
    return None


def _run_softmax(qi, t, n_chains, make_step, finish, acc_ref, all_zero=None):
    acc_ref[...] = jnp.zeros_like(acc_ref)
    fast = make_step(acc_ref)

    def wide(j, carry):
        if all_zero is None:
            fast(j, None, False, 2 * t)
        else:
            @pl.when(jnp.logical_not(all_zero(j, 2 * t)))
            def _():
                fast(j, None, False, 2 * t)
        return carry

    lax.fori_loop(0, qi // 2, wide, 0)

    @pl.when(qi % 2 == 1)
    def _():
        fast(qi // 2, None, True, 2 * t)

    @pl.when(qi % 2 == 0)
    def _():
        fast(qi, None, True, t)

    accs = [acc_ref[c] for c in range(n_chains)]
    l_min = accs[0][:, HEAD_DIM:HEAD_DIM + 1]
    for a in accs[1:]:
        l_min = jnp.minimum(l_min, a[:, HEAD_DIM:HEAD_DIM + 1])
    ok = jnp.min(l_min) >= TRUSTED_ROW_SUM

    @pl.when(ok)
    def _():
        finish([_softmax_out(a) for a in accs])

    @pl.when(jnp.logical_not(ok))
    def _():
        slow = make_step(None)
        init = tuple(_softmax_init(t) for _ in range(n_chains))
        carry = lax.fori_loop(0, qi, lambda j, c: slow(j, c, False, t), init)
        finish([_softmax_out(c[1]) for c in slow(qi, carry, True, t)])


def _key_norm_max(k_ref, h, seq, extra_ref=None):
    chunk = ATT_T

    def body(c, mx):
        kk = k_ref[_rows(c, chunk), _hs(h)].astype(F32)
        n2 = jnp.sum(kk * kk, axis=1, keepdims=True)
        if extra_ref is not None:
            e = extra_ref[_rows(c, chunk), :].astype(F32)
            n2 = n2 + jnp.sum(e * e, axis=1, keepdims=True)
        return jnp.maximum(mx, n2)

    mx = lax.fori_loop(0, seq // chunk, body, jnp.zeros((chunk, 1), F32))
    return jnp.sqrt(jnp.max(mx, axis=0, keepdims=True))


def _store_key_norms(kmax_ref, qi, norm_fn):
    @pl.when(qi == 0)
    def _():
        for h in range(kmax_ref.shape[0]):
            kmax_ref[h] = jnp.broadcast_to(norm_fn(h), kmax_ref.shape[1:])


def _row_norm(x):
    xf = x.astype(F32)
    return jnp.sqrt(jnp.sum(xf * xf, axis=1, keepdims=True))


def _fox_kernel(q_ref, k_ref, v_ref, kx_ref, cumq_ref, cumk_ref, o_ref, kmax_ref, acc_ref, *, t, seq):
    group, qi = pl.program_id(1), pl.program_id(2)
    hps = kmax_ref.shape[0]
    _store_key_norms(kmax_ref, qi, lambda h: _key_norm_max(k_ref, h, seq))
    lane = lax.broadcasted_iota(jnp.int32, (t, LANES), 1)
    minus_fill = jnp.where((lane >= OFFSET_PARTS) & (lane < 2 * OFFSET_PARTS), -1.0, 0.0)
    cumq = cumq_ref[...]

    def all_zero(j, w):
        head_lane = lax.broadcasted_iota(jnp.int32, (1, LANES), 1)
        mine = (head_lane >= group * hps) & (head_lane < (group + 1) * hps)
        gap = cumq_ref[0:1, :] - cumk_ref[pl.ds((j + 1) * w - 1, 1), :]
        return jnp.max(jnp.where(mine, gap, NEG_INF)) < DEAD_LOG2

    def q_aug(h, fast):
        q = q_ref[:, _hs(h)]
        c = jnp.sum(jnp.where(lane == group * hps + h, cumq, 0.0), axis=1, keepdims=True)
        if fast:
            c = c - _row_norm(q) * kmax_ref[h][0:1, 0:1]
        return jnp.concatenate([q, _place(lane, _split3(c), 0, minus_fill).astype(BF16)], axis=1)

    def make_step(acc_ref):
        fast = acc_ref is not None
        qs = [q_aug(h, fast) for h in range(hps)]

        def step(j, carry, masked, w):
            mask = _diag_mask(t, w) if masked else None
            out = []
            for h in range(hps):
                k = jnp.concatenate([k_ref[_rows(j, w), _hs(h)], kx_ref[h, _rows(j, w), :]], axis=1)
                out.append(_update(_dot_nt(qs[h], k), v_ref[_rows(j, w), _hs(h)], None if fast else carry[h], mask, acc_ref, h))
            return tuple(out)

        return step

    def finish(outs):
        o_ref[...] = jnp.concatenate(outs, axis=1).astype(o_ref.dtype)

    _run_softmax(qi, t, hps, make_step, finish, acc_ref, all_zero)


def _mla_kernel(qn_ref, qp_ref, kn_ref, kp_ref, v_ref, o_ref, kmax_ref, acc_ref, *, t, seq):
    qi = pl.program_id(2)
    hps = kmax_ref.shape[0]
    _store_key_norms(kmax_ref, qi, lambda h: _key_norm_max(kn_ref, h, seq, kp_ref))
    lane = lax.broadcasted_iota(jnp.int32, (t, LANES), 1)

    def q_aug(h, fast):
        qn, qp = qn_ref[:, _hs(h)], qp_ref[:, _hs(h)]
        if fast:
            qn_f, qp_f = qn.astype(F32), qp.astype(F32)
            norm = jnp.sqrt(jnp.sum(qn_f * qn_f, axis=1, keepdims=True)
                            + jnp.sum(qp_f * qp_f, axis=1, keepdims=True))
            qp = _place(lane, _split3(-norm * kmax_ref[h][0:1, 0:1]), MLA_ROPE, qp_f).astype(BF16)
        return jnp.concatenate([qn, qp], axis=1)

    def make_step(acc_ref):
        fast = acc_ref is not None
        qs = [q_aug(h, fast) for h in range(hps)]

        def step(j, carry, masked, w):
            mask = _diag_mask(t, w) if masked else None
            kp = kp_ref[_rows(j, w), :]
            out = []
            for h in range(hps):
                k = jnp.concatenate([kn_ref[_rows(j, w), _hs(h)], kp], axis=1)
                out.append(_update(_dot_nt(qs[h], k), v_ref[_rows(j, w), _hs(h)], None if fast else carry[h], mask, acc_ref, h))
            return tuple(out)

        return step

    def finish(outs):
        o_ref[...] = jnp.concatenate(outs, axis=1).astype(o_ref.dtype)

    _run_softmax(qi, t, hps, make_step, finish, acc_ref)


def _diff_kernel(lq1_ref, lk1_ref, lq2_ref, lk2_ref, q_ref, k_ref, v_ref, posq_ref, posk_ref,
                 bias_ref, subg_ref, o_ref, kmax_ref, acc_ref, *, t, seq, lam_init):
    qi = pl.program_id(2)
    _store_key_norms(kmax_ref, qi, lambda h: _key_norm_max(k_ref, h, seq))
    lam = (jnp.exp(jnp.sum(lq1_ref[...] * lk1_ref[...], axis=-1, keepdims=True))
           - jnp.exp(jnp.sum(lq2_ref[...] * lk2_ref[...], axis=-1, keepdims=True)) + lam_init)
    lane = lax.broadcasted_iota(jnp.int32, (t, LANES), 1)
    zero_fill = jnp.zeros((t, LANES), F32)

    def key_units(w):
        unit = lax.broadcasted_iota(jnp.int32, (w, LANES), 1) < OFFSET_PARTS
        return jnp.where(unit, 1.0, 0.0).astype(BF16)

    posq = posq_ref[...]
    posq_min = jnp.min(posq)
    bias_rows = [jnp.broadcast_to(bias_ref[h], (t, LANES)) for h in range(HEADS_PER_STEP)]
    last = REL_MAX_DIST - 1
    far_bias = [bias_ref[h][:, last:] for h in range(HEADS_PER_STEP)]

    def half_q(h, c):
        q = q_ref[:, _hs(h)]
        return jnp.where((lane < DIFF_QK) if c == 0 else (lane >= DIFF_QK), q, jnp.zeros_like(q))

    def make_step(acc_ref):
        fast = acc_ref is not None
        q_near, q_far = [], []
        for h in range(HEADS_PER_STEP):
            for c in range(2):
                qz = half_q(h, c)
                if fast:
                    bound = (_row_norm(qz) * kmax_ref[h][0:1, 0:1]
                             + jnp.max(bias_ref[h], axis=1, keepdims=True))
                    q_near.append(jnp.concatenate(
                        [qz, _place(lane, _split3(-bound), 0, zero_fill).astype(BF16)], axis=1))
                    q_far.append(jnp.concatenate(
                        [qz, _place(lane, _split3(far_bias[h] - bound), 0, zero_fill).astype(BF16)], axis=1))
                else:
                    q_near.append(qz)
                    q_far.append(qz)

        def chains(j, carry, masked, w, gathered):
            mask = _diag_mask(t, w) if masked else None
            out = []
            for h in range(HEADS_PER_STEP):
                k = k_ref[_rows(j, w), _hs(h)]
                if fast:
                    k = jnp.concatenate([k, key_units(w)], axis=1)
                v = v_ref[_rows(j, w), _hs(h)]
                for c in range(2):
                    i = 2 * h + c
                    if gathered is not None:
                        s = _dot_nt(q_near[i], k) + gathered[h]
                    elif fast:
                        s = _dot_nt(q_far[i], k)
                    else:
                        s = _dot_nt(q_far[i], k) + far_bias[h]
                    out.append(_update(s, v, None if fast else carry[i], mask, acc_ref, i))
            return tuple(out)

        def step(j, carry, masked, w):
            posk = posk_ref[:, _rows(j, w)]

            def near(c):
                dist = jnp.clip(posq - posk, 0, last)
                gathered = [jnp.concatenate(
                    [jnp.take_along_axis(bias_rows[h], dist[:, cc * LANES:(cc + 1) * LANES], axis=1,
                                         mode="promise_in_bounds")
                     for cc in range(w // LANES)], axis=1) for h in range(HEADS_PER_STEP)]
                return chains(j, c, masked, w, gathered)

            def far(c):
                return chains(j, c, masked, w, None)

            return lax.cond(posq_min - jnp.max(posk) < last, near, far, carry)

        return step

    def finish(outs):
        normed = []
        for h in range(HEADS_PER_STEP):
            o = outs[2 * h] - lam * outs[2 * h + 1]
            normed.append(_rms(o, subg_ref[...]) * (1.0 - lam_init))
        o_ref[...] = jnp.concatenate(normed, axis=1).astype(o_ref.dtype)

    _run_softmax(qi, t, 2 * HEADS_PER_STEP, make_step, finish, acc_ref)


def _sb_kernel(q_ref, k_ref, v_ref, o_ref, *, t):
    qi = pl.program_id(2)
    w = t
    row, col = _tile_iota(t)
    later = jnp.where(row > col, 1.0, 0.0).astype(BF16)
    qs = [q_ref[:, _hs(h)] for h in range(SB_HEADS_PER_STEP)]

    def step(j, carry, masked):
        out = []
        for h in range(SB_HEADS_PER_STEP):
            tail, acc = carry[h]
            z = _dot_nt(qs[h], k_ref[_rows(j, w), _hs(h)])
            sp = jnp.log(1.0 + jnp.exp2(-jnp.abs(z))) * LOG2E
            log_beta = jnp.minimum(z, 0.0) - sp
            log_1m = -jnp.maximum(z, 0.0) - sp
            if masked:
                log_1m = jnp.where(col < row, log_1m, 0.0)
            hi = log_1m.astype(BF16)
            lo = (log_1m - hi.astype(F32)).astype(BF16)
            both = _dot(jnp.concatenate([hi, lo], axis=0), later)
            suffix = both[:t] + both[t:]
            a = jnp.exp2(log_beta + suffix + tail)
            if masked:
                a = jnp.where(col < row, a, 0.0)
            acc = acc + _dot(a.astype(BF16), v_ref[_rows(j, w), _hs(h)])
            tail = tail + suffix[:, :1] + log_1m[:, :1]
            out.append((tail, acc))
        return tuple(out)

    def live(carry):
        worst = carry[0][0]
        for h in range(1, SB_HEADS_PER_STEP):
            worst = jnp.maximum(worst, carry[h][0])
        return (jnp.max(worst) > DEAD_LOG2).astype(jnp.int32)

    init = tuple((jnp.zeros((t, 1), F32), jnp.zeros((t, HEAD_DIM), F32)) for _ in range(SB_HEADS_PER_STEP))
    carry = step(qi, init, True)

    def body(state):
        j, _, c = state
        c = step(j, c, False)
        return j - 1, live(c), c

    _, _, carry = lax.while_loop(lambda st: (st[0] >= 0) & (st[1] > 0), body,
                                 (qi - 1, live(carry), carry))
    o_ref[...] = jnp.concatenate([c[1] for c in carry], axis=1).astype(o_ref.dtype)


def _attn_call(kernel, t, batch, seq, in_arrays, in_specs, name, hps=HEADS_PER_STEP, softmax_chains=0):
    nq = seq // t
    width = hps * HEAD_DIM
    scratch = []
    if softmax_chains:
        scratch = [pltpu.VMEM((hps, SUBLANES, LANES), F32),
                   pltpu.VMEM((softmax_chains, t, 2 * HEAD_DIM), F32)]
    return pl.pallas_call(
        kernel,
        grid=(batch, GROUP_HEADS // hps, nq),
        in_specs=in_specs,
        out_specs=pl.BlockSpec((t, width), lambda b, h, i: (b * nq + i, h)),
        out_shape=jax.ShapeDtypeStruct((batch * seq, GROUP_WIDTH), BF16),
        scratch_shapes=scratch,
        compiler_params=_cparams(("arbitrary", "arbitrary", "arbitrary")),
        name=name,
    )(*in_arrays)


def _q_spec(t, seq, col0, hps=HEADS_PER_STEP):
    nq = seq // t
    return pl.BlockSpec((t, hps * HEAD_DIM), lambda b, h, i: (b * nq + i, col0 + h))


def _kv_spec(seq, col0, hps=HEADS_PER_STEP):
    return pl.BlockSpec((seq, hps * HEAD_DIM), lambda b, h, i: (b, col0 + h))


def _shared_spec(seq):
    return pl.BlockSpec((seq, HEAD_DIM), lambda b, h, i: (b, 0))


def _full_spec(shape):
    return pl.BlockSpec(shape, lambda b, h, i: (0,) * len(shape))


def _mem_kernel(x_ref, g_ref, wq_ref, k_ref, v_ref, wo_ref, o_ref):
    x = x_ref[...]
    h = _rms(x, g_ref[...]).astype(BF16)
    q = _dot(h, wq_ref[...]).astype(BF16)
    outs = []
    for hd in range(MEM_HEADS):
        sl = slice(hd * MEM_HEAD_DIM, (hd + 1) * MEM_HEAD_DIM)
        s = _dot_nt(q[:, sl], k_ref[:, sl])
        p = jnp.exp2(s - jnp.max(s, axis=-1, keepdims=True))
        o = _dot(p.astype(BF16), v_ref[:, sl]) / jnp.sum(p, axis=-1, keepdims=True)
        outs.append(o.astype(BF16))
    o_ref[...] = x + _dot(jnp.concatenate(outs, axis=-1), wo_ref[...])


def _mem_block(x, g, wq, k, v, wo, layer, batch, seq, tm=256):
    m, d = x.shape
    n_mem = k.shape[0] // batch
    tiles_per_batch = seq // tm
    width = MEM_HEADS * MEM_HEAD_DIM
    return pl.pallas_call(
        _mem_kernel,
        grid=(m // tm,),
        in_specs=[pl.BlockSpec((tm, d), lambda i: (i, 0)),
                  pl.BlockSpec((1, d), lambda i: (0, 0)),
                  _w_spec(wq, layer, (d, width), lambda i: (0, 0)),
                  pl.BlockSpec((n_mem, width), lambda i: (i // tiles_per_batch, 0)),
                  pl.BlockSpec((n_mem, width), lambda i: (i // tiles_per_batch, 0)),
                  _w_spec(wo, layer, (width, d), lambda i: (0, 0))],
        out_specs=pl.BlockSpec((tm, d), lambda i: (i, 0)),
        out_shape=jax.ShapeDtypeStruct((m, d), F32),
        compiler_params=_cparams(("parallel",)),
        name="mem_block",
    )(x, g.reshape(1, d).astype(F32), wq, k, v, wo)


def _rel_bias_by_distance(rel_table):
    d = jnp.arange(REL_MAX_DIST, dtype=jnp.int32)
    max_exact = REL_BUCKETS // 2
    nf = jnp.maximum(d, 1).astype(F32)
    large = max_exact + (jnp.log(nf / max_exact) / math.log(REL_MAX_DIST / max_exact)
                         * (REL_BUCKETS - max_exact)).astype(jnp.int32)
    large = jnp.minimum(large, REL_BUCKETS - 1)
    bucket = jnp.where(d < max_exact, d, large)
    return jnp.take(rel_table, bucket, axis=0).T.astype(F32)


def _in_proj_col_scale():
    gw = GROUP_WIDTH
    scale = np.ones((PROJ_COLS,), np.float32)
    scale[PROJ_A0:PROJ_A0 + gw] = DIFF_QK ** -0.5 * LOG2E
    scale[PROJ_C0:PROJ_C0 + gw] = HEAD_DIM ** -0.5 * LOG2E
    scale[PROJ_D0:PROJ_D0 + gw] = HEAD_DIM ** -0.5 * LOG2E
    return scale


PREP_CHUNK = PROJ_ALIGN
PREP_ALIGNED = PROJ_C0 // PREP_CHUNK
PREP_SKEW = PREP_CHUNK - (PROJ_C0 - (A_COLS + B_COLS))


def _regroup_kernel(x_ref, tail_ref, s_ref, o_ref):
    c = pl.program_id(1)

    @pl.when(c < PREP_ALIGNED)
    def _():
        o_ref[...] = (x_ref[...].T * s_ref[...]).astype(o_ref.dtype)

    @pl.when(c >= PREP_ALIGNED)
    def _():
        x = jnp.concatenate([x_ref[PREP_SKEW:, :], tail_ref[...]], axis=0)
        o_ref[...] = (x.T * s_ref[...]).astype(o_ref.dtype)


def _regroup_in_proj(w_in_t, layer, tr=2048):
    d = w_in_t.shape[2]
    tails_per_chunk = PREP_CHUNK // PREP_SKEW
    scale = jnp.asarray(_in_proj_col_scale()).reshape(1, PROJ_COLS)
    return pl.pallas_call(
        _regroup_kernel,
        grid=(d // tr, PROJ_COLS // PREP_CHUNK),
        in_specs=[
            pl.BlockSpec((None, PREP_CHUNK, tr),
                         lambda r, c: (layer, jnp.where(c < PREP_ALIGNED, c, c - 1), r)),
            pl.BlockSpec((None, PREP_SKEW, tr),
                         lambda r, c: (layer, jnp.where(c < PREP_ALIGNED, 0, c * tails_per_chunk), r)),
            pl.BlockSpec((1, PREP_CHUNK), lambda r, c: (0, c))],
        out_specs=pl.BlockSpec((tr, PREP_CHUNK), lambda r, c: (r, c)),
        out_shape=jax.ShapeDtypeStruct((d, PROJ_COLS), BF16),
        compiler_params=_cparams(("parallel", "parallel")),
        name="regroup_in_proj",
    )(w_in_t, w_in_t, scale)


SIDE_COLS = 4 * LANES


def _side_proj_kernel(h_ref, kr_ref, f_ref, o_ref):
    kr, f = kr_ref[...], f_ref[...]
    half = MLA_ROPE // 2

    def zeros(n):
        return jnp.zeros((n, kr.shape[1]), kr.dtype)

    w_t = jnp.concatenate([kr, zeros(LANES - MLA_ROPE), kr[half:], kr[:half], zeros(LANES - MLA_ROPE),
                           f, zeros(2 * LANES - GROUP_HEADS)], axis=0)
    o_ref[...] = _dot_nt(h_ref[...], w_t.astype(h_ref.dtype))


def _side_proj(h, w_in_t, layer, tm=1024):
    m, d = h.shape
    kr_row0 = A_COLS + MLA_Q_RANK + MLA_KV_RANK
    f_row0 = A_COLS + B_COLS + C_COLS + 3 * GROUP_WIDTH
    return pl.pallas_call(
        _side_proj_kernel,
        grid=(m // tm,),
        in_specs=[pl.BlockSpec((tm, d), lambda i: (i, 0)),
                  pl.BlockSpec((None, MLA_ROPE, d), lambda i: (layer, kr_row0 // MLA_ROPE, 0)),
                  pl.BlockSpec((None, GROUP_HEADS, d), lambda i: (layer, f_row0 // GROUP_HEADS, 0))],
        out_specs=pl.BlockSpec((tm, SIDE_COLS), lambda i: (i, 0)),
        out_shape=jax.ShapeDtypeStruct((m, SIDE_COLS), F32),
        compiler_params=_cparams(("parallel",)),
        name="in_proj_side",
    )(h, w_in_t, w_in_t)


def _prep_uq(w):
    qk = MLA_NOPE + MLA_ROPE
    half = MLA_ROPE // 2
    w = (w * (qk ** -0.5 * LOG2E)).reshape(w.shape[0], GROUP_HEADS, qk)
    pad = jnp.zeros((w.shape[0], GROUP_HEADS, LANES - MLA_ROPE), w.dtype)
    nope = w[:, :, :MLA_NOPE]
    r1 = w[:, :, MLA_NOPE:MLA_NOPE + half]
    r2 = w[:, :, MLA_NOPE + half:]
    rope = jnp.concatenate([r1, r2, pad], axis=-1)
    swapped = jnp.concatenate([r2, r1, pad], axis=-1)
    return jnp.concatenate([nope.reshape(w.shape[0], -1), rope.reshape(w.shape[0], -1),
                            swapped.reshape(w.shape[0], -1)], axis=1).astype(BF16)


def _prep_ukv(w):
    w = w.reshape(w.shape[0], GROUP_HEADS, MLA_NOPE + HEAD_DIM)
    return jnp.concatenate([w[:, :, :MLA_NOPE].reshape(w.shape[0], -1),
                            w[:, :, MLA_NOPE:].reshape(w.shape[0], -1)], axis=1).astype(BF16)


def kernel(x, mem, positions, attn_norm_g, w_in, w_out, rel_table, diff_lam_q1, diff_lam_k1, diff_lam_q2, diff_lam_k2, diff_sub_g, mla_q_norm_g, mla_kv_norm_g, mla_w_uq, mla_w_ukv, fox_b_f, mem_q_norm_g, mem_kv_norm_g, mem_w_q, mem_w_k, mem_w_v, mem_w_o, mlp_norm_g, w_up, w_down, final_norm_g):
    batch, seq, d = x.shape
    m = batch * seq
    gw = GROUP_WIDTH
    nh = GROUP_HEADS
    xf = x.reshape(m, d)
    mem2 = mem.reshape(batch * mem.shape[1], d)
    pos_col = positions.reshape(m, 1).astype(jnp.int32)
    pos_row = positions.reshape(batch, 1, seq).astype(jnp.int32)
    cos, sin = _rope_tables(pos_col)
    bias_by_dist = (_rel_bias_by_distance(rel_table) * LOG2E).reshape(nh, 1, LANES)
    npair = nh // HEADS_PER_STEP
    pair_cols = HEADS_PER_STEP * HEAD_DIM
    a0 = PROJ_A0 // pair_cols
    nwide = nh // WIDE_HEADS_PER_STEP
    fox0 = PROJ_D0 // (WIDE_HEADS_PER_STEP * HEAD_DIM)
    nsb = nh // SB_HEADS_PER_STEP
    sb0 = PROJ_C0 // (SB_HEADS_PER_STEP * HEAD_DIM)

    w_out_b, w_up_b, w_down_b = w_out.astype(BF16), w_up.astype(BF16), w_down.astype(BF16)
    mem_wq_b = (mem_w_q * (MEM_HEAD_DIM ** -0.5 * LOG2E)).astype(BF16)
    mem_wk_b, mem_wv_b, mem_wo_b = mem_w_k.astype(BF16), mem_w_v.astype(BF16), mem_w_o.astype(BF16)

    w_in_t = jnp.swapaxes(w_in, 1, 2)

    for l in range(DEPTH):
        w_main = _regroup_in_proj(w_in_t, l)
        h = _rmsnorm(xf, attn_norm_g[l], BF16)
        proj = _matmul(h, w_main, BF16, name="in_proj")
        side = _side_proj(h, w_in_t, l)
        b_f = jnp.concatenate([fox_b_f[l].astype(F32), jnp.zeros((LANES - nh,), F32)]).reshape(1, LANES)
        cum, fox_kx = _forget_cum(side, b_f, batch, seq)

        lam_init = 0.8 - 0.6 * math.exp(-0.3 * l)
        lam_vecs = [v[l].reshape(1, DIFF_QK).astype(F32)
                    for v in (diff_lam_q1, diff_lam_k1, diff_lam_q2, diff_lam_k2)]
        o_a = _attn_call(
            functools.partial(_diff_kernel, t=ATT_T, seq=seq, lam_init=lam_init), ATT_T, batch, seq,
            lam_vecs + [proj, proj, proj, pos_col, pos_row, bias_by_dist,
                        diff_sub_g[l].reshape(1, HEAD_DIM).astype(F32)],
            [_full_spec((1, DIFF_QK))] * 4
            + [_q_spec(ATT_T, seq, a0), _kv_spec(seq, a0 + npair), _kv_spec(seq, a0 + 2 * npair),
               pl.BlockSpec((ATT_T, 1), lambda b, h, i: (b * (seq // ATT_T) + i, 0)),
               pl.BlockSpec((None, 1, seq), lambda b, h, i: (b, 0, 0)),
               pl.BlockSpec((HEADS_PER_STEP, 1, LANES), lambda b, h, i: (h, 0, 0)),
               _full_spec((1, HEAD_DIM))],
            "diff_attn", softmax_chains=2 * HEADS_PER_STEP)

        q_b = _mla_q(proj, mla_q_norm_g[l], _prep_uq(mla_w_uq[l]), cos, sin)
        kv_b, kpe_b = _mla_kv(proj, mla_kv_norm_g[l], _prep_ukv(mla_w_ukv[l]), side, cos, sin)
        o_b = _attn_call(
            functools.partial(_mla_kernel, t=ATT_T, seq=seq), ATT_T, batch, seq,
            [q_b, q_b, kv_b, kpe_b, kv_b],
            [_q_spec(ATT_T, seq, 0, WIDE_HEADS_PER_STEP), _q_spec(ATT_T, seq, nwide, WIDE_HEADS_PER_STEP),
             _kv_spec(seq, 0, WIDE_HEADS_PER_STEP), _shared_spec(seq),
             _kv_spec(seq, nwide, WIDE_HEADS_PER_STEP)],
            "mla_attn", hps=WIDE_HEADS_PER_STEP, softmax_chains=WIDE_HEADS_PER_STEP)

        o_c = _attn_call(
            functools.partial(_sb_kernel, t=SB_T), SB_T, batch, seq,
            [proj, proj, proj],
            [_q_spec(SB_T, seq, sb0, SB_HEADS_PER_STEP), _kv_spec(seq, sb0 + nsb, SB_HEADS_PER_STEP),
             _kv_spec(seq, sb0 + 2 * nsb, SB_HEADS_PER_STEP)],
            "stickbreak_attn", hps=SB_HEADS_PER_STEP)

        o_d = _attn_call(
            functools.partial(_fox_kernel, t=ATT_T, seq=seq), ATT_T, batch, seq,
            [proj, proj, proj, fox_kx, cum, cum],
            [_q_spec(ATT_T, seq, fox0, WIDE_HEADS_PER_STEP), _kv_spec(seq, fox0 + nwide, WIDE_HEADS_PER_STEP),
             _kv_spec(seq, fox0 + 2 * nwide, WIDE_HEADS_PER_STEP),
             pl.BlockSpec((None, WIDE_HEADS_PER_STEP, seq, LANES), lambda b, h, i: (b, h, 0, 0)),
             pl.BlockSpec((ATT_T, LANES), lambda b, h, i: (b * (seq // ATT_T) + i, 0)),
             pl.BlockSpec((seq, LANES), lambda b, h, i: (b, 0))],
            "fox_attn", hps=WIDE_HEADS_PER_STEP, softmax_chains=WIDE_HEADS_PER_STEP)

        xf = _out_proj([o_a, o_b, o_c, o_d], w_out_b, l, xf)

        mem_n = _rmsnorm(mem2, mem_kv_norm_g[l], BF16)
        k_m = _matmul(mem_n, mem_wk_b, BF16, layer=l, name="mem_k")
        v_m = _matmul(mem_n, mem_wv_b, BF16, layer=l, name="mem_v")
        xf = _mem_block(xf, mem_q_norm_g[l], mem_wq_b, k_m, v_m, mem_wo_b, l, batch, seq)

        h = _rmsnorm(xf, mlp_norm_g[l], BF16)
        u = _matmul(h, w_up_b, BF16, layer=l, relu2=True, name="mlp_up")
        xf = _matmul(u, w_down_b, F32, layer=l, res=xf, tn=1024, tk=2048, name="mlp_down")

    return _rmsnorm(xf, final_norm_g, F32).reshape(batch, seq, d)
```

```python
import functools
import math

import numpy as np
import jax
import jax.numpy as jnp
from jax import lax
from jax.experimental import pallas as pl
from jax.experimental.pallas import tpu as pltpu

F32 = jnp.float32
BF16 = jnp.bfloat16

DEPTH = 2
HEAD_DIM = 128
GROUP_HEADS = 8
GROUP_WIDTH = GROUP_HEADS * HEAD_DIM
DIFF_QK = HEAD_DIM // 2
MLA_Q_RANK = 1536
MLA_KV_RANK = 512
MLA_NOPE = 128
MLA_ROPE = 64
ROPE_THETA = 10000.0
REL_BUCKETS = 32
REL_MAX_DIST = 128
MEM_HEADS = 4
MEM_HEAD_DIM = 128
NORM_EPS = 1e-6
NEG_INF = -1e30

A_COLS = 3 * GROUP_WIDTH
B_COLS = MLA_Q_RANK + MLA_KV_RANK + MLA_ROPE
C_COLS = 3 * GROUP_WIDTH

PROJ_A0 = 0
PROJ_CQ0 = A_COLS
PROJ_CKV0 = A_COLS + MLA_Q_RANK
PROJ_ALIGN = 512
PROJ_C0 = -(-(A_COLS + B_COLS) // PROJ_ALIGN) * PROJ_ALIGN
PROJ_D0 = PROJ_C0 + C_COLS
PROJ_COLS = PROJ_D0 + 3 * GROUP_WIDTH

LANES = 128
SUBLANES = 8
VMEM_LIMIT = 48 * 1024 * 1024
OFFSET_PARTS = 3

LOG2E = math.log2(math.e)

ATT_T = 512
SB_T = 256
HEADS_PER_STEP = 2
WIDE_HEADS_PER_STEP = 4
SB_HEADS_PER_STEP = 4
DEAD_LOG2 = -160.0
TRUSTED_ROW_SUM = 2.0 ** -60


def _cparams(sem):
    return pltpu.CompilerParams(dimension_semantics=sem, vmem_limit_bytes=VMEM_LIMIT)


def _dot(a, b):
    return jnp.dot(a, b, preferred_element_type=F32)


def _dot_nt(a, b):
    return lax.dot_general(a, b, (((1,), (1,)), ((), ())), preferred_element_type=F32)


def _rms(x, g):
    return x * lax.rsqrt(jnp.mean(x * x, axis=-1, keepdims=True) + NORM_EPS) * g


def _rmsnorm_kernel(x_ref, g_ref, o_ref):
    o_ref[...] = _rms(x_ref[...].astype(F32), g_ref[...]).astype(o_ref.dtype)


def _rmsnorm(x, g, out_dtype, tm=256):
    m, d = x.shape
    return pl.pallas_call(
        _rmsnorm_kernel,
        grid=(m // tm,),
        in_specs=[pl.BlockSpec((tm, d), lambda i: (i, 0)),
                  pl.BlockSpec((1, d), lambda i: (0, 0))],
        out_specs=pl.BlockSpec((tm, d), lambda i: (i, 0)),
        out_shape=jax.ShapeDtypeStruct((m, d), out_dtype),
        compiler_params=_cparams(("parallel",)),
        name="rmsnorm",
    )(x, g.reshape(1, d).astype(F32))


def _mm_epilogue(r, res_ref, o_ref, relu2):
    if relu2:
        r = jnp.square(jnp.maximum(r, 0.0))
    if res_ref is not None:
        r = r + res_ref[...]
    o_ref[...] = r.astype(o_ref.dtype)


def _mm_kernel(a_ref, w_ref, *rest, nk, relu2, has_res):
    res_ref = rest[0] if has_res else None
    o_ref = rest[1] if has_res else rest[0]
    if nk == 1:
        _mm_epilogue(_dot(a_ref[...], w_ref[...].astype(a_ref.dtype)), res_ref, o_ref, relu2)
        return
    acc_ref = rest[-1]
    k = pl.program_id(2)

    @pl.when(k == 0)
    def _():
        acc_ref[...] = _dot(a_ref[...], w_ref[...])

    @pl.when(k > 0)
    def _():
        acc_ref[...] += _dot(a_ref[...], w_ref[...])

    @pl.when(k == nk - 1)
    def _():
        _mm_epilogue(acc_ref[...], res_ref, o_ref, relu2)


def _w_spec(w, layer, block, index_map):
    if w.ndim == 2:
        return pl.BlockSpec(block, index_map)
    return pl.BlockSpec((None,) + block, lambda *g: (layer,) + index_map(*g))


def _matmul(a, w, out_dtype, *, layer=None, res=None, relu2=False, tm=1024, tn=512, tk=4096,
            name="matmul"):
    m, kd = a.shape
    n = w.shape[-1]
    tm, tn, tk = min(tm, m), min(tn, n), min(tk, kd)
    assert m % tm == 0 and n % tn == 0 and kd % tk == 0
    nk = kd // tk
    in_specs = [pl.BlockSpec((tm, tk), lambda i, j, k: (i, k)),
                _w_spec(w, layer, (tk, tn), lambda i, j, k: (k, j))]
    args = [a, w]
    if res is not None:
        in_specs.append(pl.BlockSpec((tm, tn), lambda i, j, k: (i, j)))
        args.append(res)
    return pl.pallas_call(
        functools.partial(_mm_kernel, nk=nk, relu2=relu2, has_res=res is not None),
        grid=(m // tm, n // tn, nk),
        in_specs=in_specs,
        out_specs=pl.BlockSpec((tm, tn), lambda i, j, k: (i, j)),
        out_shape=jax.ShapeDtypeStruct((m, n), out_dtype),
        scratch_shapes=[pltpu.VMEM((tm, tn), F32)] if nk > 1 else [],
        compiler_params=_cparams(("parallel", "parallel", "arbitrary")),
        name=name,
    )(*args)


def _out_proj_kernel(a0, a1, a2, a3, w_ref, res_ref, o_ref):
    acc = res_ref[...]
    for g, a_ref in enumerate((a0, a1, a2, a3)):
        w = w_ref[g * GROUP_WIDTH:(g + 1) * GROUP_WIDTH, :].astype(a_ref.dtype)
        acc = acc + _dot(a_ref[...], w)
    o_ref[...] = acc


def _out_proj(mixes, w, layer, res, tm=1024, tn=512):
    m = res.shape[0]
    n = w.shape[-1]
    a_spec = pl.BlockSpec((tm, GROUP_WIDTH), lambda i, j: (i, 0))
    return pl.pallas_call(
        _out_proj_kernel,
        grid=(m // tm, n // tn),
        in_specs=[a_spec, a_spec, a_spec, a_spec,
                  _w_spec(w, layer, (w.shape[-2], tn), lambda i, j: (0, j)),
                  pl.BlockSpec((tm, tn), lambda i, j: (i, j))],
        out_specs=pl.BlockSpec((tm, tn), lambda i, j: (i, j)),
        out_shape=jax.ShapeDtypeStruct((m, n), F32),
        compiler_params=_cparams(("parallel", "parallel")),
        name="out_proj",
    )(*mixes, w, res)


def _rope_kernel(pos_ref, tab_ref, cos_ref, sin_ref):
    ang = pos_ref[...].astype(F32) * tab_ref[0:1, :]
    cos_ref[...] = jnp.cos(ang) * tab_ref[1:2, :]
    sin_ref[...] = jnp.sin(ang) * tab_ref[2:3, :]


def _rope_tables(pos_col, tm=1024):
    m = pos_col.shape[0]
    half = MLA_ROPE // 2
    inv = ROPE_THETA ** (-jnp.arange(half, dtype=F32) * 2.0 / MLA_ROPE)
    zeros = jnp.zeros((LANES - MLA_ROPE,), F32)
    ones = jnp.ones((half,), F32)
    tab = jnp.stack([jnp.concatenate([inv, inv, zeros]),
                     jnp.concatenate([ones, ones, zeros]),
                     jnp.concatenate([-ones, ones, zeros])]
                    + [jnp.zeros((LANES,), F32)] * 5)
    spec = pl.BlockSpec((tm, LANES), lambda i: (i, 0))
    return pl.pallas_call(
        _rope_kernel,
        grid=(m // tm,),
        in_specs=[pl.BlockSpec((tm, 1), lambda i: (i, 0)),
                  pl.BlockSpec((SUBLANES, LANES), lambda i: (0, 0))],
        out_specs=[spec, spec],
        out_shape=[jax.ShapeDtypeStruct((m, LANES), F32)] * 2,
        compiler_params=_cparams(("parallel",)),
        name="rope_tables",
    )(pos_col, tab)


def _mla_q_kernel(cq_ref, g_ref, w_ref, cos_ref, sin_ref, o_ref):
    n = _rms(cq_ref[...].astype(F32), g_ref[...]).astype(BF16)
    t = _dot(n, w_ref[...])
    o_ref[:, :GROUP_WIDTH] = t[:, :GROUP_WIDTH].astype(o_ref.dtype)
    cos, sin = cos_ref[...], sin_ref[...]
    for h in range(GROUP_HEADS):
        lo = GROUP_WIDTH + h * LANES
        pe = t[:, lo:lo + LANES] * cos + t[:, lo + GROUP_WIDTH:lo + GROUP_WIDTH + LANES] * sin
        o_ref[:, lo:lo + LANES] = pe.astype(o_ref.dtype)


def _mla_q(proj, g, w, cos, sin, tm=512):
    m = proj.shape[0]
    cq_block = PROJ_CQ0 // MLA_Q_RANK
    return pl.pallas_call(
        _mla_q_kernel,
        grid=(m // tm,),
        in_specs=[pl.BlockSpec((tm, MLA_Q_RANK), lambda i: (i, cq_block)),
                  pl.BlockSpec((1, MLA_Q_RANK), lambda i: (0, 0)),
                  pl.BlockSpec(w.shape, lambda i: (0, 0)),
                  pl.BlockSpec((tm, LANES), lambda i: (i, 0)),
                  pl.BlockSpec((tm, LANES), lambda i: (i, 0))],
        out_specs=pl.BlockSpec((tm, 2 * GROUP_WIDTH), lambda i: (i, 0)),
        out_shape=jax.ShapeDtypeStruct((m, 2 * GROUP_WIDTH), BF16),
        compiler_params=_cparams(("parallel",)),
        name="mla_q",
    )(proj, g.reshape(1, -1).astype(F32), w, cos, sin)


def _mla_kv_kernel(ckv_ref, g_ref, w_ref, kr_ref, cos_ref, sin_ref, kv_ref, kpe_ref):
    n = _rms(ckv_ref[...].astype(F32), g_ref[...]).astype(BF16)
    kv_ref[...] = _dot(n, w_ref[...]).astype(kv_ref.dtype)
    kr = kr_ref[...]
    kpe = kr[:, :LANES] * cos_ref[...] + kr[:, LANES:] * sin_ref[...]
    lane = lax.broadcasted_iota(jnp.int32, kpe.shape, 1)
    kpe = jnp.where((lane >= MLA_ROPE) & (lane < MLA_ROPE + OFFSET_PARTS), 1.0, kpe)
    kpe_ref[...] = kpe.astype(kpe_ref.dtype)


def _mla_kv(proj, g, w, kr, cos, sin, tm=512):
    m = proj.shape[0]
    ckv_block = PROJ_CKV0 // MLA_KV_RANK
    return pl.pallas_call(
        _mla_kv_kernel,
        grid=(m // tm,),
        in_specs=[pl.BlockSpec((tm, MLA_KV_RANK), lambda i: (i, ckv_block)),
                  pl.BlockSpec((1, MLA_KV_RANK), lambda i: (0, 0)),
                  pl.BlockSpec(w.shape, lambda i: (0, 0)),
                  pl.BlockSpec((tm, 2 * LANES), lambda i: (i, 0)),
                  pl.BlockSpec((tm, LANES), lambda i: (i, 0)),
                  pl.BlockSpec((tm, LANES), lambda i: (i, 0))],
        out_specs=[pl.BlockSpec((tm, 2 * GROUP_WIDTH), lambda i: (i, 0)),
                   pl.BlockSpec((tm, LANES), lambda i: (i, 0))],
        out_shape=[jax.ShapeDtypeStruct((m, 2 * GROUP_WIDTH), BF16),
                   jax.ShapeDtypeStruct((m, LANES), BF16)],
        compiler_params=_cparams(("parallel",)),
        name="mla_kv",
    )(proj, g.reshape(1, -1).astype(F32), w, kr, cos, sin)


def _split3(x):
    x1 = x.astype(BF16)
    r1 = x - x1.astype(F32)
    x2 = r1.astype(BF16)
    x3 = (r1 - x2.astype(F32)).astype(BF16)
    return x1, x2, x3


def _place(lane, cols, base, fill):
    out = fill
    for i, c in enumerate(cols):
        out = jnp.where(lane == base + i, c.astype(F32), out)
    return out


def _forget_cum_kernel(f_ref, b_ref, cum_ref, kx_ref, *, chunk):
    r = lax.broadcasted_iota(jnp.int32, (chunk, chunk), 0)
    c = lax.broadcasted_iota(jnp.int32, (chunk, chunk), 1)
    tri = jnp.where(r >= c, 1.0, 0.0).astype(BF16)
    lane = lax.broadcasted_iota(jnp.int32, (chunk, LANES), 1)
    unit_fill = jnp.where(lane < OFFSET_PARTS, 1.0, 0.0)

    def body(ci, carry):
        rows = _rows(ci, chunk)
        x = f_ref[rows, :] + b_ref[...]
        logf = (jnp.minimum(x, 0.0) - jnp.log1p(jnp.exp(-jnp.abs(x)))) * LOG2E
        parts = _split3(logf)
        cs = carry + _dot(tri, parts[0]) + _dot(tri, parts[1]) + _dot(tri, parts[2])
        cum_ref[rows, :] = cs
        for h in range(GROUP_HEADS):
            kx_ref[h, rows, :] = _place(lane, _split3(cs[:, h:h + 1]), OFFSET_PARTS,
                                        unit_fill).astype(kx_ref.dtype)
        return cs[chunk - 1:chunk, :]

    lax.fori_loop(0, f_ref.shape[0] // chunk, body, jnp.zeros((1, LANES), F32))


def _forget_cum(side, b_f, batch, seq):
    f_block = 2
    return pl.pallas_call(
        functools.partial(_forget_cum_kernel, chunk=256),
        grid=(batch,),
        in_specs=[pl.BlockSpec((seq, LANES), lambda b: (b, f_block)),
                  pl.BlockSpec((1, LANES), lambda b: (0, 0))],
        out_specs=[pl.BlockSpec((seq, LANES), lambda b: (b, 0)),
                   pl.BlockSpec((None, GROUP_HEADS, seq, LANES), lambda b: (b, 0, 0, 0))],
        out_shape=[jax.ShapeDtypeStruct((batch * seq, LANES), F32),
                   jax.ShapeDtypeStruct((batch, GROUP_HEADS, seq, LANES), BF16)],
        compiler_params=_cparams(("parallel",)),
        name="forget_cum",
    )(side, b_f)


def _tile_iota(t):
    return (lax.broadcasted_iota(jnp.int32, (t, t), 0),
            lax.broadcasted_iota(jnp.int32, (t, t), 1))


def _diag_mask(t, w):
    row = lax.broadcasted_iota(jnp.int32, (t, w), 0)
    col = lax.broadcasted_iota(jnp.int32, (t, w), 1)
    return col <= row + (w - t)


def _softmax_update(s, v, m, acc):
    m_new = jnp.maximum(m, jnp.max(s, axis=-1, keepdims=True))
    alpha = jnp.exp2(m - m_new)
    p = jnp.exp2(s - m_new).astype(BF16)
    v_aug = jnp.concatenate([v, jnp.ones_like(v)], axis=1)
    return m_new, alpha * acc + _dot(p, v_aug)


def _softmax_init(t):
    return (jnp.full((t, 1), NEG_INF, F32), jnp.zeros((t, 2 * HEAD_DIM), F32))


def _softmax_out(acc):
    return acc[:, :HEAD_DIM] / acc[:, HEAD_DIM:]


def _hs(h):
    return slice(h * HEAD_DIM, (h + 1) * HEAD_DIM)


def _rows(j, w):
    return pl.ds(pl.multiple_of(j * w, w), w)


def _update(s, v, state, mask, acc_ref, chain):
    if mask is not None:
        s = jnp.where(mask, s, NEG_INF)
    if acc_ref is None:
        return _softmax_update(s, v, *state)
    v_aug = jnp.concatenate([v, jnp.ones_like(v)], axis=1)
    acc_ref[chain] += _dot(jnp.exp2(s).astype(BF16), v_aug)
    return None


def _run_softmax(qi, t, n_chains, make_step, finish, acc_ref, all_zero=None):
    acc_ref[...] = jnp.zeros_like(acc_ref)
    fast = make_step(acc_ref)

    def wide(j, carry):
        if all_zero is None:
            fast(j, None, False, 2 * t)
        else:
            @pl.when(jnp.logical_not(all_zero(j, 2 * t)))
            def _():
                fast(j, None, False, 2 * t)
        return carry

    lax.fori_loop(0, qi // 2, wide, 0)

    @pl.when(qi % 2 == 1)
    def _():
        fast(qi // 2, None, True, 2 * t)

    @pl.when(qi % 2 == 0)
    def _():
        fast(qi, None, True, t)

    accs = [acc_ref[c] for c in range(n_chains)]
    l_min = accs[0][:, HEAD_DIM:HEAD_DIM + 1]
    for a in accs[1:]:
        l_min = jnp.minimum(l_min, a[:, HEAD_DIM:HEAD_DIM + 1])
    ok = jnp.min(l_min) >= TRUSTED_ROW_SUM

    @pl.when(ok)
    def _():
        finish([_softmax_out(a) for a in accs])

    @pl.when(jnp.logical_not(ok))
    def _():
        slow = make_step(None)
        init = tuple(_softmax_init(t) for _ in range(n_chains))
        carry = lax.fori_loop(0, qi, lambda j, c: slow(j, c, False, t), init)
        finish([_softmax_out(c[1]) for c in slow(qi, carry, True, t)])


def _key_norm_max(k_ref, h, seq, extra_ref=None):
    chunk = ATT_T

    def body(c, mx):
        kk = k_ref[_rows(c, chunk), _hs(h)].astype(F32)
        n2 = jnp.sum(kk * kk, axis=1, keepdims=True)
        if extra_ref is not None:
            e = extra_ref[_rows(c, chunk), :].astype(F32)
            n2 = n2 + jnp.sum(e * e, axis=1, keepdims=True)
        return jnp.maximum(mx, n2)

    mx = lax.fori_loop(0, seq // chunk, body, jnp.zeros((chunk, 1), F32))
    return jnp.sqrt(jnp.max(mx, axis=0, keepdims=True))


def _store_key_norms(kmax_ref, qi, norm_fn):
    @pl.when(qi == 0)
    def _():
        for h in range(kmax_ref.shape[0]):
            kmax_ref[h] = jnp.broadcast_to(norm_fn(h), kmax_ref.shape[1:])


def _row_norm(x):
    xf = x.astype(F32)
    return jnp.sqrt(jnp.sum(xf * xf, axis=1, keepdims=True))


def _fox_kernel(q_ref, k_ref, v_ref, kx_ref, cumq_ref, cumk_ref, o_ref, kmax_ref, acc_ref, *, t, seq):
    group, qi = pl.program_id(1), pl.program_id(2)
    hps = kmax_ref.shape[0]
    _store_key_norms(kmax_ref, qi, lambda h: _key_norm_max(k_ref, h, seq))
    lane = lax.broadcasted_iota(jnp.int32, (t, LANES), 1)
    minus_fill = jnp.where((lane >= OFFSET_PARTS) & (lane < 2 * OFFSET_PARTS), -1.0, 0.0)
    cumq = cumq_ref[...]

    def all_zero(j, w):
        head_lane = lax.broadcasted_iota(jnp.int32, (1, LANES), 1)
        mine = (head_lane >= group * hps) & (head_lane < (group + 1) * hps)
        gap = cumq_ref[0:1, :] - cumk_ref[pl.ds((j + 1) * w - 1, 1), :]
        return jnp.max(jnp.where(mine, gap, NEG_INF)) < DEAD_LOG2

    def q_aug(h, fast):
        q = q_ref[:, _hs(h)]
        c = jnp.sum(jnp.where(lane == group * hps + h, cumq, 0.0), axis=1, keepdims=True)
        if fast:
            c = c - _row_norm(q) * kmax_ref[h][0:1, 0:1]
        return jnp.concatenate([q, _place(lane, _split3(c), 0, minus_fill).astype(BF16)], axis=1)

    def make_step(acc_ref):
        fast = acc_ref is not None
        qs = [q_aug(h, fast) for h in range(hps)]

        def step(j, carry, masked, w):
            mask = _diag_mask(t, w) if masked else None
            out = []
            for h in range(hps):
                k = jnp.concatenate([k_ref[_rows(j, w), _hs(h)], kx_ref[h, _rows(j, w), :]], axis=1)
                out.append(_update(_dot_nt(qs[h], k), v_ref[_rows(j, w), _hs(h)], None if fast else carry[h], mask, acc_ref, h))
            return tuple(out)

        return step

    def finish(outs):
        o_ref[...] = jnp.concatenate(outs, axis=1).astype(o_ref.dtype)

    _run_softmax(qi, t, hps, make_step, finish, acc_ref, all_zero)


def _mla_kernel(qn_ref, qp_ref, kn_ref, kp_ref, v_ref, o_ref, kmax_ref, acc_ref, *, t, seq):
    qi = pl.program_id(2)
    hps = kmax_ref.shape[0]
    _store_key_norms(kmax_ref, qi, lambda h: _key_norm_max(kn_ref, h, seq, kp_ref))
    lane = lax.broadcasted_iota(jnp.int32, (t, LANES), 1)

    def q_aug(h, fast):
        qn, qp = qn_ref[:, _hs(h)], qp_ref[:, _hs(h)]
        if fast:
            qn_f, qp_f = qn.astype(F32), qp.astype(F32)
            norm = jnp.sqrt(jnp.sum(qn_f * qn_f, axis=1, keepdims=True)
                            + jnp.sum(qp_f * qp_f, axis=1, keepdims=True))
            qp = _place(lane, _split3(-norm * kmax_ref[h][0:1, 0:1]), MLA_ROPE, qp_f).astype(BF16)
        return jnp.concatenate([qn, qp], axis=1)

    def make_step(acc_ref):
        fast = acc_ref is not None
        qs = [q_aug(h, fast) for h in range(hps)]

        def step(j, carry, masked, w):
            mask = _diag_mask(t, w) if masked else None
            kp = kp_ref[_rows(j, w), :]
            out = []
            for h in range(hps):
                k = jnp.concatenate([kn_ref[_rows(j, w), _hs(h)], kp], axis=1)
                out.append(_update(_dot_nt(qs[h], k), v_ref[_rows(j, w), _hs(h)], None if fast else carry[h], mask, acc_ref, h))
            return tuple(out)

        return step

    def finish(outs):
        o_ref[...] = jnp.concatenate(outs, axis=1).astype(o_ref.dtype)

    _run_softmax(qi, t, hps, make_step, finish, acc_ref)


def _diff_kernel(lq1_ref, lk1_ref, lq2_ref, lk2_ref, q_ref, k_ref, v_ref, posq_ref, posk_ref,
                 bias_ref, subg_ref, o_ref, kmax_ref, acc_ref, *, t, seq, lam_init):
    qi = pl.program_id(2)
    _store_key_norms(kmax_ref, qi, lambda h: _key_norm_max(k_ref, h, seq))
    lam = (jnp.exp(jnp.sum(lq1_ref[...] * lk1_ref[...], axis=-1, keepdims=True))
           - jnp.exp(jnp.sum(lq2_ref[...] * lk2_ref[...], axis=-1, keepdims=True)) + lam_init)
    lane = lax.broadcasted_iota(jnp.int32, (t, LANES), 1)
    zero_fill = jnp.zeros((t, LANES), F32)

    def key_units(w):
        unit = lax.broadcasted_iota(jnp.int32, (w, LANES), 1) < OFFSET_PARTS
        return jnp.where(unit, 1.0, 0.0).astype(BF16)

    posq = posq_ref[...]
    posq_min = jnp.min(posq)
    bias_rows = [jnp.broadcast_to(bias_ref[h], (t, LANES)) for h in range(HEADS_PER_STEP)]
    last = REL_MAX_DIST - 1
    far_bias = [bias_ref[h][:, last:] for h in range(HEADS_PER_STEP)]

    def half_q(h, c):
        q = q_ref[:, _hs(h)]
        return jnp.where((lane < DIFF_QK) if c == 0 else (lane >= DIFF_QK), q, jnp.zeros_like(q))

    def make_step(acc_ref):
        fast = acc_ref is not None
        q_near, q_far = [], []
        for h in range(HEADS_PER_STEP):
            for c in range(2):
                qz = half_q(h, c)
                if fast:
                    bound = (_row_norm(qz) * kmax_ref[h][0:1, 0:1]
                             + jnp.max(bias_ref[h], axis=1, keepdims=True))
                    q_near.append(jnp.concatenate(
                        [qz, _place(lane, _split3(-bound), 0, zero_fill).astype(BF16)], axis=1))
                    q_far.append(jnp.concatenate(
                        [qz, _place(lane, _split3(far_bias[h] - bound), 0, zero_fill).astype(BF16)], axis=1))
                else:
                    q_near.append(qz)
                    q_far.append(qz)

        def chains(j, carry, masked, w, gathered):
            mask = _diag_mask(t, w) if masked else None
            out = []
            for h in range(HEADS_PER_STEP):
                k = k_ref[_rows(j, w), _hs(h)]
                if fast:
                    k = jnp.concatenate([k, key_units(w)], axis=1)
                v = v_ref[_rows(j, w), _hs(h)]
                for c in range(2):
                    i = 2 * h + c
                    if gathered is not None:
                        s = _dot_nt(q_near[i], k) + gathered[h]
                    elif fast:
                        s = _dot_nt(q_far[i], k)
                    else:
                        s = _dot_nt(q_far[i], k) + far_bias[h]
                    out.append(_update(s, v, None if fast else carry[i], mask, acc_ref, i))
            return tuple(out)

        def step(j, carry, masked, w):
            posk = posk_ref[:, _rows(j, w)]

            def near(c):
                dist = jnp.clip(posq - posk, 0, last)
                gathered = [jnp.concatenate(
                    [jnp.take_along_axis(bias_rows[h], dist[:, cc * LANES:(cc + 1) * LANES], axis=1,
                                         mode="promise_in_bounds")
                     for cc in range(w // LANES)], axis=1) for h in range(HEADS_PER_STEP)]
                return chains(j, c, masked, w, gathered)

            def far(c):
                return chains(j, c, masked, w, None)

            return lax.cond(posq_min - jnp.max(posk) < last, near, far, carry)

        return step

    def finish(outs):
        normed = []
        for h in range(HEADS_PER_STEP):
            o = outs[2 * h] - lam * outs[2 * h + 1]
            normed.append(_rms(o, subg_ref[...]) * (1.0 - lam_init))
        o_ref[...] = jnp.concatenate(normed, axis=1).astype(o_ref.dtype)

    _run_softmax(qi, t, 2 * HEADS_PER_STEP, make_step, finish, acc_ref)


def _sb_kernel(q_ref, k_ref, v_ref, o_ref, *, t):
    qi = pl.program_id(2)
    w = t
    row, col = _tile_iota(t)
    later = jnp.where(row > col, 1.0, 0.0).astype(BF16)
    qs = [q_ref[:, _hs(h)] for h in range(SB_HEADS_PER_STEP)]

    def step(j, carry, masked):
        out = []
        for h in range(SB_HEADS_PER_STEP):
            tail, acc = carry[h]
            z = _dot_nt(qs[h], k_ref[_rows(j, w), _hs(h)])
            sp = jnp.log(1.0 + jnp.exp2(-jnp.abs(z))) * LOG2E
            log_beta = jnp.minimum(z, 0.0) - sp
            log_1m = -jnp.maximum(z, 0.0) - sp
            if masked:
                log_1m = jnp.where(col < row, log_1m, 0.0)
            hi = log_1m.astype(BF16)
            lo = (log_1m - hi.astype(F32)).astype(BF16)
            both = _dot(jnp.concatenate([hi, lo], axis=0), later)
            suffix = both[:t] + both[t:]
            a = jnp.exp2(log_beta + suffix + tail)
            if masked:
                a = jnp.where(col < row, a, 0.0)
            acc = acc + _dot(a.astype(BF16), v_ref[_rows(j, w), _hs(h)])
            tail = tail + suffix[:, :1] + log_1m[:, :1]
            out.append((tail, acc))
        return tuple(out)

    def live(carry):
        worst = carry[0][0]
        for h in range(1, SB_HEADS_PER_STEP):
            worst = jnp.maximum(worst, carry[h][0])
        return (jnp.max(worst) > DEAD_LOG2).astype(jnp.int32)

    init = tuple((jnp.zeros((t, 1), F32), jnp.zeros((t, HEAD_DIM), F32)) for _ in range(SB_HEADS_PER_STEP))
    carry = step(qi, init, True)

    def body(state):
        j, _, c = state
        c = step(j, c, False)
        return j - 1, live(c), c

    _, _, carry = lax.while_loop(lambda st: (st[0] >= 0) & (st[1] > 0), body,
                                 (qi - 1, live(carry), carry))
    o_ref[...] = jnp.concatenate([c[1] for c in carry], axis=1).astype(o_ref.dtype)


def _attn_call(kernel, t, batch, seq, in_arrays, in_specs, name, hps=HEADS_PER_STEP, softmax_chains=0):
    nq = seq // t
    width = hps * HEAD_DIM
    scratch = []
    if softmax_chains:
        scratch = [pltpu.VMEM((hps, SUBLANES, LANES), F32),
                   pltpu.VMEM((softmax_chains, t, 2 * HEAD_DIM), F32)]
    return pl.pallas_call(
        kernel,
        grid=(batch, GROUP_HEADS // hps, nq),
        in_specs=in_specs,
        out_specs=pl.BlockSpec((t, width), lambda b, h, i: (b * nq + i, h)),
        out_shape=jax.ShapeDtypeStruct((batch * seq, GROUP_WIDTH), BF16),
        scratch_shapes=scratch,
        compiler_params=_cparams(("arbitrary", "arbitrary", "arbitrary")),
        name=name,
    )(*in_arrays)


def _q_spec(t, seq, col0, hps=HEADS_PER_STEP):
    nq = seq // t
    return pl.BlockSpec((t, hps * HEAD_DIM), lambda b, h, i: (b * nq + i, col0 + h))


def _kv_spec(seq, col0, hps=HEADS_PER_STEP):
    return pl.BlockSpec((seq, hps * HEAD_DIM), lambda b, h, i: (b, col0 + h))


def _shared_spec(seq):
    return pl.BlockSpec((seq, HEAD_DIM), lambda b, h, i: (b, 0))


def _full_spec(shape):
    return pl.BlockSpec(shape, lambda b, h, i: (0,) * len(shape))


def _mem_kernel(x_ref, g_ref, wq_ref, k_ref, v_ref, wo_ref, o_ref):
    x = x_ref[...]
    h = _rms(x, g_ref[...]).astype(BF16)
    q = _dot(h, wq_ref[...]).astype(BF16)
    outs = []
    for hd in range(MEM_HEADS):
        sl = slice(hd * MEM_HEAD_DIM, (hd + 1) * MEM_HEAD_DIM)
        s = _dot_nt(q[:, sl], k_ref[:, sl])
        p = jnp.exp2(s - jnp.max(s, axis=-1, keepdims=True))
        o = _dot(p.astype(BF16), v_ref[:, sl]) / jnp.sum(p, axis=-1, keepdims=True)
        outs.append(o.astype(BF16))
    o_ref[...] = x + _dot(jnp.concatenate(outs, axis=-1), wo_ref[...])


def _mem_block(x, g, wq, k, v, wo, layer, batch, seq, tm=256):
    m, d = x.shape
    n_mem = k.shape[0] // batch
    tiles_per_batch = seq // tm
    width = MEM_HEADS * MEM_HEAD_DIM
    return pl.pallas_call(
        _mem_kernel,
        grid=(m // tm,),
        in_specs=[pl.BlockSpec((tm, d), lambda i: (i, 0)),
                  pl.BlockSpec((1, d), lambda i: (0, 0)),
                  _w_spec(wq, layer, (d, width), lambda i: (0, 0)),
                  pl.BlockSpec((n_mem, width), lambda i: (i // tiles_per_batch, 0)),
                  pl.BlockSpec((n_mem, width), lambda i: (i // tiles_per_batch, 0)),
                  _w_spec(wo, layer, (width, d), lambda i: (0, 0))],
        out_specs=pl.BlockSpec((tm, d), lambda i: (i, 0)),
        out_shape=jax.ShapeDtypeStruct((m, d), F32),
        compiler_params=_cparams(("parallel",)),
        name="mem_block",
    )(x, g.reshape(1, d).astype(F32), wq, k, v, wo)


def _rel_bias_by_distance(rel_table):
    d = jnp.arange(REL_MAX_DIST, dtype=jnp.int32)
    max_exact = REL_BUCKETS // 2
    nf = jnp.maximum(d, 1).astype(F32)
    large = max_exact + (jnp.log(nf / max_exact) / math.log(REL_MAX_DIST / max_exact)
                         * (REL_BUCKETS - max_exact)).astype(jnp.int32)
    large = jnp.minimum(large, REL_BUCKETS - 1)
    bucket = jnp.where(d < max_exact, d, large)
    return jnp.take(rel_table, bucket, axis=0).T.astype(F32)


def _in_proj_col_scale():
    gw = GROUP_WIDTH
    scale = np.ones((PROJ_COLS,), np.float32)
    scale[PROJ_A0:PROJ_A0 + gw] = DIFF_QK ** -0.5 * LOG2E
    scale[PROJ_C0:PROJ_C0 + gw] = HEAD_DIM ** -0.5 * LOG2E
    scale[PROJ_D0:PROJ_D0 + gw] = HEAD_DIM ** -0.5 * LOG2E
    return scale


PREP_CHUNK = PROJ_ALIGN
PREP_ALIGNED = PROJ_C0 // PREP_CHUNK
PREP_SKEW = PREP_CHUNK - (PROJ_C0 - (A_COLS + B_COLS))


def _regroup_kernel(x_ref, tail_ref, s_ref, o_ref):
    c = pl.program_id(1)

    @pl.when(c < PREP_ALIGNED)
    def _():
        o_ref[...] = (x_ref[...].T * s_ref[...]).astype(o_ref.dtype)

    @pl.when(c >= PREP_ALIGNED)
    def _():
        x = jnp.concatenate([x_ref[PREP_SKEW:, :], tail_ref[...]], axis=0)
        o_ref[...] = (x.T * s_ref[...]).astype(o_ref.dtype)


def _regroup_in_proj(w_in_t, layer, tr=2048):
    d = w_in_t.shape[2]
    tails_per_chunk = PREP_CHUNK // PREP_SKEW
    scale = jnp.asarray(_in_proj_col_scale()).reshape(1, PROJ_COLS)
    return pl.pallas_call(
        _regroup_kernel,
        grid=(d // tr, PROJ_COLS // PREP_CHUNK),
        in_specs=[
            pl.BlockSpec((None, PREP_CHUNK, tr),
                         lambda r, c: (layer, jnp.where(c < PREP_ALIGNED, c, c - 1), r)),
            pl.BlockSpec((None, PREP_SKEW, tr),
                         lambda r, c: (layer, jnp.where(c < PREP_ALIGNED, 0, c * tails_per_chunk), r)),
            pl.BlockSpec((1, PREP_CHUNK), lambda r, c: (0, c))],
        out_specs=pl.BlockSpec((tr, PREP_CHUNK), lambda r, c: (r, c)),
        out_shape=jax.ShapeDtypeStruct((d, PROJ_COLS), BF16),
        compiler_params=_cparams(("parallel", "parallel")),
        name="regroup_in_proj",
    )(w_in_t, w_in_t, scale)


SIDE_COLS = 4 * LANES


def _side_proj_kernel(h_ref, kr_ref, f_ref, o_ref):
    kr, f = kr_ref[...], f_ref[...]
    half = MLA_ROPE // 2

    def zeros(n):
        return jnp.zeros((n, kr.shape[1]), kr.dtype)

    w_t = jnp.concatenate([kr, zeros(LANES - MLA_ROPE), kr[half:], kr[:half], zeros(LANES - MLA_ROPE),
                           f, zeros(2 * LANES - GROUP_HEADS)], axis=0)
    o_ref[...] = _dot_nt(h_ref[...], w_t.astype(h_ref.dtype))


def _side_proj(h, w_in_t, layer, tm=1024):
    m, d = h.shape
    kr_row0 = A_COLS + MLA_Q_RANK + MLA_KV_RANK
    f_row0 = A_COLS + B_COLS + C_COLS + 3 * GROUP_WIDTH
    return pl.pallas_call(
        _side_proj_kernel,
        grid=(m // tm,),
        in_specs=[pl.BlockSpec((tm, d), lambda i: (i, 0)),
                  pl.BlockSpec((None, MLA_ROPE, d), lambda i: (layer, kr_row0 // MLA_ROPE, 0)),
                  pl.BlockSpec((None, GROUP_HEADS, d), lambda i: (layer, f_row0 // GROUP_HEADS, 0))],
        out_specs=pl.BlockSpec((tm, SIDE_COLS), lambda i: (i, 0)),
        out_shape=jax.ShapeDtypeStruct((m, SIDE_COLS), F32),
        compiler_params=_cparams(("parallel",)),
        name="in_proj_side",
    )(h, w_in_t, w_in_t)


def _prep_uq(w):
    qk = MLA_NOPE + MLA_ROPE
    half = MLA_ROPE // 2
    w = (w * (qk ** -0.5 * LOG2E)).reshape(w.shape[0], GROUP_HEADS, qk)
    pad = jnp.zeros((w.shape[0], GROUP_HEADS, LANES - MLA_ROPE), w.dtype)
    nope = w[:, :, :MLA_NOPE]
    r1 = w[:, :, MLA_NOPE:MLA_NOPE + half]
    r2 = w[:, :, MLA_NOPE + half:]
    rope = jnp.concatenate([r1, r2, pad], axis=-1)
    swapped = jnp.concatenate([r2, r1, pad], axis=-1)
    return jnp.concatenate([nope.reshape(w.shape[0], -1), rope.reshape(w.shape[0], -1),
                            swapped.reshape(w.shape[0], -1)], axis=1).astype(BF16)


def _prep_ukv(w):
    w = w.reshape(w.shape[0], GROUP_HEADS, MLA_NOPE + HEAD_DIM)
    return jnp.concatenate([w[:, :, :MLA_NOPE].reshape(w.shape[0], -1),
                            w[:, :, MLA_NOPE:].reshape(w.shape[0], -1)], axis=1).astype(BF16)


def kernel(x, mem, positions, attn_norm_g, w_in, w_out, rel_table, diff_lam_q1, diff_lam_k1, diff_lam_q2, diff_lam_k2, diff_sub_g, mla_q_norm_g, mla_kv_norm_g, mla_w_uq, mla_w_ukv, fox_b_f, mem_q_norm_g, mem_kv_norm_g, mem_w_q, mem_w_k, mem_w_v, mem_w_o, mlp_norm_g, w_up, w_down, final_norm_g):
    batch, seq, d = x.shape
    m = batch * seq
    gw = GROUP_WIDTH
    nh = GROUP_HEADS
    xf = x.reshape(m, d)
    mem2 = mem.reshape(batch * mem.shape[1], d)
    pos_col = positions.reshape(m, 1).astype(jnp.int32)
    pos_row = positions.reshape(batch, 1, seq).astype(jnp.int32)
    cos, sin = _rope_tables(pos_col)
    bias_by_dist = (_rel_bias_by_distance(rel_table) * LOG2E).reshape(nh, 1, LANES)
    npair = nh // HEADS_PER_STEP
    pair_cols = HEADS_PER_STEP * HEAD_DIM
    a0 = PROJ_A0 // pair_cols
    nwide = nh // WIDE_HEADS_PER_STEP
    fox0 = PROJ_D0 // (WIDE_HEADS_PER_STEP * HEAD_DIM)
    nsb = nh // SB_HEADS_PER_STEP
    sb0 = PROJ_C0 // (SB_HEADS_PER_STEP * HEAD_DIM)

    w_down_b = w_down.astype(BF16)
    mem_wq_b = (mem_w_q * (MEM_HEAD_DIM ** -0.5 * LOG2E)).astype(BF16)
    mem_wk_b, mem_wv_b, mem_wo_b = mem_w_k.astype(BF16), mem_w_v.astype(BF16), mem_w_o.astype(BF16)

    w_in_t = jnp.swapaxes(w_in, 1, 2)

    for l in range(DEPTH):
        w_main = _regroup_in_proj(w_in_t, l)
        h = _rmsnorm(xf, attn_norm_g[l], BF16)
        proj = _matmul(h, w_main, BF16, name="in_proj")
        side = _side_proj(h, w_in_t, l)
        b_f = jnp.concatenate([fox_b_f[l].astype(F32), jnp.zeros((LANES - nh,), F32)]).reshape(1, LANES)
        cum, fox_kx = _forget_cum(side, b_f, batch, seq)

        lam_init = 0.8 - 0.6 * math.exp(-0.3 * l)
        lam_vecs = [v[l].reshape(1, DIFF_QK).astype(F32)
                    for v in (diff_lam_q1, diff_lam_k1, diff_lam_q2, diff_lam_k2)]
        o_a = _attn_call(
            functools.partial(_diff_kernel, t=ATT_T, seq=seq, lam_init=lam_init), ATT_T, batch, seq,
            lam_vecs + [proj, proj, proj, pos_col, pos_row, bias_by_dist,
                        diff_sub_g[l].reshape(1, HEAD_DIM).astype(F32)],
            [_full_spec((1, DIFF_QK))] * 4
            + [_q_spec(ATT_T, seq, a0), _kv_spec(seq, a0 + npair), _kv_spec(seq, a0 + 2 * npair),
               pl.BlockSpec((ATT_T, 1), lambda b, h, i: (b * (seq // ATT_T) + i, 0)),
               pl.BlockSpec((None, 1, seq), lambda b, h, i: (b, 0, 0)),
               pl.BlockSpec((HEADS_PER_STEP, 1, LANES), lambda b, h, i: (h, 0, 0)),
               _full_spec((1, HEAD_DIM))],
            "diff_attn", softmax_chains=2 * HEADS_PER_STEP)

        q_b = _mla_q(proj, mla_q_norm_g[l], _prep_uq(mla_w_uq[l]), cos, sin)
        kv_b, kpe_b = _mla_kv(proj, mla_kv_norm_g[l], _prep_ukv(mla_w_ukv[l]), side, cos, sin)
        o_b = _attn_call(
            functools.partial(_mla_kernel, t=ATT_T, seq=seq), ATT_T, batch, seq,
            [q_b, q_b, kv_b, kpe_b, kv_b],
            [_q_spec(ATT_T, seq, 0, WIDE_HEADS_PER_STEP), _q_spec(ATT_T, seq, nwide, WIDE_HEADS_PER_STEP),
             _kv_spec(seq, 0, WIDE_HEADS_PER_STEP), _shared_spec(seq),
             _kv_spec(seq, nwide, WIDE_HEADS_PER_STEP)],
            "mla_attn", hps=WIDE_HEADS_PER_STEP, softmax_chains=WIDE_HEADS_PER_STEP)

        o_c = _attn_call(
            functools.partial(_sb_kernel, t=SB_T), SB_T, batch, seq,
            [proj, proj, proj],
            [_q_spec(SB_T, seq, sb0, SB_HEADS_PER_STEP), _kv_spec(seq, sb0 + nsb, SB_HEADS_PER_STEP),
             _kv_spec(seq, sb0 + 2 * nsb, SB_HEADS_PER_STEP)],
            "stickbreak_attn", hps=SB_HEADS_PER_STEP)

        o_d = _attn_call(
            functools.partial(_fox_kernel, t=ATT_T, seq=seq), ATT_T, batch, seq,
            [proj, proj, proj, fox_kx, cum, cum],
            [_q_spec(ATT_T, seq, fox0, WIDE_HEADS_PER_STEP), _kv_spec(seq, fox0 + nwide, WIDE_HEADS_PER_STEP),
             _kv_spec(seq, fox0 + 2 * nwide, WIDE_HEADS_PER_STEP),
             pl.BlockSpec((None, WIDE_HEADS_PER_STEP, seq, LANES), lambda b, h, i: (b, h, 0, 0)),
             pl.BlockSpec((ATT_T, LANES), lambda b, h, i: (b * (seq // ATT_T) + i, 0)),
             pl.BlockSpec((seq, LANES), lambda b, h, i: (b, 0))],
            "fox_attn", hps=WIDE_HEADS_PER_STEP, softmax_chains=WIDE_HEADS_PER_STEP)

        xf = _out_proj([o_a, o_b, o_c, o_d], w_out, l, xf)

        mem_n = _rmsnorm(mem2, mem_kv_norm_g[l], BF16)
        k_m = _matmul(mem_n, mem_wk_b, BF16, layer=l, name="mem_k")
        v_m = _matmul(mem_n, mem_wv_b, BF16, layer=l, name="mem_v")
        xf = _mem_block(xf, mem_q_norm_g[l], mem_wq_b, k_m, v_m, mem_wo_b, l, batch, seq)

        h = _rmsnorm(xf, mlp_norm_g[l], BF16)
        u = _matmul(h, w_up, BF16, layer=l, relu2=True, name="mlp_up")
        xf = _matmul(u, w_down_b, F32, layer=l, res=xf, tn=1024, tk=2048, name="mlp_down")

    return _rmsnorm(xf, final_norm_g, F32).reshape(batch, seq, d)
```

```python
import functools
import math

import numpy as np
import jax
import jax.numpy as jnp
from jax import lax
from jax.experimental import pallas as pl
from jax.experimental.pallas import tpu as pltpu

F32 = jnp.float32
BF16 = jnp.bfloat16

DEPTH = 2
HEAD_DIM = 128
GROUP_HEADS = 8
GROUP_WIDTH = GROUP_HEADS * HEAD_DIM
DIFF_QK = HEAD_DIM // 2
MLA_Q_RANK = 1536
MLA_KV_RANK = 512
MLA_NOPE = 128
MLA_ROPE = 64
ROPE_THETA = 10000.0
REL_BUCKETS = 32
REL_MAX_DIST = 128
MEM_HEADS = 4
MEM_HEAD_DIM = 128
NORM_EPS = 1e-6
NEG_INF = -1e30

A_COLS = 3 * GROUP_WIDTH
B_COLS = MLA_Q_RANK + MLA_KV_RANK + MLA_ROPE
C_COLS = 3 * GROUP_WIDTH

PROJ_A0 = 0
PROJ_CQ0 = A_COLS
PROJ_CKV0 = A_COLS + MLA_Q_RANK
PROJ_ALIGN = 512
PROJ_C0 = -(-(A_COLS + B_COLS) // PROJ_ALIGN) * PROJ_ALIGN
PROJ_D0 = PROJ_C0 + C_COLS
PROJ_COLS = PROJ_D0 + 3 * GROUP_WIDTH

LANES = 128
SUBLANES = 8
VMEM_LIMIT = 48 * 1024 * 1024
OFFSET_PARTS = 3

LOG2E = math.log2(math.e)

ATT_T = 512
SB_T = 256
HEADS_PER_STEP = 2
WIDE_HEADS_PER_STEP = 4
SB_HEADS_PER_STEP = 4
DEAD_LOG2 = -160.0
TRUSTED_ROW_SUM = 2.0 ** -60


def _cparams(sem):
    return pltpu.CompilerParams(dimension_semantics=sem, vmem_limit_bytes=VMEM_LIMIT)


def _dot(a, b):
    return jnp.dot(a, b, preferred_element_type=F32)


def _dot_nt(a, b):
    return lax.dot_general(a, b, (((1,), (1,)), ((), ())), preferred_element_type=F32)


def _rms(x, g):
    return x * lax.rsqrt(jnp.mean(x * x, axis=-1, keepdims=True) + NORM_EPS) * g


def _rmsnorm_kernel(x_ref, g_ref, o_ref):
    o_ref[...] = _rms(x_ref[...].astype(F32), g_ref[...]).astype(o_ref.dtype)


def _rmsnorm(x, g, out_dtype, tm=256):
    m, d = x.shape
    return pl.pallas_call(
        _rmsnorm_kernel,
        grid=(m // tm,),
        in_specs=[pl.BlockSpec((tm, d), lambda i: (i, 0)),
                  pl.BlockSpec((1, d), lambda i: (0, 0))],
        out_specs=pl.BlockSpec((tm, d), lambda i: (i, 0)),
        out_shape=jax.ShapeDtypeStruct((m, d), out_dtype),
        compiler_params=_cparams(("parallel",)),
        name="rmsnorm",
    )(x, g.reshape(1, d).astype(F32))


def _mm_epilogue(r, res_ref, o_ref, relu2):
    if relu2:
        r = jnp.square(jnp.maximum(r, 0.0))
    if res_ref is not None:
        r = r + res_ref[...]
    o_ref[...] = r.astype(o_ref.dtype)


def _mm_kernel(a_ref, w_ref, *rest, nk, relu2, has_res):
    res_ref = rest[0] if has_res else None
    o_ref = rest[1] if has_res else rest[0]
    if nk == 1:
        _mm_epilogue(_dot(a_ref[...], w_ref[...].astype(a_ref.dtype)), res_ref, o_ref, relu2)
        return
    acc_ref = rest[-1]
    k = pl.program_id(2)

    @pl.when(k == 0)
    def _():
        acc_ref[...] = _dot(a_ref[...], w_ref[...])

    @pl.when(k > 0)
    def _():
        acc_ref[...] += _dot(a_ref[...], w_ref[...])

    @pl.when(k == nk - 1)
    def _():
        _mm_epilogue(acc_ref[...], res_ref, o_ref, relu2)


def _w_spec(w, layer, block, index_map):
    if w.ndim == 2:
        return pl.BlockSpec(block, index_map)
    return pl.BlockSpec((None,) + block, lambda *g: (layer,) + index_map(*g))


def _matmul(a, w, out_dtype, *, layer=None, res=None, relu2=False, tm=1024, tn=512, tk=4096,
            name="matmul"):
    m, kd = a.shape
    n = w.shape[-1]
    tm, tn, tk = min(tm, m), min(tn, n), min(tk, kd)
    assert m % tm == 0 and n % tn == 0 and kd % tk == 0
    nk = kd // tk
    in_specs = [pl.BlockSpec((tm, tk), lambda i, j, k: (i, k)),
                _w_spec(w, layer, (tk, tn), lambda i, j, k: (k, j))]
    args = [a, w]
    if res is not None:
        in_specs.append(pl.BlockSpec((tm, tn), lambda i, j, k: (i, j)))
        args.append(res)
    return pl.pallas_call(
        functools.partial(_mm_kernel, nk=nk, relu2=relu2, has_res=res is not None),
        grid=(m // tm, n // tn, nk),
        in_specs=in_specs,
        out_specs=pl.BlockSpec((tm, tn), lambda i, j, k: (i, j)),
        out_shape=jax.ShapeDtypeStruct((m, n), out_dtype),
        scratch_shapes=[pltpu.VMEM((tm, tn), F32)] if nk > 1 else [],
        compiler_params=_cparams(("parallel", "parallel", "arbitrary")),
        name=name,
    )(*args)


def _out_proj_kernel(a0, a1, a2, a3, w_ref, res_ref, o_ref):
    acc = res_ref[...]
    for g, a_ref in enumerate((a0, a1, a2, a3)):
        w = w_ref[g * GROUP_WIDTH:(g + 1) * GROUP_WIDTH, :].astype(a_ref.dtype)
        acc = acc + _dot(a_ref[...], w)
    o_ref[...] = acc


def _out_proj(mixes, w, layer, res, tm=1024, tn=512):
    m = res.shape[0]
    n = w.shape[-1]
    a_spec = pl.BlockSpec((tm, GROUP_WIDTH), lambda i, j: (i, 0))
    return pl.pallas_call(
        _out_proj_kernel,
        grid=(m // tm, n // tn),
        in_specs=[a_spec, a_spec, a_spec, a_spec,
                  _w_spec(w, layer, (w.shape[-2], tn), lambda i, j: (0, j)),
                  pl.BlockSpec((tm, tn), lambda i, j: (i, j))],
        out_specs=pl.BlockSpec((tm, tn), lambda i, j: (i, j)),
        out_shape=jax.ShapeDtypeStruct((m, n), F32),
        compiler_params=_cparams(("parallel", "parallel")),
        name="out_proj",
    )(*mixes, w, res)


def _rope_kernel(pos_ref, tab_ref, cos_ref, sin_ref):
    ang = pos_ref[...].astype(F32) * tab_ref[0:1, :]
    cos_ref[...] = jnp.cos(ang) * tab_ref[1:2, :]
    sin_ref[...] = jnp.sin(ang) * tab_ref[2:3, :]


def _rope_tables(pos_col, tm=1024):
    m = pos_col.shape[0]
    half = MLA_ROPE // 2
    inv = ROPE_THETA ** (-jnp.arange(half, dtype=F32) * 2.0 / MLA_ROPE)
    zeros = jnp.zeros((LANES - MLA_ROPE,), F32)
    ones = jnp.ones((half,), F32)
    tab = jnp.stack([jnp.concatenate([inv, inv, zeros]),
                     jnp.concatenate([ones, ones, zeros]),
                     jnp.concatenate([-ones, ones, zeros])]
                    + [jnp.zeros((LANES,), F32)] * 5)
    spec = pl.BlockSpec((tm, LANES), lambda i: (i, 0))
    return pl.pallas_call(
        _rope_kernel,
        grid=(m // tm,),
        in_specs=[pl.BlockSpec((tm, 1), lambda i: (i, 0)),
                  pl.BlockSpec((SUBLANES, LANES), lambda i: (0, 0))],
        out_specs=[spec, spec],
        out_shape=[jax.ShapeDtypeStruct((m, LANES), F32)] * 2,
        compiler_params=_cparams(("parallel",)),
        name="rope_tables",
    )(pos_col, tab)


def _mla_q_kernel(cq_ref, g_ref, w_ref, cos_ref, sin_ref, o_ref):
    n = _rms(cq_ref[...].astype(F32), g_ref[...]).astype(BF16)
    t = _dot(n, w_ref[...])
    o_ref[:, :GROUP_WIDTH] = t[:, :GROUP_WIDTH].astype(o_ref.dtype)
    cos, sin = cos_ref[...], sin_ref[...]
    for h in range(GROUP_HEADS):
        lo = GROUP_WIDTH + h * LANES
        pe = t[:, lo:lo + LANES] * cos + t[:, lo + GROUP_WIDTH:lo + GROUP_WIDTH + LANES] * sin
        o_ref[:, lo:lo + LANES] = pe.astype(o_ref.dtype)


def _mla_q(proj, g, w, cos, sin, tm=512):
    m = proj.shape[0]
    cq_block = PROJ_CQ0 // MLA_Q_RANK
    return pl.pallas_call(
        _mla_q_kernel,
        grid=(m // tm,),
        in_specs=[pl.BlockSpec((tm, MLA_Q_RANK), lambda i: (i, cq_block)),
                  pl.BlockSpec((1, MLA_Q_RANK), lambda i: (0, 0)),
                  pl.BlockSpec(w.shape, lambda i: (0, 0)),
                  pl.BlockSpec((tm, LANES), lambda i: (i, 0)),
                  pl.BlockSpec((tm, LANES), lambda i: (i, 0))],
        out_specs=pl.BlockSpec((tm, 2 * GROUP_WIDTH), lambda i: (i, 0)),
        out_shape=jax.ShapeDtypeStruct((m, 2 * GROUP_WIDTH), BF16),
        compiler_params=_cparams(("parallel",)),
        name="mla_q",
    )(proj, g.reshape(1, -1).astype(F32), w, cos, sin)


def _mla_kv_kernel(ckv_ref, g_ref, w_ref, kr_ref, cos_ref, sin_ref, kv_ref, kpe_ref):
    n = _rms(ckv_ref[...].astype(F32), g_ref[...]).astype(BF16)
    kv_ref[...] = _dot(n, w_ref[...]).astype(kv_ref.dtype)
    kr = kr_ref[...]
    kpe = kr[:, :LANES] * cos_ref[...] + kr[:, LANES:] * sin_ref[...]
    lane = lax.broadcasted_iota(jnp.int32, kpe.shape, 1)
    kpe = jnp.where((lane >= MLA_ROPE) & (lane < MLA_ROPE + OFFSET_PARTS), 1.0, kpe)
    kpe_ref[...] = kpe.astype(kpe_ref.dtype)


def _mla_kv(proj, g, w, kr, cos, sin, tm=512):
    m = proj.shape[0]
    ckv_block = PROJ_CKV0 // MLA_KV_RANK
    return pl.pallas_call(
        _mla_kv_kernel,
        grid=(m // tm,),
        in_specs=[pl.BlockSpec((tm, MLA_KV_RANK), lambda i: (i, ckv_block)),
                  pl.BlockSpec((1, MLA_KV_RANK), lambda i: (0, 0)),
                  pl.BlockSpec(w.shape, lambda i: (0, 0)),
                  pl.BlockSpec((tm, 2 * LANES), lambda i: (i, 0)),
                  pl.BlockSpec((tm, LANES), lambda i: (i, 0)),
                  pl.BlockSpec((tm, LANES), lambda i: (i, 0))],
        out_specs=[pl.BlockSpec((tm, 2 * GROUP_WIDTH), lambda i: (i, 0)),
                   pl.BlockSpec((tm, LANES), lambda i: (i, 0))],
        out_shape=[jax.ShapeDtypeStruct((m, 2 * GROUP_WIDTH), BF16),
                   jax.ShapeDtypeStruct((m, LANES), BF16)],
        compiler_params=_cparams(("parallel",)),
        name="mla_kv",
    )(proj, g.reshape(1, -1).astype(F32), w, kr, cos, sin)


def _split3(x):
    x1 = x.astype(BF16)
    r1 = x - x1.astype(F32)
    x2 = r1.astype(BF16)
    x3 = (r1 - x2.astype(F32)).astype(BF16)
    return x1, x2, x3


def _place(lane, cols, base, fill):
    out = fill
    for i, c in enumerate(cols):
        out = jnp.where(lane == base + i, c.astype(F32), out)
    return out


def _forget_cum_kernel(f_ref, b_ref, cum_ref, kx_ref, *, chunk):
    r = lax.broadcasted_iota(jnp.int32, (chunk, chunk), 0)
    c = lax.broadcasted_iota(jnp.int32, (chunk, chunk), 1)
    tri = jnp.where(r >= c, 1.0, 0.0).astype(BF16)
    lane = lax.broadcasted_iota(jnp.int32, (chunk, LANES), 1)
    unit_fill = jnp.where(lane < OFFSET_PARTS, 1.0, 0.0)

    def body(ci, carry):
        rows = _rows(ci, chunk)
        x = f_ref[rows, :] + b_ref[...]
        logf = (jnp.minimum(x, 0.0) - jnp.log1p(jnp.exp(-jnp.abs(x)))) * LOG2E
        parts = _split3(logf)
        cs = carry + _dot(tri, parts[0]) + _dot(tri, parts[1]) + _dot(tri, parts[2])
        cum_ref[rows, :] = cs
        for h in range(GROUP_HEADS):
            kx_ref[h, rows, :] = _place(lane, _split3(cs[:, h:h + 1]), OFFSET_PARTS,
                                        unit_fill).astype(kx_ref.dtype)
        return cs[chunk - 1:chunk, :]

    lax.fori_loop(0, f_ref.shape[0] // chunk, body, jnp.zeros((1, LANES), F32))


def _forget_cum(side, b_f, batch, seq):
    f_block = 2
    return pl.pallas_call(
        functools.partial(_forget_cum_kernel, chunk=256),
        grid=(batch,),
        in_specs=[pl.BlockSpec((seq, LANES), lambda b: (b, f_block)),
                  pl.BlockSpec((1, LANES), lambda b: (0, 0))],
        out_specs=[pl.BlockSpec((seq, LANES), lambda b: (b, 0)),
                   pl.BlockSpec((None, GROUP_HEADS, seq, LANES), lambda b: (b, 0, 0, 0))],
        out_shape=[jax.ShapeDtypeStruct((batch * seq, LANES), F32),
                   jax.ShapeDtypeStruct((batch, GROUP_HEADS, seq, LANES), BF16)],
        compiler_params=_cparams(("parallel",)),
        name="forget_cum",
    )(side, b_f)


def _tile_iota(t):
    return (lax.broadcasted_iota(jnp.int32, (t, t), 0),
            lax.broadcasted_iota(jnp.int32, (t, t), 1))


def _diag_mask(t, w):
    row = lax.broadcasted_iota(jnp.int32, (t, w), 0)
    col = lax.broadcasted_iota(jnp.int32, (t, w), 1)
    return col <= row + (w - t)


def _softmax_update(s, v, m, acc):
    m_new = jnp.maximum(m, jnp.max(s, axis=-1, keepdims=True))
    alpha = jnp.exp2(m - m_new)
    p = jnp.exp2(s - m_new).astype(BF16)
    v_aug = jnp.concatenate([v, jnp.ones_like(v)], axis=1)
    return m_new, alpha * acc + _dot(p, v_aug)


def _softmax_init(t):
    return (jnp.full((t, 1), NEG_INF, F32), jnp.zeros((t, 2 * HEAD_DIM), F32))


def _softmax_out(acc):
    return acc[:, :HEAD_DIM] / acc[:, HEAD_DIM:]


def _hs(h):
    return slice(h * HEAD_DIM, (h + 1) * HEAD_DIM)


def _rows(j, w):
    return pl.ds(pl.multiple_of(j * w, w), w)


def _update(s, v, state, mask, acc_ref, chain):
    if mask is not None:
        s = jnp.where(mask, s, NEG_INF)
    if acc_ref is None:
        return _softmax_update(s, v, *state)
    v_aug = jnp.concatenate([v, jnp.ones_like(v)], axis=1)
    acc_ref[chain] += _dot(jnp.exp2(s).astype(BF16), v_aug)
    return None


def _run_softmax(qi, t, n_chains, make_step, finish, acc_ref, all_zero=None):
    acc_ref[...] = jnp.zeros_like(acc_ref)
    fast = make_step(acc_ref)

    def wide(j, carry):
        if all_zero is None:
            fast(j, None, False, 2 * t)
        else:
            @pl.when(jnp.logical_not(all_zero(j, 2 * t)))
            def _():
                fast(j, None, False, 2 * t)
        return carry

    lax.fori_loop(0, qi // 2, wide, 0)

    @pl.when(qi % 2 == 1)
    def _():
        fast(qi // 2, None, True, 2 * t)

    @pl.when(qi % 2 == 0)
    def _():
        fast(qi, None, True, t)

    accs = [acc_ref[c] for c in range(n_chains)]
    l_min = accs[0][:, HEAD_DIM:HEAD_DIM + 1]
    for a in accs[1:]:
        l_min = jnp.minimum(l_min, a[:, HEAD_DIM:HEAD_DIM + 1])
    ok = jnp.min(l_min) >= TRUSTED_ROW_SUM

    @pl.when(ok)
    def _():
        finish([_softmax_out(a) for a in accs])

    @pl.when(jnp.logical_not(ok))
    def _():
        slow = make_step(None)
        init = tuple(_softmax_init(t) for _ in range(n_chains))
        carry = lax.fori_loop(0, qi, lambda j, c: slow(j, c, False, t), init)
        finish([_softmax_out(c[1]) for c in slow(qi, carry, True, t)])


def _key_norm_max(k_ref, h, seq, extra_ref=None):
    chunk = ATT_T

    def body(c, mx):
        kk = k_ref[_rows(c, chunk), _hs(h)].astype(F32)
        n2 = jnp.sum(kk * kk, axis=1, keepdims=True)
        if extra_ref is not None:
            e = extra_ref[_rows(c, chunk), :].astype(F32)
            n2 = n2 + jnp.sum(e * e, axis=1, keepdims=True)
        return jnp.maximum(mx, n2)

    mx = lax.fori_loop(0, seq // chunk, body, jnp.zeros((chunk, 1), F32))
    return jnp.sqrt(jnp.max(mx, axis=0, keepdims=True))


def _store_key_norms(kmax_ref, qi, norm_fn):
    @pl.when(qi == 0)
    def _():
        for h in range(kmax_ref.shape[0]):
            kmax_ref[h] = jnp.broadcast_to(norm_fn(h), kmax_ref.shape[1:])


def _row_norm(x):
    xf = x.astype(F32)
    return jnp.sqrt(jnp.sum(xf * xf, axis=1, keepdims=True))


def _fox_kernel(q_ref, k_ref, v_ref, kx_ref, cumq_ref, cumk_ref, o_ref, kmax_ref, acc_ref, *, t, seq):
    group, qi = pl.program_id(1), pl.program_id(2)
    hps = kmax_ref.shape[0]
    _store_key_norms(kmax_ref, qi, lambda h: _key_norm_max(k_ref, h, seq))
    lane = lax.broadcasted_iota(jnp.int32, (t, LANES), 1)
    minus_fill = jnp.where((lane >= OFFSET_PARTS) & (lane < 2 * OFFSET_PARTS), -1.0, 0.0)
    cumq = cumq_ref[...]

    def all_zero(j, w):
        head_lane = lax.broadcasted_iota(jnp.int32, (1, LANES), 1)
        mine = (head_lane >= group * hps) & (head_lane < (group + 1) * hps)
        gap = cumq_ref[0:1, :] - cumk_ref[pl.ds((j + 1) * w - 1, 1), :]
        return jnp.max(jnp.where(mine, gap, NEG_INF)) < DEAD_LOG2

    def q_aug(h, fast):
        q = q_ref[:, _hs(h)]
        c = jnp.sum(jnp.where(lane == group * hps + h, cumq, 0.0), axis=1, keepdims=True)
        if fast:
            c = c - _row_norm(q) * kmax_ref[h][0:1, 0:1]
        return jnp.concatenate([q, _place(lane, _split3(c), 0, minus_fill).astype(BF16)], axis=1)

    def make_step(acc_ref):
        fast = acc_ref is not None
        qs = [q_aug(h, fast) for h in range(hps)]

        def step(j, carry, masked, w):
            mask = _diag_mask(t, w) if masked else None
            out = []
            for h in range(hps):
                k = jnp.concatenate([k_ref[_rows(j, w), _hs(h)], kx_ref[h, _rows(j, w), :]], axis=1)
                out.append(_update(_dot_nt(qs[h], k), v_ref[_rows(j, w), _hs(h)], None if fast else carry[h], mask, acc_ref, h))
            return tuple(out)

        return step

    def finish(outs):
        o_ref[...] = jnp.concatenate(outs, axis=1).astype(o_ref.dtype)

    _run_softmax(qi, t, hps, make_step, finish, acc_ref, all_zero)


def _mla_kernel(qn_ref, qp_ref, kn_ref, kp_ref, v_ref, o_ref, kmax_ref, acc_ref, *, t, seq):
    qi = pl.program_id(2)
    hps = kmax_ref.shape[0]
    _store_key_norms(kmax_ref, qi, lambda h: _key_norm_max(kn_ref, h, seq, kp_ref))
    lane = lax.broadcasted_iota(jnp.int32, (t, LANES), 1)

    def q_aug(h, fast):
        qn, qp = qn_ref[:, _hs(h)], qp_ref[:, _hs(h)]
        if fast:
            qn_f, qp_f = qn.astype(F32), qp.astype(F32)
            norm = jnp.sqrt(jnp.sum(qn_f * qn_f, axis=1, keepdims=True)
                            + jnp.sum(qp_f * qp_f, axis=1, keepdims=True))
            qp = _place(lane, _split3(-norm * kmax_ref[h][0:1, 0:1]), MLA_ROPE, qp_f).astype(BF16)
        return jnp.concatenate([qn, qp], axis=1)

    def make_step(acc_ref):
        fast = acc_ref is not None
        qs = [q_aug(h, fast) for h in range(hps)]

        def step(j, carry, masked, w):
            mask = _diag_mask(t, w) if masked else None
            kp = kp_ref[_rows(j, w), :]
            out = []
            for h in range(hps):
                k = jnp.concatenate([kn_ref[_rows(j, w), _hs(h)], kp], axis=1)
                out.append(_update(_dot_nt(qs[h], k), v_ref[_rows(j, w), _hs(h)], None if fast else carry[h], mask, acc_ref, h))
            return tuple(out)

        return step

    def finish(outs):
        o_ref[...] = jnp.concatenate(outs, axis=1).astype(o_ref.dtype)

    _run_softmax(qi, t, hps, make_step, finish, acc_ref)


def _diff_kernel(lq1_ref, lk1_ref, lq2_ref, lk2_ref, q_ref, k_ref, v_ref, posq_ref, posk_ref,
                 bias_ref, subg_ref, o_ref, kmax_ref, acc_ref, *, t, seq, lam_init):
    qi = pl.program_id(2)
    _store_key_norms(kmax_ref, qi, lambda h: _key_norm_max(k_ref, h, seq))
    lam = (jnp.exp(jnp.sum(lq1_ref[...] * lk1_ref[...], axis=-1, keepdims=True))
           - jnp.exp(jnp.sum(lq2_ref[...] * lk2_ref[...], axis=-1, keepdims=True)) + lam_init)
    lane = lax.broadcasted_iota(jnp.int32, (t, LANES), 1)
    zero_fill = jnp.zeros((t, LANES), F32)

    def key_units(w):
        unit = lax.broadcasted_iota(jnp.int32, (w, LANES), 1) < OFFSET_PARTS
        return jnp.where(unit, 1.0, 0.0).astype(BF16)

    posq = posq_ref[...]
    posq_min = jnp.min(posq)
    bias_rows = [jnp.broadcast_to(bias_ref[h], (t, LANES)) for h in range(HEADS_PER_STEP)]
    last = REL_MAX_DIST - 1
    far_bias = [bias_ref[h][:, last:] for h in range(HEADS_PER_STEP)]

    def half_q(h, c):
        q = q_ref[:, _hs(h)]
        return jnp.where((lane < DIFF_QK) if c == 0 else (lane >= DIFF_QK), q, jnp.zeros_like(q))

    def make_step(acc_ref):
        fast = acc_ref is not None
        q_near, q_far = [], []
        for h in range(HEADS_PER_STEP):
            for c in range(2):
                qz = half_q(h, c)
                if fast:
                    bound = (_row_norm(qz) * kmax_ref[h][0:1, 0:1]
                             + jnp.max(bias_ref[h], axis=1, keepdims=True))
                    q_near.append(jnp.concatenate(
                        [qz, _place(lane, _split3(-bound), 0, zero_fill).astype(BF16)], axis=1))
                    q_far.append(jnp.concatenate(
                        [qz, _place(lane, _split3(far_bias[h] - bound), 0, zero_fill).astype(BF16)], axis=1))
                else:
                    q_near.append(qz)
                    q_far.append(qz)

        def chains(j, carry, masked, w, gathered):
            mask = _diag_mask(t, w) if masked else None
            out = []
            for h in range(HEADS_PER_STEP):
                k = k_ref[_rows(j, w), _hs(h)]
                if fast:
                    k = jnp.concatenate([k, key_units(w)], axis=1)
                v = v_ref[_rows(j, w), _hs(h)]
                for c in range(2):
                    i = 2 * h + c
                    if gathered is not None:
                        s = _dot_nt(q_near[i], k) + gathered[h]
                    elif fast:
                        s = _dot_nt(q_far[i], k)
                    else:
                        s = _dot_nt(q_far[i], k) + far_bias[h]
                    out.append(_update(s, v, None if fast else carry[i], mask, acc_ref, i))
            return tuple(out)

        def step(j, carry, masked, w):
            posk = posk_ref[:, _rows(j, w)]

            def near(c):
                dist = jnp.clip(posq - posk, 0, last)
                gathered = [jnp.concatenate(
                    [jnp.take_along_axis(bias_rows[h], dist[:, cc * LANES:(cc + 1) * LANES], axis=1,
                                         mode="promise_in_bounds")
                     for cc in range(w // LANES)], axis=1) for h in range(HEADS_PER_STEP)]
                return chains(j, c, masked, w, gathered)

            def far(c):
                return chains(j, c, masked, w, None)

            return lax.cond(posq_min - jnp.max(posk) < last, near, far, carry)

        return step

    def finish(outs):
        normed = []
        for h in range(HEADS_PER_STEP):
            o = outs[2 * h] - lam * outs[2 * h + 1]
            normed.append(_rms(o, subg_ref[...]) * (1.0 - lam_init))
        o_ref[...] = jnp.concatenate(normed, axis=1).astype(o_ref.dtype)

    _run_softmax(qi, t, 2 * HEADS_PER_STEP, make_step, finish, acc_ref)


def _sb_kernel(q_ref, k_ref, v_ref, o_ref, *, t):
    qi = pl.program_id(2)
    w = t
    row, col = _tile_iota(t)
    later = jnp.where(row > col, 1.0, 0.0).astype(BF16)
    qs = [q_ref[:, _hs(h)] for h in range(SB_HEADS_PER_STEP)]

    def step(j, carry, masked):
        out = []
        for h in range(SB_HEADS_PER_STEP):
            tail, acc = carry[h]
            z = _dot_nt(qs[h], k_ref[_rows(j, w), _hs(h)])
            sp = jnp.log(1.0 + jnp.exp2(-jnp.abs(z))) * LOG2E
            log_beta = jnp.minimum(z, 0.0) - sp
            log_1m = -jnp.maximum(z, 0.0) - sp
            if masked:
                log_1m = jnp.where(col < row, log_1m, 0.0)
            hi = log_1m.astype(BF16)
            lo = (log_1m - hi.astype(F32)).astype(BF16)
            both = _dot(jnp.concatenate([hi, lo], axis=0), later)
            suffix = both[:t] + both[t:]
            a = jnp.exp2(log_beta + suffix + tail)
            if masked:
                a = jnp.where(col < row, a, 0.0)
            acc = acc + _dot(a.astype(BF16), v_ref[_rows(j, w), _hs(h)])
            tail = tail + suffix[:, :1] + log_1m[:, :1]
            out.append((tail, acc))
        return tuple(out)

    def live(carry):
        worst = carry[0][0]
        for h in range(1, SB_HEADS_PER_STEP):
            worst = jnp.maximum(worst, carry[h][0])
        return (jnp.max(worst) > DEAD_LOG2).astype(jnp.int32)

    init = tuple((jnp.zeros((t, 1), F32), jnp.zeros((t, HEAD_DIM), F32)) for _ in range(SB_HEADS_PER_STEP))
    carry = step(qi, init, True)

    def body(state):
        j, _, c = state
        c = step(j, c, False)
        return j - 1, live(c), c

    _, _, carry = lax.while_loop(lambda st: (st[0] >= 0) & (st[1] > 0), body,
                                 (qi - 1, live(carry), carry))
    o_ref[...] = jnp.concatenate([c[1] for c in carry], axis=1).astype(o_ref.dtype)


def _attn_call(kernel, t, batch, seq, in_arrays, in_specs, name, hps=HEADS_PER_STEP, softmax_chains=0):
    nq = seq // t
    width = hps * HEAD_DIM
    scratch = []
    if softmax_chains:
        scratch = [pltpu.VMEM((hps, SUBLANES, LANES), F32),
                   pltpu.VMEM((softmax_chains, t, 2 * HEAD_DIM), F32)]
    return pl.pallas_call(
        kernel,
        grid=(batch, GROUP_HEADS // hps, nq),
        in_specs=in_specs,
        out_specs=pl.BlockSpec((t, width), lambda b, h, i: (b * nq + i, h)),
        out_shape=jax.ShapeDtypeStruct((batch * seq, GROUP_WIDTH), BF16),
        scratch_shapes=scratch,
        compiler_params=_cparams(("arbitrary", "arbitrary", "arbitrary")),
        name=name,
    )(*in_arrays)


def _q_spec(t, seq, col0, hps=HEADS_PER_STEP):
    nq = seq // t
    return pl.BlockSpec((t, hps * HEAD_DIM), lambda b, h, i: (b * nq + i, col0 + h))


def _kv_spec(seq, col0, hps=HEADS_PER_STEP):
    return pl.BlockSpec((seq, hps * HEAD_DIM), lambda b, h, i: (b, col0 + h))


def _shared_spec(seq):
    return pl.BlockSpec((seq, HEAD_DIM), lambda b, h, i: (b, 0))


def _full_spec(shape):
    return pl.BlockSpec(shape, lambda b, h, i: (0,) * len(shape))


def _mem_kernel(x_ref, g_ref, wq_ref, k_ref, v_ref, wo_ref, g_next_ref, o_ref, h_next_ref):
    x = x_ref[...]
    h = _rms(x, g_ref[...]).astype(BF16)
    q = _dot(h, wq_ref[...]).astype(BF16)
    outs = []
    for hd in range(MEM_HEADS):
        sl = slice(hd * MEM_HEAD_DIM, (hd + 1) * MEM_HEAD_DIM)
        s = _dot_nt(q[:, sl], k_ref[:, sl])
        p = jnp.exp2(s - jnp.max(s, axis=-1, keepdims=True))
        o = _dot(p.astype(BF16), v_ref[:, sl]) / jnp.sum(p, axis=-1, keepdims=True)
        outs.append(o.astype(BF16))
    y = x + _dot(jnp.concatenate(outs, axis=-1), wo_ref[...])
    o_ref[...] = y
    h_next_ref[...] = _rms(y, g_next_ref[...]).astype(h_next_ref.dtype)


def _mem_block(x, g, wq, k, v, wo, g_next, layer, batch, seq, tm=256):
    m, d = x.shape
    n_mem = k.shape[0] // batch
    tiles_per_batch = seq // tm
    width = MEM_HEADS * MEM_HEAD_DIM
    return pl.pallas_call(
        _mem_kernel,
        grid=(m // tm,),
        in_specs=[pl.BlockSpec((tm, d), lambda i: (i, 0)),
                  pl.BlockSpec((1, d), lambda i: (0, 0)),
                  _w_spec(wq, layer, (d, width), lambda i: (0, 0)),
                  pl.BlockSpec((n_mem, width), lambda i: (i // tiles_per_batch, 0)),
                  pl.BlockSpec((n_mem, width), lambda i: (i // tiles_per_batch, 0)),
                  _w_spec(wo, layer, (width, d), lambda i: (0, 0)),
                  pl.BlockSpec((1, d), lambda i: (0, 0))],
        out_specs=[pl.BlockSpec((tm, d), lambda i: (i, 0)),
                   pl.BlockSpec((tm, d), lambda i: (i, 0))],
        out_shape=[jax.ShapeDtypeStruct((m, d), F32),
                   jax.ShapeDtypeStruct((m, d), BF16)],
        compiler_params=_cparams(("parallel",)),
        name="mem_block",
    )(x, g.reshape(1, d).astype(F32), wq, k, v, wo, g_next.reshape(1, d).astype(F32))


def _rel_bias_by_distance(rel_table):
    d = jnp.arange(REL_MAX_DIST, dtype=jnp.int32)
    max_exact = REL_BUCKETS // 2
    nf = jnp.maximum(d, 1).astype(F32)
    large = max_exact + (jnp.log(nf / max_exact) / math.log(REL_MAX_DIST / max_exact)
                         * (REL_BUCKETS - max_exact)).astype(jnp.int32)
    large = jnp.minimum(large, REL_BUCKETS - 1)
    bucket = jnp.where(d < max_exact, d, large)
    return jnp.take(rel_table, bucket, axis=0).T.astype(F32)


def _in_proj_col_scale():
    gw = GROUP_WIDTH
    scale = np.ones((PROJ_COLS,), np.float32)
    scale[PROJ_A0:PROJ_A0 + gw] = DIFF_QK ** -0.5 * LOG2E
    scale[PROJ_C0:PROJ_C0 + gw] = HEAD_DIM ** -0.5 * LOG2E
    scale[PROJ_D0:PROJ_D0 + gw] = HEAD_DIM ** -0.5 * LOG2E
    return scale


PREP_CHUNK = PROJ_ALIGN
PREP_ALIGNED = PROJ_C0 // PREP_CHUNK
PREP_SKEW = PREP_CHUNK - (PROJ_C0 - (A_COLS + B_COLS))


def _regroup_kernel(x_ref, tail_ref, s_ref, o_ref):
    c = pl.program_id(1)

    @pl.when(c < PREP_ALIGNED)
    def _():
        o_ref[...] = (x_ref[...].T * s_ref[...]).astype(o_ref.dtype)

    @pl.when(c >= PREP_ALIGNED)
    def _():
        x = jnp.concatenate([x_ref[PREP_SKEW:, :], tail_ref[...]], axis=0)
        o_ref[...] = (x.T * s_ref[...]).astype(o_ref.dtype)


def _regroup_in_proj(w_in_t, layer, tr=2048):
    d = w_in_t.shape[2]
    tails_per_chunk = PREP_CHUNK // PREP_SKEW
    scale = jnp.asarray(_in_proj_col_scale()).reshape(1, PROJ_COLS)
    return pl.pallas_call(
        _regroup_kernel,
        grid=(d // tr, PROJ_COLS // PREP_CHUNK),
        in_specs=[
            pl.BlockSpec((None, PREP_CHUNK, tr),
                         lambda r, c: (layer, jnp.where(c < PREP_ALIGNED, c, c - 1), r)),
            pl.BlockSpec((None, PREP_SKEW, tr),
                         lambda r, c: (layer, jnp.where(c < PREP_ALIGNED, 0, c * tails_per_chunk), r)),
            pl.BlockSpec((1, PREP_CHUNK), lambda r, c: (0, c))],
        out_specs=pl.BlockSpec((tr, PREP_CHUNK), lambda r, c: (r, c)),
        out_shape=jax.ShapeDtypeStruct((d, PROJ_COLS), BF16),
        compiler_params=_cparams(("parallel", "parallel")),
        name="regroup_in_proj",
    )(w_in_t, w_in_t, scale)


SIDE_COLS = 4 * LANES


def _side_proj_kernel(h_ref, kr_ref, f_ref, o_ref):
    kr, f = kr_ref[...], f_ref[...]
    half = MLA_ROPE // 2

    def zeros(n):
        return jnp.zeros((n, kr.shape[1]), kr.dtype)

    w_t = jnp.concatenate([kr, zeros(LANES - MLA_ROPE), kr[half:], kr[:half], zeros(LANES - MLA_ROPE),
                           f, zeros(2 * LANES - GROUP_HEADS)], axis=0)
    o_ref[...] = _dot_nt(h_ref[...], w_t.astype(h_ref.dtype))


def _side_proj(h, w_in_t, layer, tm=1024):
    m, d = h.shape
    kr_row0 = A_COLS + MLA_Q_RANK + MLA_KV_RANK
    f_row0 = A_COLS + B_COLS + C_COLS + 3 * GROUP_WIDTH
    return pl.pallas_call(
        _side_proj_kernel,
        grid=(m // tm,),
        in_specs=[pl.BlockSpec((tm, d), lambda i: (i, 0)),
                  pl.BlockSpec((None, MLA_ROPE, d), lambda i: (layer, kr_row0 // MLA_ROPE, 0)),
                  pl.BlockSpec((None, GROUP_HEADS, d), lambda i: (layer, f_row0 // GROUP_HEADS, 0))],
        out_specs=pl.BlockSpec((tm, SIDE_COLS), lambda i: (i, 0)),
        out_shape=jax.ShapeDtypeStruct((m, SIDE_COLS), F32),
        compiler_params=_cparams(("parallel",)),
        name="in_proj_side",
    )(h, w_in_t, w_in_t)


def _prep_uq(w):
    qk = MLA_NOPE + MLA_ROPE
    half = MLA_ROPE // 2
    w = (w * (qk ** -0.5 * LOG2E)).reshape(w.shape[0], GROUP_HEADS, qk)
    pad = jnp.zeros((w.shape[0], GROUP_HEADS, LANES - MLA_ROPE), w.dtype)
    nope = w[:, :, :MLA_NOPE]
    r1 = w[:, :, MLA_NOPE:MLA_NOPE + half]
    r2 = w[:, :, MLA_NOPE + half:]
    rope = jnp.concatenate([r1, r2, pad], axis=-1)
    swapped = jnp.concatenate([r2, r1, pad], axis=-1)
    return jnp.concatenate([nope.reshape(w.shape[0], -1), rope.reshape(w.shape[0], -1),
                            swapped.reshape(w.shape[0], -1)], axis=1).astype(BF16)


def _prep_ukv(w):
    w = w.reshape(w.shape[0], GROUP_HEADS, MLA_NOPE + HEAD_DIM)
    return jnp.concatenate([w[:, :, :MLA_NOPE].reshape(w.shape[0], -1),
                            w[:, :, MLA_NOPE:].reshape(w.shape[0], -1)], axis=1).astype(BF16)


def kernel(x, mem, positions, attn_norm_g, w_in, w_out, rel_table, diff_lam_q1, diff_lam_k1, diff_lam_q2, diff_lam_k2, diff_sub_g, mla_q_norm_g, mla_kv_norm_g, mla_w_uq, mla_w_ukv, fox_b_f, mem_q_norm_g, mem_kv_norm_g, mem_w_q, mem_w_k, mem_w_v, mem_w_o, mlp_norm_g, w_up, w_down, final_norm_g):
    batch, seq, d = x.shape
    m = batch * seq
    gw = GROUP_WIDTH
    nh = GROUP_HEADS
    xf = x.reshape(m, d)
    mem2 = mem.reshape(batch * mem.shape[1], d)
    pos_col = positions.reshape(m, 1).astype(jnp.int32)
    pos_row = positions.reshape(batch, 1, seq).astype(jnp.int32)
    cos, sin = _rope_tables(pos_col)
    bias_by_dist = (_rel_bias_by_distance(rel_table) * LOG2E).reshape(nh, 1, LANES)
    npair = nh // HEADS_PER_STEP
    pair_cols = HEADS_PER_STEP * HEAD_DIM
    a0 = PROJ_A0 // pair_cols
    nwide = nh // WIDE_HEADS_PER_STEP
    fox0 = PROJ_D0 // (WIDE_HEADS_PER_STEP * HEAD_DIM)
    nsb = nh // SB_HEADS_PER_STEP
    sb0 = PROJ_C0 // (SB_HEADS_PER_STEP * HEAD_DIM)

    w_down_b = w_down.astype(BF16)
    mem_wq_b = (mem_w_q * (MEM_HEAD_DIM ** -0.5 * LOG2E)).astype(BF16)
    mem_wk_b, mem_wv_b, mem_wo_b = mem_w_k.astype(BF16), mem_w_v.astype(BF16), mem_w_o.astype(BF16)

    w_in_t = jnp.swapaxes(w_in, 1, 2)

    for l in range(DEPTH):
        w_main = _regroup_in_proj(w_in_t, l)
        h = _rmsnorm(xf, attn_norm_g[l], BF16)
        proj = _matmul(h, w_main, BF16, name="in_proj")
        side = _side_proj(h, w_in_t, l)
        b_f = jnp.concatenate([fox_b_f[l].astype(F32), jnp.zeros((LANES - nh,), F32)]).reshape(1, LANES)
        cum, fox_kx = _forget_cum(side, b_f, batch, seq)

        lam_init = 0.8 - 0.6 * math.exp(-0.3 * l)
        lam_vecs = [v[l].reshape(1, DIFF_QK).astype(F32)
                    for v in (diff_lam_q1, diff_lam_k1, diff_lam_q2, diff_lam_k2)]
        o_a = _attn_call(
            functools.partial(_diff_kernel, t=ATT_T, seq=seq, lam_init=lam_init), ATT_T, batch, seq,
            lam_vecs + [proj, proj, proj, pos_col, pos_row, bias_by_dist,
                        diff_sub_g[l].reshape(1, HEAD_DIM).astype(F32)],
            [_full_spec((1, DIFF_QK))] * 4
            + [_q_spec(ATT_T, seq, a0), _kv_spec(seq, a0 + npair), _kv_spec(seq, a0 + 2 * npair),
               pl.BlockSpec((ATT_T, 1), lambda b, h, i: (b * (seq // ATT_T) + i, 0)),
               pl.BlockSpec((None, 1, seq), lambda b, h, i: (b, 0, 0)),
               pl.BlockSpec((HEADS_PER_STEP, 1, LANES), lambda b, h, i: (h, 0, 0)),
               _full_spec((1, HEAD_DIM))],
            "diff_attn", softmax_chains=2 * HEADS_PER_STEP)

        q_b = _mla_q(proj, mla_q_norm_g[l], _prep_uq(mla_w_uq[l]), cos, sin)
        kv_b, kpe_b = _mla_kv(proj, mla_kv_norm_g[l], _prep_ukv(mla_w_ukv[l]), side, cos, sin)
        o_b = _attn_call(
            functools.partial(_mla_kernel, t=ATT_T, seq=seq), ATT_T, batch, seq,
            [q_b, q_b, kv_b, kpe_b, kv_b],
            [_q_spec(ATT_T, seq, 0, WIDE_HEADS_PER_STEP), _q_spec(ATT_T, seq, nwide, WIDE_HEADS_PER_STEP),
             _kv_spec(seq, 0, WIDE_HEADS_PER_STEP), _shared_spec(seq),
             _kv_spec(seq, nwide, WIDE_HEADS_PER_STEP)],
            "mla_attn", hps=WIDE_HEADS_PER_STEP, softmax_chains=WIDE_HEADS_PER_STEP)

        o_c = _attn_call(
            functools.partial(_sb_kernel, t=SB_T), SB_T, batch, seq,
            [proj, proj, proj],
            [_q_spec(SB_T, seq, sb0, SB_HEADS_PER_STEP), _kv_spec(seq, sb0 + nsb, SB_HEADS_PER_STEP),
             _kv_spec(seq, sb0 + 2 * nsb, SB_HEADS_PER_STEP)],
            "stickbreak_attn", hps=SB_HEADS_PER_STEP)

        o_d = _attn_call(
            functools.partial(_fox_kernel, t=ATT_T, seq=seq), ATT_T, batch, seq,
            [proj, proj, proj, fox_kx, cum, cum],
            [_q_spec(ATT_T, seq, fox0, WIDE_HEADS_PER_STEP), _kv_spec(seq, fox0 + nwide, WIDE_HEADS_PER_STEP),
             _kv_spec(seq, fox0 + 2 * nwide, WIDE_HEADS_PER_STEP),
             pl.BlockSpec((None, WIDE_HEADS_PER_STEP, seq, LANES), lambda b, h, i: (b, h, 0, 0)),
             pl.BlockSpec((ATT_T, LANES), lambda b, h, i: (b * (seq // ATT_T) + i, 0)),
             pl.BlockSpec((seq, LANES), lambda b, h, i: (b, 0))],
            "fox_attn", hps=WIDE_HEADS_PER_STEP, softmax_chains=WIDE_HEADS_PER_STEP)

        xf = _out_proj([o_a, o_b, o_c, o_d], w_out, l, xf)

        mem_n = _rmsnorm(mem2, mem_kv_norm_g[l], BF16)
        k_m = _matmul(mem_n, mem_wk_b, BF16, layer=l, name="mem_k")
        v_m = _matmul(mem_n, mem_wv_b, BF16, layer=l, name="mem_v")
        xf, h = _mem_block(xf, mem_q_norm_g[l], mem_wq_b, k_m, v_m, mem_wo_b, mlp_norm_g[l],
                           l, batch, seq)

        u = _matmul(h, w_up, BF16, layer=l, relu2=True, name="mlp_up")
        xf = _matmul(u, w_down_b, F32, layer=l, res=xf, tn=1024, tk=2048, name="mlp_down")

    return _rmsnorm(xf, final_norm_g, F32).reshape(batch, seq, d)
```
